```python
import math
import jax, jax.numpy as jnp
from jax import lax
import numpy as np

D_MODEL = 1024
BATCH = 4
SEQ = 4096
DEPTH = 1

CHUNK = 64
D_MIX = D_MODEL
RET_WIDTH = D_MIX // 2
RET_HEADS = 4
RET_HEAD_DIM = RET_WIDTH // RET_HEADS
POOL_WIDTH = D_MIX - RET_WIDTH
POOL_WINDOWS = (2, 4, 8, 16)
POOL_GROUPS = len(POOL_WINDOWS)
POOL_GROUP_WIDTH = POOL_WIDTH // POOL_GROUPS
D_IN = 4 * RET_WIDTH + POOL_WIDTH
ROPE_BASE = 10000.0
N_EXPERTS = 32
TOP_K = 4
D_FF_EXPERT = D_MODEL
SWIGLU_LIMIT = 7.0
SWIGLU_ALPHA = 1.702
MOE_BLOCK = 128
EPS = 1e-6

kernel_name = "hybrid_retention_pool_moe_block"


def rms_norm(x, w):
    x32 = x.astype(jnp.float32)
    y = x32 * lax.rsqrt(jnp.mean(x32 * x32, axis=-1, keepdims=True) + EPS)
    return (y * w.astype(jnp.float32)).astype(x.dtype)


def rope_tables(seq_len, dim):
    half = dim // 2
    inv = ROPE_BASE ** (-jnp.arange(half, dtype=jnp.float32) / half)
    ang = jnp.arange(seq_len, dtype=jnp.float32)[:, None] * inv[None, :]
    return jnp.cos(ang), jnp.sin(ang)


def apply_rope(t, cos, sin):
    half = t.shape[-1] // 2
    t1, t2 = t[..., :half], t[..., half:]
    c, s = cos.astype(t.dtype), sin.astype(t.dtype)
    return jnp.concatenate([t1 * c - t2 * s, t1 * s + t2 * c], axis=-1)


def retention(q, k, v):
    B, S, _ = q.shape
    H, d, C = RET_HEADS, RET_HEAD_DIM, CHUNK
    NC = S // C

    def heads(t):
        return t.reshape(B, S, H, d).transpose(0, 2, 1, 3)

    cos, sin = rope_tables(S, d)
    qh = apply_rope(heads(q), cos, sin)
    kh = apply_rope(heads(k), cos, sin) * (d ** -0.5)
    vh = heads(v)
    qc = qh.reshape(B, H, NC, C, d)
    kc = kh.reshape(B, H, NC, C, d)
    vc = vh.reshape(B, H, NC, C, d)

    log_g = jnp.log1p(-jnp.exp2(-5.0 - jnp.arange(H, dtype=jnp.float32)))
    pos = jnp.arange(C, dtype=jnp.float32)
    dt = q.dtype

    inner_decay = jnp.exp(log_g[:, None, None] * jnp.abs(pos[:, None] - pos[None, :])).astype(dt)
    scores = jnp.einsum('bhcnd,bhcmd->bhcnm', qc, kc) * inner_decay[None, :, None]
    inner = jnp.einsum('bhcnm,bhcmv->bhcnv', scores, vc)

    k_decay = jnp.exp(log_g[:, None] * (C - 1 - pos)[None, :]).astype(dt)
    U = jnp.einsum('bhcmd,bhcmv->cbhdv', kc * k_decay[None, :, None, :, None], vc)
    chunk_decay = jnp.exp(log_g * C).astype(U.dtype)[None, :, None, None]

    def step(R, U_c):
        return chunk_decay * R + U_c, R

    _, R_prev = lax.scan(step, jnp.zeros_like(U[0]), U)
    q_decay = jnp.exp(log_g[:, None] * (pos + 1.0)[None, :]).astype(dt)
    cross = jnp.einsum('bhcnd,cbhdv->bhcnv', qc * q_decay[None, :, None, :, None], R_prev)

    o = (inner + cross).reshape(B, H, S, d).astype(jnp.float32)
    o = o * lax.rsqrt(jnp.mean(o * o, axis=-1, keepdims=True) + EPS)
    return o.transpose(0, 2, 1, 3).reshape(B, S, H * d).astype(dt)


def pool_mixer(p, w_pool, pool_scale):
    B, S, _ = p.shape
    cs = jnp.cumsum(p.astype(jnp.float32), axis=1)
    t = jnp.arange(S, dtype=jnp.int32)
    outs = []
    for g, w in enumerate(POOL_WINDOWS):
        c = cs[..., g * POOL_GROUP_WIDTH:(g + 1) * POOL_GROUP_WIDTH]
        lagged = jnp.pad(c, ((0, 0), (w, 0), (0, 0)))[:, :S]
        count = jnp.minimum(t + 1, w).astype(jnp.float32)[None, :, None]
        outs.append((c - lagged) / count)
    pooled = jnp.stack(outs, axis=2) - p.astype(jnp.float32).reshape(B, S, POOL_GROUPS, POOL_GROUP_WIDTH)
    mixed = jnp.einsum('bsgc,gcd->bsgd', pooled.astype(p.dtype), w_pool)
    return mixed.reshape(B, S, POOL_WIDTH) * pool_scale


def moe(h, w_router, b_router, w_gu, b_gu, w_down, b_down):
    B, S, D = h.shape
    N = B * S
    A = N * TOP_K
    F = D_FF_EXPERT
    xt = h.reshape(N, D)
    logits = (xt @ w_router + b_router).astype(jnp.float32)
    top_vals, top_idx = lax.top_k(logits, TOP_K)
    gates = jax.nn.softmax(top_vals, axis=-1).astype(h.dtype)

    e_flat = top_idx.reshape(A).astype(jnp.int32)
    g_flat = gates.reshape(A)
    tok_flat = jnp.arange(A, dtype=jnp.int32) // TOP_K
    order = jnp.argsort(e_flat)
    e_sorted = e_flat[order]
    counts = jnp.zeros((N_EXPERTS,), jnp.int32).at[e_flat].add(1)
    padded = ((counts + MOE_BLOCK - 1) // MOE_BLOCK) * MOE_BLOCK
    pad_end = jnp.cumsum(padded)
    pad_start = pad_end - padded
    start = jnp.cumsum(counts) - counts
    dest = pad_start[e_sorted] + (jnp.arange(A, dtype=jnp.int32) - start[e_sorted])

    n_blocks = (A + MOE_BLOCK - 1) // MOE_BLOCK + N_EXPERTS
    P = n_blocks * MOE_BLOCK
    row_tok = jnp.full((P,), N, jnp.int32).at[dest].set(tok_flat[order])
    row_gate = jnp.zeros((P,), h.dtype).at[dest].set(g_flat[order])
    block_start = jnp.arange(n_blocks, dtype=jnp.int32) * MOE_BLOCK
    block_exp = jnp.minimum(jnp.searchsorted(pad_end, block_start, side='right'), N_EXPERTS - 1)

    x_pad = jnp.concatenate([xt, jnp.zeros((1, D), xt.dtype)], axis=0)
    xb = x_pad[row_tok].reshape(n_blocks, MOE_BLOCK, D)

    def expert_block(args):
        xblk, e = args
        gu = xblk @ w_gu[e] + b_gu[e]
        x_glu = jnp.minimum(gu[:, :F], SWIGLU_LIMIT)
        x_lin = jnp.clip(gu[:, F:], -SWIGLU_LIMIT, SWIGLU_LIMIT)
        act = x_glu * jax.nn.sigmoid(SWIGLU_ALPHA * x_glu) * (x_lin + 1.0)
        return act @ w_down[e] + b_down[e]

    yb = lax.map(expert_block, (xb, block_exp))
    y = yb.reshape(P, D) * row_gate[:, None]
    out = jax.ops.segment_sum(y, row_tok, num_segments=N + 1)[:N]
    return out.reshape(B, S, D)


def setup_inputs(seed: int = 0) -> dict:
    key = jax.random.key(seed)
    ks = jax.random.split(key, 16)
    f32 = jnp.float32
    L, E, D, F = DEPTH, N_EXPERTS, D_MODEL, D_FF_EXPERT
    return {
        "x": jax.random.normal(ks[0], (BATCH, SEQ, D), f32),
        "norm_mix_w": 1.0 + 0.02 * jax.random.normal(ks[1], (L, D), f32),
        "w_in": jax.random.normal(ks[2], (L, D, D_IN), f32) * D ** -0.5,
        "ret_norm_w": 1.0 + 0.02 * jax.random.normal(ks[3], (L, RET_WIDTH), f32),
        "w_pool": jax.random.normal(ks[4], (L, POOL_GROUPS, POOL_GROUP_WIDTH, POOL_GROUP_WIDTH), f32) * POOL_GROUP_WIDTH ** -0.5,
        "pool_scale": 1.0 + 0.02 * jax.random.normal(ks[5], (L, POOL_WIDTH), f32),
        "w_out": jax.random.normal(ks[6], (L, D_MIX, D), f32) * D_MIX ** -0.5,
        "norm_ffn_w": 1.0 + 0.02 * jax.random.normal(ks[7], (L, D), f32),
        "w_router": jax.random.normal(ks[8], (L, D, E), f32) * D ** -0.5,
        "b_router": 0.01 * jax.random.normal(ks[9], (L, E), f32),
        "w_gu": jax.random.normal(ks[10], (L, E, D, 2 * F), f32) * D ** -0.5,
        "b_gu": 0.02 * jax.random.normal(ks[11], (L, E, 2 * F), f32),
        "w_down": jax.random.normal(ks[12], (L, E, F, D), f32) * F ** -0.5,
        "b_down": 0.02 * jax.random.normal(ks[13], (L, E, D), f32),
        "norm_final_w": 1.0 + 0.02 * jax.random.normal(ks[14], (D,), f32),
    }


def reference(x, norm_mix_w, w_in, ret_norm_w, w_pool, pool_scale, w_out, norm_ffn_w,
              w_router, b_router, w_gu, b_gu, w_down, b_down, norm_final_w):
    for l in range(DEPTH):
        h = rms_norm(x, norm_mix_w[l])
        z = h @ w_in[l]
        q, k, v, g, p = jnp.split(z, [RET_WIDTH, 2 * RET_WIDTH, 3 * RET_WIDTH, 4 * RET_WIDTH], axis=-1)
        ret = retention(q, k, v) * ret_norm_w[l] * jax.nn.silu(g)
        pool = pool_mixer(p, w_pool[l], pool_scale[l])
        x = x + jnp.concatenate([ret, pool], axis=-1) @ w_out[l]
        h2 = rms_norm(x, norm_ffn_w[l])
        x = x + moe(h2, w_router[l], b_router[l], w_gu[l], b_gu[l], w_down[l], b_down[l])
    return rms_norm(x, norm_final_w)
```

```python
import functools
import math

import jax
import jax.numpy as jnp
from jax import lax
from jax.experimental import pallas as pl
from jax.experimental.pallas import tpu as pltpu

CHUNK = 64
RET_HEADS = 4
POOL_WINDOWS = (2, 4, 8, 16)
ROPE_BASE = 10000.0
TOP_K = 4
SWIGLU_LIMIT = 7.0
SWIGLU_ALPHA = 1.702
EPS = 1e-6

LANES = 128
SUBLANES = 8
VMEM_LIMIT_BYTES = 56 * 1024 * 1024

SEQ_TILE = 256
ROW_TILE = 256
TOK_TILE = 256
POOL_HALO = 16
PACK = 2


def _rms(x, w):
    return x * lax.rsqrt(jnp.mean(x * x, axis=-1, keepdims=True) + EPS) * w


def _bf16_bits_hi(x):
    return lax.bitcast_convert_type(x.astype(jnp.bfloat16).astype(jnp.float32), jnp.uint32)


def _mix_route_kernel(x_ref, nmw_ref, win_ref, cos_ref, sin_ref, dmat_ref, qdec_ref, kdec_ref,
                      rnw_ref, wpool_ref, pscale_ref, wout_ref, nfw_ref, wrt_ref, brt_ref, utri_ref,
                      x1_ref, h2p_ref, eid_ref, gate_ref, rank_ref, cnt_ref,
                      state_ref, p1_ref, p2_ref, p4_ref, p8_ref, carry_ref,
                      *, tile_decay):
    b = pl.program_id(0)
    j = pl.program_id(1)
    ts = x_ref.shape[1]
    d_model = x_ref.shape[2]
    n_heads = RET_HEADS
    rw = rnw_ref.shape[1]
    hd = rw // n_heads
    pgw = wpool_ref.shape[1]
    n_exp = wrt_ref.shape[0]
    halo = POOL_HALO

    @pl.when(jnp.logical_and(b == 0, j == 0))
    def _():
        carry_ref[...] = jnp.zeros_like(carry_ref)

    @pl.when(j == 0)
    def _():
        state_ref[...] = jnp.zeros_like(state_ref)
        for r in (p1_ref, p2_ref, p4_ref, p8_ref):
            r[0:halo, :] = jnp.zeros((halo, r.shape[1]), jnp.float32)

    @pl.when(j > 0)
    def _():
        for r in (p1_ref, p2_ref, p4_ref, p8_ref):
            r[0:halo, :] = r[ts:ts + halo, :]

    x = x_ref[0]
    h = _rms(x, nmw_ref[...]).astype(jnp.bfloat16)
    z = jnp.dot(h, win_ref[...], preferred_element_type=jnp.float32)

    cos = cos_ref[...]
    sin = sin_ref[...]
    k_scale = hd ** -0.5

    ret_parts = []
    for hh in range(n_heads):
        q = z[:, hh * hd:(hh + 1) * hd]
        k = z[:, rw + hh * hd:rw + (hh + 1) * hd]
        v = z[:, 2 * rw + hh * hd:2 * rw + (hh + 1) * hd]
        g = z[:, 3 * rw + hh * hd:3 * rw + (hh + 1) * hd]
        q = q * cos + pltpu.roll(q, hd // 2, 1) * sin
        k = (k * cos + pltpu.roll(k, hd // 2, 1) * sin) * k_scale
        qb = q.astype(jnp.bfloat16)
        kb = k.astype(jnp.bfloat16)
        vb = v.astype(jnp.bfloat16)
        s = lax.dot_general(qb, kb, (((1,), (1,)), ((), ())), preferred_element_type=jnp.float32)
        pmat = (s * dmat_ref[hh]).astype(jnp.bfloat16)
        o = jnp.dot(pmat, vb, preferred_element_type=jnp.float32)
        state = state_ref[hh]
        qd = (q * qdec_ref[hh]).astype(jnp.bfloat16)
        o = o + jnp.dot(qd, state.astype(jnp.bfloat16), preferred_element_type=jnp.float32)
        kd = (k * kdec_ref[hh]).astype(jnp.bfloat16)
        u = lax.dot_general(kd, vb, (((0,), (0,)), ((), ())), preferred_element_type=jnp.float32)
        state_ref[hh] = state * tile_decay[hh] + u
        o = o * lax.rsqrt(jnp.mean(o * o, axis=-1, keepdims=True) + EPS)
        o = o * rnw_ref[:, hh * hd:(hh + 1) * hd] * (g * (1.0 / (1.0 + jnp.exp(-g))))
        ret_parts.append(o.astype(jnp.bfloat16))

    p = z[:, 4 * rw:]
    p1_ref[halo:halo + ts, :] = p
    s2 = p + p1_ref[halo - 1:halo - 1 + ts, :]
    p2_ref[halo:halo + ts, :] = s2[:, pgw:]
    s4 = s2[:, pgw:] + p2_ref[halo - 2:halo - 2 + ts, :]
    p4_ref[halo:halo + ts, :] = s4[:, pgw:]
    s8 = s4[:, pgw:] + p4_ref[halo - 4:halo - 4 + ts, :]
    p8_ref[halo:halo + ts, :] = s8[:, pgw:]
    s16 = s8[:, pgw:] + p8_ref[halo - 8:halo - 8 + ts, :]
    sums = (s2[:, :pgw], s4[:, :pgw], s8[:, :pgw], s16)
    tpos = (j * ts + lax.broadcasted_iota(jnp.int32, (ts, pgw), 0) + 1).astype(jnp.float32)
    pool_parts = []
    for gi, w in enumerate(POOL_WINDOWS):
        cnt = jnp.minimum(tpos, float(w))
        pooled = sums[gi] / cnt - p[:, gi * pgw:(gi + 1) * pgw]
        mixed = jnp.dot(pooled.astype(jnp.bfloat16), wpool_ref[gi], preferred_element_type=jnp.float32)
        mixed = mixed * pscale_ref[:, gi * pgw:(gi + 1) * pgw]
        pool_parts.append(mixed.astype(jnp.bfloat16))

    mix = jnp.concatenate(ret_parts + pool_parts, axis=-1)
    x1 = x + jnp.dot(mix, wout_ref[...], preferred_element_type=jnp.float32)
    x1_ref[0] = x1

    h2 = _rms(x1, nfw_ref[...])
    half = d_model // PACK
    packed = (_bf16_bits_hi(h2[:, :half]) >> 16) | _bf16_bits_hi(h2[:, half:])
    rows_per_tok = half // LANES
    for si in range(rows_per_tok):
        h2p_ref[pl.ds(si, ts, stride=rows_per_tok), :] = packed[:, si * LANES:(si + 1) * LANES]

    h2b = h2.astype(jnp.bfloat16)
    logits = lax.dot_general(wrt_ref[...], h2b, (((1,), (1,)), ((), ())),
                             preferred_element_type=jnp.float32) + brt_ref[...]
    eiota = lax.broadcasted_iota(jnp.int32, (n_exp, ts), 0)
    work = logits
    vals, ids, sels = [], [], []
    for _ in range(TOP_K):
        m = jnp.max(work, axis=0, keepdims=True)
        idx = jnp.min(jnp.where(work == m, eiota, n_exp), axis=0, keepdims=True)
        sel = eiota == idx
        vals.append(m)
        ids.append(idx)
        sels.append(sel)
        work = jnp.where(sel, -jnp.inf, work)
    ex = [jnp.exp(vk - vals[0]) for vk in vals]
    den = ex[0] + ex[1] + ex[2] + ex[3]
    gate_ref[...] = jnp.concatenate([e / den for e in ex], axis=0)
    eid_ref[...] = jnp.concatenate(ids, axis=0)

    member = jnp.zeros((n_exp, ts), jnp.float32)
    for sel in sels:
        member = member + sel.astype(jnp.float32)
    before = jnp.dot(member.astype(jnp.bfloat16), utri_ref[...], preferred_element_type=jnp.float32)
    before = before + carry_ref[:, 0:1]
    ranks = [jnp.sum(jnp.where(sel, before, 0.0), axis=0, keepdims=True) for sel in sels]
    rank_ref[...] = jnp.concatenate(ranks, axis=0).astype(jnp.int32)
    new_carry = carry_ref[...] + jnp.sum(member, axis=1, keepdims=True)
    carry_ref[...] = new_carry
    cnt_ref[...] = new_carry


def _mix_route(x, nmw, win_b, cos2, sin2, dmat, qdec, kdec, rnw, wpool_b, pscale, wout_b, nfw,
               wrt_b, brt, utri, tile_decay):
    bsz, seq, d_model = x.shape
    ts = SEQ_TILE
    n_tok = bsz * seq
    n_exp = wrt_b.shape[0]
    rw = rnw.shape[1]
    hd = rw // RET_HEADS
    pw = pscale.shape[1]
    pgw = pw // len(POOL_WINDOWS)
    nj = seq // ts
    rows_per_tok = d_model // PACK // LANES

    def const(shape):
        return pl.BlockSpec(shape, lambda b, j: (0,) * len(shape))

    in_specs = [
        pl.BlockSpec((1, ts, d_model), lambda b, j: (b, j, 0)),
        const(nmw.shape), const(win_b.shape),
        pl.BlockSpec((ts, hd), lambda b, j: (j, 0)),
        pl.BlockSpec((ts, hd), lambda b, j: (j, 0)),
        const(dmat.shape), const(qdec.shape), const(kdec.shape),
        const(rnw.shape), const(wpool_b.shape), const(pscale.shape), const(wout_b.shape),
        const(nfw.shape), const(wrt_b.shape), const(brt.shape), const(utri.shape),
    ]
    out_shape = [
        jax.ShapeDtypeStruct((bsz, seq, d_model), jnp.float32),
        jax.ShapeDtypeStruct((n_tok * rows_per_tok, LANES), jnp.uint32),
        jax.ShapeDtypeStruct((TOP_K, n_tok), jnp.int32),
        jax.ShapeDtypeStruct((TOP_K, n_tok), jnp.float32),
        jax.ShapeDtypeStruct((TOP_K, n_tok), jnp.int32),
        jax.ShapeDtypeStruct((n_exp, LANES), jnp.float32),
    ]
    out_specs = [
        pl.BlockSpec((1, ts, d_model), lambda b, j: (b, j, 0)),
        pl.BlockSpec((ts * rows_per_tok, LANES), lambda b, j: (b * nj + j, 0)),
        pl.BlockSpec((TOP_K, ts), lambda b, j: (0, b * nj + j)),
        pl.BlockSpec((TOP_K, ts), lambda b, j: (0, b * nj + j)),
        pl.BlockSpec((TOP_K, ts), lambda b, j: (0, b * nj + j)),
        pl.BlockSpec((n_exp, LANES), lambda b, j: (0, 0)),
    ]
    scratch = [
        pltpu.VMEM((RET_HEADS, hd, hd), jnp.float32),
        pltpu.VMEM((POOL_HALO + ts, pw), jnp.float32),
        pltpu.VMEM((POOL_HALO + ts, pw - pgw), jnp.float32),
        pltpu.VMEM((POOL_HALO + ts, pw - 2 * pgw), jnp.float32),
        pltpu.VMEM((POOL_HALO + ts, pw - 3 * pgw), jnp.float32),
        pltpu.VMEM((n_exp, LANES), jnp.float32),
    ]
    return pl.pallas_call(
        functools.partial(_mix_route_kernel, tile_decay=tile_decay),
        grid=(bsz, nj),
        in_specs=in_specs,
        out_specs=out_specs,
        out_shape=out_shape,
        scratch_shapes=scratch,
        compiler_params=pltpu.CompilerParams(
            dimension_semantics=("arbitrary", "arbitrary"),
            vmem_limit_bytes=VMEM_LIMIT_BYTES),
        name="mix_route",
    )(x, nmw, win_b, cos2, sin2, dmat, qdec, kdec, rnw, wpool_b, pscale, wout_b, nfw, wrt_b, brt, utri)


def _unpack_rows(g_ref, n_rows, rows_per_tok):
    lo, hi = [], []
    for si in range(rows_per_tok):
        u = g_ref[pl.ds(si, n_rows, stride=rows_per_tok), :]
        lo.append(lax.bitcast_convert_type(u << 16, jnp.float32).astype(jnp.bfloat16))
        hi.append(lax.bitcast_convert_type(u & jnp.uint32(0xFFFF0000), jnp.float32).astype(jnp.bfloat16))
    return jnp.concatenate(lo + hi, axis=-1)


def _dispatch_kernel(src_ref, nused_ref, h2p_ref, xs_ref, g_ref):
    i = pl.program_id(0)
    tm = xs_ref.shape[0]
    rpt = g_ref.shape[0] // tm

    @pl.when(i < nused_ref[0])
    def _():
        base = i * tm
        for m in range(tm):
            src = pl.multiple_of(src_ref[base + m], rpt)
            g_ref[m * rpt:(m + 1) * rpt, :] = h2p_ref[pl.ds(src, rpt), :]
        xs_ref[...] = _unpack_rows(g_ref, tm, rpt)

    @pl.when(i >= nused_ref[0])
    def _():
        xs_ref[...] = jnp.zeros_like(xs_ref)


def _dispatch(row_src, n_used, h2p, n_tiles, d_model):
    tm = ROW_TILE
    rpt = d_model // PACK // LANES
    grid_spec = pltpu.PrefetchScalarGridSpec(
        num_scalar_prefetch=2,
        grid=(n_tiles,),
        in_specs=[pl.BlockSpec(h2p.shape, lambda i, s, n: (0, 0), pipeline_mode=pl.Buffered(1))],
        out_specs=pl.BlockSpec((tm, d_model), lambda i, s, n: (i, 0)),
        scratch_shapes=[pltpu.VMEM((tm * rpt, LANES), jnp.uint32)],
    )
    return pl.pallas_call(
        _dispatch_kernel,
        grid_spec=grid_spec,
        out_shape=jax.ShapeDtypeStruct((n_tiles * tm, d_model), jnp.bfloat16),
        compiler_params=pltpu.CompilerParams(
            dimension_semantics=("arbitrary",),
            vmem_limit_bytes=VMEM_LIMIT_BYTES),
        name="dispatch",
    )(row_src, n_used, h2p)


def _experts_kernel(texp_ref, nused_ref, xs_ref, wgu_ref, bgu_ref, wd_ref, bd_ref, gate_ref,
                    y_ref, wgu_b, wd_b):
    i = pl.program_id(0)
    tm = xs_ref.shape[0]
    d_ff = wd_ref.shape[1]
    d_model = wd_ref.shape[2]
    prev = texp_ref[jnp.maximum(i - 1, 0)]
    new_expert = jnp.logical_or(i == 0, texp_ref[i] != prev)

    @pl.when(jnp.logical_and(new_expert, i < nused_ref[0]))
    def _():
        wgu_b[...] = wgu_ref[0].astype(jnp.bfloat16)
        wd_b[...] = wd_ref[0].astype(jnp.bfloat16)

    @pl.when(i < nused_ref[0])
    def _():
        gu = jnp.dot(xs_ref[...], wgu_b[...], preferred_element_type=jnp.float32) + bgu_ref[0]
        x_glu = jnp.minimum(gu[:, :d_ff], SWIGLU_LIMIT)
        x_lin = jnp.clip(gu[:, d_ff:], -SWIGLU_LIMIT, SWIGLU_LIMIT)
        act = x_glu * (1.0 / (1.0 + jnp.exp(-SWIGLU_ALPHA * x_glu))) * (x_lin + 1.0)
        y = jnp.dot(act.astype(jnp.bfloat16), wd_b[...], preferred_element_type=jnp.float32) + bd_ref[0]
        y = y * gate_ref[...]
        n_sl = d_model // LANES
        for si in range(n_sl):
            y_ref[pl.ds(si, tm, stride=n_sl), :] = y[:, si * LANES:(si + 1) * LANES]

    @pl.when(i >= nused_ref[0])
    def _():
        y_ref[...] = jnp.zeros_like(y_ref)


def _experts(tile_exp, n_used, xs, w_gu, b_gu, w_down, b_down, row_gate):
    tm = ROW_TILE
    n_rows, d_model = xs.shape
    n_tiles = n_rows // tm
    n_exp, _, d_gu = w_gu.shape
    d_ff = w_down.shape[1]
    n_sl = d_model // LANES
    grid_spec = pltpu.PrefetchScalarGridSpec(
        num_scalar_prefetch=2,
        grid=(n_tiles,),
        in_specs=[
            pl.BlockSpec((tm, d_model), lambda i, te, n: (i, 0)),
            pl.BlockSpec((1, d_model, d_gu), lambda i, te, n: (te[i], 0, 0)),
            pl.BlockSpec((1, 1, d_gu), lambda i, te, n: (te[i], 0, 0)),
            pl.BlockSpec((1, d_ff, d_model), lambda i, te, n: (te[i], 0, 0)),
            pl.BlockSpec((1, 1, d_model), lambda i, te, n: (te[i], 0, 0)),
            pl.BlockSpec((tm, 1), lambda i, te, n: (i, 0)),
        ],
        out_specs=pl.BlockSpec((tm * n_sl, LANES), lambda i, te, n: (i, 0)),
        scratch_shapes=[pltpu.VMEM((d_model, d_gu), jnp.bfloat16),
                        pltpu.VMEM((d_ff, d_model), jnp.bfloat16)],
    )
    return pl.pallas_call(
        _experts_kernel,
        grid_spec=grid_spec,
        out_shape=jax.ShapeDtypeStruct((n_rows * n_sl, LANES), jnp.float32),
        compiler_params=pltpu.CompilerParams(
            dimension_semantics=("arbitrary",),
            vmem_limit_bytes=VMEM_LIMIT_BYTES),
        name="experts",
    )(tile_exp, n_used, xs, w_gu, b_gu.reshape(n_exp, 1, d_gu), w_down,
      b_down.reshape(n_exp, 1, d_model), row_gate)


def _combine_kernel(pos_ref, x1_ref, nw_ref, y_hbm, out_ref, buf_ref, sem):
    i = pl.program_id(0)
    tb = x1_ref.shape[0]
    d_model = x1_ref.shape[1]
    n_sl = d_model // LANES
    n_tok = pos_ref.shape[0] // TOP_K

    def row_copy(k, t):
        src = pl.multiple_of(pos_ref[k * n_tok + i * tb + t] * n_sl, n_sl)
        dst = (k * tb + t) * n_sl
        return pltpu.make_async_copy(y_hbm.at[pl.ds(src, n_sl), :],
                                     buf_ref.at[pl.ds(dst, n_sl), :], sem)

    def issue(t, carry):
        for k in range(TOP_K):
            row_copy(k, t).start()
        return carry

    lax.fori_loop(0, tb, issue, 0)
    pltpu.make_async_copy(y_hbm.at[pl.ds(0, buf_ref.shape[0]), :], buf_ref, sem).wait()

    parts = []
    for si in range(n_sl):
        acc = buf_ref[pl.ds(si, tb, stride=n_sl), :]
        for k in range(1, TOP_K):
            acc = acc + buf_ref[pl.ds(k * tb * n_sl + si, tb, stride=n_sl), :]
        parts.append(acc)
    x2 = x1_ref[...] + jnp.concatenate(parts, axis=-1)
    out_ref[...] = _rms(x2, nw_ref[...])


def _combine(pos_flat, x1, norm_w, y_tm):
    tb = TOK_TILE
    n_tok, d_model = x1.shape
    n_sl = d_model // LANES
    grid_spec = pltpu.PrefetchScalarGridSpec(
        num_scalar_prefetch=1,
        grid=(n_tok // tb,),
        in_specs=[
            pl.BlockSpec((tb, d_model), lambda i, p: (i, 0)),
            pl.BlockSpec((1, d_model), lambda i, p: (0, 0)),
            pl.BlockSpec(memory_space=pl.ANY),
        ],
        out_specs=pl.BlockSpec((tb, d_model), lambda i, p: (i, 0)),
        scratch_shapes=[pltpu.VMEM((TOP_K * tb * n_sl, LANES), jnp.float32),
                        pltpu.SemaphoreType.DMA],
    )
    return pl.pallas_call(
        _combine_kernel,
        grid_spec=grid_spec,
        out_shape=jax.ShapeDtypeStruct((n_tok, d_model), jnp.float32),
        compiler_params=pltpu.CompilerParams(
            dimension_semantics=("arbitrary",),
            vmem_limit_bytes=VMEM_LIMIT_BYTES),
        name="combine",
    )(pos_flat, x1, norm_w, y_tm)


def _retention_tables(seq, hd, ts):
    half = hd // 2
    inv = ROPE_BASE ** (-jnp.arange(half, dtype=jnp.float32) / half)
    ang = jnp.arange(seq, dtype=jnp.float32)[:, None] * inv[None, :]
    cos, sin = jnp.cos(ang), jnp.sin(ang)
    cos2 = jnp.concatenate([cos, cos], axis=-1)
    sin2 = jnp.concatenate([-sin, sin], axis=-1)
    log_g = jnp.log1p(-jnp.exp2(-5.0 - jnp.arange(RET_HEADS, dtype=jnp.float32)))
    pos = jnp.arange(ts, dtype=jnp.float32)
    dist = jnp.abs(pos[:, None] - pos[None, :])
    chunk = jnp.arange(ts, dtype=jnp.int32) // CHUNK
    visible = chunk[None, :] <= chunk[:, None]
    dmat = jnp.where(visible[None], jnp.exp(log_g[:, None, None] * dist[None]), 0.0)
    qdec = jnp.exp(log_g[:, None] * (pos + 1.0)[None, :])
    kdec = jnp.exp(log_g[:, None] * (ts - 1.0 - pos)[None, :])
    qdec = jnp.broadcast_to(qdec[:, :, None], (RET_HEADS, ts, hd))
    kdec = jnp.broadcast_to(kdec[:, :, None], (RET_HEADS, ts, hd))
    tile_decay = tuple(math.exp(math.log1p(-2.0 ** (-5.0 - h)) * ts) for h in range(RET_HEADS))
    return cos2, sin2, dmat, qdec, kdec, tile_decay


def _layer(x, norm_mix_w, w_in, ret_norm_w, w_pool, pool_scale, w_out, norm_ffn_w,
           w_router, b_router, w_gu, b_gu, w_down, b_down, final_w):
    bsz, seq, d_model = x.shape
    n_tok = bsz * seq
    n_exp = w_router.shape[1]
    rw = ret_norm_w.shape[0]
    hd = rw // RET_HEADS
    ts = SEQ_TILE
    bf = jnp.bfloat16

    cos2, sin2, dmat, qdec, kdec, tile_decay = _retention_tables(seq, hd, ts)
    tri = jnp.arange(ts, dtype=jnp.int32)
    utri = (tri[:, None] < tri[None, :]).astype(bf)

    x1, h2p, eid, gate, rank, cnt = _mix_route(
        x, norm_mix_w[None], w_in.astype(bf), cos2, sin2, dmat, qdec, kdec, ret_norm_w[None],
        w_pool.astype(bf), pool_scale[None], w_out.astype(bf), norm_ffn_w[None],
        w_router.T.astype(bf), b_router[:, None], utri, tile_decay)

    tm = ROW_TILE
    n_tiles = (n_tok * TOP_K) // tm + n_exp
    counts = cnt[:, 0].astype(jnp.int32)
    tiles_per = (counts + tm - 1) // tm
    tile_end = jnp.cumsum(tiles_per)
    offs = (tile_end - tiles_per) * tm
    n_used = tile_end[-1:].astype(jnp.int32)
    tile_exp = jnp.minimum(
        jnp.searchsorted(tile_end, jnp.arange(n_tiles, dtype=jnp.int32), side="right"),
        n_exp - 1).astype(jnp.int32)
    pos = offs[eid] + rank
    rows_per_tok = d_model // PACK // LANES
    tok_src = jnp.broadcast_to(jnp.arange(n_tok, dtype=jnp.int32)[None] * rows_per_tok, pos.shape)
    row_src = jnp.zeros((n_tiles * tm,), jnp.int32).at[pos.reshape(-1)].set(tok_src.reshape(-1))
    row_gate = jnp.zeros((n_tiles * tm,), jnp.float32).at[pos.reshape(-1)].set(gate.reshape(-1))

    xs = _dispatch(row_src, n_used, h2p, n_tiles, d_model)
    y_tm = _experts(tile_exp, n_used, xs, w_gu, b_gu, w_down, b_down, row_gate[:, None])
    out = _combine(pos.reshape(-1), x1.reshape(n_tok, d_model), final_w[None], y_tm)
    return out.reshape(bsz, seq, d_model)


def kernel(x, norm_mix_w, w_in, ret_norm_w, w_pool, pool_scale, w_out, norm_ffn_w, w_router, b_router,
           w_gu, b_gu, w_down, b_down, norm_final_w):
    depth = norm_mix_w.shape[0]
    for l in range(depth):
        last = l == depth - 1
        assert last, "stacked layers need an un-normalised combine output"
        x = _layer(x, norm_mix_w[l], w_in[l], ret_norm_w[l], w_pool[l], pool_scale[l], w_out[l],
                   norm_ffn_w[l], w_router[l], b_router[l], w_gu[l], b_gu[l], w_down[l], b_down[l],
                   norm_final_w)
    return x
```

```python
import functools
import math

import jax
import jax.numpy as jnp
from jax import lax
from jax.experimental import pallas as pl
from jax.experimental.pallas import tpu as pltpu

CHUNK = 64
RET_HEADS = 4
POOL_WINDOWS = (2, 4, 8, 16)
ROPE_BASE = 10000.0
TOP_K = 4
SWIGLU_LIMIT = 7.0
SWIGLU_ALPHA = 1.702
EPS = 1e-6

LANES = 128
SUBLANES = 8
VMEM_LIMIT_BYTES = 56 * 1024 * 1024

SEQ_TILE = 256
ROW_TILE = 256
TOK_BLOCK = 1024
POOL_HALO = 16
PACK = 2
STRIP_ALIGN = 2
GATHER_UNROLL = 8


def _rms(x, w):
    return x * lax.rsqrt(jnp.mean(x * x, axis=-1, keepdims=True) + EPS) * w


def _bf16_bits_hi(x):
    return lax.bitcast_convert_type(x.astype(jnp.bfloat16).astype(jnp.float32), jnp.uint32)


def _pack_rows(x, out_ref, n_rows):
    half = x.shape[1] // PACK
    packed = (_bf16_bits_hi(x[:, :half]) >> 16) | _bf16_bits_hi(x[:, half:])
    rpt = half // LANES
    for si in range(rpt):
        out_ref[pl.ds(si, n_rows, stride=rpt), :] = packed[:, si * LANES:(si + 1) * LANES]


def _unpack_words(u):
    lo = lax.bitcast_convert_type(u << 16, jnp.float32)
    hi = lax.bitcast_convert_type(u & jnp.uint32(0xFFFF0000), jnp.float32)
    return lo, hi


def _unpack_rows(g_ref, n_rows, rpt):
    lo, hi = [], []
    for si in range(rpt):
        l, h = _unpack_words(g_ref[pl.ds(si, n_rows, stride=rpt), :])
        lo.append(l.astype(jnp.bfloat16))
        hi.append(h.astype(jnp.bfloat16))
    return jnp.concatenate(lo + hi, axis=-1)


def _mix_route_kernel(x_ref, nmw_ref, win_ref, cos_ref, sin_ref, dmat_ref, qdec_ref, kdec_ref,
                      rnw_ref, wpool_ref, pscale_ref, wout_ref, nfw_ref, wrt_ref, brt_ref, utri_ref,
                      x1_ref, h2p_ref, eid_ref, gate_ref, rank_ref, before_ref, cnt_ref,
                      state_ref, p1_ref, p2_ref, p4_ref, p8_ref, carry_ref,
                      *, tile_decay):
    b = pl.program_id(0)
    j = pl.program_id(1)
    ts = x_ref.shape[1]
    n_heads = RET_HEADS
    rw = rnw_ref.shape[1]
    hd = rw // n_heads
    pgw = wpool_ref.shape[1]
    n_exp = wrt_ref.shape[0]
    halo = POOL_HALO

    @pl.when(jnp.logical_and(b == 0, j == 0))
    def _():
        carry_ref[...] = jnp.zeros_like(carry_ref)

    @pl.when(j == 0)
    def _():
        state_ref[...] = jnp.zeros_like(state_ref)
        for r in (p1_ref, p2_ref, p4_ref, p8_ref):
            r[0:halo, :] = jnp.zeros((halo, r.shape[1]), jnp.float32)

    @pl.when(j > 0)
    def _():
        for r in (p1_ref, p2_ref, p4_ref, p8_ref):
            r[0:halo, :] = r[ts:ts + halo, :]

    x = x_ref[0]
    h = _rms(x, nmw_ref[...]).astype(jnp.bfloat16)
    z = jnp.dot(h, win_ref[...], preferred_element_type=jnp.float32)

    cos = cos_ref[...]
    sin = sin_ref[...]
    k_scale = hd ** -0.5

    ret_parts = []
    for hh in range(n_heads):
        q = z[:, hh * hd:(hh + 1) * hd]
        k = z[:, rw + hh * hd:rw + (hh + 1) * hd]
        v = z[:, 2 * rw + hh * hd:2 * rw + (hh + 1) * hd]
        g = z[:, 3 * rw + hh * hd:3 * rw + (hh + 1) * hd]
        q = q * cos + pltpu.roll(q, hd // 2, 1) * sin
        k = (k * cos + pltpu.roll(k, hd // 2, 1) * sin) * k_scale
        qb = q.astype(jnp.bfloat16)
        kb = k.astype(jnp.bfloat16)
        vb = v.astype(jnp.bfloat16)
        s = lax.dot_general(qb, kb, (((1,), (1,)), ((), ())), preferred_element_type=jnp.float32)
        pmat = (s * dmat_ref[hh]).astype(jnp.bfloat16)
        o = jnp.dot(pmat, vb, preferred_element_type=jnp.float32)
        state = state_ref[hh]
        qd = (q * qdec_ref[hh]).astype(jnp.bfloat16)
        o = o + jnp.dot(qd, state.astype(jnp.bfloat16), preferred_element_type=jnp.float32)
        kd = (k * kdec_ref[hh]).astype(jnp.bfloat16)
        u = lax.dot_general(kd, vb, (((0,), (0,)), ((), ())), preferred_element_type=jnp.float32)
        state_ref[hh] = state * tile_decay[hh] + u
        o = o * lax.rsqrt(jnp.mean(o * o, axis=-1, keepdims=True) + EPS)
        o = o * rnw_ref[:, hh * hd:(hh + 1) * hd] * (g * (1.0 / (1.0 + jnp.exp(-g))))
        ret_parts.append(o.astype(jnp.bfloat16))

    p = z[:, 4 * rw:]
    p1_ref[halo:halo + ts, :] = p
    s2 = p + p1_ref[halo - 1:halo - 1 + ts, :]
    p2_ref[halo:halo + ts, :] = s2[:, pgw:]
    s4 = s2[:, pgw:] + p2_ref[halo - 2:halo - 2 + ts, :]
    p4_ref[halo:halo + ts, :] = s4[:, pgw:]
    s8 = s4[:, pgw:] + p4_ref[halo - 4:halo - 4 + ts, :]
    p8_ref[halo:halo + ts, :] = s8[:, pgw:]
    s16 = s8[:, pgw:] + p8_ref[halo - 8:halo - 8 + ts, :]
    sums = (s2[:, :pgw], s4[:, :pgw], s8[:, :pgw], s16)
    tpos = (j * ts + lax.broadcasted_iota(jnp.int32, (ts, pgw), 0) + 1).astype(jnp.float32)
    pool_parts = []
    for gi, w in enumerate(POOL_WINDOWS):
        cnt = jnp.minimum(tpos, float(w))
        pooled = sums[gi] / cnt - p[:, gi * pgw:(gi + 1) * pgw]
        mixed = jnp.dot(pooled.astype(jnp.bfloat16), wpool_ref[gi], preferred_element_type=jnp.float32)
        mixed = mixed * pscale_ref[:, gi * pgw:(gi + 1) * pgw]
        pool_parts.append(mixed.astype(jnp.bfloat16))

    mix = jnp.concatenate(ret_parts + pool_parts, axis=-1)
    x1 = x + jnp.dot(mix, wout_ref[...], preferred_element_type=jnp.float32)
    x1_ref[0] = x1

    h2 = _rms(x1, nfw_ref[...])
    _pack_rows(h2, h2p_ref, ts)

    h2b = h2.astype(jnp.bfloat16)
    logits = lax.dot_general(wrt_ref[...], h2b, (((1,), (1,)), ((), ())),
                             preferred_element_type=jnp.float32) + brt_ref[...]
    eiota = lax.broadcasted_iota(jnp.int32, (n_exp, ts), 0)
    work = logits
    vals, ids, sels = [], [], []
    for _ in range(TOP_K):
        m = jnp.max(work, axis=0, keepdims=True)
        idx = jnp.min(jnp.where(work == m, eiota, n_exp), axis=0, keepdims=True)
        sel = eiota == idx
        vals.append(m)
        ids.append(idx)
        sels.append(sel)
        work = jnp.where(sel, -jnp.inf, work)
    ex = [jnp.exp(vk - vals[0]) for vk in vals]
    den = ex[0] + ex[1] + ex[2] + ex[3]
    gate_ref[...] = jnp.concatenate([e / den for e in ex], axis=0)
    eid_ref[...] = jnp.concatenate(ids, axis=0)

    member = jnp.zeros((n_exp, ts), jnp.float32)
    for sel in sels:
        member = member + sel.astype(jnp.float32)
    before_ref[0] = carry_ref[...]
    before = jnp.dot(member.astype(jnp.bfloat16), utri_ref[...], preferred_element_type=jnp.float32)
    before = before + carry_ref[:, 0:1]
    ranks = [jnp.sum(jnp.where(sel, before, 0.0), axis=0, keepdims=True) for sel in sels]
    rank_ref[...] = jnp.concatenate(ranks, axis=0).astype(jnp.int32)
    new_carry = carry_ref[...] + jnp.sum(member, axis=1, keepdims=True)
    carry_ref[...] = new_carry
    cnt_ref[...] = new_carry


def _mix_route(x, nmw, win_b, cos2, sin2, dmat, qdec, kdec, rnw, wpool_b, pscale, wout_b, nfw,
               wrt_b, brt, utri, tile_decay):
    bsz, seq, d_model = x.shape
    ts = SEQ_TILE
    n_tok = bsz * seq
    n_exp = wrt_b.shape[0]
    rw = rnw.shape[1]
    hd = rw // RET_HEADS
    pw = pscale.shape[1]
    pgw = pw // len(POOL_WINDOWS)
    nj = seq // ts
    rows_per_tok = d_model // PACK // LANES

    def const(shape):
        return pl.BlockSpec(shape, lambda b, j: (0,) * len(shape))

    in_specs = [
        pl.BlockSpec((1, ts, d_model), lambda b, j: (b, j, 0)),
        const(nmw.shape), const(win_b.shape),
        pl.BlockSpec((ts, hd), lambda b, j: (j, 0)),
        pl.BlockSpec((ts, hd), lambda b, j: (j, 0)),
        const(dmat.shape), const(qdec.shape), const(kdec.shape),
        const(rnw.shape), const(wpool_b.shape), const(pscale.shape), const(wout_b.shape),
        const(nfw.shape), const(wrt_b.shape), const(brt.shape), const(utri.shape),
    ]
    out_shape = [
        jax.ShapeDtypeStruct((bsz, seq, d_model), jnp.float32),
        jax.ShapeDtypeStruct((n_tok * rows_per_tok, LANES), jnp.uint32),
        jax.ShapeDtypeStruct((TOP_K, n_tok), jnp.int32),
        jax.ShapeDtypeStruct((TOP_K, n_tok), jnp.float32),
        jax.ShapeDtypeStruct((TOP_K, n_tok), jnp.int32),
        jax.ShapeDtypeStruct((n_tok // ts, n_exp, LANES), jnp.float32),
        jax.ShapeDtypeStruct((n_exp, LANES), jnp.float32),
    ]
    out_specs = [
        pl.BlockSpec((1, ts, d_model), lambda b, j: (b, j, 0)),
        pl.BlockSpec((ts * rows_per_tok, LANES), lambda b, j: (b * nj + j, 0)),
        pl.BlockSpec((TOP_K, ts), lambda b, j: (0, b * nj + j)),
        pl.BlockSpec((TOP_K, ts), lambda b, j: (0, b * nj + j)),
        pl.BlockSpec((TOP_K, ts), lambda b, j: (0, b * nj + j)),
        pl.BlockSpec((1, n_exp, LANES), lambda b, j: (b * nj + j, 0, 0)),
        pl.BlockSpec((n_exp, LANES), lambda b, j: (0, 0)),
    ]
    scratch = [
        pltpu.VMEM((RET_HEADS, hd, hd), jnp.float32),
        pltpu.VMEM((POOL_HALO + ts, pw), jnp.float32),
        pltpu.VMEM((POOL_HALO + ts, pw - pgw), jnp.float32),
        pltpu.VMEM((POOL_HALO + ts, pw - 2 * pgw), jnp.float32),
        pltpu.VMEM((POOL_HALO + ts, pw - 3 * pgw), jnp.float32),
        pltpu.VMEM((n_exp, LANES), jnp.float32),
    ]
    return pl.pallas_call(
        functools.partial(_mix_route_kernel, tile_decay=tile_decay),
        grid=(bsz, nj),
        in_specs=in_specs,
        out_specs=out_specs,
        out_shape=out_shape,
        scratch_shapes=scratch,
        compiler_params=pltpu.CompilerParams(
            dimension_semantics=("arbitrary", "arbitrary"),
            vmem_limit_bytes=VMEM_LIMIT_BYTES),
        name="mix_route",
    )(x, nmw, win_b, cos2, sin2, dmat, qdec, kdec, rnw, wpool_b, pscale, wout_b, nfw, wrt_b, brt, utri)


def _for_set_bits(n, max_bit, fn):
    for bit in range(max_bit, STRIP_ALIGN.bit_length() - 2, -1):
        size = 1 << bit
        offset = (n >> (bit + 1)) << (bit + 1)

        @pl.when((n & size) != 0)
        def _():
            fn(offset, size)


def _aligned_len(n):
    return n + (n & (STRIP_ALIGN - 1))


def _for_strips(nbe_ref, blk, n_exp, max_bit, fn):
    def body(e, carry):
        n = _aligned_len(nbe_ref[blk * n_exp + e])
        _for_set_bits(n, max_bit, functools.partial(fn, e))
        return carry
    lax.fori_loop(0, n_exp, body, 0)


def _dispatch_kernel(lidx_ref, nbe_ref, lb_ref, hrow_ref, padrow_ref, padlen_ref, nused_ref,
                     h2p_ref, xs_hbm, stage_ref, zero_ref, sem, zsem, *, nblk):
    b = pl.program_id(0)
    rpt = zero_ref.shape[0] // ROW_TILE
    tb = h2p_ref.shape[0] // rpt
    n_exp = padrow_ref.shape[0]
    n_tok = lidx_ref.shape[0] // TOP_K
    slot_rows = stage_ref.shape[0] // 2
    max_bit = tb.bit_length() - 1
    slot = b % 2

    def strip_copy(blk, sl, e, offset, size):
        src = pl.multiple_of(sl * slot_rows + (lb_ref[blk * n_exp + e] + offset) * rpt, SUBLANES)
        dst = pl.multiple_of((hrow_ref[blk * n_exp + e] + offset) * rpt, SUBLANES)
        return pltpu.make_async_copy(stage_ref.at[pl.ds(src, size * rpt), :],
                                     xs_hbm.at[pl.ds(dst, size * rpt), :], sem.at[sl])

    def start_strips(blk, sl):
        _for_strips(nbe_ref, blk, n_exp, max_bit,
                    lambda e, offset, size: strip_copy(blk, sl, e, offset, size).start())

    def wait_strips(blk, sl):
        _for_strips(nbe_ref, blk, n_exp, max_bit,
                    lambda e, offset, size: strip_copy(blk, sl, e, offset, size).wait())

    @pl.when(b >= 2)
    def _():
        wait_strips(b - 2, slot)

    def scatter_group(c, carry):
        for u in range(GATHER_UNROLL):
            t = c * GATHER_UNROLL + u
            row = h2p_ref[pl.ds(pl.multiple_of(t * rpt, rpt), rpt), :]
            for k in range(TOP_K):
                li = lidx_ref[k * n_tok + b * tb + t]
                dst = pl.multiple_of(slot * slot_rows + li * rpt, rpt)
                stage_ref[pl.ds(dst, rpt), :] = row
        return carry
    lax.fori_loop(0, tb // GATHER_UNROLL, scatter_group, 0)

    def zero_tail(e, carry):
        n = nbe_ref[b * n_exp + e]

        @pl.when((n & (STRIP_ALIGN - 1)) != 0)
        def _():
            dst = pl.multiple_of(slot * slot_rows + (lb_ref[b * n_exp + e] + n) * rpt, rpt)
            stage_ref[pl.ds(dst, rpt), :] = jnp.zeros((rpt, LANES), jnp.uint32)
        return carry
    lax.fori_loop(0, n_exp, zero_tail, 0)

    start_strips(b, slot)

    @pl.when(b == nblk - 1)
    def _():
        if nblk >= 2:
            wait_strips(b - 1, 1 - slot)
        wait_strips(b, slot)
        zero_ref[...] = jnp.zeros_like(zero_ref)
        pad_bit = ROW_TILE.bit_length() - 1

        def pad_copy(e, offset, size):
            dst = pl.multiple_of((padrow_ref[e] + offset) * rpt, SUBLANES)
            return pltpu.make_async_copy(zero_ref.at[pl.ds(0, size * rpt), :],
                                         xs_hbm.at[pl.ds(dst, size * rpt), :], zsem)

        def pad_start(e, carry):
            _for_set_bits(padlen_ref[e], pad_bit, lambda o, s: pad_copy(e, o, s).start())
            return carry

        def pad_wait(e, carry):
            _for_set_bits(padlen_ref[e], pad_bit, lambda o, s: pad_copy(e, o, s).wait())
            return carry
        lax.fori_loop(0, n_exp, pad_start, 0)

        def tail_copy(i):
            dst = pl.multiple_of(i * (ROW_TILE * rpt), SUBLANES)
            return pltpu.make_async_copy(zero_ref, xs_hbm.at[pl.ds(dst, ROW_TILE * rpt), :], zsem)

        def tail_start(i, carry):
            tail_copy(i).start()
            return carry

        def tail_wait(i, carry):
            tail_copy(i).wait()
            return carry
        n_tiles = xs_hbm.shape[0] // (ROW_TILE * rpt)
        lax.fori_loop(nused_ref[0], n_tiles, tail_start, 0)
        lax.fori_loop(0, n_exp, pad_wait, 0)
        lax.fori_loop(nused_ref[0], n_tiles, tail_wait, 0)


def _dispatch(lidx, nbe, lb, hrow, padrow, padlen, n_used, h2p, n_rows, d_model):
    tb = TOK_BLOCK
    rpt = d_model // PACK // LANES
    n_tok = h2p.shape[0] // rpt
    n_exp = padrow.shape[0]
    slot_rows = (tb * TOP_K + n_exp * (STRIP_ALIGN - 1)) * rpt
    grid_spec = pltpu.PrefetchScalarGridSpec(
        num_scalar_prefetch=7,
        grid=(n_tok // tb,),
        in_specs=[pl.BlockSpec((tb * rpt, LANES), lambda i, *_: (i, 0))],
        out_specs=pl.BlockSpec(memory_space=pl.ANY),
        scratch_shapes=[pltpu.VMEM((2 * slot_rows, LANES), jnp.uint32),
                        pltpu.VMEM((ROW_TILE * rpt, LANES), jnp.uint32),
                        pltpu.SemaphoreType.DMA((2,)),
                        pltpu.SemaphoreType.DMA],
    )
    return pl.pallas_call(
        functools.partial(_dispatch_kernel, nblk=n_tok // tb),
        grid_spec=grid_spec,
        out_shape=jax.ShapeDtypeStruct((n_rows * rpt, LANES), jnp.uint32),
        compiler_params=pltpu.CompilerParams(
            dimension_semantics=("arbitrary",),
            vmem_limit_bytes=VMEM_LIMIT_BYTES),
        name="dispatch",
    )(lidx, nbe, lb, hrow, padrow, padlen, n_used, h2p)


def _experts_kernel(texp_ref, nused_ref, xs_ref, wgu_ref, bgu_ref, wd_ref, bd_ref,
                    y_ref, wgu_b, wd_b):
    i = pl.program_id(0)
    d_ff = wd_ref.shape[1]
    d_model = wd_ref.shape[2]
    rpt = d_model // PACK // LANES
    tm = xs_ref.shape[0] // rpt
    prev = texp_ref[jnp.maximum(i - 1, 0)]
    new_expert = jnp.logical_or(i == 0, texp_ref[i] != prev)

    @pl.when(jnp.logical_and(new_expert, i < nused_ref[0]))
    def _():
        wgu_b[...] = wgu_ref[0].astype(jnp.bfloat16)
        wd_b[...] = wd_ref[0].astype(jnp.bfloat16)

    @pl.when(i < nused_ref[0])
    def _():
        x = _unpack_rows(xs_ref, tm, rpt)
        gu = jnp.dot(x, wgu_b[...], preferred_element_type=jnp.float32) + bgu_ref[0]
        x_glu = jnp.minimum(gu[:, :d_ff], SWIGLU_LIMIT)
        x_lin = jnp.clip(gu[:, d_ff:], -SWIGLU_LIMIT, SWIGLU_LIMIT)
        act = x_glu * (1.0 / (1.0 + jnp.exp(-SWIGLU_ALPHA * x_glu))) * (x_lin + 1.0)
        y = jnp.dot(act.astype(jnp.bfloat16), wd_b[...], preferred_element_type=jnp.float32) + bd_ref[0]
        _pack_rows(y, y_ref, tm)

    @pl.when(i >= nused_ref[0])
    def _():
        y_ref[...] = jnp.zeros_like(y_ref)


def _experts(tile_exp, n_used, xs, w_gu, b_gu, w_down, b_down):
    tm = ROW_TILE
    n_exp, d_model, d_gu = w_gu.shape
    d_ff = w_down.shape[1]
    rpt = d_model // PACK // LANES
    n_tiles = xs.shape[0] // (tm * rpt)
    grid_spec = pltpu.PrefetchScalarGridSpec(
        num_scalar_prefetch=2,
        grid=(n_tiles,),
        in_specs=[
            pl.BlockSpec((tm * rpt, LANES), lambda i, te, n: (jnp.minimum(i, n[0] - 1), 0)),
            pl.BlockSpec((1, d_model, d_gu), lambda i, te, n: (te[i], 0, 0)),
            pl.BlockSpec((1, 1, d_gu), lambda i, te, n: (te[i], 0, 0)),
            pl.BlockSpec((1, d_ff, d_model), lambda i, te, n: (te[i], 0, 0)),
            pl.BlockSpec((1, 1, d_model), lambda i, te, n: (te[i], 0, 0)),
        ],
        out_specs=pl.BlockSpec((tm * rpt, LANES), lambda i, te, n: (i, 0)),
        scratch_shapes=[pltpu.VMEM((d_model, d_gu), jnp.bfloat16),
                        pltpu.VMEM((d_ff, d_model), jnp.bfloat16)],
    )
    return pl.pallas_call(
        _experts_kernel,
        grid_spec=grid_spec,
        out_shape=jax.ShapeDtypeStruct(xs.shape, jnp.uint32),
        compiler_params=pltpu.CompilerParams(
            dimension_semantics=("arbitrary",),
            vmem_limit_bytes=VMEM_LIMIT_BYTES),
        name="experts",
    )(tile_exp, n_used, xs, w_gu, b_gu.reshape(n_exp, 1, d_gu), w_down,
      b_down.reshape(n_exp, 1, d_model))


def _combine_kernel(lidx_ref, gate_ref, nbe_ref, lb_ref, hrow_ref,
                    x1_ref, nw_ref, y_hbm, out_ref, stage_ref, alo_ref, ahi_ref, sem, *, nblk):
    b = pl.program_id(0)
    tb = x1_ref.shape[0]
    rpt = alo_ref.shape[0] // tb
    n_exp = nbe_ref.shape[0] // nblk
    n_tok = lidx_ref.shape[0] // TOP_K
    slot_rows = stage_ref.shape[0] // 2
    max_bit = tb.bit_length() - 1
    slot = b % 2

    def strip_copy(blk, sl, e, offset, size):
        src = pl.multiple_of((hrow_ref[blk * n_exp + e] + offset) * rpt, SUBLANES)
        dst = pl.multiple_of(sl * slot_rows + (lb_ref[blk * n_exp + e] + offset) * rpt, SUBLANES)
        return pltpu.make_async_copy(y_hbm.at[pl.ds(src, size * rpt), :],
                                     stage_ref.at[pl.ds(dst, size * rpt), :], sem.at[sl])

    def start_strips(blk, sl):
        _for_strips(nbe_ref, blk, n_exp, max_bit,
                    lambda e, offset, size: strip_copy(blk, sl, e, offset, size).start())

    @pl.when(b == 0)
    def _():
        start_strips(0, 0)

    @pl.when(b + 1 < nblk)
    def _():
        start_strips(b + 1, 1 - slot)

    _for_strips(nbe_ref, b, n_exp, max_bit,
                lambda e, offset, size: strip_copy(b, slot, e, offset, size).wait())

    def gather_group(c, carry):
        for u in range(GATHER_UNROLL):
            t = c * GATHER_UNROLL + u
            lo = jnp.zeros((rpt, LANES), jnp.float32)
            hi = jnp.zeros((rpt, LANES), jnp.float32)
            for k in range(TOP_K):
                li = lidx_ref[k * n_tok + b * tb + t]
                g = gate_ref[k * n_tok + b * tb + t]
                src = pl.multiple_of(slot * slot_rows + li * rpt, rpt)
                l, h = _unpack_words(stage_ref[pl.ds(src, rpt), :])
                lo = lo + g * l
                hi = hi + g * h
            dst = pl.multiple_of(t * rpt, rpt)
            alo_ref[pl.ds(dst, rpt), :] = lo
            ahi_ref[pl.ds(dst, rpt), :] = hi
        return carry
    lax.fori_loop(0, tb // GATHER_UNROLL, gather_group, 0)

    parts = [alo_ref[pl.ds(si, tb, stride=rpt), :] for si in range(rpt)]
    parts += [ahi_ref[pl.ds(si, tb, stride=rpt), :] for si in range(rpt)]
    x2 = x1_ref[...] + jnp.concatenate(parts, axis=-1)
    out_ref[...] = _rms(x2, nw_ref[...])


def _combine(lidx, gates, nbe, lb, hrow, x1, norm_w, y_rows):
    tb = TOK_BLOCK
    n_tok, d_model = x1.shape
    rpt = d_model // PACK // LANES
    n_exp = nbe.shape[0] // (n_tok // tb)
    slot_rows = (tb * TOP_K + n_exp * (STRIP_ALIGN - 1)) * rpt
    grid_spec = pltpu.PrefetchScalarGridSpec(
        num_scalar_prefetch=5,
        grid=(n_tok // tb,),
        in_specs=[
            pl.BlockSpec((tb, d_model), lambda i, *_: (i, 0)),
            pl.BlockSpec((1, d_model), lambda i, *_: (0, 0)),
            pl.BlockSpec(memory_space=pl.ANY),
        ],
        out_specs=pl.BlockSpec((tb, d_model), lambda i, *_: (i, 0)),
        scratch_shapes=[pltpu.VMEM((2 * slot_rows, LANES), jnp.uint32),
                        pltpu.VMEM((tb * rpt, LANES), jnp.float32),
                        pltpu.VMEM((tb * rpt, LANES), jnp.float32),
                        pltpu.SemaphoreType.DMA((2,))],
    )
    return pl.pallas_call(
        functools.partial(_combine_kernel, nblk=n_tok // tb),
        grid_spec=grid_spec,
        out_shape=jax.ShapeDtypeStruct((n_tok, d_model), jnp.float32),
        compiler_params=pltpu.CompilerParams(
            dimension_semantics=("arbitrary",),
            vmem_limit_bytes=VMEM_LIMIT_BYTES),
        name="combine",
    )(lidx, gates, nbe, lb, hrow, x1, norm_w, y_rows)


def _retention_tables(seq, hd, ts):
    half = hd // 2
    inv = ROPE_BASE ** (-jnp.arange(half, dtype=jnp.float32) / half)
    ang = jnp.arange(seq, dtype=jnp.float32)[:, None] * inv[None, :]
    cos, sin = jnp.cos(ang), jnp.sin(ang)
    cos2 = jnp.concatenate([cos, cos], axis=-1)
    sin2 = jnp.concatenate([-sin, sin], axis=-1)
    log_g = jnp.log1p(-jnp.exp2(-5.0 - jnp.arange(RET_HEADS, dtype=jnp.float32)))
    pos = jnp.arange(ts, dtype=jnp.float32)
    dist = jnp.abs(pos[:, None] - pos[None, :])
    chunk = jnp.arange(ts, dtype=jnp.int32) // CHUNK
    visible = chunk[None, :] <= chunk[:, None]
    dmat = jnp.where(visible[None], jnp.exp(log_g[:, None, None] * dist[None]), 0.0)
    qdec = jnp.exp(log_g[:, None] * (pos + 1.0)[None, :])
    kdec = jnp.exp(log_g[:, None] * (ts - 1.0 - pos)[None, :])
    qdec = jnp.broadcast_to(qdec[:, :, None], (RET_HEADS, ts, hd))
    kdec = jnp.broadcast_to(kdec[:, :, None], (RET_HEADS, ts, hd))
    tile_decay = tuple(math.exp(math.log1p(-2.0 ** (-5.0 - h)) * ts) for h in range(RET_HEADS))
    return cos2, sin2, dmat, qdec, kdec, tile_decay


def _routing_tables(eid, rank, before, counts, n_tok, n_exp):
    tm, tb, ts = ROW_TILE, TOK_BLOCK, SEQ_TILE
    cb = before[::tb // ts]
    nbe = jnp.concatenate([cb[1:], counts[None]], axis=0) - cb
    nal = nbe + (nbe & (STRIP_ALIGN - 1))
    region = jnp.sum(nal, axis=0)
    tiles_per = (region + tm - 1) // tm
    tile_end = jnp.cumsum(tiles_per)
    offs = (tile_end - tiles_per) * tm
    n_tiles = (n_tok * TOP_K + (n_tok // tb) * n_exp * (STRIP_ALIGN - 1)) // tm + n_exp
    tile_exp = jnp.minimum(
        jnp.sum(tile_end[None, :] <= jnp.arange(n_tiles, dtype=jnp.int32)[:, None], axis=1),
        n_exp - 1).astype(jnp.int32)
    hrow = offs[None, :] + jnp.cumsum(nal, axis=0) - nal
    lb = jnp.cumsum(nal, axis=1) - nal
    delta = jnp.repeat(lb - cb, tb, axis=0)
    onehot = eid[:, :, None] == jnp.arange(n_exp, dtype=jnp.int32)
    lidx = rank + jnp.sum(jnp.where(onehot, delta[None], 0), axis=-1)
    padrow = offs + region
    padlen = tiles_per * tm - region
    as_i32 = lambda a: a.reshape(-1).astype(jnp.int32)
    return (n_tiles, tile_exp, tile_end[-1:].astype(jnp.int32), as_i32(lidx), as_i32(nbe), as_i32(lb),
            as_i32(hrow), as_i32(padrow), as_i32(padlen))


def _layer(x, norm_mix_w, w_in, ret_norm_w, w_pool, pool_scale, w_out, norm_ffn_w,
           w_router, b_router, w_gu, b_gu, w_down, b_down, final_w):
    bsz, seq, d_model = x.shape
    n_tok = bsz * seq
    n_exp = w_router.shape[1]
    rw = ret_norm_w.shape[0]
    hd = rw // RET_HEADS
    ts = SEQ_TILE
    bf = jnp.bfloat16
    rpt = d_model // PACK // LANES
    assert d_model % (PACK * LANES) == 0 and (STRIP_ALIGN * rpt) % SUBLANES == 0
    assert seq % ts == 0 and ts % CHUNK == 0 and n_tok % TOK_BLOCK == 0 and TOK_BLOCK % ts == 0
    assert TOK_BLOCK & (TOK_BLOCK - 1) == 0 and ROW_TILE & (ROW_TILE - 1) == 0

    cos2, sin2, dmat, qdec, kdec, tile_decay = _retention_tables(seq, hd, ts)
    tri = jnp.arange(ts, dtype=jnp.int32)
    utri = (tri[:, None] < tri[None, :]).astype(bf)

    x1, h2p, eid, gate, rank, before, cnt = _mix_route(
        x, norm_mix_w[None], w_in.astype(bf), cos2, sin2, dmat, qdec, kdec, ret_norm_w[None],
        w_pool.astype(bf), pool_scale[None], w_out.astype(bf), norm_ffn_w[None],
        w_router.T.astype(bf), b_router[:, None], utri, tile_decay)

    (n_tiles, tile_exp, n_used, lidx, nbe, lb, hrow, padrow, padlen) = _routing_tables(
        eid, rank, before[:, :, 0].astype(jnp.int32), cnt[:, 0].astype(jnp.int32), n_tok, n_exp)

    xs = _dispatch(lidx, nbe, lb, hrow, padrow, padlen, n_used, h2p, n_tiles * ROW_TILE, d_model)
    y_rows = _experts(tile_exp, n_used, xs, w_gu, b_gu, w_down, b_down)
    out = _combine(lidx, gate.reshape(-1), nbe, lb, hrow, x1.reshape(n_tok, d_model), final_w[None], y_rows)
    return out.reshape(bsz, seq, d_model)


def kernel(x, norm_mix_w, w_in, ret_norm_w, w_pool, pool_scale, w_out, norm_ffn_w, w_router, b_router,
           w_gu, b_gu, w_down, b_down, norm_final_w):
    depth = norm_mix_w.shape[0]
    for l in range(depth):
        last = l == depth - 1
        assert last, "stacked layers need an un-normalised combine output"
        x = _layer(x, norm_mix_w[l], w_in[l], ret_norm_w[l], w_pool[l], pool_scale[l], w_out[l],
                   norm_ffn_w[l], w_router[l], b_router[l], w_gu[l], b_gu[l], w_down[l], b_down[l],
                   norm_final_w)
    return x
```

```python
import functools
import math

import jax
import jax.numpy as jnp
from jax import lax
from jax.experimental import pallas as pl
from jax.experimental.pallas import tpu as pltpu

CHUNK = 64
RET_HEADS = 4
POOL_WINDOWS = (2, 4, 8, 16)
ROPE_BASE = 10000.0
TOP_K = 4
SWIGLU_LIMIT = 7.0
SWIGLU_ALPHA = 1.702
EPS = 1e-6

LANES = 128
SUBLANES = 8
VMEM_LIMIT_BYTES = 56 * 1024 * 1024

SEQ_TILE = 256
ROW_TILE = 256
TOK_BLOCK = 1024
POOL_HALO = 16
PACK = 2
STRIP_ALIGN = 2
GATHER_UNROLL = 8


def _rms(x, w):
    return x * lax.rsqrt(jnp.mean(x * x, axis=-1, keepdims=True) + EPS) * w


def _bf16_bits_hi(x):
    return lax.bitcast_convert_type(x.astype(jnp.bfloat16).astype(jnp.float32), jnp.uint32)


def _pack_rows(x, out_ref, n_rows):
    half = x.shape[1] // PACK
    packed = (_bf16_bits_hi(x[:, :half]) >> 16) | _bf16_bits_hi(x[:, half:])
    rpt = half // LANES
    for si in range(rpt):
        out_ref[pl.ds(si, n_rows, stride=rpt), :] = packed[:, si * LANES:(si + 1) * LANES]


def _unpack_words(u):
    lo = lax.bitcast_convert_type(u << 16, jnp.float32)
    hi = lax.bitcast_convert_type(u & jnp.uint32(0xFFFF0000), jnp.float32)
    return lo, hi


def _unpack_rows(g_ref, n_rows, rpt):
    lo, hi = [], []
    for si in range(rpt):
        l, h = _unpack_words(g_ref[pl.ds(si, n_rows, stride=rpt), :])
        lo.append(l.astype(jnp.bfloat16))
        hi.append(h.astype(jnp.bfloat16))
    return jnp.concatenate(lo + hi, axis=-1)


def _mix_route_kernel(x_ref, nmw_ref, win_ref, cos_ref, sin_ref, dmat_ref, qdec_ref, kdec_ref,
                      rnw_ref, wpool_ref, pscale_ref, wout_ref, nfw_ref, wrt_ref, brt_ref, utri_ref,
                      x1_ref, h2p_ref, eid_ref, gate_ref, rank_ref, before_ref, cnt_ref,
                      state_ref, p1_ref, p2_ref, p4_ref, p8_ref, carry_ref,
                      *, tile_decay):
    b = pl.program_id(0)
    j = pl.program_id(1)
    ts = x_ref.shape[1]
    n_heads = RET_HEADS
    rw = rnw_ref.shape[1]
    hd = rw // n_heads
    pgw = wpool_ref.shape[1]
    n_exp = wrt_ref.shape[0]
    halo = POOL_HALO

    @pl.when(jnp.logical_and(b == 0, j == 0))
    def _():
        carry_ref[...] = jnp.zeros_like(carry_ref)

    @pl.when(j == 0)
    def _():
        state_ref[...] = jnp.zeros_like(state_ref)
        for r in (p1_ref, p2_ref, p4_ref, p8_ref):
            r[0:halo, :] = jnp.zeros((halo, r.shape[1]), jnp.float32)

    @pl.when(j > 0)
    def _():
        for r in (p1_ref, p2_ref, p4_ref, p8_ref):
            r[0:halo, :] = r[ts:ts + halo, :]

    x = x_ref[0]
    h = _rms(x, nmw_ref[...]).astype(jnp.bfloat16)
    z = jnp.dot(h, win_ref[...], preferred_element_type=jnp.float32)

    cos = cos_ref[...]
    sin = sin_ref[...]
    k_scale = hd ** -0.5

    ret_parts = []
    for hh in range(n_heads):
        q = z[:, hh * hd:(hh + 1) * hd]
        k = z[:, rw + hh * hd:rw + (hh + 1) * hd]
        v = z[:, 2 * rw + hh * hd:2 * rw + (hh + 1) * hd]
        g = z[:, 3 * rw + hh * hd:3 * rw + (hh + 1) * hd]
        q = q * cos + pltpu.roll(q, hd // 2, 1) * sin
        k = (k * cos + pltpu.roll(k, hd // 2, 1) * sin) * k_scale
        qb = q.astype(jnp.bfloat16)
        kb = k.astype(jnp.bfloat16)
        vb = v.astype(jnp.bfloat16)
        s = lax.dot_general(qb, kb, (((1,), (1,)), ((), ())), preferred_element_type=jnp.float32)
        pmat = (s * dmat_ref[hh]).astype(jnp.bfloat16)
        o = jnp.dot(pmat, vb, preferred_element_type=jnp.float32)
        state = state_ref[hh]
        qd = (q * qdec_ref[hh]).astype(jnp.bfloat16)
        o = o + jnp.dot(qd, state.astype(jnp.bfloat16), preferred_element_type=jnp.float32)
        kd = (k * kdec_ref[hh]).astype(jnp.bfloat16)
        u = lax.dot_general(kd, vb, (((0,), (0,)), ((), ())), preferred_element_type=jnp.float32)
        state_ref[hh] = state * tile_decay[hh] + u
        o = o * lax.rsqrt(jnp.mean(o * o, axis=-1, keepdims=True) + EPS)
        o = o * rnw_ref[:, hh * hd:(hh + 1) * hd] * (g * (1.0 / (1.0 + jnp.exp(-g))))
        ret_parts.append(o.astype(jnp.bfloat16))

    p = z[:, 4 * rw:]
    p1_ref[halo:halo + ts, :] = p
    s2 = p + p1_ref[halo - 1:halo - 1 + ts, :]
    p2_ref[halo:halo + ts, :] = s2[:, pgw:]
    s4 = s2[:, pgw:] + p2_ref[halo - 2:halo - 2 + ts, :]
    p4_ref[halo:halo + ts, :] = s4[:, pgw:]
    s8 = s4[:, pgw:] + p4_ref[halo - 4:halo - 4 + ts, :]
    p8_ref[halo:halo + ts, :] = s8[:, pgw:]
    s16 = s8[:, pgw:] + p8_ref[halo - 8:halo - 8 + ts, :]
    sums = (s2[:, :pgw], s4[:, :pgw], s8[:, :pgw], s16)
    tpos = (j * ts + lax.broadcasted_iota(jnp.int32, (ts, pgw), 0) + 1).astype(jnp.float32)
    pool_parts = []
    for gi, w in enumerate(POOL_WINDOWS):
        cnt = jnp.minimum(tpos, float(w))
        pooled = sums[gi] / cnt - p[:, gi * pgw:(gi + 1) * pgw]
        mixed = jnp.dot(pooled.astype(jnp.bfloat16), wpool_ref[gi], preferred_element_type=jnp.float32)
        mixed = mixed * pscale_ref[:, gi * pgw:(gi + 1) * pgw]
        pool_parts.append(mixed.astype(jnp.bfloat16))

    mix = jnp.concatenate(ret_parts + pool_parts, axis=-1)
    x1 = x + jnp.dot(mix, wout_ref[...], preferred_element_type=jnp.float32)
    x1_ref[0] = x1

    h2 = _rms(x1, nfw_ref[...])
    _pack_rows(h2, h2p_ref, ts)

    h2b = h2.astype(jnp.bfloat16)
    logits = lax.dot_general(wrt_ref[...], h2b, (((1,), (1,)), ((), ())),
                             preferred_element_type=jnp.float32) + brt_ref[...]
    eiota = lax.broadcasted_iota(jnp.int32, (n_exp, ts), 0)
    work = logits
    vals, ids, sels = [], [], []
    for _ in range(TOP_K):
        m = jnp.max(work, axis=0, keepdims=True)
        idx = jnp.min(jnp.where(work == m, eiota, n_exp), axis=0, keepdims=True)
        sel = eiota == idx
        vals.append(m)
        ids.append(idx)
        sels.append(sel)
        work = jnp.where(sel, -jnp.inf, work)
    ex = [jnp.exp(vk - vals[0]) for vk in vals]
    den = ex[0] + ex[1] + ex[2] + ex[3]
    gate_ref[...] = jnp.concatenate([e / den for e in ex], axis=0)
    eid_ref[...] = jnp.concatenate(ids, axis=0)

    member = jnp.zeros((n_exp, ts), jnp.float32)
    for sel in sels:
        member = member + sel.astype(jnp.float32)
    before_ref[0] = carry_ref[...]
    before = jnp.dot(member.astype(jnp.bfloat16), utri_ref[...], preferred_element_type=jnp.float32)
    before = before + carry_ref[:, 0:1]
    ranks = [jnp.sum(jnp.where(sel, before, 0.0), axis=0, keepdims=True) for sel in sels]
    rank_ref[...] = jnp.concatenate(ranks, axis=0).astype(jnp.int32)
    new_carry = carry_ref[...] + jnp.sum(member, axis=1, keepdims=True)
    carry_ref[...] = new_carry
    cnt_ref[...] = new_carry


def _mix_route(x, nmw, win_b, cos2, sin2, dmat, qdec, kdec, rnw, wpool_b, pscale, wout_b, nfw,
               wrt_b, brt, utri, tile_decay):
    bsz, seq, d_model = x.shape
    ts = SEQ_TILE
    n_tok = bsz * seq
    n_exp = wrt_b.shape[0]
    rw = rnw.shape[1]
    hd = rw // RET_HEADS
    pw = pscale.shape[1]
    pgw = pw // len(POOL_WINDOWS)
    nj = seq // ts
    rows_per_tok = d_model // PACK // LANES

    def const(shape):
        return pl.BlockSpec(shape, lambda b, j: (0,) * len(shape))

    in_specs = [
        pl.BlockSpec((1, ts, d_model), lambda b, j: (b, j, 0)),
        const(nmw.shape), const(win_b.shape),
        pl.BlockSpec((ts, hd), lambda b, j: (j, 0)),
        pl.BlockSpec((ts, hd), lambda b, j: (j, 0)),
        const(dmat.shape), const(qdec.shape), const(kdec.shape),
        const(rnw.shape), const(wpool_b.shape), const(pscale.shape), const(wout_b.shape),
        const(nfw.shape), const(wrt_b.shape), const(brt.shape), const(utri.shape),
    ]
    out_shape = [
        jax.ShapeDtypeStruct((bsz, seq, d_model), jnp.float32),
        jax.ShapeDtypeStruct((n_tok * rows_per_tok, LANES), jnp.uint32),
        jax.ShapeDtypeStruct((TOP_K, n_tok), jnp.int32),
        jax.ShapeDtypeStruct((TOP_K, n_tok), jnp.float32),
        jax.ShapeDtypeStruct((TOP_K, n_tok), jnp.int32),
        jax.ShapeDtypeStruct((n_tok // ts, n_exp, LANES), jnp.float32),
        jax.ShapeDtypeStruct((n_exp, LANES), jnp.float32),
    ]
    out_specs = [
        pl.BlockSpec((1, ts, d_model), lambda b, j: (b, j, 0)),
        pl.BlockSpec((ts * rows_per_tok, LANES), lambda b, j: (b * nj + j, 0)),
        pl.BlockSpec((TOP_K, ts), lambda b, j: (0, b * nj + j)),
        pl.BlockSpec((TOP_K, ts), lambda b, j: (0, b * nj + j)),
        pl.BlockSpec((TOP_K, ts), lambda b, j: (0, b * nj + j)),
        pl.BlockSpec((1, n_exp, LANES), lambda b, j: (b * nj + j, 0, 0)),
        pl.BlockSpec((n_exp, LANES), lambda b, j: (0, 0)),
    ]
    scratch = [
        pltpu.VMEM((RET_HEADS, hd, hd), jnp.float32),
        pltpu.VMEM((POOL_HALO + ts, pw), jnp.float32),
        pltpu.VMEM((POOL_HALO + ts, pw - pgw), jnp.float32),
        pltpu.VMEM((POOL_HALO + ts, pw - 2 * pgw), jnp.float32),
        pltpu.VMEM((POOL_HALO + ts, pw - 3 * pgw), jnp.float32),
        pltpu.VMEM((n_exp, LANES), jnp.float32),
    ]
    return pl.pallas_call(
        functools.partial(_mix_route_kernel, tile_decay=tile_decay),
        grid=(bsz, nj),
        in_specs=in_specs,
        out_specs=out_specs,
        out_shape=out_shape,
        scratch_shapes=scratch,
        compiler_params=pltpu.CompilerParams(
            dimension_semantics=("arbitrary", "arbitrary"),
            vmem_limit_bytes=VMEM_LIMIT_BYTES),
        name="mix_route",
    )(x, nmw, win_b, cos2, sin2, dmat, qdec, kdec, rnw, wpool_b, pscale, wout_b, nfw, wrt_b, brt, utri)


def _for_set_bits(n, max_bit, fn):
    for bit in range(max_bit, STRIP_ALIGN.bit_length() - 2, -1):
        size = 1 << bit
        offset = (n >> (bit + 1)) << (bit + 1)

        @pl.when((n & size) != 0)
        def _():
            fn(offset, size)


def _aligned_len(n):
    return n + (n & (STRIP_ALIGN - 1))


def _for_strips(nbe_ref, blk, n_exp, max_bit, fn):
    def body(e, carry):
        n = _aligned_len(nbe_ref[blk * n_exp + e])
        _for_set_bits(n, max_bit, functools.partial(fn, e))
        return carry
    lax.fori_loop(0, n_exp, body, 0)


def _dispatch_kernel(lidx_ref, nbe_ref, lb_ref, hrow_ref, padrow_ref, padlen_ref, nused_ref,
                     h2p_ref, xs_hbm, stage_ref, zero_ref, sem, zsem, *, nblk):
    b = pl.program_id(0)
    rpt = zero_ref.shape[0] // ROW_TILE
    tb = h2p_ref.shape[0] // rpt
    n_exp = padrow_ref.shape[0]
    n_tok = lidx_ref.shape[0] // TOP_K
    slot_rows = stage_ref.shape[0] // 2
    max_bit = tb.bit_length() - 1
    slot = b % 2

    def strip_copy(blk, sl, e, offset, size):
        src = pl.multiple_of(sl * slot_rows + (lb_ref[blk * n_exp + e] + offset) * rpt, SUBLANES)
        dst = pl.multiple_of((hrow_ref[blk * n_exp + e] + offset) * rpt, SUBLANES)
        return pltpu.make_async_copy(stage_ref.at[pl.ds(src, size * rpt), :],
                                     xs_hbm.at[pl.ds(dst, size * rpt), :], sem.at[sl])

    def start_strips(blk, sl):
        _for_strips(nbe_ref, blk, n_exp, max_bit,
                    lambda e, offset, size: strip_copy(blk, sl, e, offset, size).start())

    def wait_strips(blk, sl):
        _for_strips(nbe_ref, blk, n_exp, max_bit,
                    lambda e, offset, size: strip_copy(blk, sl, e, offset, size).wait())

    @pl.when(b >= 2)
    def _():
        wait_strips(b - 2, slot)

    def scatter_group(c, carry):
        for u in range(GATHER_UNROLL):
            t = c * GATHER_UNROLL + u
            row = h2p_ref[pl.ds(pl.multiple_of(t * rpt, rpt), rpt), :]
            for k in range(TOP_K):
                li = lidx_ref[k * n_tok + b * tb + t]
                dst = pl.multiple_of(slot * slot_rows + li * rpt, rpt)
                stage_ref[pl.ds(dst, rpt), :] = row
        return carry
    lax.fori_loop(0, tb // GATHER_UNROLL, scatter_group, 0)

    def zero_tail(e, carry):
        n = nbe_ref[b * n_exp + e]

        @pl.when((n & (STRIP_ALIGN - 1)) != 0)
        def _():
            dst = pl.multiple_of(slot * slot_rows + (lb_ref[b * n_exp + e] + n) * rpt, rpt)
            stage_ref[pl.ds(dst, rpt), :] = jnp.zeros((rpt, LANES), jnp.uint32)
        return carry
    lax.fori_loop(0, n_exp, zero_tail, 0)

    start_strips(b, slot)

    @pl.when(b == nblk - 1)
    def _():
        if nblk >= 2:
            wait_strips(b - 1, 1 - slot)
        wait_strips(b, slot)
        zero_ref[...] = jnp.zeros_like(zero_ref)
        pad_bit = ROW_TILE.bit_length() - 1

        def pad_copy(e, offset, size):
            dst = pl.multiple_of((padrow_ref[e] + offset) * rpt, SUBLANES)
            return pltpu.make_async_copy(zero_ref.at[pl.ds(0, size * rpt), :],
                                         xs_hbm.at[pl.ds(dst, size * rpt), :], zsem)

        def pad_start(e, carry):
            _for_set_bits(padlen_ref[e], pad_bit, lambda o, s: pad_copy(e, o, s).start())
            return carry

        def pad_wait(e, carry):
            _for_set_bits(padlen_ref[e], pad_bit, lambda o, s: pad_copy(e, o, s).wait())
            return carry
        lax.fori_loop(0, n_exp, pad_start, 0)

        def tail_copy(i):
            dst = pl.multiple_of(i * (ROW_TILE * rpt), SUBLANES)
            return pltpu.make_async_copy(zero_ref, xs_hbm.at[pl.ds(dst, ROW_TILE * rpt), :], zsem)

        def tail_start(i, carry):
            tail_copy(i).start()
            return carry

        def tail_wait(i, carry):
            tail_copy(i).wait()
            return carry
        n_tiles = xs_hbm.shape[0] // (ROW_TILE * rpt)
        lax.fori_loop(nused_ref[0], n_tiles, tail_start, 0)
        lax.fori_loop(0, n_exp, pad_wait, 0)
        lax.fori_loop(nused_ref[0], n_tiles, tail_wait, 0)


def _dispatch(lidx, nbe, lb, hrow, padrow, padlen, n_used, h2p, n_rows, d_model):
    tb = TOK_BLOCK
    rpt = d_model // PACK // LANES
    n_tok = h2p.shape[0] // rpt
    n_exp = padrow.shape[0]
    slot_rows = (tb * TOP_K + n_exp * (STRIP_ALIGN - 1)) * rpt
    grid_spec = pltpu.PrefetchScalarGridSpec(
        num_scalar_prefetch=7,
        grid=(n_tok // tb,),
        in_specs=[pl.BlockSpec((tb * rpt, LANES), lambda i, *_: (i, 0))],
        out_specs=pl.BlockSpec(memory_space=pl.ANY),
        scratch_shapes=[pltpu.VMEM((2 * slot_rows, LANES), jnp.uint32),
                        pltpu.VMEM((ROW_TILE * rpt, LANES), jnp.uint32),
                        pltpu.SemaphoreType.DMA((2,)),
                        pltpu.SemaphoreType.DMA],
    )
    return pl.pallas_call(
        functools.partial(_dispatch_kernel, nblk=n_tok // tb),
        grid_spec=grid_spec,
        out_shape=jax.ShapeDtypeStruct((n_rows * rpt, LANES), jnp.uint32),
        compiler_params=pltpu.CompilerParams(
            dimension_semantics=("arbitrary",),
            vmem_limit_bytes=VMEM_LIMIT_BYTES),
        name="dispatch",
    )(lidx, nbe, lb, hrow, padrow, padlen, n_used, h2p)


def _experts_kernel(tstart_ref, tcount_ref, wgu_ref, bgu_ref, wd_ref, bd_ref, xs_hbm,
                    y_hbm, wgu_b, wd_b, xbuf, ybuf, xsem, ysem, *, n_exp):
    e = pl.program_id(0)
    d_ff = wd_ref.shape[1]
    d_model = wd_ref.shape[2]
    rpt = d_model // PACK // LANES
    tile_rows = xbuf.shape[1]
    tm = tile_rows // rpt
    n_tiles = y_hbm.shape[0] // tile_rows
    t0 = tstart_ref[e]
    nt = tcount_ref[e]

    def hbm_tile(ref, tile):
        return ref.at[pl.ds(pl.multiple_of(tile * tile_rows, SUBLANES), tile_rows), :]

    def x_copy(j, sl):
        return pltpu.make_async_copy(hbm_tile(xs_hbm, t0 + j), xbuf.at[sl], xsem.at[sl])

    def y_copy(tile, sl):
        return pltpu.make_async_copy(ybuf.at[sl], hbm_tile(y_hbm, tile), ysem.at[sl])

    @pl.when(nt > 0)
    def _():
        x_copy(0, 0).start()
        wgu_b[...] = wgu_ref[0].astype(jnp.bfloat16)
        wd_b[...] = wd_ref[0].astype(jnp.bfloat16)

        def tile_step(j, carry):
            sl = j % 2

            @pl.when(j + 1 < nt)
            def _():
                x_copy(j + 1, 1 - sl).start()
            x_copy(j, sl).wait()

            @pl.when(j >= 2)
            def _():
                y_copy(t0 + j - 2, sl).wait()
            x = _unpack_rows(xbuf.at[sl], tm, rpt)
            gu = jnp.dot(x, wgu_b[...], preferred_element_type=jnp.float32) + bgu_ref[0]
            x_glu = jnp.minimum(gu[:, :d_ff], SWIGLU_LIMIT)
            x_lin = jnp.clip(gu[:, d_ff:], -SWIGLU_LIMIT, SWIGLU_LIMIT)
            act = x_glu * (1.0 / (1.0 + jnp.exp(-SWIGLU_ALPHA * x_glu))) * (x_lin + 1.0)
            y = jnp.dot(act.astype(jnp.bfloat16), wd_b[...], preferred_element_type=jnp.float32) + bd_ref[0]
            _pack_rows(y, ybuf.at[sl], tm)
            y_copy(t0 + j, sl).start()
            return carry
        lax.fori_loop(0, nt, tile_step, 0)

        @pl.when(nt >= 2)
        def _():
            y_copy(t0 + nt - 2, nt % 2).wait()
        y_copy(t0 + nt - 1, (nt - 1) % 2).wait()

    @pl.when(e == n_exp - 1)
    def _():
        ybuf[0] = jnp.zeros(ybuf.shape[1:], ybuf.dtype)

        def tail_start(i, carry):
            y_copy(i, 0).start()
            return carry

        def tail_wait(i, carry):
            y_copy(i, 0).wait()
            return carry
        lax.fori_loop(t0 + nt, n_tiles, tail_start, 0)
        lax.fori_loop(t0 + nt, n_tiles, tail_wait, 0)


def _experts(tile_start, tile_count, xs, w_gu, b_gu, w_down, b_down):
    tm = ROW_TILE
    n_exp, d_model, d_gu = w_gu.shape
    d_ff = w_down.shape[1]
    rpt = d_model // PACK // LANES
    grid_spec = pltpu.PrefetchScalarGridSpec(
        num_scalar_prefetch=2,
        grid=(n_exp,),
        in_specs=[
            pl.BlockSpec((1, d_model, d_gu), lambda e, *_: (e, 0, 0)),
            pl.BlockSpec((1, 1, d_gu), lambda e, *_: (e, 0, 0)),
            pl.BlockSpec((1, d_ff, d_model), lambda e, *_: (e, 0, 0)),
            pl.BlockSpec((1, 1, d_model), lambda e, *_: (e, 0, 0)),
            pl.BlockSpec(memory_space=pl.ANY),
        ],
        out_specs=pl.BlockSpec(memory_space=pl.ANY),
        scratch_shapes=[pltpu.VMEM((d_model, d_gu), jnp.bfloat16),
                        pltpu.VMEM((d_ff, d_model), jnp.bfloat16),
                        pltpu.VMEM((2, tm * rpt, LANES), jnp.uint32),
                        pltpu.VMEM((2, tm * rpt, LANES), jnp.uint32),
                        pltpu.SemaphoreType.DMA((2,)),
                        pltpu.SemaphoreType.DMA((2,))],
    )
    return pl.pallas_call(
        functools.partial(_experts_kernel, n_exp=n_exp),
        grid_spec=grid_spec,
        out_shape=jax.ShapeDtypeStruct(xs.shape, jnp.uint32),
        compiler_params=pltpu.CompilerParams(
            dimension_semantics=("arbitrary",),
            vmem_limit_bytes=VMEM_LIMIT_BYTES),
        name="experts",
    )(tile_start, tile_count, w_gu, b_gu.reshape(n_exp, 1, d_gu), w_down,
      b_down.reshape(n_exp, 1, d_model), xs)


def _combine_kernel(lidx_ref, gate_ref, nbe_ref, lb_ref, hrow_ref,
                    x1_ref, nw_ref, y_hbm, out_ref, stage_ref, alo_ref, ahi_ref, sem, *, nblk):
    b = pl.program_id(0)
    tb = x1_ref.shape[0]
    rpt = alo_ref.shape[0] // tb
    n_exp = nbe_ref.shape[0] // nblk
    n_tok = lidx_ref.shape[0] // TOP_K
    slot_rows = stage_ref.shape[0] // 2
    max_bit = tb.bit_length() - 1
    slot = b % 2

    def strip_copy(blk, sl, e, offset, size):
        src = pl.multiple_of((hrow_ref[blk * n_exp + e] + offset) * rpt, SUBLANES)
        dst = pl.multiple_of(sl * slot_rows + (lb_ref[blk * n_exp + e] + offset) * rpt, SUBLANES)
        return pltpu.make_async_copy(y_hbm.at[pl.ds(src, size * rpt), :],
                                     stage_ref.at[pl.ds(dst, size * rpt), :], sem.at[sl])

    def start_strips(blk, sl):
        _for_strips(nbe_ref, blk, n_exp, max_bit,
                    lambda e, offset, size: strip_copy(blk, sl, e, offset, size).start())

    @pl.when(b == 0)
    def _():
        start_strips(0, 0)

    @pl.when(b + 1 < nblk)
    def _():
        start_strips(b + 1, 1 - slot)

    _for_strips(nbe_ref, b, n_exp, max_bit,
                lambda e, offset, size: strip_copy(b, slot, e, offset, size).wait())

    def gather_group(c, carry):
        for u in range(GATHER_UNROLL):
            t = c * GATHER_UNROLL + u
            lo = jnp.zeros((rpt, LANES), jnp.float32)
            hi = jnp.zeros((rpt, LANES), jnp.float32)
            for k in range(TOP_K):
                li = lidx_ref[k * n_tok + b * tb + t]
                g = gate_ref[k * n_tok + b * tb + t]
                src = pl.multiple_of(slot * slot_rows + li * rpt, rpt)
                l, h = _unpack_words(stage_ref[pl.ds(src, rpt), :])
                lo = lo + g * l
                hi = hi + g * h
            dst = pl.multiple_of(t * rpt, rpt)
            alo_ref[pl.ds(dst, rpt), :] = lo
            ahi_ref[pl.ds(dst, rpt), :] = hi
        return carry
    lax.fori_loop(0, tb // GATHER_UNROLL, gather_group, 0)

    parts = [alo_ref[pl.ds(si, tb, stride=rpt), :] for si in range(rpt)]
    parts += [ahi_ref[pl.ds(si, tb, stride=rpt), :] for si in range(rpt)]
    x2 = x1_ref[...] + jnp.concatenate(parts, axis=-1)
    out_ref[...] = _rms(x2, nw_ref[...])


def _combine(lidx, gates, nbe, lb, hrow, x1, norm_w, y_rows):
    tb = TOK_BLOCK
    n_tok, d_model = x1.shape
    rpt = d_model // PACK // LANES
    n_exp = nbe.shape[0] // (n_tok // tb)
    slot_rows = (tb * TOP_K + n_exp * (STRIP_ALIGN - 1)) * rpt
    grid_spec = pltpu.PrefetchScalarGridSpec(
        num_scalar_prefetch=5,
        grid=(n_tok // tb,),
        in_specs=[
            pl.BlockSpec((tb, d_model), lambda i, *_: (i, 0)),
            pl.BlockSpec((1, d_model), lambda i, *_: (0, 0)),
            pl.BlockSpec(memory_space=pl.ANY),
        ],
        out_specs=pl.BlockSpec((tb, d_model), lambda i, *_: (i, 0)),
        scratch_shapes=[pltpu.VMEM((2 * slot_rows, LANES), jnp.uint32),
                        pltpu.VMEM((tb * rpt, LANES), jnp.float32),
                        pltpu.VMEM((tb * rpt, LANES), jnp.float32),
                        pltpu.SemaphoreType.DMA((2,))],
    )
    return pl.pallas_call(
        functools.partial(_combine_kernel, nblk=n_tok // tb),
        grid_spec=grid_spec,
        out_shape=jax.ShapeDtypeStruct((n_tok, d_model), jnp.float32),
        compiler_params=pltpu.CompilerParams(
            dimension_semantics=("arbitrary",),
            vmem_limit_bytes=VMEM_LIMIT_BYTES),
        name="combine",
    )(lidx, gates, nbe, lb, hrow, x1, norm_w, y_rows)


def _retention_tables(seq, hd, ts):
    half = hd // 2
    inv = ROPE_BASE ** (-jnp.arange(half, dtype=jnp.float32) / half)
    ang = jnp.arange(seq, dtype=jnp.float32)[:, None] * inv[None, :]
    cos, sin = jnp.cos(ang), jnp.sin(ang)
    cos2 = jnp.concatenate([cos, cos], axis=-1)
    sin2 = jnp.concatenate([-sin, sin], axis=-1)
    log_g = jnp.log1p(-jnp.exp2(-5.0 - jnp.arange(RET_HEADS, dtype=jnp.float32)))
    pos = jnp.arange(ts, dtype=jnp.float32)
    dist = jnp.abs(pos[:, None] - pos[None, :])
    chunk = jnp.arange(ts, dtype=jnp.int32) // CHUNK
    visible = chunk[None, :] <= chunk[:, None]
    dmat = jnp.where(visible[None], jnp.exp(log_g[:, None, None] * dist[None]), 0.0)
    qdec = jnp.exp(log_g[:, None] * (pos + 1.0)[None, :])
    kdec = jnp.exp(log_g[:, None] * (ts - 1.0 - pos)[None, :])
    qdec = jnp.broadcast_to(qdec[:, :, None], (RET_HEADS, ts, hd))
    kdec = jnp.broadcast_to(kdec[:, :, None], (RET_HEADS, ts, hd))
    tile_decay = tuple(math.exp(math.log1p(-2.0 ** (-5.0 - h)) * ts) for h in range(RET_HEADS))
    return cos2, sin2, dmat, qdec, kdec, tile_decay


def _routing_tables(eid, rank, before, counts, n_tok, n_exp):
    tm, tb, ts = ROW_TILE, TOK_BLOCK, SEQ_TILE
    cb = before[::tb // ts]
    nbe = jnp.concatenate([cb[1:], counts[None]], axis=0) - cb
    nal = nbe + (nbe & (STRIP_ALIGN - 1))
    region = jnp.sum(nal, axis=0)
    tiles_per = (region + tm - 1) // tm
    tile_end = jnp.cumsum(tiles_per)
    offs = (tile_end - tiles_per) * tm
    n_tiles = (n_tok * TOP_K + (n_tok // tb) * n_exp * (STRIP_ALIGN - 1)) // tm + n_exp
    hrow = offs[None, :] + jnp.cumsum(nal, axis=0) - nal
    lb = jnp.cumsum(nal, axis=1) - nal
    delta = jnp.repeat(lb - cb, tb, axis=0)
    onehot = eid[:, :, None] == jnp.arange(n_exp, dtype=jnp.int32)
    lidx = rank + jnp.sum(jnp.where(onehot, delta[None], 0), axis=-1)
    padrow = offs + region
    padlen = tiles_per * tm - region
    as_i32 = lambda a: a.reshape(-1).astype(jnp.int32)
    return (n_tiles, as_i32(tile_end - tiles_per), as_i32(tiles_per), as_i32(tile_end[-1:]),
            as_i32(lidx), as_i32(nbe), as_i32(lb), as_i32(hrow), as_i32(padrow), as_i32(padlen))


def _layer(x, norm_mix_w, w_in, ret_norm_w, w_pool, pool_scale, w_out, norm_ffn_w,
           w_router, b_router, w_gu, b_gu, w_down, b_down, final_w):
    bsz, seq, d_model = x.shape
    n_tok = bsz * seq
    n_exp = w_router.shape[1]
    rw = ret_norm_w.shape[0]
    hd = rw // RET_HEADS
    ts = SEQ_TILE
    bf = jnp.bfloat16
    rpt = d_model // PACK // LANES
    assert d_model % (PACK * LANES) == 0 and (STRIP_ALIGN * rpt) % SUBLANES == 0
    assert seq % ts == 0 and ts % CHUNK == 0 and n_tok % TOK_BLOCK == 0 and TOK_BLOCK % ts == 0
    assert TOK_BLOCK & (TOK_BLOCK - 1) == 0 and ROW_TILE & (ROW_TILE - 1) == 0

    cos2, sin2, dmat, qdec, kdec, tile_decay = _retention_tables(seq, hd, ts)
    tri = jnp.arange(ts, dtype=jnp.int32)
    utri = (tri[:, None] < tri[None, :]).astype(bf)

    x1, h2p, eid, gate, rank, before, cnt = _mix_route(
        x, norm_mix_w[None], w_in.astype(bf), cos2, sin2, dmat, qdec, kdec, ret_norm_w[None],
        w_pool.astype(bf), pool_scale[None], w_out.astype(bf), norm_ffn_w[None],
        w_router.T.astype(bf), b_router[:, None], utri, tile_decay)

    (n_tiles, tile_start, tile_count, n_used, lidx, nbe, lb, hrow, padrow, padlen) = _routing_tables(
        eid, rank, before[:, :, 0].astype(jnp.int32), cnt[:, 0].astype(jnp.int32), n_tok, n_exp)

    xs = _dispatch(lidx, nbe, lb, hrow, padrow, padlen, n_used, h2p, n_tiles * ROW_TILE, d_model)
    y_rows = _experts(tile_start, tile_count, xs, w_gu, b_gu, w_down, b_down)
    out = _combine(lidx, gate.reshape(-1), nbe, lb, hrow, x1.reshape(n_tok, d_model), final_w[None], y_rows)
    return out.reshape(bsz, seq, d_model)


def kernel(x, norm_mix_w, w_in, ret_norm_w, w_pool, pool_scale, w_out, norm_ffn_w, w_router, b_router,
           w_gu, b_gu, w_down, b_down, norm_final_w):
    depth = norm_mix_w.shape[0]
    for l in range(depth):
        last = l == depth - 1
        assert last, "stacked layers need an un-normalised combine output"
        x = _layer(x, norm_mix_w[l], w_in[l], ret_norm_w[l], w_pool[l], pool_scale[l], w_out[l],
                   norm_ffn_w[l], w_router[l], b_router[l], w_gu[l], b_gu[l], w_down[l], b_down[l],
                   norm_final_w)
    return x
```

```python
import functools
import math

import jax
import jax.numpy as jnp
from jax import lax
from jax.experimental import pallas as pl
from jax.experimental.pallas import tpu as pltpu

CHUNK = 64
RET_HEADS = 4
POOL_WINDOWS = (2, 4, 8, 16)
ROPE_BASE = 10000.0
TOP_K = 4
SWIGLU_LIMIT = 7.0
SWIGLU_ALPHA = 1.702
EPS = 1e-6

LANES = 128
SUBLANES = 8
VMEM_LIMIT_BYTES = 56 * 1024 * 1024

SEQ_TILE = 256
ROW_TILE = 512
TOK_BLOCK = 1024
POOL_HALO = 16
PACK = 2
STRIP_ALIGN = 2
GATHER_UNROLL = 8


def _rms(x, w):
    return x * lax.rsqrt(jnp.mean(x * x, axis=-1, keepdims=True) + EPS) * w


def _bf16_bits_hi(x):
    return lax.bitcast_convert_type(x.astype(jnp.bfloat16).astype(jnp.float32), jnp.uint32)


def _pack_rows(x, out_ref, n_rows):
    half = x.shape[1] // PACK
    packed = (_bf16_bits_hi(x[:, :half]) >> 16) | _bf16_bits_hi(x[:, half:])
    rpt = half // LANES
    for si in range(rpt):
        out_ref[pl.ds(si, n_rows, stride=rpt), :] = packed[:, si * LANES:(si + 1) * LANES]


def _unpack_words(u):
    lo = lax.bitcast_convert_type(u << 16, jnp.float32)
    hi = lax.bitcast_convert_type(u & jnp.uint32(0xFFFF0000), jnp.float32)
    return lo, hi


def _unpack_rows(g_ref, n_rows, rpt):
    lo, hi = [], []
    for si in range(rpt):
        l, h = _unpack_words(g_ref[pl.ds(si, n_rows, stride=rpt), :])
        lo.append(l.astype(jnp.bfloat16))
        hi.append(h.astype(jnp.bfloat16))
    return jnp.concatenate(lo + hi, axis=-1)


def _mix_route_kernel(x_ref, nmw_ref, win_ref, cos_ref, sin_ref, dmat_ref, qdec_ref, kdec_ref,
                      rnw_ref, wpool_ref, pscale_ref, wout_ref, nfw_ref, wrt_ref, brt_ref, utri_ref,
                      x1_ref, h2p_ref, eid_ref, gate_ref, rank_ref, before_ref, cnt_ref,
                      state_ref, p1_ref, p2_ref, p4_ref, p8_ref, carry_ref,
                      *, tile_decay):
    b = pl.program_id(0)
    j = pl.program_id(1)
    ts = x_ref.shape[1]
    n_heads = RET_HEADS
    rw = rnw_ref.shape[1]
    hd = rw // n_heads
    pgw = wpool_ref.shape[1]
    n_exp = wrt_ref.shape[0]
    halo = POOL_HALO

    @pl.when(jnp.logical_and(b == 0, j == 0))
    def _():
        carry_ref[...] = jnp.zeros_like(carry_ref)

    @pl.when(j == 0)
    def _():
        state_ref[...] = jnp.zeros_like(state_ref)
        for r in (p1_ref, p2_ref, p4_ref, p8_ref):
            r[0:halo, :] = jnp.zeros((halo, r.shape[1]), jnp.float32)

    @pl.when(j > 0)
    def _():
        for r in (p1_ref, p2_ref, p4_ref, p8_ref):
            r[0:halo, :] = r[ts:ts + halo, :]

    x = x_ref[0]
    h = _rms(x, nmw_ref[...]).astype(jnp.bfloat16)
    z = jnp.dot(h, win_ref[...], preferred_element_type=jnp.float32)

    cos = cos_ref[...]
    sin = sin_ref[...]
    k_scale = hd ** -0.5

    ret_parts = []
    for hh in range(n_heads):
        q = z[:, hh * hd:(hh + 1) * hd]
        k = z[:, rw + hh * hd:rw + (hh + 1) * hd]
        v = z[:, 2 * rw + hh * hd:2 * rw + (hh + 1) * hd]
        g = z[:, 3 * rw + hh * hd:3 * rw + (hh + 1) * hd]
        q = q * cos + pltpu.roll(q, hd // 2, 1) * sin
        k = (k * cos + pltpu.roll(k, hd // 2, 1) * sin) * k_scale
        qb = q.astype(jnp.bfloat16)
        kb = k.astype(jnp.bfloat16)
        vb = v.astype(jnp.bfloat16)
        s = lax.dot_general(qb, kb, (((1,), (1,)), ((), ())), preferred_element_type=jnp.float32)
        pmat = (s * dmat_ref[hh]).astype(jnp.bfloat16)
        o = jnp.dot(pmat, vb, preferred_element_type=jnp.float32)
        state = state_ref[hh]
        qd = (q * qdec_ref[hh]).astype(jnp.bfloat16)
        o = o + jnp.dot(qd, state.astype(jnp.bfloat16), preferred_element_type=jnp.float32)
        kd = (k * kdec_ref[hh]).astype(jnp.bfloat16)
        u = lax.dot_general(kd, vb, (((0,), (0,)), ((), ())), preferred_element_type=jnp.float32)
        state_ref[hh] = state * tile_decay[hh] + u
        o = o * lax.rsqrt(jnp.mean(o * o, axis=-1, keepdims=True) + EPS)
        o = o * rnw_ref[:, hh * hd:(hh + 1) * hd] * (g * (1.0 / (1.0 + jnp.exp(-g))))
        ret_parts.append(o.astype(jnp.bfloat16))

    p = z[:, 4 * rw:]
    p1_ref[halo:halo + ts, :] = p
    s2 = p + p1_ref[halo - 1:halo - 1 + ts, :]
    p2_ref[halo:halo + ts, :] = s2[:, pgw:]
    s4 = s2[:, pgw:] + p2_ref[halo - 2:halo - 2 + ts, :]
    p4_ref[halo:halo + ts, :] = s4[:, pgw:]
    s8 = s4[:, pgw:] + p4_ref[halo - 4:halo - 4 + ts, :]
    p8_ref[halo:halo + ts, :] = s8[:, pgw:]
    s16 = s8[:, pgw:] + p8_ref[halo - 8:halo - 8 + ts, :]
    sums = (s2[:, :pgw], s4[:, :pgw], s8[:, :pgw], s16)
    tpos = (j * ts + lax.broadcasted_iota(jnp.int32, (ts, pgw), 0) + 1).astype(jnp.float32)
    pool_parts = []
    for gi, w in enumerate(POOL_WINDOWS):
        cnt = jnp.minimum(tpos, float(w))
        pooled = sums[gi] / cnt - p[:, gi * pgw:(gi + 1) * pgw]
        mixed = jnp.dot(pooled.astype(jnp.bfloat16), wpool_ref[gi], preferred_element_type=jnp.float32)
        mixed = mixed * pscale_ref[:, gi * pgw:(gi + 1) * pgw]
        pool_parts.append(mixed.astype(jnp.bfloat16))

    mix = jnp.concatenate(ret_parts + pool_parts, axis=-1)
    x1 = x + jnp.dot(mix, wout_ref[...], preferred_element_type=jnp.float32)
    x1_ref[0] = x1

    h2 = _rms(x1, nfw_ref[...])
    _pack_rows(h2, h2p_ref, ts)

    h2b = h2.astype(jnp.bfloat16)
    logits = lax.dot_general(wrt_ref[...], h2b, (((1,), (1,)), ((), ())),
                             preferred_element_type=jnp.float32) + brt_ref[...]
    eiota = lax.broadcasted_iota(jnp.int32, (n_exp, ts), 0)
    work = logits
    vals, ids, sels = [], [], []
    for _ in range(TOP_K):
        m = jnp.max(work, axis=0, keepdims=True)
        idx = jnp.min(jnp.where(work == m, eiota, n_exp), axis=0, keepdims=True)
        sel = eiota == idx
        vals.append(m)
        ids.append(idx)
        sels.append(sel)
        work = jnp.where(sel, -jnp.inf, work)
    ex = [jnp.exp(vk - vals[0]) for vk in vals]
    den = ex[0] + ex[1] + ex[2] + ex[3]
    gate_ref[...] = jnp.concatenate([e / den for e in ex], axis=0)
    eid_ref[...] = jnp.concatenate(ids, axis=0)

    member = jnp.zeros((n_exp, ts), jnp.float32)
    for sel in sels:
        member = member + sel.astype(jnp.float32)
    before_ref[0] = carry_ref[...]
    before = jnp.dot(member.astype(jnp.bfloat16), utri_ref[...], preferred_element_type=jnp.float32)
    before = before + carry_ref[:, 0:1]
    ranks = [jnp.sum(jnp.where(sel, before, 0.0), axis=0, keepdims=True) for sel in sels]
    rank_ref[...] = jnp.concatenate(ranks, axis=0).astype(jnp.int32)
    new_carry = carry_ref[...] + jnp.sum(member, axis=1, keepdims=True)
    carry_ref[...] = new_carry
    cnt_ref[...] = new_carry


def _mix_route(x, nmw, win_b, cos2, sin2, dmat, qdec, kdec, rnw, wpool_b, pscale, wout_b, nfw,
               wrt_b, brt, utri, tile_decay):
    bsz, seq, d_model = x.shape
    ts = SEQ_TILE
    n_tok = bsz * seq
    n_exp = wrt_b.shape[0]
    rw = rnw.shape[1]
    hd = rw // RET_HEADS
    pw = pscale.shape[1]
    pgw = pw // len(POOL_WINDOWS)
    nj = seq // ts
    rows_per_tok = d_model // PACK // LANES

    def const(shape):
        return pl.BlockSpec(shape, lambda b, j: (0,) * len(shape))

    in_specs = [
        pl.BlockSpec((1, ts, d_model), lambda b, j: (b, j, 0)),
        const(nmw.shape), const(win_b.shape),
        pl.BlockSpec((ts, hd), lambda b, j: (j, 0)),
        pl.BlockSpec((ts, hd), lambda b, j: (j, 0)),
        const(dmat.shape), const(qdec.shape), const(kdec.shape),
        const(rnw.shape), const(wpool_b.shape), const(pscale.shape), const(wout_b.shape),
        const(nfw.shape), const(wrt_b.shape), const(brt.shape), const(utri.shape),
    ]
    out_shape = [
        jax.ShapeDtypeStruct((bsz, seq, d_model), jnp.float32),
        jax.ShapeDtypeStruct((n_tok * rows_per_tok, LANES), jnp.uint32),
        jax.ShapeDtypeStruct((TOP_K, n_tok), jnp.int32),
        jax.ShapeDtypeStruct((TOP_K, n_tok), jnp.float32),
        jax.ShapeDtypeStruct((TOP_K, n_tok), jnp.int32),
        jax.ShapeDtypeStruct((n_tok // ts, n_exp, LANES), jnp.float32),
        jax.ShapeDtypeStruct((n_exp, LANES), jnp.float32),
    ]
    out_specs = [
        pl.BlockSpec((1, ts, d_model), lambda b, j: (b, j, 0)),
        pl.BlockSpec((ts * rows_per_tok, LANES), lambda b, j: (b * nj + j, 0)),
        pl.BlockSpec((TOP_K, ts), lambda b, j: (0, b * nj + j)),
        pl.BlockSpec((TOP_K, ts), lambda b, j: (0, b * nj + j)),
        pl.BlockSpec((TOP_K, ts), lambda b, j: (0, b * nj + j)),
        pl.BlockSpec((1, n_exp, LANES), lambda b, j: (b * nj + j, 0, 0)),
        pl.BlockSpec((n_exp, LANES), lambda b, j: (0, 0)),
    ]
    scratch = [
        pltpu.VMEM((RET_HEADS, hd, hd), jnp.float32),
        pltpu.VMEM((POOL_HALO + ts, pw), jnp.float32),
        pltpu.VMEM((POOL_HALO + ts, pw - pgw), jnp.float32),
        pltpu.VMEM((POOL_HALO + ts, pw - 2 * pgw), jnp.float32),
        pltpu.VMEM((POOL_HALO + ts, pw - 3 * pgw), jnp.float32),
        pltpu.VMEM((n_exp, LANES), jnp.float32),
    ]
    return pl.pallas_call(
        functools.partial(_mix_route_kernel, tile_decay=tile_decay),
        grid=(bsz, nj),
        in_specs=in_specs,
        out_specs=out_specs,
        out_shape=out_shape,
        scratch_shapes=scratch,
        compiler_params=pltpu.CompilerParams(
            dimension_semantics=("arbitrary", "arbitrary"),
            vmem_limit_bytes=VMEM_LIMIT_BYTES),
        name="mix_route",
    )(x, nmw, win_b, cos2, sin2, dmat, qdec, kdec, rnw, wpool_b, pscale, wout_b, nfw, wrt_b, brt, utri)


def _for_set_bits(n, max_bit, fn):
    for bit in range(max_bit, STRIP_ALIGN.bit_length() - 2, -1):
        size = 1 << bit
        offset = (n >> (bit + 1)) << (bit + 1)

        @pl.when((n & size) != 0)
        def _():
            fn(offset, size)


def _aligned_len(n):
    return n + (n & (STRIP_ALIGN - 1))


def _for_strips(nbe_ref, blk, n_exp, max_bit, fn):
    def body(e, carry):
        n = _aligned_len(nbe_ref[blk * n_exp + e])
        _for_set_bits(n, max_bit, functools.partial(fn, e))
        return carry
    lax.fori_loop(0, n_exp, body, 0)


def _dispatch_kernel(lidx_ref, nbe_ref, lb_ref, hrow_ref, padrow_ref, padlen_ref, nused_ref,
                     h2p_ref, xs_hbm, stage_ref, zero_ref, sem, zsem, *, nblk):
    b = pl.program_id(0)
    rpt = zero_ref.shape[0] // ROW_TILE
    tb = h2p_ref.shape[0] // rpt
    n_exp = padrow_ref.shape[0]
    n_tok = lidx_ref.shape[0] // TOP_K
    slot_rows = stage_ref.shape[0] // 2
    max_bit = tb.bit_length() - 1
    slot = b % 2

    def strip_copy(blk, sl, e, offset, size):
        src = pl.multiple_of(sl * slot_rows + (lb_ref[blk * n_exp + e] + offset) * rpt, SUBLANES)
        dst = pl.multiple_of((hrow_ref[blk * n_exp + e] + offset) * rpt, SUBLANES)
        return pltpu.make_async_copy(stage_ref.at[pl.ds(src, size * rpt), :],
                                     xs_hbm.at[pl.ds(dst, size * rpt), :], sem.at[sl])

    def start_strips(blk, sl):
        _for_strips(nbe_ref, blk, n_exp, max_bit,
                    lambda e, offset, size: strip_copy(blk, sl, e, offset, size).start())

    def wait_strips(blk, sl):
        _for_strips(nbe_ref, blk, n_exp, max_bit,
                    lambda e, offset, size: strip_copy(blk, sl, e, offset, size).wait())

    @pl.when(b >= 2)
    def _():
        wait_strips(b - 2, slot)

    def scatter_group(c, carry):
        for u in range(GATHER_UNROLL):
            t = c * GATHER_UNROLL + u
            row = h2p_ref[pl.ds(pl.multiple_of(t * rpt, rpt), rpt), :]
            for k in range(TOP_K):
                li = lidx_ref[k * n_tok + b * tb + t]
                dst = pl.multiple_of(slot * slot_rows + li * rpt, rpt)
                stage_ref[pl.ds(dst, rpt), :] = row
        return carry
    lax.fori_loop(0, tb // GATHER_UNROLL, scatter_group, 0)

    def zero_tail(e, carry):
        n = nbe_ref[b * n_exp + e]

        @pl.when((n & (STRIP_ALIGN - 1)) != 0)
        def _():
            dst = pl.multiple_of(slot * slot_rows + (lb_ref[b * n_exp + e] + n) * rpt, rpt)
            stage_ref[pl.ds(dst, rpt), :] = jnp.zeros((rpt, LANES), jnp.uint32)
        return carry
    lax.fori_loop(0, n_exp, zero_tail, 0)

    start_strips(b, slot)

    @pl.when(b == nblk - 1)
    def _():
        if nblk >= 2:
            wait_strips(b - 1, 1 - slot)
        wait_strips(b, slot)
        zero_ref[...] = jnp.zeros_like(zero_ref)
        pad_bit = ROW_TILE.bit_length() - 1

        def pad_copy(e, offset, size):
            dst = pl.multiple_of((padrow_ref[e] + offset) * rpt, SUBLANES)
            return pltpu.make_async_copy(zero_ref.at[pl.ds(0, size * rpt), :],
                                         xs_hbm.at[pl.ds(dst, size * rpt), :], zsem)

        def pad_start(e, carry):
            _for_set_bits(padlen_ref[e], pad_bit, lambda o, s: pad_copy(e, o, s).start())
            return carry

        def pad_wait(e, carry):
            _for_set_bits(padlen_ref[e], pad_bit, lambda o, s: pad_copy(e, o, s).wait())
            return carry
        lax.fori_loop(0, n_exp, pad_start, 0)

        def tail_copy(i):
            dst = pl.multiple_of(i * (ROW_TILE * rpt), SUBLANES)
            return pltpu.make_async_copy(zero_ref, xs_hbm.at[pl.ds(dst, ROW_TILE * rpt), :], zsem)

        def tail_start(i, carry):
            tail_copy(i).start()
            return carry

        def tail_wait(i, carry):
            tail_copy(i).wait()
            return carry
        n_tiles = xs_hbm.shape[0] // (ROW_TILE * rpt)
        lax.fori_loop(nused_ref[0], n_tiles, tail_start, 0)
        lax.fori_loop(0, n_exp, pad_wait, 0)
        lax.fori_loop(nused_ref[0], n_tiles, tail_wait, 0)


def _dispatch(lidx, nbe, lb, hrow, padrow, padlen, n_used, h2p, n_rows, d_model):
    tb = TOK_BLOCK
    rpt = d_model // PACK // LANES
    n_tok = h2p.shape[0] // rpt
    n_exp = padrow.shape[0]
    slot_rows = (tb * TOP_K + n_exp * (STRIP_ALIGN - 1)) * rpt
    grid_spec = pltpu.PrefetchScalarGridSpec(
        num_scalar_prefetch=7,
        grid=(n_tok // tb,),
        in_specs=[pl.BlockSpec((tb * rpt, LANES), lambda i, *_: (i, 0))],
        out_specs=pl.BlockSpec(memory_space=pl.ANY),
        scratch_shapes=[pltpu.VMEM((2 * slot_rows, LANES), jnp.uint32),
                        pltpu.VMEM((ROW_TILE * rpt, LANES), jnp.uint32),
                        pltpu.SemaphoreType.DMA((2,)),
                        pltpu.SemaphoreType.DMA],
    )
    return pl.pallas_call(
        functools.partial(_dispatch_kernel, nblk=n_tok // tb),
        grid_spec=grid_spec,
        out_shape=jax.ShapeDtypeStruct((n_rows * rpt, LANES), jnp.uint32),
        compiler_params=pltpu.CompilerParams(
            dimension_semantics=("arbitrary",),
            vmem_limit_bytes=VMEM_LIMIT_BYTES),
        name="dispatch",
    )(lidx, nbe, lb, hrow, padrow, padlen, n_used, h2p)


def _experts_kernel(tstart_ref, tcount_ref, wgu_ref, bgu_ref, wd_ref, bd_ref, xs_hbm,
                    y_hbm, wgu_b, wd_b, xbuf, ybuf, xsem, ysem, *, n_exp):
    e = pl.program_id(0)
    d_ff = wd_ref.shape[1]
    d_model = wd_ref.shape[2]
    rpt = d_model // PACK // LANES
    tile_rows = xbuf.shape[1]
    tm = tile_rows // rpt
    n_tiles = y_hbm.shape[0] // tile_rows
    t0 = tstart_ref[e]
    nt = tcount_ref[e]

    def hbm_tile(ref, tile):
        return ref.at[pl.ds(pl.multiple_of(tile * tile_rows, SUBLANES), tile_rows), :]

    def x_copy(j, sl):
        return pltpu.make_async_copy(hbm_tile(xs_hbm, t0 + j), xbuf.at[sl], xsem.at[sl])

    def y_copy(tile, sl):
        return pltpu.make_async_copy(ybuf.at[sl], hbm_tile(y_hbm, tile), ysem.at[sl])

    @pl.when(nt > 0)
    def _():
        x_copy(0, 0).start()
        wgu_b[...] = wgu_ref[0].astype(jnp.bfloat16)
        wd_b[...] = wd_ref[0].astype(jnp.bfloat16)

        def tile_step(j, carry):
            sl = j % 2

            @pl.when(j + 1 < nt)
            def _():
                x_copy(j + 1, 1 - sl).start()
            x_copy(j, sl).wait()

            @pl.when(j >= 2)
            def _():
                y_copy(t0 + j - 2, sl).wait()
            x = _unpack_rows(xbuf.at[sl], tm, rpt)
            gu = jnp.dot(x, wgu_b[...], preferred_element_type=jnp.float32) + bgu_ref[0]
            x_glu = jnp.minimum(gu[:, :d_ff], SWIGLU_LIMIT)
            x_lin = jnp.clip(gu[:, d_ff:], -SWIGLU_LIMIT, SWIGLU_LIMIT)
            act = x_glu * (1.0 / (1.0 + jnp.exp(-SWIGLU_ALPHA * x_glu))) * (x_lin + 1.0)
            y = jnp.dot(act.astype(jnp.bfloat16), wd_b[...], preferred_element_type=jnp.float32) + bd_ref[0]
            _pack_rows(y, ybuf.at[sl], tm)
            y_copy(t0 + j, sl).start()
            return carry
        lax.fori_loop(0, nt, tile_step, 0)

        @pl.when(nt >= 2)
        def _():
            y_copy(t0 + nt - 2, nt % 2).wait()
        y_copy(t0 + nt - 1, (nt - 1) % 2).wait()

    @pl.when(e == n_exp - 1)
    def _():
        ybuf[0] = jnp.zeros(ybuf.shape[1:], ybuf.dtype)

        def tail_start(i, carry):
            y_copy(i, 0).start()
            return carry

        def tail_wait(i, carry):
            y_copy(i, 0).wait()
            return carry
        lax.fori_loop(t0 + nt, n_tiles, tail_start, 0)
        lax.fori_loop(t0 + nt, n_tiles, tail_wait, 0)


def _experts(tile_start, tile_count, xs, w_gu, b_gu, w_down, b_down):
    tm = ROW_TILE
    n_exp, d_model, d_gu = w_gu.shape
    d_ff = w_down.shape[1]
    rpt = d_model // PACK // LANES
    grid_spec = pltpu.PrefetchScalarGridSpec(
        num_scalar_prefetch=2,
        grid=(n_exp,),
        in_specs=[
            pl.BlockSpec((1, d_model, d_gu), lambda e, *_: (e, 0, 0)),
            pl.BlockSpec((1, 1, d_gu), lambda e, *_: (e, 0, 0)),
            pl.BlockSpec((1, d_ff, d_model), lambda e, *_: (e, 0, 0)),
            pl.BlockSpec((1, 1, d_model), lambda e, *_: (e, 0, 0)),
            pl.BlockSpec(memory_space=pl.ANY),
        ],
        out_specs=pl.BlockSpec(memory_space=pl.ANY),
        scratch_shapes=[pltpu.VMEM((d_model, d_gu), jnp.bfloat16),
                        pltpu.VMEM((d_ff, d_model), jnp.bfloat16),
                        pltpu.VMEM((2, tm * rpt, LANES), jnp.uint32),
                        pltpu.VMEM((2, tm * rpt, LANES), jnp.uint32),
                        pltpu.SemaphoreType.DMA((2,)),
                        pltpu.SemaphoreType.DMA((2,))],
    )
    return pl.pallas_call(
        functools.partial(_experts_kernel, n_exp=n_exp),
        grid_spec=grid_spec,
        out_shape=jax.ShapeDtypeStruct(xs.shape, jnp.uint32),
        compiler_params=pltpu.CompilerParams(
            dimension_semantics=("arbitrary",),
            vmem_limit_bytes=VMEM_LIMIT_BYTES),
        name="experts",
    )(tile_start, tile_count, w_gu, b_gu.reshape(n_exp, 1, d_gu), w_down,
      b_down.reshape(n_exp, 1, d_model), xs)


def _combine_kernel(lidx_ref, gate_ref, nbe_ref, lb_ref, hrow_ref,
                    x1_ref, nw_ref, y_hbm, out_ref, stage_ref, alo_ref, ahi_ref, sem, *, nblk):
    b = pl.program_id(0)
    tb = x1_ref.shape[0]
    rpt = alo_ref.shape[0] // tb
    n_exp = nbe_ref.shape[0] // nblk
    n_tok = lidx_ref.shape[0] // TOP_K
    slot_rows = stage_ref.shape[0] // 2
    max_bit = tb.bit_length() - 1
    slot = b % 2

    def strip_copy(blk, sl, e, offset, size):
        src = pl.multiple_of((hrow_ref[blk * n_exp + e] + offset) * rpt, SUBLANES)
        dst = pl.multiple_of(sl * slot_rows + (lb_ref[blk * n_exp + e] + offset) * rpt, SUBLANES)
        return pltpu.make_async_copy(y_hbm.at[pl.ds(src, size * rpt), :],
                                     stage_ref.at[pl.ds(dst, size * rpt), :], sem.at[sl])

    def start_strips(blk, sl):
        _for_strips(nbe_ref, blk, n_exp, max_bit,
                    lambda e, offset, size: strip_copy(blk, sl, e, offset, size).start())

    @pl.when(b == 0)
    def _():
        start_strips(0, 0)

    @pl.when(b + 1 < nblk)
    def _():
        start_strips(b + 1, 1 - slot)

    _for_strips(nbe_ref, b, n_exp, max_bit,
                lambda e, offset, size: strip_copy(b, slot, e, offset, size).wait())

    def gather_group(c, carry):
        for u in range(GATHER_UNROLL):
            t = c * GATHER_UNROLL + u
            lo = jnp.zeros((rpt, LANES), jnp.float32)
            hi = jnp.zeros((rpt, LANES), jnp.float32)
            for k in range(TOP_K):
                li = lidx_ref[k * n_tok + b * tb + t]
                g = gate_ref[k * n_tok + b * tb + t]
                src = pl.multiple_of(slot * slot_rows + li * rpt, rpt)
                l, h = _unpack_words(stage_ref[pl.ds(src, rpt), :])
                lo = lo + g * l
                hi = hi + g * h
            dst = pl.multiple_of(t * rpt, rpt)
            alo_ref[pl.ds(dst, rpt), :] = lo
            ahi_ref[pl.ds(dst, rpt), :] = hi
        return carry
    lax.fori_loop(0, tb // GATHER_UNROLL, gather_group, 0)

    parts = [alo_ref[pl.ds(si, tb, stride=rpt), :] for si in range(rpt)]
    parts += [ahi_ref[pl.ds(si, tb, stride=rpt), :] for si in range(rpt)]
    x2 = x1_ref[...] + jnp.concatenate(parts, axis=-1)
    out_ref[...] = _rms(x2, nw_ref[...])


def _combine(lidx, gates, nbe, lb, hrow, x1, norm_w, y_rows):
    tb = TOK_BLOCK
    n_tok, d_model = x1.shape
    rpt = d_model // PACK // LANES
    n_exp = nbe.shape[0] // (n_tok // tb)
    slot_rows = (tb * TOP_K + n_exp * (STRIP_ALIGN - 1)) * rpt
    grid_spec = pltpu.PrefetchScalarGridSpec(
        num_scalar_prefetch=5,
        grid=(n_tok // tb,),
        in_specs=[
            pl.BlockSpec((tb, d_model), lambda i, *_: (i, 0)),
            pl.BlockSpec((1, d_model), lambda i, *_: (0, 0)),
            pl.BlockSpec(memory_space=pl.ANY),
        ],
        out_specs=pl.BlockSpec((tb, d_model), lambda i, *_: (i, 0)),
        scratch_shapes=[pltpu.VMEM((2 * slot_rows, LANES), jnp.uint32),
                        pltpu.VMEM((tb * rpt, LANES), jnp.float32),
                        pltpu.VMEM((tb * rpt, LANES), jnp.float32),
                        pltpu.SemaphoreType.DMA((2,))],
    )
    return pl.pallas_call(
        functools.partial(_combine_kernel, nblk=n_tok // tb),
        grid_spec=grid_spec,
        out_shape=jax.ShapeDtypeStruct((n_tok, d_model), jnp.float32),
        compiler_params=pltpu.CompilerParams(
            dimension_semantics=("arbitrary",),
            vmem_limit_bytes=VMEM_LIMIT_BYTES),
        name="combine",
    )(lidx, gates, nbe, lb, hrow, x1, norm_w, y_rows)


def _retention_tables(seq, hd, ts):
    half = hd // 2
    inv = ROPE_BASE ** (-jnp.arange(half, dtype=jnp.float32) / half)
    ang = jnp.arange(seq, dtype=jnp.float32)[:, None] * inv[None, :]
    cos, sin = jnp.cos(ang), jnp.sin(ang)
    cos2 = jnp.concatenate([cos, cos], axis=-1)
    sin2 = jnp.concatenate([-sin, sin], axis=-1)
    log_g = jnp.log1p(-jnp.exp2(-5.0 - jnp.arange(RET_HEADS, dtype=jnp.float32)))
    pos = jnp.arange(ts, dtype=jnp.float32)
    dist = jnp.abs(pos[:, None] - pos[None, :])
    chunk = jnp.arange(ts, dtype=jnp.int32) // CHUNK
    visible = chunk[None, :] <= chunk[:, None]
    dmat = jnp.where(visible[None], jnp.exp(log_g[:, None, None] * dist[None]), 0.0)
    qdec = jnp.exp(log_g[:, None] * (pos + 1.0)[None, :])
    kdec = jnp.exp(log_g[:, None] * (ts - 1.0 - pos)[None, :])
    qdec = jnp.broadcast_to(qdec[:, :, None], (RET_HEADS, ts, hd))
    kdec = jnp.broadcast_to(kdec[:, :, None], (RET_HEADS, ts, hd))
    tile_decay = tuple(math.exp(math.log1p(-2.0 ** (-5.0 - h)) * ts) for h in range(RET_HEADS))
    return cos2, sin2, dmat, qdec, kdec, tile_decay


def _routing_tables(eid, rank, before, counts, n_tok, n_exp):
    tm, tb, ts = ROW_TILE, TOK_BLOCK, SEQ_TILE
    cb = before[::tb // ts]
    nbe = jnp.concatenate([cb[1:], counts[None]], axis=0) - cb
    nal = nbe + (nbe & (STRIP_ALIGN - 1))
    region = jnp.sum(nal, axis=0)
    tiles_per = (region + tm - 1) // tm
    tile_end = jnp.cumsum(tiles_per)
    offs = (tile_end - tiles_per) * tm
    n_tiles = (n_tok * TOP_K + (n_tok // tb) * n_exp * (STRIP_ALIGN - 1)) // tm + n_exp
    hrow = offs[None, :] + jnp.cumsum(nal, axis=0) - nal
    lb = jnp.cumsum(nal, axis=1) - nal
    delta = jnp.repeat(lb - cb, tb, axis=0)
    onehot = eid[:, :, None] == jnp.arange(n_exp, dtype=jnp.int32)
    lidx = rank + jnp.sum(jnp.where(onehot, delta[None], 0), axis=-1)
    padrow = offs + region
    padlen = tiles_per * tm - region
    as_i32 = lambda a: a.reshape(-1).astype(jnp.int32)
    return (n_tiles, as_i32(tile_end - tiles_per), as_i32(tiles_per), as_i32(tile_end[-1:]),
            as_i32(lidx), as_i32(nbe), as_i32(lb), as_i32(hrow), as_i32(padrow), as_i32(padlen))


def _layer(x, norm_mix_w, w_in, ret_norm_w, w_pool, pool_scale, w_out, norm_ffn_w,
           w_router, b_router, w_gu, b_gu, w_down, b_down, final_w):
    bsz, seq, d_model = x.shape
    n_tok = bsz * seq
    n_exp = w_router.shape[1]
    rw = ret_norm_w.shape[0]
    hd = rw // RET_HEADS
    ts = SEQ_TILE
    bf = jnp.bfloat16
    rpt = d_model // PACK // LANES
    assert d_model % (PACK * LANES) == 0 and (STRIP_ALIGN * rpt) % SUBLANES == 0
    assert seq % ts == 0 and ts % CHUNK == 0 and n_tok % TOK_BLOCK == 0 and TOK_BLOCK % ts == 0
    assert TOK_BLOCK & (TOK_BLOCK - 1) == 0 and ROW_TILE & (ROW_TILE - 1) == 0

    cos2, sin2, dmat, qdec, kdec, tile_decay = _retention_tables(seq, hd, ts)
    tri = jnp.arange(ts, dtype=jnp.int32)
    utri = (tri[:, None] < tri[None, :]).astype(bf)

    x1, h2p, eid, gate, rank, before, cnt = _mix_route(
        x, norm_mix_w[None], w_in.astype(bf), cos2, sin2, dmat, qdec, kdec, ret_norm_w[None],
        w_pool.astype(bf), pool_scale[None], w_out.astype(bf), norm_ffn_w[None],
        w_router.T.astype(bf), b_router[:, None], utri, tile_decay)

    (n_tiles, tile_start, tile_count, n_used, lidx, nbe, lb, hrow, padrow, padlen) = _routing_tables(
        eid, rank, before[:, :, 0].astype(jnp.int32), cnt[:, 0].astype(jnp.int32), n_tok, n_exp)

    xs = _dispatch(lidx, nbe, lb, hrow, padrow, padlen, n_used, h2p, n_tiles * ROW_TILE, d_model)
    y_rows = _experts(tile_start, tile_count, xs, w_gu, b_gu, w_down, b_down)
    out = _combine(lidx, gate.reshape(-1), nbe, lb, hrow, x1.reshape(n_tok, d_model), final_w[None], y_rows)
    return out.reshape(bsz, seq, d_model)


def kernel(x, norm_mix_w, w_in, ret_norm_w, w_pool, pool_scale, w_out, norm_ffn_w, w_router, b_router,
           w_gu, b_gu, w_down, b_down, norm_final_w):
    depth = norm_mix_w.shape[0]
    for l in range(depth):
        last = l == depth - 1
        assert last, "stacked layers need an un-normalised combine output"
        x = _layer(x, norm_mix_w[l], w_in[l], ret_norm_w[l], w_pool[l], pool_scale[l], w_out[l],
                   norm_ffn_w[l], w_router[l], b_router[l], w_gu[l], b_gu[l], w_down[l], b_down[l],
                   norm_final_w)
    return x
```

```python
import functools
import math

import jax
import jax.numpy as jnp
from jax import lax
from jax.experimental import pallas as pl
from jax.experimental.pallas import tpu as pltpu

CHUNK = 64
RET_HEADS = 4
POOL_WINDOWS = (2, 4, 8, 16)
ROPE_BASE = 10000.0
TOP_K = 4
SWIGLU_LIMIT = 7.0
SWIGLU_ALPHA = 1.702
EPS = 1e-6

LANES = 128
SUBLANES = 8
VMEM_LIMIT_BYTES = 56 * 1024 * 1024

SEQ_TILE = 256
ROW_TILE = 256
TOK_BLOCK = 1024
POOL_HALO = 16
PACK = 2
STRIP_ALIGN = 2
GATHER_UNROLL = 8
RARE_STRIP_BIT = 8


def _rms(x, w):
    return x * lax.rsqrt(jnp.mean(x * x, axis=-1, keepdims=True) + EPS) * w


def _bf16_bits_hi(x):
    return lax.bitcast_convert_type(x.astype(jnp.bfloat16).astype(jnp.float32), jnp.uint32)


def _pack_rows(x, out_ref, n_rows):
    half = x.shape[1] // PACK
    packed = (_bf16_bits_hi(x[:, :half]) >> 16) | _bf16_bits_hi(x[:, half:])
    rpt = half // LANES
    for si in range(rpt):
        out_ref[pl.ds(si, n_rows, stride=rpt), :] = packed[:, si * LANES:(si + 1) * LANES]


def _unpack_words(u):
    lo = lax.bitcast_convert_type(u << 16, jnp.float32)
    hi = lax.bitcast_convert_type(u & jnp.uint32(0xFFFF0000), jnp.float32)
    return lo, hi


def _unpack_rows(g_ref, n_rows, rpt):
    lo, hi = [], []
    for si in range(rpt):
        l, h = _unpack_words(g_ref[pl.ds(si, n_rows, stride=rpt), :])
        lo.append(l.astype(jnp.bfloat16))
        hi.append(h.astype(jnp.bfloat16))
    return jnp.concatenate(lo + hi, axis=-1)


def _mix_route_kernel(x_ref, nmw_ref, win_ref, cos_ref, sin_ref, dmat_ref, qdec_ref, kdec_ref,
                      rnw_ref, wpool_ref, pscale_ref, wout_ref, nfw_ref, wrt_ref, brt_ref, utri_ref,
                      x1_ref, h2p_ref, eid_ref, gate_ref, rank_ref, before_ref, cnt_ref,
                      state_ref, p1_ref, p2_ref, p4_ref, p8_ref, carry_ref,
                      *, tile_decay):
    b = pl.program_id(0)
    j = pl.program_id(1)
    ts = x_ref.shape[1]
    n_heads = RET_HEADS
    rw = rnw_ref.shape[1]
    hd = rw // n_heads
    pgw = wpool_ref.shape[1]
    n_exp = wrt_ref.shape[0]
    halo = POOL_HALO

    @pl.when(jnp.logical_and(b == 0, j == 0))
    def _():
        carry_ref[...] = jnp.zeros_like(carry_ref)

    @pl.when(j == 0)
    def _():
        state_ref[...] = jnp.zeros_like(state_ref)
        for r in (p1_ref, p2_ref, p4_ref, p8_ref):
            r[0:halo, :] = jnp.zeros((halo, r.shape[1]), jnp.float32)

    @pl.when(j > 0)
    def _():
        for r in (p1_ref, p2_ref, p4_ref, p8_ref):
            r[0:halo, :] = r[ts:ts + halo, :]

    x = x_ref[0]
    h = _rms(x, nmw_ref[...]).astype(jnp.bfloat16)
    z = jnp.dot(h, win_ref[...], preferred_element_type=jnp.float32)

    cos = cos_ref[...]
    sin = sin_ref[...]
    k_scale = hd ** -0.5

    ret_parts = []
    for hh in range(n_heads):
        q = z[:, hh * hd:(hh + 1) * hd]
        k = z[:, rw + hh * hd:rw + (hh + 1) * hd]
        v = z[:, 2 * rw + hh * hd:2 * rw + (hh + 1) * hd]
        g = z[:, 3 * rw + hh * hd:3 * rw + (hh + 1) * hd]
        q = q * cos + pltpu.roll(q, hd // 2, 1) * sin
        k = (k * cos + pltpu.roll(k, hd // 2, 1) * sin) * k_scale
        qb = q.astype(jnp.bfloat16)
        kb = k.astype(jnp.bfloat16)
        vb = v.astype(jnp.bfloat16)
        s = lax.dot_general(qb, kb, (((1,), (1,)), ((), ())), preferred_element_type=jnp.float32)
        pmat = (s * dmat_ref[hh]).astype(jnp.bfloat16)
        o = jnp.dot(pmat, vb, preferred_element_type=jnp.float32)
        state = state_ref[hh]
        qd = (q * qdec_ref[hh]).astype(jnp.bfloat16)
        o = o + jnp.dot(qd, state.astype(jnp.bfloat16), preferred_element_type=jnp.float32)
        kd = (k * kdec_ref[hh]).astype(jnp.bfloat16)
        u = lax.dot_general(kd, vb, (((0,), (0,)), ((), ())), preferred_element_type=jnp.float32)
        state_ref[hh] = state * tile_decay[hh] + u
        o = o * lax.rsqrt(jnp.mean(o * o, axis=-1, keepdims=True) + EPS)
        o = o * rnw_ref[:, hh * hd:(hh + 1) * hd] * (g * (1.0 / (1.0 + jnp.exp(-g))))
        ret_parts.append(o.astype(jnp.bfloat16))

    p = z[:, 4 * rw:]
    p1_ref[halo:halo + ts, :] = p
    s2 = p + p1_ref[halo - 1:halo - 1 + ts, :]
    p2_ref[halo:halo + ts, :] = s2[:, pgw:]
    s4 = s2[:, pgw:] + p2_ref[halo - 2:halo - 2 + ts, :]
    p4_ref[halo:halo + ts, :] = s4[:, pgw:]
    s8 = s4[:, pgw:] + p4_ref[halo - 4:halo - 4 + ts, :]
    p8_ref[halo:halo + ts, :] = s8[:, pgw:]
    s16 = s8[:, pgw:] + p8_ref[halo - 8:halo - 8 + ts, :]
    sums = (s2[:, :pgw], s4[:, :pgw], s8[:, :pgw], s16)
    tpos = (j * ts + lax.broadcasted_iota(jnp.int32, (ts, pgw), 0) + 1).astype(jnp.float32)
    pool_parts = []
    for gi, w in enumerate(POOL_WINDOWS):
        cnt = jnp.minimum(tpos, float(w))
        pooled = sums[gi] / cnt - p[:, gi * pgw:(gi + 1) * pgw]
        mixed = jnp.dot(pooled.astype(jnp.bfloat16), wpool_ref[gi], preferred_element_type=jnp.float32)
        mixed = mixed * pscale_ref[:, gi * pgw:(gi + 1) * pgw]
        pool_parts.append(mixed.astype(jnp.bfloat16))

    mix = jnp.concatenate(ret_parts + pool_parts, axis=-1)
    x1 = x + jnp.dot(mix, wout_ref[...], preferred_element_type=jnp.float32)
    x1_ref[0] = x1

    h2 = _rms(x1, nfw_ref[...])
    _pack_rows(h2, h2p_ref, ts)

    h2b = h2.astype(jnp.bfloat16)
    logits = lax.dot_general(wrt_ref[...], h2b, (((1,), (1,)), ((), ())),
                             preferred_element_type=jnp.float32) + brt_ref[...]
    eiota = lax.broadcasted_iota(jnp.int32, (n_exp, ts), 0)
    work = logits
    vals, ids, sels = [], [], []
    for _ in range(TOP_K):
        m = jnp.max(work, axis=0, keepdims=True)
        idx = jnp.min(jnp.where(work == m, eiota, n_exp), axis=0, keepdims=True)
        sel = eiota == idx
        vals.append(m)
        ids.append(idx)
        sels.append(sel)
        work = jnp.where(sel, -jnp.inf, work)
    ex = [jnp.exp(vk - vals[0]) for vk in vals]
    den = ex[0] + ex[1] + ex[2] + ex[3]
    gate_ref[...] = jnp.concatenate([e / den for e in ex], axis=0)
    eid_ref[...] = jnp.concatenate(ids, axis=0)

    member = jnp.zeros((n_exp, ts), jnp.float32)
    for sel in sels:
        member = member + sel.astype(jnp.float32)
    before_ref[0] = carry_ref[...]
    before = jnp.dot(member.astype(jnp.bfloat16), utri_ref[...], preferred_element_type=jnp.float32)
    before = before + carry_ref[:, 0:1]
    ranks = [jnp.sum(jnp.where(sel, before, 0.0), axis=0, keepdims=True) for sel in sels]
    rank_ref[...] = jnp.concatenate(ranks, axis=0).astype(jnp.int32)
    new_carry = carry_ref[...] + jnp.sum(member, axis=1, keepdims=True)
    carry_ref[...] = new_carry
    cnt_ref[...] = new_carry


def _mix_route(x, nmw, win_b, cos2, sin2, dmat, qdec, kdec, rnw, wpool_b, pscale, wout_b, nfw,
               wrt_b, brt, utri, tile_decay):
    bsz, seq, d_model = x.shape
    ts = SEQ_TILE
    n_tok = bsz * seq
    n_exp = wrt_b.shape[0]
    rw = rnw.shape[1]
    hd = rw // RET_HEADS
    pw = pscale.shape[1]
    pgw = pw // len(POOL_WINDOWS)
    nj = seq // ts
    rows_per_tok = d_model // PACK // LANES

    def const(shape):
        return pl.BlockSpec(shape, lambda b, j: (0,) * len(shape))

    in_specs = [
        pl.BlockSpec((1, ts, d_model), lambda b, j: (b, j, 0)),
        const(nmw.shape), const(win_b.shape),
        pl.BlockSpec((ts, hd), lambda b, j: (j, 0)),
        pl.BlockSpec((ts, hd), lambda b, j: (j, 0)),
        const(dmat.shape), const(qdec.shape), const(kdec.shape),
        const(rnw.shape), const(wpool_b.shape), const(pscale.shape), const(wout_b.shape),
        const(nfw.shape), const(wrt_b.shape), const(brt.shape), const(utri.shape),
    ]
    out_shape = [
        jax.ShapeDtypeStruct((bsz, seq, d_model), jnp.float32),
        jax.ShapeDtypeStruct((n_tok * rows_per_tok, LANES), jnp.uint32),
        jax.ShapeDtypeStruct((TOP_K, n_tok), jnp.int32),
        jax.ShapeDtypeStruct((TOP_K, n_tok), jnp.float32),
        jax.ShapeDtypeStruct((TOP_K, n_tok), jnp.int32),
        jax.ShapeDtypeStruct((n_tok // ts, n_exp, LANES), jnp.float32),
        jax.ShapeDtypeStruct((n_exp, LANES), jnp.float32),
    ]
    out_specs = [
        pl.BlockSpec((1, ts, d_model), lambda b, j: (b, j, 0)),
        pl.BlockSpec((ts * rows_per_tok, LANES), lambda b, j: (b * nj + j, 0)),
        pl.BlockSpec((TOP_K, ts), lambda b, j: (0, b * nj + j)),
        pl.BlockSpec((TOP_K, ts), lambda b, j: (0, b * nj + j)),
        pl.BlockSpec((TOP_K, ts), lambda b, j: (0, b * nj + j)),
        pl.BlockSpec((1, n_exp, LANES), lambda b, j: (b * nj + j, 0, 0)),
        pl.BlockSpec((n_exp, LANES), lambda b, j: (0, 0)),
    ]
    scratch = [
        pltpu.VMEM((RET_HEADS, hd, hd), jnp.float32),
        pltpu.VMEM((POOL_HALO + ts, pw), jnp.float32),
        pltpu.VMEM((POOL_HALO + ts, pw - pgw), jnp.float32),
        pltpu.VMEM((POOL_HALO + ts, pw - 2 * pgw), jnp.float32),
        pltpu.VMEM((POOL_HALO + ts, pw - 3 * pgw), jnp.float32),
        pltpu.VMEM((n_exp, LANES), jnp.float32),
    ]
    return pl.pallas_call(
        functools.partial(_mix_route_kernel, tile_decay=tile_decay),
        grid=(bsz, nj),
        in_specs=in_specs,
        out_specs=out_specs,
        out_shape=out_shape,
        scratch_shapes=scratch,
        compiler_params=pltpu.CompilerParams(
            dimension_semantics=("arbitrary", "arbitrary"),
            vmem_limit_bytes=VMEM_LIMIT_BYTES),
        name="mix_route",
    )(x, nmw, win_b, cos2, sin2, dmat, qdec, kdec, rnw, wpool_b, pscale, wout_b, nfw, wrt_b, brt, utri)


def _for_set_bits(n, max_bit, fn):
    def piece(bit):
        size = 1 << bit
        offset = (n >> (bit + 1)) << (bit + 1)

        @pl.when((n & size) != 0)
        def _():
            fn(offset, size)

    min_bit = STRIP_ALIGN.bit_length() - 1
    rare_bit = min(max_bit + 1, RARE_STRIP_BIT)

    @pl.when(n >= (1 << rare_bit))
    def _():
        for bit in range(max_bit, rare_bit - 1, -1):
            piece(bit)
    for bit in range(rare_bit - 1, min_bit - 1, -1):
        piece(bit)


def _aligned_len(n):
    return n + (n & (STRIP_ALIGN - 1))


def _for_strips(nbe_ref, blk, n_exp, max_bit, fn):
    def body(e, carry):
        n = _aligned_len(nbe_ref[blk * n_exp + e])
        _for_set_bits(n, max_bit, functools.partial(fn, e))
        return carry
    lax.fori_loop(0, n_exp, body, 0)


def _dispatch_kernel(lidx_ref, nbe_ref, lb_ref, hrow_ref, padrow_ref, padlen_ref, nused_ref,
                     h2p_ref, xs_hbm, stage_ref, zero_ref, sem, zsem, *, nblk):
    b = pl.program_id(0)
    rpt = zero_ref.shape[0] // ROW_TILE
    tb = h2p_ref.shape[0] // rpt
    n_exp = padrow_ref.shape[0]
    slot_rows = stage_ref.shape[0] // 2
    max_bit = tb.bit_length() - 1
    slot = b % 2

    def strip_copy(blk, sl, e, offset, size):
        src = pl.multiple_of(sl * slot_rows + (lb_ref[blk * n_exp + e] + offset) * rpt, SUBLANES)
        dst = pl.multiple_of((hrow_ref[blk * n_exp + e] + offset) * rpt, SUBLANES)
        return pltpu.make_async_copy(stage_ref.at[pl.ds(src, size * rpt), :],
                                     xs_hbm.at[pl.ds(dst, size * rpt), :], sem.at[sl])

    def start_strips(blk, sl):
        _for_strips(nbe_ref, blk, n_exp, max_bit,
                    lambda e, offset, size: strip_copy(blk, sl, e, offset, size).start())

    def wait_strips(blk, sl):
        _for_strips(nbe_ref, blk, n_exp, max_bit,
                    lambda e, offset, size: strip_copy(blk, sl, e, offset, size).wait())

    @pl.when(b >= 2)
    def _():
        wait_strips(b - 2, slot)

    def scatter_block(static_slot):
        def scatter_group(c, carry):
            tok0 = c * GATHER_UNROLL
            idx0 = (b * tb + tok0) * TOP_K
            for u in range(GATHER_UNROLL):
                row = h2p_ref[pl.ds(pl.multiple_of((tok0 + u) * rpt, rpt), rpt), :]
                for k in range(TOP_K):
                    dst = pl.multiple_of(lidx_ref[idx0 + u * TOP_K + k], rpt)
                    stage_ref[pl.ds(static_slot * slot_rows + dst, rpt), :] = row
            return carry
        lax.fori_loop(0, tb // GATHER_UNROLL, scatter_group, 0)

    for static_slot in range(2):
        pl.when(slot == static_slot)(functools.partial(scatter_block, static_slot))

    def zero_tail(e, carry):
        n = nbe_ref[b * n_exp + e]

        @pl.when((n & (STRIP_ALIGN - 1)) != 0)
        def _():
            dst = pl.multiple_of(slot * slot_rows + (lb_ref[b * n_exp + e] + n) * rpt, rpt)
            stage_ref[pl.ds(dst, rpt), :] = jnp.zeros((rpt, LANES), jnp.uint32)
        return carry
    lax.fori_loop(0, n_exp, zero_tail, 0)

    start_strips(b, slot)

    @pl.when(b == nblk - 1)
    def _():
        if nblk >= 2:
            wait_strips(b - 1, 1 - slot)
        wait_strips(b, slot)
        zero_ref[...] = jnp.zeros_like(zero_ref)
        pad_bit = ROW_TILE.bit_length() - 1

        def pad_copy(e, offset, size):
            dst = pl.multiple_of((padrow_ref[e] + offset) * rpt, SUBLANES)
            return pltpu.make_async_copy(zero_ref.at[pl.ds(0, size * rpt), :],
                                         xs_hbm.at[pl.ds(dst, size * rpt), :], zsem)

        def pad_start(e, carry):
            _for_set_bits(padlen_ref[e], pad_bit, lambda o, s: pad_copy(e, o, s).start())
            return carry

        def pad_wait(e, carry):
            _for_set_bits(padlen_ref[e], pad_bit, lambda o, s: pad_copy(e, o, s).wait())
            return carry
        lax.fori_loop(0, n_exp, pad_start, 0)

        def tail_copy(i):
            dst = pl.multiple_of(i * (ROW_TILE * rpt), SUBLANES)
            return pltpu.make_async_copy(zero_ref, xs_hbm.at[pl.ds(dst, ROW_TILE * rpt), :], zsem)

        def tail_start(i, carry):
            tail_copy(i).start()
            return carry

        def tail_wait(i, carry):
            tail_copy(i).wait()
            return carry
        n_tiles = xs_hbm.shape[0] // (ROW_TILE * rpt)
        lax.fori_loop(nused_ref[0], n_tiles, tail_start, 0)
        lax.fori_loop(0, n_exp, pad_wait, 0)
        lax.fori_loop(nused_ref[0], n_tiles, tail_wait, 0)


def _dispatch(lidx, nbe, lb, hrow, padrow, padlen, n_used, h2p, n_rows, d_model):
    tb = TOK_BLOCK
    rpt = d_model // PACK // LANES
    n_tok = h2p.shape[0] // rpt
    n_exp = padrow.shape[0]
    slot_rows = (tb * TOP_K + n_exp * (STRIP_ALIGN - 1)) * rpt
    grid_spec = pltpu.PrefetchScalarGridSpec(
        num_scalar_prefetch=7,
        grid=(n_tok // tb,),
        in_specs=[pl.BlockSpec((tb * rpt, LANES), lambda i, *_: (i, 0))],
        out_specs=pl.BlockSpec(memory_space=pl.ANY),
        scratch_shapes=[pltpu.VMEM((2 * slot_rows, LANES), jnp.uint32),
                        pltpu.VMEM((ROW_TILE * rpt, LANES), jnp.uint32),
                        pltpu.SemaphoreType.DMA((2,)),
                        pltpu.SemaphoreType.DMA],
    )
    return pl.pallas_call(
        functools.partial(_dispatch_kernel, nblk=n_tok // tb),
        grid_spec=grid_spec,
        out_shape=jax.ShapeDtypeStruct((n_rows * rpt, LANES), jnp.uint32),
        compiler_params=pltpu.CompilerParams(
            dimension_semantics=("arbitrary",),
            vmem_limit_bytes=VMEM_LIMIT_BYTES),
        name="dispatch",
    )(lidx, nbe, lb, hrow, padrow, padlen, n_used, h2p)


def _experts_kernel(tstart_ref, tcount_ref, wgu_ref, bgu_ref, wd_ref, bd_ref, xs_hbm,
                    y_hbm, wgu_b, wd_b, xbuf, ybuf, xsem, ysem, *, n_exp):
    e = pl.program_id(0)
    d_ff = wd_ref.shape[1]
    d_model = wd_ref.shape[2]
    rpt = d_model // PACK // LANES
    tile_rows = xbuf.shape[1]
    tm = tile_rows // rpt
    n_tiles = y_hbm.shape[0] // tile_rows
    t0 = tstart_ref[e]
    nt = tcount_ref[e]

    def hbm_tile(ref, tile):
        return ref.at[pl.ds(pl.multiple_of(tile * tile_rows, SUBLANES), tile_rows), :]

    def x_copy(j, sl):
        return pltpu.make_async_copy(hbm_tile(xs_hbm, t0 + j), xbuf.at[sl], xsem.at[sl])

    def y_copy(tile, sl):
        return pltpu.make_async_copy(ybuf.at[sl], hbm_tile(y_hbm, tile), ysem.at[sl])

    @pl.when(nt > 0)
    def _():
        x_copy(0, 0).start()
        wgu_b[...] = wgu_ref[0].astype(jnp.bfloat16)
        wd_b[...] = wd_ref[0].astype(jnp.bfloat16)

        def tile_step(j, carry):
            sl = j % 2

            @pl.when(j + 1 < nt)
            def _():
                x_copy(j + 1, 1 - sl).start()
            x_copy(j, sl).wait()

            @pl.when(j >= 2)
            def _():
                y_copy(t0 + j - 2, sl).wait()
            x = _unpack_rows(xbuf.at[sl], tm, rpt)
            gu = jnp.dot(x, wgu_b[...], preferred_element_type=jnp.float32) + bgu_ref[0]
            x_glu = jnp.minimum(gu[:, :d_ff], SWIGLU_LIMIT)
            x_lin = jnp.clip(gu[:, d_ff:], -SWIGLU_LIMIT, SWIGLU_LIMIT)
            act = x_glu * (1.0 / (1.0 + jnp.exp(-SWIGLU_ALPHA * x_glu))) * (x_lin + 1.0)
            y = jnp.dot(act.astype(jnp.bfloat16), wd_b[...], preferred_element_type=jnp.float32) + bd_ref[0]
            _pack_rows(y, ybuf.at[sl], tm)
            y_copy(t0 + j, sl).start()
            return carry
        lax.fori_loop(0, nt, tile_step, 0)

        @pl.when(nt >= 2)
        def _():
            y_copy(t0 + nt - 2, nt % 2).wait()
        y_copy(t0 + nt - 1, (nt - 1) % 2).wait()

    @pl.when(e == n_exp - 1)
    def _():
        ybuf[0] = jnp.zeros(ybuf.shape[1:], ybuf.dtype)

        def tail_start(i, carry):
            y_copy(i, 0).start()
            return carry

        def tail_wait(i, carry):
            y_copy(i, 0).wait()
            return carry
        lax.fori_loop(t0 + nt, n_tiles, tail_start, 0)
        lax.fori_loop(t0 + nt, n_tiles, tail_wait, 0)


def _experts(tile_start, tile_count, xs, w_gu, b_gu, w_down, b_down):
    tm = ROW_TILE
    n_exp, d_model, d_gu = w_gu.shape
    d_ff = w_down.shape[1]
    rpt = d_model // PACK // LANES
    grid_spec = pltpu.PrefetchScalarGridSpec(
        num_scalar_prefetch=2,
        grid=(n_exp,),
        in_specs=[
            pl.BlockSpec((1, d_model, d_gu), lambda e, *_: (e, 0, 0)),
            pl.BlockSpec((1, 1, d_gu), lambda e, *_: (e, 0, 0)),
            pl.BlockSpec((1, d_ff, d_model), lambda e, *_: (e, 0, 0)),
            pl.BlockSpec((1, 1, d_model), lambda e, *_: (e, 0, 0)),
            pl.BlockSpec(memory_space=pl.ANY),
        ],
        out_specs=pl.BlockSpec(memory_space=pl.ANY),
        scratch_shapes=[pltpu.VMEM((d_model, d_gu), jnp.bfloat16),
                        pltpu.VMEM((d_ff, d_model), jnp.bfloat16),
                        pltpu.VMEM((2, tm * rpt, LANES), jnp.uint32),
                        pltpu.VMEM((2, tm * rpt, LANES), jnp.uint32),
                        pltpu.SemaphoreType.DMA((2,)),
                        pltpu.SemaphoreType.DMA((2,))],
    )
    return pl.pallas_call(
        functools.partial(_experts_kernel, n_exp=n_exp),
        grid_spec=grid_spec,
        out_shape=jax.ShapeDtypeStruct(xs.shape, jnp.uint32),
        compiler_params=pltpu.CompilerParams(
            dimension_semantics=("arbitrary",),
            vmem_limit_bytes=VMEM_LIMIT_BYTES),
        name="experts",
    )(tile_start, tile_count, w_gu, b_gu.reshape(n_exp, 1, d_gu), w_down,
      b_down.reshape(n_exp, 1, d_model), xs)


def _combine_kernel(lidx_ref, gate_ref, nbe_ref, lb_ref, hrow_ref,
                    x1_ref, nw_ref, y_hbm, out_ref, stage_ref, alo_ref, ahi_ref, sem, *, nblk):
    b = pl.program_id(0)
    tb = x1_ref.shape[0]
    rpt = alo_ref.shape[0] // tb
    n_exp = nbe_ref.shape[0] // nblk
    slot_rows = stage_ref.shape[0] // 2
    max_bit = tb.bit_length() - 1
    slot = b % 2

    def strip_copy(blk, sl, e, offset, size):
        src = pl.multiple_of((hrow_ref[blk * n_exp + e] + offset) * rpt, SUBLANES)
        dst = pl.multiple_of(sl * slot_rows + (lb_ref[blk * n_exp + e] + offset) * rpt, SUBLANES)
        return pltpu.make_async_copy(y_hbm.at[pl.ds(src, size * rpt), :],
                                     stage_ref.at[pl.ds(dst, size * rpt), :], sem.at[sl])

    def start_strips(blk, sl):
        _for_strips(nbe_ref, blk, n_exp, max_bit,
                    lambda e, offset, size: strip_copy(blk, sl, e, offset, size).start())

    @pl.when(b == 0)
    def _():
        start_strips(0, 0)

    @pl.when(b + 1 < nblk)
    def _():
        start_strips(b + 1, 1 - slot)

    _for_strips(nbe_ref, b, n_exp, max_bit,
                lambda e, offset, size: strip_copy(b, slot, e, offset, size).wait())

    def gather_block(static_slot):
        def gather_group(c, carry):
            tok0 = c * GATHER_UNROLL
            idx0 = (b * tb + tok0) * TOP_K
            for u in range(GATHER_UNROLL):
                lo = hi = None
                for k in range(TOP_K):
                    src = pl.multiple_of(lidx_ref[idx0 + u * TOP_K + k], rpt)
                    g = gate_ref[idx0 + u * TOP_K + k]
                    l, h = _unpack_words(stage_ref[pl.ds(static_slot * slot_rows + src, rpt), :])
                    lo = g * l if lo is None else lo + g * l
                    hi = g * h if hi is None else hi + g * h
                dst = pl.multiple_of((tok0 + u) * rpt, rpt)
                alo_ref[pl.ds(dst, rpt), :] = lo
                ahi_ref[pl.ds(dst, rpt), :] = hi
            return carry
        lax.fori_loop(0, tb // GATHER_UNROLL, gather_group, 0)

    for static_slot in range(2):
        pl.when(slot == static_slot)(functools.partial(gather_block, static_slot))

    parts = [alo_ref[pl.ds(si, tb, stride=rpt), :] for si in range(rpt)]
    parts += [ahi_ref[pl.ds(si, tb, stride=rpt), :] for si in range(rpt)]
    x2 = x1_ref[...] + jnp.concatenate(parts, axis=-1)
    out_ref[...] = _rms(x2, nw_ref[...])


def _combine(lidx, gates, nbe, lb, hrow, x1, norm_w, y_rows):
    tb = TOK_BLOCK
    n_tok, d_model = x1.shape
    rpt = d_model // PACK // LANES
    n_exp = nbe.shape[0] // (n_tok // tb)
    slot_rows = (tb * TOP_K + n_exp * (STRIP_ALIGN - 1)) * rpt
    grid_spec = pltpu.PrefetchScalarGridSpec(
        num_scalar_prefetch=5,
        grid=(n_tok // tb,),
        in_specs=[
            pl.BlockSpec((tb, d_model), lambda i, *_: (i, 0)),
            pl.BlockSpec((1, d_model), lambda i, *_: (0, 0)),
            pl.BlockSpec(memory_space=pl.ANY),
        ],
        out_specs=pl.BlockSpec((tb, d_model), lambda i, *_: (i, 0)),
        scratch_shapes=[pltpu.VMEM((2 * slot_rows, LANES), jnp.uint32),
                        pltpu.VMEM((tb * rpt, LANES), jnp.float32),
                        pltpu.VMEM((tb * rpt, LANES), jnp.float32),
                        pltpu.SemaphoreType.DMA((2,))],
    )
    return pl.pallas_call(
        functools.partial(_combine_kernel, nblk=n_tok // tb),
        grid_spec=grid_spec,
        out_shape=jax.ShapeDtypeStruct((n_tok, d_model), jnp.float32),
        compiler_params=pltpu.CompilerParams(
            dimension_semantics=("arbitrary",),
            vmem_limit_bytes=VMEM_LIMIT_BYTES),
        name="combine",
    )(lidx, gates, nbe, lb, hrow, x1, norm_w, y_rows)


def _retention_tables(seq, hd, ts):
    half = hd // 2
    inv = ROPE_BASE ** (-jnp.arange(half, dtype=jnp.float32) / half)
    ang = jnp.arange(seq, dtype=jnp.float32)[:, None] * inv[None, :]
    cos, sin = jnp.cos(ang), jnp.sin(ang)
    cos2 = jnp.concatenate([cos, cos], axis=-1)
    sin2 = jnp.concatenate([-sin, sin], axis=-1)
    log_g = jnp.log1p(-jnp.exp2(-5.0 - jnp.arange(RET_HEADS, dtype=jnp.float32)))
    pos = jnp.arange(ts, dtype=jnp.float32)
    dist = jnp.abs(pos[:, None] - pos[None, :])
    chunk = jnp.arange(ts, dtype=jnp.int32) // CHUNK
    visible = chunk[None, :] <= chunk[:, None]
    dmat = jnp.where(visible[None], jnp.exp(log_g[:, None, None] * dist[None]), 0.0)
    qdec = jnp.exp(log_g[:, None] * (pos + 1.0)[None, :])
    kdec = jnp.exp(log_g[:, None] * (ts - 1.0 - pos)[None, :])
    qdec = jnp.broadcast_to(qdec[:, :, None], (RET_HEADS, ts, hd))
    kdec = jnp.broadcast_to(kdec[:, :, None], (RET_HEADS, ts, hd))
    tile_decay = tuple(math.exp(math.log1p(-2.0 ** (-5.0 - h)) * ts) for h in range(RET_HEADS))
    return cos2, sin2, dmat, qdec, kdec, tile_decay


def _routing_tables(eid, rank, before, counts, n_tok, n_exp, rows_per_tok):
    tm, tb, ts = ROW_TILE, TOK_BLOCK, SEQ_TILE
    cb = before[::tb // ts]
    nbe = jnp.concatenate([cb[1:], counts[None]], axis=0) - cb
    nal = nbe + (nbe & (STRIP_ALIGN - 1))
    region = jnp.sum(nal, axis=0)
    tiles_per = (region + tm - 1) // tm
    tile_end = jnp.cumsum(tiles_per)
    offs = (tile_end - tiles_per) * tm
    n_tiles = (n_tok * TOP_K + (n_tok // tb) * n_exp * (STRIP_ALIGN - 1)) // tm + n_exp
    hrow = offs[None, :] + jnp.cumsum(nal, axis=0) - nal
    lb = jnp.cumsum(nal, axis=1) - nal
    delta = jnp.repeat(lb - cb, tb, axis=0)
    onehot = eid[:, :, None] == jnp.arange(n_exp, dtype=jnp.int32)
    lidx = rank + jnp.sum(jnp.where(onehot, delta[None], 0), axis=-1)
    lidx = lidx.T * rows_per_tok
    padrow = offs + region
    padlen = tiles_per * tm - region
    as_i32 = lambda a: a.reshape(-1).astype(jnp.int32)
    return (n_tiles, as_i32(tile_end - tiles_per), as_i32(tiles_per), as_i32(tile_end[-1:]),
            as_i32(lidx), as_i32(nbe), as_i32(lb), as_i32(hrow), as_i32(padrow), as_i32(padlen))


def _layer(x, norm_mix_w, w_in, ret_norm_w, w_pool, pool_scale, w_out, norm_ffn_w,
           w_router, b_router, w_gu, b_gu, w_down, b_down, final_w):
    bsz, seq, d_model = x.shape
    n_tok = bsz * seq
    n_exp = w_router.shape[1]
    rw = ret_norm_w.shape[0]
    hd = rw // RET_HEADS
    ts = SEQ_TILE
    bf = jnp.bfloat16
    rpt = d_model // PACK // LANES
    assert d_model % (PACK * LANES) == 0 and (STRIP_ALIGN * rpt) % SUBLANES == 0
    assert seq % ts == 0 and ts % CHUNK == 0 and n_tok % TOK_BLOCK == 0 and TOK_BLOCK % ts == 0
    assert TOK_BLOCK & (TOK_BLOCK - 1) == 0 and ROW_TILE & (ROW_TILE - 1) == 0

    cos2, sin2, dmat, qdec, kdec, tile_decay = _retention_tables(seq, hd, ts)
    tri = jnp.arange(ts, dtype=jnp.int32)
    utri = (tri[:, None] < tri[None, :]).astype(bf)

    x1, h2p, eid, gate, rank, before, cnt = _mix_route(
        x, norm_mix_w[None], w_in.astype(bf), cos2, sin2, dmat, qdec, kdec, ret_norm_w[None],
        w_pool.astype(bf), pool_scale[None], w_out.astype(bf), norm_ffn_w[None],
        w_router.T.astype(bf), b_router[:, None], utri, tile_decay)

    (n_tiles, tile_start, tile_count, n_used, lidx, nbe, lb, hrow, padrow, padlen) = _routing_tables(
        eid, rank, before[:, :, 0].astype(jnp.int32), cnt[:, 0].astype(jnp.int32), n_tok, n_exp, rpt)

    xs = _dispatch(lidx, nbe, lb, hrow, padrow, padlen, n_used, h2p, n_tiles * ROW_TILE, d_model)
    y_rows = _experts(tile_start, tile_count, xs, w_gu, b_gu, w_down, b_down)
    out = _combine(lidx, gate.T.reshape(-1), nbe, lb, hrow, x1.reshape(n_tok, d_model), final_w[None], y_rows)
    return out.reshape(bsz, seq, d_model)


def kernel(x, norm_mix_w, w_in, ret_norm_w, w_pool, pool_scale, w_out, norm_ffn_w, w_router, b_router,
           w_gu, b_gu, w_down, b_down, norm_final_w):
    depth = norm_mix_w.shape[0]
    for l in range(depth):
        last = l == depth - 1
        assert last, "stacked layers need an un-normalised combine output"
        x = _layer(x, norm_mix_w[l], w_in[l], ret_norm_w[l], w_pool[l], pool_scale[l], w_out[l],
                   norm_ffn_w[l], w_router[l], b_router[l], w_gu[l], b_gu[l], w_down[l], b_down[l],
                   norm_final_w)
    return x
```

```python
import functools
import math

import jax
import jax.numpy as jnp
from jax import lax
from jax.experimental import pallas as pl
from jax.experimental.pallas import tpu as pltpu

CHUNK = 64
RET_HEADS = 4
POOL_WINDOWS = (2, 4, 8, 16)
ROPE_BASE = 10000.0
TOP_K = 4
SWIGLU_LIMIT = 7.0
SWIGLU_ALPHA = 1.702
EPS = 1e-6

LANES = 128
SUBLANES = 8
VMEM_LIMIT_BYTES = 56 * 1024 * 1024

SEQ_TILE = 256
ROW_GRANULE = 128
ROW_TILE = 512
TOK_BLOCK = 1024
POOL_HALO = 16
PACK = 2
STRIP_ALIGN = 2
GATHER_UNROLL = 8
RARE_STRIP_BIT = 8


def _rms(x, w):
    return x * lax.rsqrt(jnp.mean(x * x, axis=-1, keepdims=True) + EPS) * w


def _bf16_bits_hi(x):
    return lax.bitcast_convert_type(x.astype(jnp.bfloat16).astype(jnp.float32), jnp.uint32)


def _pack_rows(x, out_ref, n_rows):
    half = x.shape[1] // PACK
    packed = (_bf16_bits_hi(x[:, :half]) >> 16) | _bf16_bits_hi(x[:, half:])
    rpt = half // LANES
    for si in range(rpt):
        out_ref[pl.ds(si, n_rows, stride=rpt), :] = packed[:, si * LANES:(si + 1) * LANES]


def _unpack_words(u):
    lo = lax.bitcast_convert_type(u << 16, jnp.float32)
    hi = lax.bitcast_convert_type(u & jnp.uint32(0xFFFF0000), jnp.float32)
    return lo, hi


def _unpack_rows(g_ref, n_rows, rpt):
    lo, hi = [], []
    for si in range(rpt):
        l, h = _unpack_words(g_ref[pl.ds(si, n_rows, stride=rpt), :])
        lo.append(l.astype(jnp.bfloat16))
        hi.append(h.astype(jnp.bfloat16))
    return jnp.concatenate(lo + hi, axis=-1)


def _mix_route_kernel(x_ref, nmw_ref, win_ref, cos_ref, sin_ref, dmat_ref, qdec_ref, kdec_ref,
                      rnw_ref, wpool_ref, pscale_ref, wout_ref, nfw_ref, wrt_ref, brt_ref, utri_ref,
                      x1_ref, h2p_ref, eid_ref, gate_ref, rank_ref, before_ref, cnt_ref,
                      state_ref, p1_ref, p2_ref, p4_ref, p8_ref, carry_ref,
                      *, tile_decay):
    b = pl.program_id(0)
    j = pl.program_id(1)
    ts = x_ref.shape[1]
    n_heads = RET_HEADS
    rw = rnw_ref.shape[1]
    hd = rw // n_heads
    pgw = wpool_ref.shape[1]
    n_exp = wrt_ref.shape[0]
    halo = POOL_HALO

    @pl.when(jnp.logical_and(b == 0, j == 0))
    def _():
        carry_ref[...] = jnp.zeros_like(carry_ref)

    @pl.when(j == 0)
    def _():
        state_ref[...] = jnp.zeros_like(state_ref)
        for r in (p1_ref, p2_ref, p4_ref, p8_ref):
            r[0:halo, :] = jnp.zeros((halo, r.shape[1]), jnp.float32)

    @pl.when(j > 0)
    def _():
        for r in (p1_ref, p2_ref, p4_ref, p8_ref):
            r[0:halo, :] = r[ts:ts + halo, :]

    x = x_ref[0]
    h = _rms(x, nmw_ref[...]).astype(jnp.bfloat16)
    z = jnp.dot(h, win_ref[...], preferred_element_type=jnp.float32)

    cos = cos_ref[...]
    sin = sin_ref[...]
    k_scale = hd ** -0.5

    ret_parts = []
    for hh in range(n_heads):
        q = z[:, hh * hd:(hh + 1) * hd]
        k = z[:, rw + hh * hd:rw + (hh + 1) * hd]
        v = z[:, 2 * rw + hh * hd:2 * rw + (hh + 1) * hd]
        g = z[:, 3 * rw + hh * hd:3 * rw + (hh + 1) * hd]
        q = q * cos + pltpu.roll(q, hd // 2, 1) * sin
        k = (k * cos + pltpu.roll(k, hd // 2, 1) * sin) * k_scale
        qb = q.astype(jnp.bfloat16)
        kb = k.astype(jnp.bfloat16)
        vb = v.astype(jnp.bfloat16)
        s = lax.dot_general(qb, kb, (((1,), (1,)), ((), ())), preferred_element_type=jnp.float32)
        pmat = (s * dmat_ref[hh]).astype(jnp.bfloat16)
        o = jnp.dot(pmat, vb, preferred_element_type=jnp.float32)
        state = state_ref[hh]
        qd = (q * qdec_ref[hh]).astype(jnp.bfloat16)
        o = o + jnp.dot(qd, state.astype(jnp.bfloat16), preferred_element_type=jnp.float32)
        kd = (k * kdec_ref[hh]).astype(jnp.bfloat16)
        u = lax.dot_general(kd, vb, (((0,), (0,)), ((), ())), preferred_element_type=jnp.float32)
        state_ref[hh] = state * tile_decay[hh] + u
        o = o * lax.rsqrt(jnp.mean(o * o, axis=-1, keepdims=True) + EPS)
        o = o * rnw_ref[:, hh * hd:(hh + 1) * hd] * (g * (1.0 / (1.0 + jnp.exp(-g))))
        ret_parts.append(o.astype(jnp.bfloat16))

    p = z[:, 4 * rw:]
    p1_ref[halo:halo + ts, :] = p
    s2 = p + p1_ref[halo - 1:halo - 1 + ts, :]
    p2_ref[halo:halo + ts, :] = s2[:, pgw:]
    s4 = s2[:, pgw:] + p2_ref[halo - 2:halo - 2 + ts, :]
    p4_ref[halo:halo + ts, :] = s4[:, pgw:]
    s8 = s4[:, pgw:] + p4_ref[halo - 4:halo - 4 + ts, :]
    p8_ref[halo:halo + ts, :] = s8[:, pgw:]
    s16 = s8[:, pgw:] + p8_ref[halo - 8:halo - 8 + ts, :]
    sums = (s2[:, :pgw], s4[:, :pgw], s8[:, :pgw], s16)
    tpos = (j * ts + lax.broadcasted_iota(jnp.int32, (ts, pgw), 0) + 1).astype(jnp.float32)
    pool_parts = []
    for gi, w in enumerate(POOL_WINDOWS):
        cnt = jnp.minimum(tpos, float(w))
        pooled = sums[gi] / cnt - p[:, gi * pgw:(gi + 1) * pgw]
        mixed = jnp.dot(pooled.astype(jnp.bfloat16), wpool_ref[gi], preferred_element_type=jnp.float32)
        mixed = mixed * pscale_ref[:, gi * pgw:(gi + 1) * pgw]
        pool_parts.append(mixed.astype(jnp.bfloat16))

    mix = jnp.concatenate(ret_parts + pool_parts, axis=-1)
    x1 = x + jnp.dot(mix, wout_ref[...], preferred_element_type=jnp.float32)
    x1_ref[0] = x1

    h2 = _rms(x1, nfw_ref[...])
    _pack_rows(h2, h2p_ref, ts)

    h2b = h2.astype(jnp.bfloat16)
    logits = lax.dot_general(wrt_ref[...], h2b, (((1,), (1,)), ((), ())),
                             preferred_element_type=jnp.float32) + brt_ref[...]
    eiota = lax.broadcasted_iota(jnp.int32, (n_exp, ts), 0)
    work = logits
    vals, ids, sels = [], [], []
    for _ in range(TOP_K):
        m = jnp.max(work, axis=0, keepdims=True)
        idx = jnp.min(jnp.where(work == m, eiota, n_exp), axis=0, keepdims=True)
        sel = eiota == idx
        vals.append(m)
        ids.append(idx)
        sels.append(sel)
        work = jnp.where(sel, -jnp.inf, work)
    ex = [jnp.exp(vk - vals[0]) for vk in vals]
    den = ex[0] + ex[1] + ex[2] + ex[3]
    gate_ref[...] = jnp.concatenate([e / den for e in ex], axis=0)
    eid_ref[...] = jnp.concatenate(ids, axis=0)

    member = jnp.zeros((n_exp, ts), jnp.float32)
    for sel in sels:
        member = member + sel.astype(jnp.float32)
    before_ref[0] = carry_ref[...]
    before = jnp.dot(member.astype(jnp.bfloat16), utri_ref[...], preferred_element_type=jnp.float32)
    before = before + carry_ref[:, 0:1]
    ranks = [jnp.sum(jnp.where(sel, before, 0.0), axis=0, keepdims=True) for sel in sels]
    rank_ref[...] = jnp.concatenate(ranks, axis=0).astype(jnp.int32)
    new_carry = carry_ref[...] + jnp.sum(member, axis=1, keepdims=True)
    carry_ref[...] = new_carry
    cnt_ref[...] = new_carry


def _mix_route(x, nmw, win_b, cos2, sin2, dmat, qdec, kdec, rnw, wpool_b, pscale, wout_b, nfw,
               wrt_b, brt, utri, tile_decay):
    bsz, seq, d_model = x.shape
    ts = SEQ_TILE
    n_tok = bsz * seq
    n_exp = wrt_b.shape[0]
    rw = rnw.shape[1]
    hd = rw // RET_HEADS
    pw = pscale.shape[1]
    pgw = pw // len(POOL_WINDOWS)
    nj = seq // ts
    rows_per_tok = d_model // PACK // LANES

    def const(shape):
        return pl.BlockSpec(shape, lambda b, j: (0,) * len(shape))

    in_specs = [
        pl.BlockSpec((1, ts, d_model), lambda b, j: (b, j, 0)),
        const(nmw.shape), const(win_b.shape),
        pl.BlockSpec((ts, hd), lambda b, j: (j, 0)),
        pl.BlockSpec((ts, hd), lambda b, j: (j, 0)),
        const(dmat.shape), const(qdec.shape), const(kdec.shape),
        const(rnw.shape), const(wpool_b.shape), const(pscale.shape), const(wout_b.shape),
        const(nfw.shape), const(wrt_b.shape), const(brt.shape), const(utri.shape),
    ]
    out_shape = [
        jax.ShapeDtypeStruct((bsz, seq, d_model), jnp.float32),
        jax.ShapeDtypeStruct((n_tok * rows_per_tok, LANES), jnp.uint32),
        jax.ShapeDtypeStruct((TOP_K, n_tok), jnp.int32),
        jax.ShapeDtypeStruct((TOP_K, n_tok), jnp.float32),
        jax.ShapeDtypeStruct((TOP_K, n_tok), jnp.int32),
        jax.ShapeDtypeStruct((n_tok // ts, n_exp, LANES), jnp.float32),
        jax.ShapeDtypeStruct((n_exp, LANES), jnp.float32),
    ]
    out_specs = [
        pl.BlockSpec((1, ts, d_model), lambda b, j: (b, j, 0)),
        pl.BlockSpec((ts * rows_per_tok, LANES), lambda b, j: (b * nj + j, 0)),
        pl.BlockSpec((TOP_K, ts), lambda b, j: (0, b * nj + j)),
        pl.BlockSpec((TOP_K, ts), lambda b, j: (0, b * nj + j)),
        pl.BlockSpec((TOP_K, ts), lambda b, j: (0, b * nj + j)),
        pl.BlockSpec((1, n_exp, LANES), lambda b, j: (b * nj + j, 0, 0)),
        pl.BlockSpec((n_exp, LANES), lambda b, j: (0, 0)),
    ]
    scratch = [
        pltpu.VMEM((RET_HEADS, hd, hd), jnp.float32),
        pltpu.VMEM((POOL_HALO + ts, pw), jnp.float32),
        pltpu.VMEM((POOL_HALO + ts, pw - pgw), jnp.float32),
        pltpu.VMEM((POOL_HALO + ts, pw - 2 * pgw), jnp.float32),
        pltpu.VMEM((POOL_HALO + ts, pw - 3 * pgw), jnp.float32),
        pltpu.VMEM((n_exp, LANES), jnp.float32),
    ]
    return pl.pallas_call(
        functools.partial(_mix_route_kernel, tile_decay=tile_decay),
        grid=(bsz, nj),
        in_specs=in_specs,
        out_specs=out_specs,
        out_shape=out_shape,
        scratch_shapes=scratch,
        compiler_params=pltpu.CompilerParams(
            dimension_semantics=("arbitrary", "arbitrary"),
            vmem_limit_bytes=VMEM_LIMIT_BYTES),
        name="mix_route",
    )(x, nmw, win_b, cos2, sin2, dmat, qdec, kdec, rnw, wpool_b, pscale, wout_b, nfw, wrt_b, brt, utri)


def _for_set_bits(n, max_bit, fn):
    def piece(bit):
        size = 1 << bit
        offset = (n >> (bit + 1)) << (bit + 1)

        @pl.when((n & size) != 0)
        def _():
            fn(offset, size)

    min_bit = STRIP_ALIGN.bit_length() - 1
    rare_bit = min(max_bit + 1, RARE_STRIP_BIT)

    @pl.when(n >= (1 << rare_bit))
    def _():
        for bit in range(max_bit, rare_bit - 1, -1):
            piece(bit)
    for bit in range(rare_bit - 1, min_bit - 1, -1):
        piece(bit)


def _aligned_len(n):
    return n + (n & (STRIP_ALIGN - 1))


def _for_strips(nbe_ref, blk, n_exp, max_bit, fn):
    def body(e, carry):
        n = _aligned_len(nbe_ref[blk * n_exp + e])
        _for_set_bits(n, max_bit, functools.partial(fn, e))
        return carry
    lax.fori_loop(0, n_exp, body, 0)


def _dispatch_kernel(lidx_ref, nbe_ref, lb_ref, hrow_ref, padrow_ref, padlen_ref, nused_ref,
                     h2p_ref, xs_hbm, stage_ref, zero_ref, sem, zsem, *, nblk):
    b = pl.program_id(0)
    rpt = zero_ref.shape[0] // ROW_GRANULE
    tb = h2p_ref.shape[0] // rpt
    n_exp = padrow_ref.shape[0]
    slot_rows = stage_ref.shape[0] // 2
    max_bit = tb.bit_length() - 1
    slot = b % 2

    def strip_copy(blk, sl, e, offset, size):
        src = pl.multiple_of(sl * slot_rows + (lb_ref[blk * n_exp + e] + offset) * rpt, SUBLANES)
        dst = pl.multiple_of((hrow_ref[blk * n_exp + e] + offset) * rpt, SUBLANES)
        return pltpu.make_async_copy(stage_ref.at[pl.ds(src, size * rpt), :],
                                     xs_hbm.at[pl.ds(dst, size * rpt), :], sem.at[sl])

    def start_strips(blk, sl):
        _for_strips(nbe_ref, blk, n_exp, max_bit,
                    lambda e, offset, size: strip_copy(blk, sl, e, offset, size).start())

    def wait_strips(blk, sl):
        _for_strips(nbe_ref, blk, n_exp, max_bit,
                    lambda e, offset, size: strip_copy(blk, sl, e, offset, size).wait())

    @pl.when(b >= 2)
    def _():
        wait_strips(b - 2, slot)

    def scatter_block(static_slot):
        def scatter_group(c, carry):
            tok0 = c * GATHER_UNROLL
            idx0 = (b * tb + tok0) * TOP_K
            for u in range(GATHER_UNROLL):
                row = h2p_ref[pl.ds(pl.multiple_of((tok0 + u) * rpt, rpt), rpt), :]
                for k in range(TOP_K):
                    dst = pl.multiple_of(lidx_ref[idx0 + u * TOP_K + k], rpt)
                    stage_ref[pl.ds(static_slot * slot_rows + dst, rpt), :] = row
            return carry
        lax.fori_loop(0, tb // GATHER_UNROLL, scatter_group, 0)

    for static_slot in range(2):
        pl.when(slot == static_slot)(functools.partial(scatter_block, static_slot))

    def zero_tail(e, carry):
        n = nbe_ref[b * n_exp + e]

        @pl.when((n & (STRIP_ALIGN - 1)) != 0)
        def _():
            dst = pl.multiple_of(slot * slot_rows + (lb_ref[b * n_exp + e] + n) * rpt, rpt)
            stage_ref[pl.ds(dst, rpt), :] = jnp.zeros((rpt, LANES), jnp.uint32)
        return carry
    lax.fori_loop(0, n_exp, zero_tail, 0)

    start_strips(b, slot)

    @pl.when(b == nblk - 1)
    def _():
        if nblk >= 2:
            wait_strips(b - 1, 1 - slot)
        wait_strips(b, slot)
        zero_ref[...] = jnp.zeros_like(zero_ref)
        pad_bit = ROW_GRANULE.bit_length() - 2

        def pad_copy(e, offset, size):
            dst = pl.multiple_of((padrow_ref[e] + offset) * rpt, SUBLANES)
            return pltpu.make_async_copy(zero_ref.at[pl.ds(0, size * rpt), :],
                                         xs_hbm.at[pl.ds(dst, size * rpt), :], zsem)

        def pad_start(e, carry):
            _for_set_bits(padlen_ref[e], pad_bit, lambda o, s: pad_copy(e, o, s).start())
            return carry

        def pad_wait(e, carry):
            _for_set_bits(padlen_ref[e], pad_bit, lambda o, s: pad_copy(e, o, s).wait())
            return carry
        lax.fori_loop(0, n_exp, pad_start, 0)

        def tail_copy(i):
            dst = pl.multiple_of(i * (ROW_GRANULE * rpt), SUBLANES)
            return pltpu.make_async_copy(zero_ref, xs_hbm.at[pl.ds(dst, ROW_GRANULE * rpt), :], zsem)

        def tail_start(i, carry):
            tail_copy(i).start()
            return carry

        def tail_wait(i, carry):
            tail_copy(i).wait()
            return carry
        n_gran = xs_hbm.shape[0] // (ROW_GRANULE * rpt)
        lax.fori_loop(nused_ref[0], n_gran, tail_start, 0)
        lax.fori_loop(0, n_exp, pad_wait, 0)
        lax.fori_loop(nused_ref[0], n_gran, tail_wait, 0)


def _dispatch(lidx, nbe, lb, hrow, padrow, padlen, n_used, h2p, n_rows, d_model):
    tb = TOK_BLOCK
    rpt = d_model // PACK // LANES
    n_tok = h2p.shape[0] // rpt
    n_exp = padrow.shape[0]
    slot_rows = (tb * TOP_K + n_exp * (STRIP_ALIGN - 1)) * rpt
    grid_spec = pltpu.PrefetchScalarGridSpec(
        num_scalar_prefetch=7,
        grid=(n_tok // tb,),
        in_specs=[pl.BlockSpec((tb * rpt, LANES), lambda i, *_: (i, 0))],
        out_specs=pl.BlockSpec(memory_space=pl.ANY),
        scratch_shapes=[pltpu.VMEM((2 * slot_rows, LANES), jnp.uint32),
                        pltpu.VMEM((ROW_GRANULE * rpt, LANES), jnp.uint32),
                        pltpu.SemaphoreType.DMA((2,)),
                        pltpu.SemaphoreType.DMA],
    )
    return pl.pallas_call(
        functools.partial(_dispatch_kernel, nblk=n_tok // tb),
        grid_spec=grid_spec,
        out_shape=jax.ShapeDtypeStruct((n_rows * rpt, LANES), jnp.uint32),
        compiler_params=pltpu.CompilerParams(
            dimension_semantics=("arbitrary",),
            vmem_limit_bytes=VMEM_LIMIT_BYTES),
        name="dispatch",
    )(lidx, nbe, lb, hrow, padrow, padlen, n_used, h2p)


def _experts_kernel(tstart_ref, tcount_ref, wgu_ref, bgu_ref, wd_ref, bd_ref, xs_hbm,
                    y_hbm, wgu_b, wd_b, xbuf, ybuf, xsem, ysem, *, n_exp):
    e = pl.program_id(0)
    d_ff = wd_ref.shape[1]
    d_model = wd_ref.shape[2]
    rpt = d_model // PACK // LANES
    gran_rows = ROW_GRANULE * rpt
    gpt = ROW_TILE // ROW_GRANULE
    n_gran = y_hbm.shape[0] // gran_rows
    g0 = tstart_ref[e]
    ng = tcount_ref[e]
    nt = ng // gpt
    tail = ng - nt * gpt

    def hbm_rows(ref, gran, count):
        return ref.at[pl.ds(pl.multiple_of(gran * gran_rows, SUBLANES), count * gran_rows), :]

    def x_copy(gran, sl, count=gpt):
        return pltpu.make_async_copy(hbm_rows(xs_hbm, gran, count),
                                     xbuf.at[sl, pl.ds(0, count * gran_rows), :], xsem.at[sl])

    def y_copy(gran, sl, count=gpt):
        return pltpu.make_async_copy(ybuf.at[sl, pl.ds(0, count * gran_rows), :],
                                     hbm_rows(y_hbm, gran, count), ysem.at[sl])

    def mlp(sl, tm):
        x = _unpack_rows(xbuf.at[sl], tm, rpt)
        gu = jnp.dot(x, wgu_b[...], preferred_element_type=jnp.float32) + bgu_ref[0]
        x_glu = jnp.minimum(gu[:, :d_ff], SWIGLU_LIMIT)
        x_lin = jnp.clip(gu[:, d_ff:], -SWIGLU_LIMIT, SWIGLU_LIMIT)
        act = x_glu * (1.0 / (1.0 + jnp.exp(-SWIGLU_ALPHA * x_glu))) * (x_lin + 1.0)
        y = jnp.dot(act.astype(jnp.bfloat16), wd_b[...], preferred_element_type=jnp.float32) + bd_ref[0]
        _pack_rows(y, ybuf.at[sl], tm)

    @pl.when(ng > 0)
    def _():
        @pl.when(nt > 0)
        def _():
            x_copy(g0, 0).start()
        wgu_b[...] = wgu_ref[0].astype(jnp.bfloat16)
        wd_b[...] = wd_ref[0].astype(jnp.bfloat16)

        def tile_step(j, carry):
            sl = j % 2
            gran = g0 + j * gpt

            @pl.when(j + 1 < nt)
            def _():
                x_copy(gran + gpt, 1 - sl).start()
            x_copy(gran, sl).wait()

            @pl.when(j >= 2)
            def _():
                y_copy(gran - 2 * gpt, sl).wait()
            mlp(sl, ROW_TILE)
            y_copy(gran, sl).start()
            return carry
        lax.fori_loop(0, nt, tile_step, 0)

        @pl.when(nt >= 2)
        def _():
            y_copy(g0 + (nt - 2) * gpt, nt % 2).wait()

        @pl.when(nt >= 1)
        def _():
            y_copy(g0 + (nt - 1) * gpt, (nt - 1) % 2).wait()

        for count in range(1, gpt):
            @pl.when(tail == count)
            def _():
                gran = g0 + nt * gpt
                x_copy(gran, 0, count).start()
                x_copy(gran, 0, count).wait()
                mlp(0, count * ROW_GRANULE)
                y_copy(gran, 0, count).start()
                y_copy(gran, 0, count).wait()

    @pl.when(e == n_exp - 1)
    def _():
        ybuf[0, 0:gran_rows, :] = jnp.zeros((gran_rows, LANES), ybuf.dtype)

        def tail_start(i, carry):
            y_copy(i, 0, 1).start()
            return carry

        def tail_wait(i, carry):
            y_copy(i, 0, 1).wait()
            return carry
        lax.fori_loop(g0 + ng, n_gran, tail_start, 0)
        lax.fori_loop(g0 + ng, n_gran, tail_wait, 0)


def _experts(tile_start, tile_count, xs, w_gu, b_gu, w_down, b_down):
    tm = ROW_TILE
    n_exp, d_model, d_gu = w_gu.shape
    d_ff = w_down.shape[1]
    rpt = d_model // PACK // LANES
    grid_spec = pltpu.PrefetchScalarGridSpec(
        num_scalar_prefetch=2,
        grid=(n_exp,),
        in_specs=[
            pl.BlockSpec((1, d_model, d_gu), lambda e, *_: (e, 0, 0)),
            pl.BlockSpec((1, 1, d_gu), lambda e, *_: (e, 0, 0)),
            pl.BlockSpec((1, d_ff, d_model), lambda e, *_: (e, 0, 0)),
            pl.BlockSpec((1, 1, d_model), lambda e, *_: (e, 0, 0)),
            pl.BlockSpec(memory_space=pl.ANY),
        ],
        out_specs=pl.BlockSpec(memory_space=pl.ANY),
        scratch_shapes=[pltpu.VMEM((d_model, d_gu), jnp.bfloat16),
                        pltpu.VMEM((d_ff, d_model), jnp.bfloat16),
                        pltpu.VMEM((2, tm * rpt, LANES), jnp.uint32),
                        pltpu.VMEM((2, tm * rpt, LANES), jnp.uint32),
                        pltpu.SemaphoreType.DMA((2,)),
                        pltpu.SemaphoreType.DMA((2,))],
    )
    return pl.pallas_call(
        functools.partial(_experts_kernel, n_exp=n_exp),
        grid_spec=grid_spec,
        out_shape=jax.ShapeDtypeStruct(xs.shape, jnp.uint32),
        compiler_params=pltpu.CompilerParams(
            dimension_semantics=("arbitrary",),
            vmem_limit_bytes=VMEM_LIMIT_BYTES),
        name="experts",
    )(tile_start, tile_count, w_gu, b_gu.reshape(n_exp, 1, d_gu), w_down,
      b_down.reshape(n_exp, 1, d_model), xs)


def _combine_kernel(lidx_ref, gate_ref, nbe_ref, lb_ref, hrow_ref,
                    x1_ref, nw_ref, y_hbm, out_ref, stage_ref, alo_ref, ahi_ref, sem, *, nblk):
    b = pl.program_id(0)
    tb = x1_ref.shape[0]
    rpt = alo_ref.shape[0] // tb
    n_exp = nbe_ref.shape[0] // nblk
    slot_rows = stage_ref.shape[0] // 2
    max_bit = tb.bit_length() - 1
    slot = b % 2

    def strip_copy(blk, sl, e, offset, size):
        src = pl.multiple_of((hrow_ref[blk * n_exp + e] + offset) * rpt, SUBLANES)
        dst = pl.multiple_of(sl * slot_rows + (lb_ref[blk * n_exp + e] + offset) * rpt, SUBLANES)
        return pltpu.make_async_copy(y_hbm.at[pl.ds(src, size * rpt), :],
                                     stage_ref.at[pl.ds(dst, size * rpt), :], sem.at[sl])

    def start_strips(blk, sl):
        _for_strips(nbe_ref, blk, n_exp, max_bit,
                    lambda e, offset, size: strip_copy(blk, sl, e, offset, size).start())

    @pl.when(b == 0)
    def _():
        start_strips(0, 0)

    @pl.when(b + 1 < nblk)
    def _():
        start_strips(b + 1, 1 - slot)

    _for_strips(nbe_ref, b, n_exp, max_bit,
                lambda e, offset, size: strip_copy(b, slot, e, offset, size).wait())

    def gather_block(static_slot):
        def gather_group(c, carry):
            tok0 = c * GATHER_UNROLL
            idx0 = (b * tb + tok0) * TOP_K
            for u in range(GATHER_UNROLL):
                lo = hi = None
                for k in range(TOP_K):
                    src = pl.multiple_of(lidx_ref[idx0 + u * TOP_K + k], rpt)
                    g = gate_ref[idx0 + u * TOP_K + k]
                    l, h = _unpack_words(stage_ref[pl.ds(static_slot * slot_rows + src, rpt), :])
                    lo = g * l if lo is None else lo + g * l
                    hi = g * h if hi is None else hi + g * h
                dst = pl.multiple_of((tok0 + u) * rpt, rpt)
                alo_ref[pl.ds(dst, rpt), :] = lo
                ahi_ref[pl.ds(dst, rpt), :] = hi
            return carry
        lax.fori_loop(0, tb // GATHER_UNROLL, gather_group, 0)

    for static_slot in range(2):
        pl.when(slot == static_slot)(functools.partial(gather_block, static_slot))

    parts = [alo_ref[pl.ds(si, tb, stride=rpt), :] for si in range(rpt)]
    parts += [ahi_ref[pl.ds(si, tb, stride=rpt), :] for si in range(rpt)]
    x2 = x1_ref[...] + jnp.concatenate(parts, axis=-1)
    out_ref[...] = _rms(x2, nw_ref[...])


def _combine(lidx, gates, nbe, lb, hrow, x1, norm_w, y_rows):
    tb = TOK_BLOCK
    n_tok, d_model = x1.shape
    rpt = d_model // PACK // LANES
    n_exp = nbe.shape[0] // (n_tok // tb)
    slot_rows = (tb * TOP_K + n_exp * (STRIP_ALIGN - 1)) * rpt
    grid_spec = pltpu.PrefetchScalarGridSpec(
        num_scalar_prefetch=5,
        grid=(n_tok // tb,),
        in_specs=[
            pl.BlockSpec((tb, d_model), lambda i, *_: (i, 0)),
            pl.BlockSpec((1, d_model), lambda i, *_: (0, 0)),
            pl.BlockSpec(memory_space=pl.ANY),
        ],
        out_specs=pl.BlockSpec((tb, d_model), lambda i, *_: (i, 0)),
        scratch_shapes=[pltpu.VMEM((2 * slot_rows, LANES), jnp.uint32),
                        pltpu.VMEM((tb * rpt, LANES), jnp.float32),
                        pltpu.VMEM((tb * rpt, LANES), jnp.float32),
                        pltpu.SemaphoreType.DMA((2,))],
    )
    return pl.pallas_call(
        functools.partial(_combine_kernel, nblk=n_tok // tb),
        grid_spec=grid_spec,
        out_shape=jax.ShapeDtypeStruct((n_tok, d_model), jnp.float32),
        compiler_params=pltpu.CompilerParams(
            dimension_semantics=("arbitrary",),
            vmem_limit_bytes=VMEM_LIMIT_BYTES),
        name="combine",
    )(lidx, gates, nbe, lb, hrow, x1, norm_w, y_rows)


def _retention_tables(seq, hd, ts):
    half = hd // 2
    inv = ROPE_BASE ** (-jnp.arange(half, dtype=jnp.float32) / half)
    ang = jnp.arange(seq, dtype=jnp.float32)[:, None] * inv[None, :]
    cos, sin = jnp.cos(ang), jnp.sin(ang)
    cos2 = jnp.concatenate([cos, cos], axis=-1)
    sin2 = jnp.concatenate([-sin, sin], axis=-1)
    log_g = jnp.log1p(-jnp.exp2(-5.0 - jnp.arange(RET_HEADS, dtype=jnp.float32)))
    pos = jnp.arange(ts, dtype=jnp.float32)
    dist = jnp.abs(pos[:, None] - pos[None, :])
    chunk = jnp.arange(ts, dtype=jnp.int32) // CHUNK
    visible = chunk[None, :] <= chunk[:, None]
    dmat = jnp.where(visible[None], jnp.exp(log_g[:, None, None] * dist[None]), 0.0)
    qdec = jnp.exp(log_g[:, None] * (pos + 1.0)[None, :])
    kdec = jnp.exp(log_g[:, None] * (ts - 1.0 - pos)[None, :])
    qdec = jnp.broadcast_to(qdec[:, :, None], (RET_HEADS, ts, hd))
    kdec = jnp.broadcast_to(kdec[:, :, None], (RET_HEADS, ts, hd))
    tile_decay = tuple(math.exp(math.log1p(-2.0 ** (-5.0 - h)) * ts) for h in range(RET_HEADS))
    return cos2, sin2, dmat, qdec, kdec, tile_decay


def _routing_tables(eid, rank, before, counts, n_tok, n_exp, rows_per_tok):
    tm, tb, ts = ROW_GRANULE, TOK_BLOCK, SEQ_TILE
    cb = before[::tb // ts]
    nbe = jnp.concatenate([cb[1:], counts[None]], axis=0) - cb
    nal = nbe + (nbe & (STRIP_ALIGN - 1))
    region = jnp.sum(nal, axis=0)
    tiles_per = (region + tm - 1) // tm
    tile_end = jnp.cumsum(tiles_per)
    offs = (tile_end - tiles_per) * tm
    n_tiles = (n_tok * TOP_K + (n_tok // tb) * n_exp * (STRIP_ALIGN - 1)) // tm + n_exp
    hrow = offs[None, :] + jnp.cumsum(nal, axis=0) - nal
    lb = jnp.cumsum(nal, axis=1) - nal
    delta = jnp.repeat(lb - cb, tb, axis=0)
    onehot = eid[:, :, None] == jnp.arange(n_exp, dtype=jnp.int32)
    lidx = rank + jnp.sum(jnp.where(onehot, delta[None], 0), axis=-1)
    lidx = lidx.T * rows_per_tok
    padrow = offs + region
    padlen = tiles_per * tm - region
    as_i32 = lambda a: a.reshape(-1).astype(jnp.int32)
    return (n_tiles, as_i32(tile_end - tiles_per), as_i32(tiles_per), as_i32(tile_end[-1:]),
            as_i32(lidx), as_i32(nbe), as_i32(lb), as_i32(hrow), as_i32(padrow), as_i32(padlen))


def _layer(x, norm_mix_w, w_in, ret_norm_w, w_pool, pool_scale, w_out, norm_ffn_w,
           w_router, b_router, w_gu, b_gu, w_down, b_down, final_w):
    bsz, seq, d_model = x.shape
    n_tok = bsz * seq
    n_exp = w_router.shape[1]
    rw = ret_norm_w.shape[0]
    hd = rw // RET_HEADS
    ts = SEQ_TILE
    bf = jnp.bfloat16
    rpt = d_model // PACK // LANES
    assert d_model % (PACK * LANES) == 0 and (STRIP_ALIGN * rpt) % SUBLANES == 0
    assert seq % ts == 0 and ts % CHUNK == 0 and n_tok % TOK_BLOCK == 0 and TOK_BLOCK % ts == 0
    assert TOK_BLOCK & (TOK_BLOCK - 1) == 0 and ROW_GRANULE & (ROW_GRANULE - 1) == 0
    assert ROW_TILE % ROW_GRANULE == 0

    cos2, sin2, dmat, qdec, kdec, tile_decay = _retention_tables(seq, hd, ts)
    tri = jnp.arange(ts, dtype=jnp.int32)
    utri = (tri[:, None] < tri[None, :]).astype(bf)

    x1, h2p, eid, gate, rank, before, cnt = _mix_route(
        x, norm_mix_w[None], w_in.astype(bf), cos2, sin2, dmat, qdec, kdec, ret_norm_w[None],
        w_pool.astype(bf), pool_scale[None], w_out.astype(bf), norm_ffn_w[None],
        w_router.T.astype(bf), b_router[:, None], utri, tile_decay)

    (n_tiles, tile_start, tile_count, n_used, lidx, nbe, lb, hrow, padrow, padlen) = _routing_tables(
        eid, rank, before[:, :, 0].astype(jnp.int32), cnt[:, 0].astype(jnp.int32), n_tok, n_exp, rpt)

    xs = _dispatch(lidx, nbe, lb, hrow, padrow, padlen, n_used, h2p, n_tiles * ROW_GRANULE, d_model)
    y_rows = _experts(tile_start, tile_count, xs, w_gu, b_gu, w_down, b_down)
    out = _combine(lidx, gate.T.reshape(-1), nbe, lb, hrow, x1.reshape(n_tok, d_model), final_w[None], y_rows)
    return out.reshape(bsz, seq, d_model)


def kernel(x, norm_mix_w, w_in, ret_norm_w, w_pool, pool_scale, w_out, norm_ffn_w, w_router, b_router,
           w_gu, b_gu, w_down, b_down, norm_final_w):
    depth = norm_mix_w.shape[0]
    for l in range(depth):
        last = l == depth - 1
        assert last, "stacked layers need an un-normalised combine output"
        x = _layer(x, norm_mix_w[l], w_in[l], ret_norm_w[l], w_pool[l], pool_scale[l], w_out[l],
                   norm_ffn_w[l], w_router[l], b_router[l], w_gu[l], b_gu[l], w_down[l], b_down[l],
                   norm_final_w)
    return x
```

```python
import functools
import math

import jax
import jax.numpy as jnp
from jax import lax
from jax.experimental import pallas as pl
from jax.experimental.pallas import tpu as pltpu

CHUNK = 64
RET_HEADS = 4
POOL_WINDOWS = (2, 4, 8, 16)
ROPE_BASE = 10000.0
TOP_K = 4
SWIGLU_LIMIT = 7.0
SWIGLU_ALPHA = 1.702
EPS = 1e-6

LANES = 128
SUBLANES = 8
VMEM_LIMIT_BYTES = 56 * 1024 * 1024

SEQ_TILE = 256
ROW_GRANULE = 128
ROW_TILE = 512
TOK_BLOCK = 1024
POOL_HALO = 16
PACK = 2
STRIP_ALIGN = 2
GATHER_UNROLL = 8
RARE_STRIP_BIT = 8


def _rms(x, w):
    return x * lax.rsqrt(jnp.mean(x * x, axis=-1, keepdims=True) + EPS) * w


def _bf16_bits_hi(x):
    return lax.bitcast_convert_type(x.astype(jnp.bfloat16).astype(jnp.float32), jnp.uint32)


def _pack_rows(x, out_ref, n_rows):
    half = x.shape[1] // PACK
    packed = (_bf16_bits_hi(x[:, :half]) >> 16) | _bf16_bits_hi(x[:, half:])
    rpt = half // LANES
    for si in range(rpt):
        out_ref[pl.ds(si, n_rows, stride=rpt), :] = packed[:, si * LANES:(si + 1) * LANES]


def _unpack_words(u):
    lo = lax.bitcast_convert_type(u << 16, jnp.float32)
    hi = lax.bitcast_convert_type(u & jnp.uint32(0xFFFF0000), jnp.float32)
    return lo, hi


def _unpack_rows(g_ref, n_rows, rpt):
    lo, hi = [], []
    for si in range(rpt):
        l, h = _unpack_words(g_ref[pl.ds(si, n_rows, stride=rpt), :])
        lo.append(l.astype(jnp.bfloat16))
        hi.append(h.astype(jnp.bfloat16))
    return jnp.concatenate(lo + hi, axis=-1)


def _mix_route_kernel(x_ref, nmw_ref, win_ref, cos_ref, sin_ref, dmat_ref, qdec_ref, kdec_ref,
                      rnw_ref, wpool_ref, pscale_ref, wout_ref, nfw_ref, wrt_ref, brt_ref, utri_ref,
                      x1_ref, h2p_ref, eid_ref, gate_ref, rank_ref, before_ref, cnt_ref,
                      state_ref, p1_ref, p2_ref, p4_ref, p8_ref, carry_ref,
                      *, tile_decay):
    b = pl.program_id(0)
    j = pl.program_id(1)
    ts = x_ref.shape[1]
    n_heads = RET_HEADS
    rw = rnw_ref.shape[1]
    hd = rw // n_heads
    pgw = wpool_ref.shape[1]
    n_exp = wrt_ref.shape[0]
    halo = POOL_HALO

    @pl.when(jnp.logical_and(b == 0, j == 0))
    def _():
        carry_ref[...] = jnp.zeros_like(carry_ref)

    @pl.when(j == 0)
    def _():
        state_ref[...] = jnp.zeros_like(state_ref)
        for r in (p1_ref, p2_ref, p4_ref, p8_ref):
            r[0:halo, :] = jnp.zeros((halo, r.shape[1]), jnp.float32)

    @pl.when(j > 0)
    def _():
        for r in (p1_ref, p2_ref, p4_ref, p8_ref):
            r[0:halo, :] = r[ts:ts + halo, :]

    x = x_ref[0]
    h = _rms(x, nmw_ref[...]).astype(jnp.bfloat16)
    z = jnp.dot(h, win_ref[...], preferred_element_type=jnp.float32)

    cos = cos_ref[...]
    sin = sin_ref[...]
    k_scale = hd ** -0.5

    ret_parts = []
    for hh in range(n_heads):
        q = z[:, hh * hd:(hh + 1) * hd]
        k = z[:, rw + hh * hd:rw + (hh + 1) * hd]
        v = z[:, 2 * rw + hh * hd:2 * rw + (hh + 1) * hd]
        g = z[:, 3 * rw + hh * hd:3 * rw + (hh + 1) * hd]
        q = q * cos + pltpu.roll(q, hd // 2, 1) * sin
        k = (k * cos + pltpu.roll(k, hd // 2, 1) * sin) * k_scale
        qb = q.astype(jnp.bfloat16)
        kb = k.astype(jnp.bfloat16)
        vb = v.astype(jnp.bfloat16)
        s = lax.dot_general(qb, kb, (((1,), (1,)), ((), ())), preferred_element_type=jnp.float32)
        pmat = (s * dmat_ref[hh]).astype(jnp.bfloat16)
        o = jnp.dot(pmat, vb, preferred_element_type=jnp.float32)
        state = state_ref[hh]
        qd = (q * qdec_ref[hh]).astype(jnp.bfloat16)
        o = o + jnp.dot(qd, state.astype(jnp.bfloat16), preferred_element_type=jnp.float32)
        kd = (k * kdec_ref[hh]).astype(jnp.bfloat16)
        u = lax.dot_general(kd, vb, (((0,), (0,)), ((), ())), preferred_element_type=jnp.float32)
        state_ref[hh] = state * tile_decay[hh] + u
        o = o * lax.rsqrt(jnp.mean(o * o, axis=-1, keepdims=True) + EPS)
        o = o * rnw_ref[:, hh * hd:(hh + 1) * hd] * (g * (1.0 / (1.0 + jnp.exp(-g))))
        ret_parts.append(o.astype(jnp.bfloat16))

    p = z[:, 4 * rw:]
    p1_ref[halo:halo + ts, :] = p
    s2 = p + p1_ref[halo - 1:halo - 1 + ts, :]
    p2_ref[halo:halo + ts, :] = s2[:, pgw:]
    s4 = s2[:, pgw:] + p2_ref[halo - 2:halo - 2 + ts, :]
    p4_ref[halo:halo + ts, :] = s4[:, pgw:]
    s8 = s4[:, pgw:] + p4_ref[halo - 4:halo - 4 + ts, :]
    p8_ref[halo:halo + ts, :] = s8[:, pgw:]
    s16 = s8[:, pgw:] + p8_ref[halo - 8:halo - 8 + ts, :]
    sums = (s2[:, :pgw], s4[:, :pgw], s8[:, :pgw], s16)
    tpos = (j * ts + lax.broadcasted_iota(jnp.int32, (ts, pgw), 0) + 1).astype(jnp.float32)
    pool_parts = []
    for gi, w in enumerate(POOL_WINDOWS):
        cnt = jnp.minimum(tpos, float(w))
        pooled = sums[gi] / cnt - p[:, gi * pgw:(gi + 1) * pgw]
        mixed = jnp.dot(pooled.astype(jnp.bfloat16), wpool_ref[gi], preferred_element_type=jnp.float32)
        mixed = mixed * pscale_ref[:, gi * pgw:(gi + 1) * pgw]
        pool_parts.append(mixed.astype(jnp.bfloat16))

    mix = jnp.concatenate(ret_parts + pool_parts, axis=-1)
    x1 = x + jnp.dot(mix, wout_ref[...], preferred_element_type=jnp.float32)
    x1_ref[0] = x1

    h2 = _rms(x1, nfw_ref[...])
    _pack_rows(h2, h2p_ref, ts)

    h2b = h2.astype(jnp.bfloat16)
    logits = lax.dot_general(wrt_ref[...], h2b, (((1,), (1,)), ((), ())),
                             preferred_element_type=jnp.float32) + brt_ref[...]
    eiota = lax.broadcasted_iota(jnp.int32, (n_exp, ts), 0)
    work = logits
    vals, ids, sels = [], [], []
    for _ in range(TOP_K):
        m = jnp.max(work, axis=0, keepdims=True)
        idx = jnp.min(jnp.where(work == m, eiota, n_exp), axis=0, keepdims=True)
        sel = eiota == idx
        vals.append(m)
        ids.append(idx)
        sels.append(sel)
        work = jnp.where(sel, -jnp.inf, work)
    ex = [jnp.exp(vk - vals[0]) for vk in vals]
    den = ex[0] + ex[1] + ex[2] + ex[3]
    gate_ref[...] = jnp.concatenate([e / den for e in ex], axis=0)
    eid_ref[...] = jnp.concatenate(ids, axis=0)

    member = jnp.zeros((n_exp, ts), jnp.float32)
    for sel in sels:
        member = member + sel.astype(jnp.float32)
    before_ref[0] = carry_ref[...]
    before = jnp.dot(member.astype(jnp.bfloat16), utri_ref[...], preferred_element_type=jnp.float32)
    before = before + carry_ref[:, 0:1]
    ranks = [jnp.sum(jnp.where(sel, before, 0.0), axis=0, keepdims=True) for sel in sels]
    rank_ref[...] = jnp.concatenate(ranks, axis=0).astype(jnp.int32)
    new_carry = carry_ref[...] + jnp.sum(member, axis=1, keepdims=True)
    carry_ref[...] = new_carry
    cnt_ref[...] = new_carry


def _mix_route(x, nmw, win_b, cos2, sin2, dmat, qdec, kdec, rnw, wpool_b, pscale, wout_b, nfw,
               wrt_b, brt, utri, tile_decay):
    bsz, seq, d_model = x.shape
    ts = SEQ_TILE
    n_tok = bsz * seq
    n_exp = wrt_b.shape[0]
    rw = rnw.shape[1]
    hd = rw // RET_HEADS
    pw = pscale.shape[1]
    pgw = pw // len(POOL_WINDOWS)
    nj = seq // ts
    rows_per_tok = d_model // PACK // LANES

    def const(shape):
        return pl.BlockSpec(shape, lambda b, j: (0,) * len(shape))

    in_specs = [
        pl.BlockSpec((1, ts, d_model), lambda b, j: (b, j, 0)),
        const(nmw.shape), const(win_b.shape),
        pl.BlockSpec((ts, hd), lambda b, j: (j, 0)),
        pl.BlockSpec((ts, hd), lambda b, j: (j, 0)),
        const(dmat.shape), const(qdec.shape), const(kdec.shape),
        const(rnw.shape), const(wpool_b.shape), const(pscale.shape), const(wout_b.shape),
        const(nfw.shape), const(wrt_b.shape), const(brt.shape), const(utri.shape),
    ]
    out_shape = [
        jax.ShapeDtypeStruct((bsz, seq, d_model), jnp.float32),
        jax.ShapeDtypeStruct((n_tok * rows_per_tok, LANES), jnp.uint32),
        jax.ShapeDtypeStruct((TOP_K, n_tok), jnp.int32),
        jax.ShapeDtypeStruct((TOP_K, n_tok), jnp.float32),
        jax.ShapeDtypeStruct((TOP_K, n_tok), jnp.int32),
        jax.ShapeDtypeStruct((n_tok // ts, n_exp, LANES), jnp.float32),
        jax.ShapeDtypeStruct((n_exp, LANES), jnp.float32),
    ]
    out_specs = [
        pl.BlockSpec((1, ts, d_model), lambda b, j: (b, j, 0)),
        pl.BlockSpec((ts * rows_per_tok, LANES), lambda b, j: (b * nj + j, 0)),
        pl.BlockSpec((TOP_K, ts), lambda b, j: (0, b * nj + j)),
        pl.BlockSpec((TOP_K, ts), lambda b, j: (0, b * nj + j)),
        pl.BlockSpec((TOP_K, ts), lambda b, j: (0, b * nj + j)),
        pl.BlockSpec((1, n_exp, LANES), lambda b, j: (b * nj + j, 0, 0)),
        pl.BlockSpec((n_exp, LANES), lambda b, j: (0, 0)),
    ]
    scratch = [
        pltpu.VMEM((RET_HEADS, hd, hd), jnp.float32),
        pltpu.VMEM((POOL_HALO + ts, pw), jnp.float32),
        pltpu.VMEM((POOL_HALO + ts, pw - pgw), jnp.float32),
        pltpu.VMEM((POOL_HALO + ts, pw - 2 * pgw), jnp.float32),
        pltpu.VMEM((POOL_HALO + ts, pw - 3 * pgw), jnp.float32),
        pltpu.VMEM((n_exp, LANES), jnp.float32),
    ]
    return pl.pallas_call(
        functools.partial(_mix_route_kernel, tile_decay=tile_decay),
        grid=(bsz, nj),
        in_specs=in_specs,
        out_specs=out_specs,
        out_shape=out_shape,
        scratch_shapes=scratch,
        compiler_params=pltpu.CompilerParams(
            dimension_semantics=("arbitrary", "arbitrary"),
            vmem_limit_bytes=VMEM_LIMIT_BYTES),
        name="mix_route",
    )(x, nmw, win_b, cos2, sin2, dmat, qdec, kdec, rnw, wpool_b, pscale, wout_b, nfw, wrt_b, brt, utri)


def _for_set_bits(n, max_bit, fn):
    def piece(bit):
        size = 1 << bit
        offset = (n >> (bit + 1)) << (bit + 1)

        @pl.when((n & size) != 0)
        def _():
            fn(offset, size)

    min_bit = STRIP_ALIGN.bit_length() - 1
    rare_bit = min(max_bit + 1, RARE_STRIP_BIT)

    @pl.when(n >= (1 << rare_bit))
    def _():
        for bit in range(max_bit, rare_bit - 1, -1):
            piece(bit)
    for bit in range(rare_bit - 1, min_bit - 1, -1):
        piece(bit)


def _aligned_len(n):
    return n + (n & (STRIP_ALIGN - 1))


def _for_strips(nbe_ref, blk, n_exp, max_bit, fn):
    def body(e, carry):
        n = _aligned_len(nbe_ref[blk * n_exp + e])
        _for_set_bits(n, max_bit, functools.partial(fn, e))
        return carry
    lax.fori_loop(0, n_exp, body, 0)


def _dispatch_kernel(lidx_ref, nbe_ref, lb_ref, hrow_ref, padrow_ref, padlen_ref, nused_ref,
                     h2p_ref, xs_hbm, stage_ref, zero_ref, sem, zsem, *, nblk):
    b = pl.program_id(0)
    rpt = zero_ref.shape[0] // ROW_GRANULE
    tb = h2p_ref.shape[0] // rpt
    n_exp = padrow_ref.shape[0]
    slot_rows = stage_ref.shape[0] // 2
    max_bit = tb.bit_length() - 1
    slot = b % 2

    def strip_copy(blk, sl, e, offset, size):
        src = pl.multiple_of(sl * slot_rows + (lb_ref[blk * n_exp + e] + offset) * rpt, SUBLANES)
        dst = pl.multiple_of((hrow_ref[blk * n_exp + e] + offset) * rpt, SUBLANES)
        return pltpu.make_async_copy(stage_ref.at[pl.ds(src, size * rpt), :],
                                     xs_hbm.at[pl.ds(dst, size * rpt), :], sem.at[sl])

    def start_strips(blk, sl):
        _for_strips(nbe_ref, blk, n_exp, max_bit,
                    lambda e, offset, size: strip_copy(blk, sl, e, offset, size).start())

    def wait_strips(blk, sl):
        _for_strips(nbe_ref, blk, n_exp, max_bit,
                    lambda e, offset, size: strip_copy(blk, sl, e, offset, size).wait())

    @pl.when(b >= 2)
    def _():
        wait_strips(b - 2, slot)

    def scatter_block(static_slot):
        def scatter_group(c, carry):
            tok0 = c * GATHER_UNROLL
            idx0 = (b * tb + tok0) * TOP_K
            for u in range(GATHER_UNROLL):
                row = h2p_ref[pl.ds(pl.multiple_of((tok0 + u) * rpt, rpt), rpt), :]
                for k in range(TOP_K):
                    dst = pl.multiple_of(lidx_ref[idx0 + u * TOP_K + k], rpt)
                    stage_ref[pl.ds(static_slot * slot_rows + dst, rpt), :] = row
            return carry
        lax.fori_loop(0, tb // GATHER_UNROLL, scatter_group, 0)

    for static_slot in range(2):
        pl.when(slot == static_slot)(functools.partial(scatter_block, static_slot))

    def zero_tail(e, carry):
        n = nbe_ref[b * n_exp + e]

        @pl.when((n & (STRIP_ALIGN - 1)) != 0)
        def _():
            dst = pl.multiple_of(slot * slot_rows + (lb_ref[b * n_exp + e] + n) * rpt, rpt)
            stage_ref[pl.ds(dst, rpt), :] = jnp.zeros((rpt, LANES), jnp.uint32)
        return carry
    lax.fori_loop(0, n_exp, zero_tail, 0)

    start_strips(b, slot)

    @pl.when(b == nblk - 1)
    def _():
        if nblk >= 2:
            wait_strips(b - 1, 1 - slot)
        wait_strips(b, slot)
        zero_ref[...] = jnp.zeros_like(zero_ref)
        pad_bit = ROW_GRANULE.bit_length() - 2

        def pad_copy(e, offset, size):
            dst = pl.multiple_of((padrow_ref[e] + offset) * rpt, SUBLANES)
            return pltpu.make_async_copy(zero_ref.at[pl.ds(0, size * rpt), :],
                                         xs_hbm.at[pl.ds(dst, size * rpt), :], zsem)

        def pad_start(e, carry):
            _for_set_bits(padlen_ref[e], pad_bit, lambda o, s: pad_copy(e, o, s).start())
            return carry

        def pad_wait(e, carry):
            _for_set_bits(padlen_ref[e], pad_bit, lambda o, s: pad_copy(e, o, s).wait())
            return carry
        lax.fori_loop(0, n_exp, pad_start, 0)

        def tail_copy(i):
            dst = pl.multiple_of(i * (ROW_GRANULE * rpt), SUBLANES)
            return pltpu.make_async_copy(zero_ref, xs_hbm.at[pl.ds(dst, ROW_GRANULE * rpt), :], zsem)

        def tail_start(i, carry):
            tail_copy(i).start()
            return carry

        def tail_wait(i, carry):
            tail_copy(i).wait()
            return carry
        n_gran = xs_hbm.shape[0] // (ROW_GRANULE * rpt)
        lax.fori_loop(nused_ref[0], n_gran, tail_start, 0)
        lax.fori_loop(0, n_exp, pad_wait, 0)
        lax.fori_loop(nused_ref[0], n_gran, tail_wait, 0)


def _dispatch(lidx, nbe, lb, hrow, padrow, padlen, n_used, h2p, n_rows, d_model):
    tb = TOK_BLOCK
    rpt = d_model // PACK // LANES
    n_tok = h2p.shape[0] // rpt
    n_exp = padrow.shape[0]
    slot_rows = (tb * TOP_K + n_exp * (STRIP_ALIGN - 1)) * rpt
    grid_spec = pltpu.PrefetchScalarGridSpec(
        num_scalar_prefetch=7,
        grid=(n_tok // tb,),
        in_specs=[pl.BlockSpec((tb * rpt, LANES), lambda i, *_: (i, 0))],
        out_specs=pl.BlockSpec(memory_space=pl.ANY),
        scratch_shapes=[pltpu.VMEM((2 * slot_rows, LANES), jnp.uint32),
                        pltpu.VMEM((ROW_GRANULE * rpt, LANES), jnp.uint32),
                        pltpu.SemaphoreType.DMA((2,)),
                        pltpu.SemaphoreType.DMA],
    )
    return pl.pallas_call(
        functools.partial(_dispatch_kernel, nblk=n_tok // tb),
        grid_spec=grid_spec,
        out_shape=jax.ShapeDtypeStruct((n_rows * rpt, LANES), jnp.uint32),
        compiler_params=pltpu.CompilerParams(
            dimension_semantics=("arbitrary",),
            vmem_limit_bytes=VMEM_LIMIT_BYTES),
        name="dispatch",
    )(lidx, nbe, lb, hrow, padrow, padlen, n_used, h2p)


def _experts_kernel(tstart_ref, tcount_ref, wgu_hbm, bgu_ref, wd_hbm, bd_ref, xs_hbm,
                    y_hbm, wgu_f, wd_f, wgu_b, wd_b, xbuf, ybuf, wsem, xsem, ysem, *, n_exp):
    e = pl.program_id(0)
    d_ff = wd_f.shape[0]
    d_model = wd_f.shape[1]
    rpt = d_model // PACK // LANES
    gran_rows = ROW_GRANULE * rpt
    gpt = ROW_TILE // ROW_GRANULE
    n_gran = y_hbm.shape[0] // gran_rows
    g0 = tstart_ref[e]
    ng = tcount_ref[e]
    nt = ng // gpt
    tail = ng - nt * gpt

    def hbm_rows(ref, gran, count):
        return ref.at[pl.ds(pl.multiple_of(gran * gran_rows, SUBLANES), count * gran_rows), :]

    def x_copy(gran, sl, count=gpt):
        return pltpu.make_async_copy(hbm_rows(xs_hbm, gran, count),
                                     xbuf.at[sl, pl.ds(0, count * gran_rows), :], xsem.at[sl])

    def y_copy(gran, sl, count=gpt):
        return pltpu.make_async_copy(ybuf.at[sl, pl.ds(0, count * gran_rows), :],
                                     hbm_rows(y_hbm, gran, count), ysem.at[sl])

    def mlp(sl, tm):
        x = _unpack_rows(xbuf.at[sl], tm, rpt)
        gu = jnp.dot(x, wgu_b[...], preferred_element_type=jnp.float32) + bgu_ref[0]
        x_glu = jnp.minimum(gu[:, :d_ff], SWIGLU_LIMIT)
        x_lin = jnp.clip(gu[:, d_ff:], -SWIGLU_LIMIT, SWIGLU_LIMIT)
        act = x_glu * (1.0 / (1.0 + jnp.exp(-SWIGLU_ALPHA * x_glu))) * (x_lin + 1.0)
        y = jnp.dot(act.astype(jnp.bfloat16), wd_b[...], preferred_element_type=jnp.float32) + bd_ref[0]
        _pack_rows(y, ybuf.at[sl], tm)

    def weight_copies(ex):
        return (pltpu.make_async_copy(wgu_hbm.at[ex], wgu_f, wsem.at[0]),
                pltpu.make_async_copy(wd_hbm.at[ex], wd_f, wsem.at[1]))

    @pl.when(nt > 0)
    def _():
        x_copy(g0, 0).start()

    @pl.when(e == 0)
    def _():
        for c in weight_copies(0):
            c.start(priority=1)
    for c in weight_copies(e):
        c.wait()
    wgu_b[...] = wgu_f[...].astype(jnp.bfloat16)
    wd_b[...] = wd_f[...].astype(jnp.bfloat16)

    @pl.when(e + 1 < n_exp)
    def _():
        for c in weight_copies(e + 1):
            c.start(priority=1)

    @pl.when(ng > 0)
    def _():
        def tile_step(j, carry):
            sl = j % 2
            gran = g0 + j * gpt

            @pl.when(j + 1 < nt)
            def _():
                x_copy(gran + gpt, 1 - sl).start()
            x_copy(gran, sl).wait()

            @pl.when(j >= 2)
            def _():
                y_copy(gran - 2 * gpt, sl).wait()
            mlp(sl, ROW_TILE)
            y_copy(gran, sl).start()
            return carry
        lax.fori_loop(0, nt, tile_step, 0)

        @pl.when(nt >= 2)
        def _():
            y_copy(g0 + (nt - 2) * gpt, nt % 2).wait()

        @pl.when(nt >= 1)
        def _():
            y_copy(g0 + (nt - 1) * gpt, (nt - 1) % 2).wait()

        for count in range(1, gpt):
            @pl.when(tail == count)
            def _():
                gran = g0 + nt * gpt
                x_copy(gran, 0, count).start()
                x_copy(gran, 0, count).wait()
                mlp(0, count * ROW_GRANULE)
                y_copy(gran, 0, count).start()
                y_copy(gran, 0, count).wait()

    @pl.when(e == n_exp - 1)
    def _():
        ybuf[0, 0:gran_rows, :] = jnp.zeros((gran_rows, LANES), ybuf.dtype)

        def tail_start(i, carry):
            y_copy(i, 0, 1).start()
            return carry

        def tail_wait(i, carry):
            y_copy(i, 0, 1).wait()
            return carry
        lax.fori_loop(g0 + ng, n_gran, tail_start, 0)
        lax.fori_loop(g0 + ng, n_gran, tail_wait, 0)


def _experts(tile_start, tile_count, xs, w_gu, b_gu, w_down, b_down):
    tm = ROW_TILE
    n_exp, d_model, d_gu = w_gu.shape
    d_ff = w_down.shape[1]
    rpt = d_model // PACK // LANES
    grid_spec = pltpu.PrefetchScalarGridSpec(
        num_scalar_prefetch=2,
        grid=(n_exp,),
        in_specs=[
            pl.BlockSpec(memory_space=pl.ANY),
            pl.BlockSpec((1, 1, d_gu), lambda e, *_: (e, 0, 0)),
            pl.BlockSpec(memory_space=pl.ANY),
            pl.BlockSpec((1, 1, d_model), lambda e, *_: (e, 0, 0)),
            pl.BlockSpec(memory_space=pl.ANY),
        ],
        out_specs=pl.BlockSpec(memory_space=pl.ANY),
        scratch_shapes=[pltpu.VMEM((d_model, d_gu), jnp.float32),
                        pltpu.VMEM((d_ff, d_model), jnp.float32),
                        pltpu.VMEM((d_model, d_gu), jnp.bfloat16),
                        pltpu.VMEM((d_ff, d_model), jnp.bfloat16),
                        pltpu.VMEM((2, tm * rpt, LANES), jnp.uint32),
                        pltpu.VMEM((2, tm * rpt, LANES), jnp.uint32),
                        pltpu.SemaphoreType.DMA((2,)),
                        pltpu.SemaphoreType.DMA((2,)),
                        pltpu.SemaphoreType.DMA((2,))],
    )
    return pl.pallas_call(
        functools.partial(_experts_kernel, n_exp=n_exp),
        grid_spec=grid_spec,
        out_shape=jax.ShapeDtypeStruct(xs.shape, jnp.uint32),
        compiler_params=pltpu.CompilerParams(
            dimension_semantics=("arbitrary",),
            vmem_limit_bytes=VMEM_LIMIT_BYTES),
        name="experts",
    )(tile_start, tile_count, w_gu, b_gu.reshape(n_exp, 1, d_gu), w_down,
      b_down.reshape(n_exp, 1, d_model), xs)


def _combine_kernel(lidx_ref, gate_ref, nbe_ref, lb_ref, hrow_ref,
                    x1_ref, nw_ref, y_hbm, out_ref, stage_ref, alo_ref, ahi_ref, sem, *, nblk):
    b = pl.program_id(0)
    tb = x1_ref.shape[0]
    rpt = alo_ref.shape[0] // tb
    n_exp = nbe_ref.shape[0] // nblk
    slot_rows = stage_ref.shape[0] // 2
    max_bit = tb.bit_length() - 1
    slot = b % 2

    def strip_copy(blk, sl, e, offset, size):
        src = pl.multiple_of((hrow_ref[blk * n_exp + e] + offset) * rpt, SUBLANES)
        dst = pl.multiple_of(sl * slot_rows + (lb_ref[blk * n_exp + e] + offset) * rpt, SUBLANES)
        return pltpu.make_async_copy(y_hbm.at[pl.ds(src, size * rpt), :],
                                     stage_ref.at[pl.ds(dst, size * rpt), :], sem.at[sl])

    def start_strips(blk, sl):
        _for_strips(nbe_ref, blk, n_exp, max_bit,
                    lambda e, offset, size: strip_copy(blk, sl, e, offset, size).start())

    @pl.when(b == 0)
    def _():
        start_strips(0, 0)

    @pl.when(b + 1 < nblk)
    def _():
        start_strips(b + 1, 1 - slot)

    _for_strips(nbe_ref, b, n_exp, max_bit,
                lambda e, offset, size: strip_copy(b, slot, e, offset, size).wait())

    def gather_block(static_slot):
        def gather_group(c, carry):
            tok0 = c * GATHER_UNROLL
            idx0 = (b * tb + tok0) * TOP_K
            for u in range(GATHER_UNROLL):
                lo = hi = None
                for k in range(TOP_K):
                    src = pl.multiple_of(lidx_ref[idx0 + u * TOP_K + k], rpt)
                    g = gate_ref[idx0 + u * TOP_K + k]
                    l, h = _unpack_words(stage_ref[pl.ds(static_slot * slot_rows + src, rpt), :])
                    lo = g * l if lo is None else lo + g * l
                    hi = g * h if hi is None else hi + g * h
                dst = pl.multiple_of((tok0 + u) * rpt, rpt)
                alo_ref[pl.ds(dst, rpt), :] = lo
                ahi_ref[pl.ds(dst, rpt), :] = hi
            return carry
        lax.fori_loop(0, tb // GATHER_UNROLL, gather_group, 0)

    for static_slot in range(2):
        pl.when(slot == static_slot)(functools.partial(gather_block, static_slot))

    parts = [alo_ref[pl.ds(si, tb, stride=rpt), :] for si in range(rpt)]
    parts += [ahi_ref[pl.ds(si, tb, stride=rpt), :] for si in range(rpt)]
    x2 = x1_ref[...] + jnp.concatenate(parts, axis=-1)
    out_ref[...] = _rms(x2, nw_ref[...])


def _combine(lidx, gates, nbe, lb, hrow, x1, norm_w, y_rows):
    tb = TOK_BLOCK
    n_tok, d_model = x1.shape
    rpt = d_model // PACK // LANES
    n_exp = nbe.shape[0] // (n_tok // tb)
    slot_rows = (tb * TOP_K + n_exp * (STRIP_ALIGN - 1)) * rpt
    grid_spec = pltpu.PrefetchScalarGridSpec(
        num_scalar_prefetch=5,
        grid=(n_tok // tb,),
        in_specs=[
            pl.BlockSpec((tb, d_model), lambda i, *_: (i, 0)),
            pl.BlockSpec((1, d_model), lambda i, *_: (0, 0)),
            pl.BlockSpec(memory_space=pl.ANY),
        ],
        out_specs=pl.BlockSpec((tb, d_model), lambda i, *_: (i, 0)),
        scratch_shapes=[pltpu.VMEM((2 * slot_rows, LANES), jnp.uint32),
                        pltpu.VMEM((tb * rpt, LANES), jnp.float32),
                        pltpu.VMEM((tb * rpt, LANES), jnp.float32),
                        pltpu.SemaphoreType.DMA((2,))],
    )
    return pl.pallas_call(
        functools.partial(_combine_kernel, nblk=n_tok // tb),
        grid_spec=grid_spec,
        out_shape=jax.ShapeDtypeStruct((n_tok, d_model), jnp.float32),
        compiler_params=pltpu.CompilerParams(
            dimension_semantics=("arbitrary",),
            vmem_limit_bytes=VMEM_LIMIT_BYTES),
        name="combine",
    )(lidx, gates, nbe, lb, hrow, x1, norm_w, y_rows)


def _retention_tables(seq, hd, ts):
    half = hd // 2
    inv = ROPE_BASE ** (-jnp.arange(half, dtype=jnp.float32) / half)
    ang = jnp.arange(seq, dtype=jnp.float32)[:, None] * inv[None, :]
    cos, sin = jnp.cos(ang), jnp.sin(ang)
    cos2 = jnp.concatenate([cos, cos], axis=-1)
    sin2 = jnp.concatenate([-sin, sin], axis=-1)
    log_g = jnp.log1p(-jnp.exp2(-5.0 - jnp.arange(RET_HEADS, dtype=jnp.float32)))
    pos = jnp.arange(ts, dtype=jnp.float32)
    dist = jnp.abs(pos[:, None] - pos[None, :])
    chunk = jnp.arange(ts, dtype=jnp.int32) // CHUNK
    visible = chunk[None, :] <= chunk[:, None]
    dmat = jnp.where(visible[None], jnp.exp(log_g[:, None, None] * dist[None]), 0.0)
    qdec = jnp.exp(log_g[:, None] * (pos + 1.0)[None, :])
    kdec = jnp.exp(log_g[:, None] * (ts - 1.0 - pos)[None, :])
    qdec = jnp.broadcast_to(qdec[:, :, None], (RET_HEADS, ts, hd))
    kdec = jnp.broadcast_to(kdec[:, :, None], (RET_HEADS, ts, hd))
    tile_decay = tuple(math.exp(math.log1p(-2.0 ** (-5.0 - h)) * ts) for h in range(RET_HEADS))
    return cos2, sin2, dmat, qdec, kdec, tile_decay


def _routing_tables(eid, rank, before, counts, n_tok, n_exp, rows_per_tok):
    tm, tb, ts = ROW_GRANULE, TOK_BLOCK, SEQ_TILE
    cb = before[::tb // ts]
    nbe = jnp.concatenate([cb[1:], counts[None]], axis=0) - cb
    nal = nbe + (nbe & (STRIP_ALIGN - 1))
    region = jnp.sum(nal, axis=0)
    tiles_per = (region + tm - 1) // tm
    tile_end = jnp.cumsum(tiles_per)
    offs = (tile_end - tiles_per) * tm
    n_tiles = (n_tok * TOP_K + (n_tok // tb) * n_exp * (STRIP_ALIGN - 1)) // tm + n_exp
    hrow = offs[None, :] + jnp.cumsum(nal, axis=0) - nal
    lb = jnp.cumsum(nal, axis=1) - nal
    delta = jnp.repeat(lb - cb, tb, axis=0)
    onehot = eid[:, :, None] == jnp.arange(n_exp, dtype=jnp.int32)
    lidx = rank + jnp.sum(jnp.where(onehot, delta[None], 0), axis=-1)
    lidx = lidx.T * rows_per_tok
    padrow = offs + region
    padlen = tiles_per * tm - region
    as_i32 = lambda a: a.reshape(-1).astype(jnp.int32)
    return (n_tiles, as_i32(tile_end - tiles_per), as_i32(tiles_per), as_i32(tile_end[-1:]),
            as_i32(lidx), as_i32(nbe), as_i32(lb), as_i32(hrow), as_i32(padrow), as_i32(padlen))


def _layer(x, norm_mix_w, w_in, ret_norm_w, w_pool, pool_scale, w_out, norm_ffn_w,
           w_router, b_router, w_gu, b_gu, w_down, b_down, final_w):
    bsz, seq, d_model = x.shape
    n_tok = bsz * seq
    n_exp = w_router.shape[1]
    rw = ret_norm_w.shape[0]
    hd = rw // RET_HEADS
    ts = SEQ_TILE
    bf = jnp.bfloat16
    rpt = d_model // PACK // LANES
    assert d_model % (PACK * LANES) == 0 and (STRIP_ALIGN * rpt) % SUBLANES == 0
    assert seq % ts == 0 and ts % CHUNK == 0 and n_tok % TOK_BLOCK == 0 and TOK_BLOCK % ts == 0
    assert TOK_BLOCK & (TOK_BLOCK - 1) == 0 and ROW_GRANULE & (ROW_GRANULE - 1) == 0
    assert ROW_TILE % ROW_GRANULE == 0

    cos2, sin2, dmat, qdec, kdec, tile_decay = _retention_tables(seq, hd, ts)
    tri = jnp.arange(ts, dtype=jnp.int32)
    utri = (tri[:, None] < tri[None, :]).astype(bf)

    x1, h2p, eid, gate, rank, before, cnt = _mix_route(
        x, norm_mix_w[None], w_in.astype(bf), cos2, sin2, dmat, qdec, kdec, ret_norm_w[None],
        w_pool.astype(bf), pool_scale[None], w_out.astype(bf), norm_ffn_w[None],
        w_router.T.astype(bf), b_router[:, None], utri, tile_decay)

    (n_tiles, tile_start, tile_count, n_used, lidx, nbe, lb, hrow, padrow, padlen) = _routing_tables(
        eid, rank, before[:, :, 0].astype(jnp.int32), cnt[:, 0].astype(jnp.int32), n_tok, n_exp, rpt)

    xs = _dispatch(lidx, nbe, lb, hrow, padrow, padlen, n_used, h2p, n_tiles * ROW_GRANULE, d_model)
    y_rows = _experts(tile_start, tile_count, xs, w_gu, b_gu, w_down, b_down)
    out = _combine(lidx, gate.T.reshape(-1), nbe, lb, hrow, x1.reshape(n_tok, d_model), final_w[None], y_rows)
    return out.reshape(bsz, seq, d_model)


def kernel(x, norm_mix_w, w_in, ret_norm_w, w_pool, pool_scale, w_out, norm_ffn_w, w_router, b_router,
           w_gu, b_gu, w_down, b_down, norm_final_w):
    depth = norm_mix_w.shape[0]
    for l in range(depth):
        last = l == depth - 1
        assert last, "stacked layers need an un-normalised combine output"
        x = _layer(x, norm_mix_w[l], w_in[l], ret_norm_w[l], w_pool[l], pool_scale[l], w_out[l],
                   norm_ffn_w[l], w_router[l], b_router[l], w_gu[l], b_gu[l], w_down[l], b_down[l],
                   norm_final_w)
    return x
```

```python
import functools
import math

import jax
import jax.numpy as jnp
from jax import lax
from jax.experimental import pallas as pl
from jax.experimental.pallas import tpu as pltpu

CHUNK = 64
RET_HEADS = 4
POOL_WINDOWS = (2, 4, 8, 16)
ROPE_BASE = 10000.0
TOP_K = 4
SWIGLU_LIMIT = 7.0
SWIGLU_ALPHA = 1.702
EPS = 1e-6

LANES = 128
SUBLANES = 8
VMEM_LIMIT_BYTES = 56 * 1024 * 1024

SEQ_TILE = 256
ROW_GRANULE = 128
ROW_TILE = 512
TOK_BLOCK = 1024
POOL_HALO = 16
PACK = 2
STRIP_ALIGN = 2
GATHER_UNROLL = 8
RARE_STRIP_BIT = 8


def _rms(x, w):
    return x * lax.rsqrt(jnp.mean(x * x, axis=-1, keepdims=True) + EPS) * w


def _bf16_bits_hi(x):
    return lax.bitcast_convert_type(x.astype(jnp.bfloat16).astype(jnp.float32), jnp.uint32)


def _pack_rows(x, out_ref, n_rows):
    half = x.shape[1] // PACK
    packed = (_bf16_bits_hi(x[:, :half]) >> 16) | _bf16_bits_hi(x[:, half:])
    rpt = half // LANES
    for si in range(rpt):
        out_ref[pl.ds(si, n_rows, stride=rpt), :] = packed[:, si * LANES:(si + 1) * LANES]


def _unpack_words(u):
    lo = lax.bitcast_convert_type(u << 16, jnp.float32)
    hi = lax.bitcast_convert_type(u & jnp.uint32(0xFFFF0000), jnp.float32)
    return lo, hi


def _unpack_rows(g_ref, n_rows, rpt):
    lo, hi = [], []
    for si in range(rpt):
        l, h = _unpack_words(g_ref[pl.ds(si, n_rows, stride=rpt), :])
        lo.append(l.astype(jnp.bfloat16))
        hi.append(h.astype(jnp.bfloat16))
    return jnp.concatenate(lo + hi, axis=-1)


def _mix_route_kernel(x_ref, nmw_ref, win_ref, cos_ref, sin_ref, dmat_ref, qdec_ref, kdec_ref,
                      rnw_ref, wpool_ref, pscale_ref, wout_ref, nfw_ref, wrt_ref, brt_ref, utri_ref,
                      x1_ref, h2p_ref, eid_ref, gate_ref, rank_ref, before_ref, cnt_ref,
                      state_ref, p1_ref, p2_ref, p4_ref, p8_ref, carry_ref,
                      *, tile_decay):
    b = pl.program_id(0)
    j = pl.program_id(1)
    ts = x_ref.shape[1]
    n_heads = RET_HEADS
    rw = rnw_ref.shape[1]
    hd = rw // n_heads
    pgw = wpool_ref.shape[1]
    n_exp = wrt_ref.shape[0]
    halo = POOL_HALO

    @pl.when(jnp.logical_and(b == 0, j == 0))
    def _():
        carry_ref[...] = jnp.zeros_like(carry_ref)

    @pl.when(j == 0)
    def _():
        state_ref[...] = jnp.zeros_like(state_ref)
        for r in (p1_ref, p2_ref, p4_ref, p8_ref):
            r[0:halo, :] = jnp.zeros((halo, r.shape[1]), jnp.float32)

    @pl.when(j > 0)
    def _():
        for r in (p1_ref, p2_ref, p4_ref, p8_ref):
            r[0:halo, :] = r[ts:ts + halo, :]

    x = x_ref[0]
    h = _rms(x, nmw_ref[...]).astype(jnp.bfloat16)
    z = jnp.dot(h, win_ref[...], preferred_element_type=jnp.float32)

    cos = cos_ref[...]
    sin = sin_ref[...]
    k_scale = hd ** -0.5

    ret_parts = []
    for hh in range(n_heads):
        q = z[:, hh * hd:(hh + 1) * hd]
        k = z[:, rw + hh * hd:rw + (hh + 1) * hd]
        v = z[:, 2 * rw + hh * hd:2 * rw + (hh + 1) * hd]
        g = z[:, 3 * rw + hh * hd:3 * rw + (hh + 1) * hd]
        q = q * cos + pltpu.roll(q, hd // 2, 1) * sin
        k = (k * cos + pltpu.roll(k, hd // 2, 1) * sin) * k_scale
        qb = q.astype(jnp.bfloat16)
        kb = k.astype(jnp.bfloat16)
        vb = v.astype(jnp.bfloat16)
        s = lax.dot_general(qb, kb, (((1,), (1,)), ((), ())), preferred_element_type=jnp.float32)
        pmat = (s * dmat_ref[hh]).astype(jnp.bfloat16)
        o = jnp.dot(pmat, vb, preferred_element_type=jnp.float32)
        state = state_ref[hh]
        qd = (q * qdec_ref[hh]).astype(jnp.bfloat16)
        o = o + jnp.dot(qd, state.astype(jnp.bfloat16), preferred_element_type=jnp.float32)
        kd = (k * kdec_ref[hh]).astype(jnp.bfloat16)
        u = lax.dot_general(kd, vb, (((0,), (0,)), ((), ())), preferred_element_type=jnp.float32)
        state_ref[hh] = state * tile_decay[hh] + u
        o = o * lax.rsqrt(jnp.mean(o * o, axis=-1, keepdims=True) + EPS)
        o = o * rnw_ref[:, hh * hd:(hh + 1) * hd] * (g * (1.0 / (1.0 + jnp.exp(-g))))
        ret_parts.append(o.astype(jnp.bfloat16))

    p = z[:, 4 * rw:]
    p1_ref[halo:halo + ts, :] = p
    s2 = p + p1_ref[halo - 1:halo - 1 + ts, :]
    p2_ref[halo:halo + ts, :] = s2[:, pgw:]
    s4 = s2[:, pgw:] + p2_ref[halo - 2:halo - 2 + ts, :]
    p4_ref[halo:halo + ts, :] = s4[:, pgw:]
    s8 = s4[:, pgw:] + p4_ref[halo - 4:halo - 4 + ts, :]
    p8_ref[halo:halo + ts, :] = s8[:, pgw:]
    s16 = s8[:, pgw:] + p8_ref[halo - 8:halo - 8 + ts, :]
    sums = (s2[:, :pgw], s4[:, :pgw], s8[:, :pgw], s16)
    tpos = (j * ts + lax.broadcasted_iota(jnp.int32, (ts, pgw), 0) + 1).astype(jnp.float32)
    pool_parts = []
    for gi, w in enumerate(POOL_WINDOWS):
        cnt = jnp.minimum(tpos, float(w))
        pooled = sums[gi] / cnt - p[:, gi * pgw:(gi + 1) * pgw]
        mixed = jnp.dot(pooled.astype(jnp.bfloat16), wpool_ref[gi], preferred_element_type=jnp.float32)
        mixed = mixed * pscale_ref[:, gi * pgw:(gi + 1) * pgw]
        pool_parts.append(mixed.astype(jnp.bfloat16))

    mix = jnp.concatenate(ret_parts + pool_parts, axis=-1)
    x1 = x + jnp.dot(mix, wout_ref[...], preferred_element_type=jnp.float32)
    x1_ref[0] = x1

    h2 = _rms(x1, nfw_ref[...])
    _pack_rows(h2, h2p_ref, ts)

    h2b = h2.astype(jnp.bfloat16)
    logits = lax.dot_general(wrt_ref[...], h2b, (((1,), (1,)), ((), ())),
                             preferred_element_type=jnp.float32) + brt_ref[...]
    eiota = lax.broadcasted_iota(jnp.int32, (n_exp, ts), 0)
    work = logits
    vals, ids, sels = [], [], []
    for _ in range(TOP_K):
        m = jnp.max(work, axis=0, keepdims=True)
        idx = jnp.min(jnp.where(work == m, eiota, n_exp), axis=0, keepdims=True)
        sel = eiota == idx
        vals.append(m)
        ids.append(idx)
        sels.append(sel)
        work = jnp.where(sel, -jnp.inf, work)
    ex = [jnp.exp(vk - vals[0]) for vk in vals]
    den = ex[0] + ex[1] + ex[2] + ex[3]
    gate_ref[...] = jnp.concatenate([e / den for e in ex], axis=0)
    eid_ref[...] = jnp.concatenate(ids, axis=0)

    member = jnp.zeros((n_exp, ts), jnp.float32)
    for sel in sels:
        member = member + sel.astype(jnp.float32)
    before_ref[0] = carry_ref[...]
    before = jnp.dot(member.astype(jnp.bfloat16), utri_ref[...], preferred_element_type=jnp.float32)
    before = before + carry_ref[:, 0:1]
    ranks = [jnp.sum(jnp.where(sel, before, 0.0), axis=0, keepdims=True) for sel in sels]
    rank_ref[...] = jnp.concatenate(ranks, axis=0).astype(jnp.int32)
    new_carry = carry_ref[...] + jnp.sum(member, axis=1, keepdims=True)
    carry_ref[...] = new_carry
    cnt_ref[...] = new_carry


def _mix_route(x, nmw, win_b, cos2, sin2, dmat, qdec, kdec, rnw, wpool_b, pscale, wout_b, nfw,
               wrt_b, brt, utri, tile_decay):
    bsz, seq, d_model = x.shape
    ts = SEQ_TILE
    n_tok = bsz * seq
    n_exp = wrt_b.shape[0]
    rw = rnw.shape[1]
    hd = rw // RET_HEADS
    pw = pscale.shape[1]
    pgw = pw // len(POOL_WINDOWS)
    nj = seq // ts
    rows_per_tok = d_model // PACK // LANES

    def const(shape):
        return pl.BlockSpec(shape, lambda b, j: (0,) * len(shape))

    in_specs = [
        pl.BlockSpec((1, ts, d_model), lambda b, j: (b, j, 0)),
        const(nmw.shape), const(win_b.shape),
        pl.BlockSpec((ts, hd), lambda b, j: (j, 0)),
        pl.BlockSpec((ts, hd), lambda b, j: (j, 0)),
        const(dmat.shape), const(qdec.shape), const(kdec.shape),
        const(rnw.shape), const(wpool_b.shape), const(pscale.shape), const(wout_b.shape),
        const(nfw.shape), const(wrt_b.shape), const(brt.shape), const(utri.shape),
    ]
    out_shape = [
        jax.ShapeDtypeStruct((bsz, seq, d_model), jnp.float32),
        jax.ShapeDtypeStruct((n_tok * rows_per_tok, LANES), jnp.uint32),
        jax.ShapeDtypeStruct((TOP_K, n_tok), jnp.int32),
        jax.ShapeDtypeStruct((TOP_K, n_tok), jnp.float32),
        jax.ShapeDtypeStruct((TOP_K, n_tok), jnp.int32),
        jax.ShapeDtypeStruct((n_tok // ts, n_exp, LANES), jnp.float32),
        jax.ShapeDtypeStruct((n_exp, LANES), jnp.float32),
    ]
    out_specs = [
        pl.BlockSpec((1, ts, d_model), lambda b, j: (b, j, 0)),
        pl.BlockSpec((ts * rows_per_tok, LANES), lambda b, j: (b * nj + j, 0)),
        pl.BlockSpec((TOP_K, ts), lambda b, j: (0, b * nj + j)),
        pl.BlockSpec((TOP_K, ts), lambda b, j: (0, b * nj + j)),
        pl.BlockSpec((TOP_K, ts), lambda b, j: (0, b * nj + j)),
        pl.BlockSpec((1, n_exp, LANES), lambda b, j: (b * nj + j, 0, 0)),
        pl.BlockSpec((n_exp, LANES), lambda b, j: (0, 0)),
    ]
    scratch = [
        pltpu.VMEM((RET_HEADS, hd, hd), jnp.float32),
        pltpu.VMEM((POOL_HALO + ts, pw), jnp.float32),
        pltpu.VMEM((POOL_HALO + ts, pw - pgw), jnp.float32),
        pltpu.VMEM((POOL_HALO + ts, pw - 2 * pgw), jnp.float32),
        pltpu.VMEM((POOL_HALO + ts, pw - 3 * pgw), jnp.float32),
        pltpu.VMEM((n_exp, LANES), jnp.float32),
    ]
    return pl.pallas_call(
        functools.partial(_mix_route_kernel, tile_decay=tile_decay),
        grid=(bsz, nj),
        in_specs=in_specs,
        out_specs=out_specs,
        out_shape=out_shape,
        scratch_shapes=scratch,
        compiler_params=pltpu.CompilerParams(
            dimension_semantics=("arbitrary", "arbitrary"),
            vmem_limit_bytes=VMEM_LIMIT_BYTES),
        name="mix_route",
    )(x, nmw, win_b, cos2, sin2, dmat, qdec, kdec, rnw, wpool_b, pscale, wout_b, nfw, wrt_b, brt, utri)


def _for_set_bits(n, max_bit, fn):
    def piece(bit):
        size = 1 << bit
        offset = (n >> (bit + 1)) << (bit + 1)

        @pl.when((n & size) != 0)
        def _():
            fn(offset, size)

    min_bit = STRIP_ALIGN.bit_length() - 1
    rare_bit = min(max_bit + 1, RARE_STRIP_BIT)

    @pl.when(n >= (1 << rare_bit))
    def _():
        for bit in range(max_bit, rare_bit - 1, -1):
            piece(bit)
    for bit in range(rare_bit - 1, min_bit - 1, -1):
        piece(bit)


def _aligned_len(n):
    return n + (n & (STRIP_ALIGN - 1))


def _for_strips(nbe_ref, blk, n_exp, max_bit, fn):
    def body(e, carry):
        n = _aligned_len(nbe_ref[blk * n_exp + e])
        _for_set_bits(n, max_bit, functools.partial(fn, e))
        return carry
    lax.fori_loop(0, n_exp, body, 0)


def _dispatch_kernel(lidx_ref, nbe_ref, lb_ref, hrow_ref, padrow_ref, padlen_ref, nused_ref,
                     h2p_ref, xs_hbm, stage_ref, zero_ref, sem, zsem, *, nblk):
    b = pl.program_id(0)
    rpt = zero_ref.shape[0] // ROW_GRANULE
    tb = h2p_ref.shape[0] // rpt
    n_exp = padrow_ref.shape[0]
    n_tok = lidx_ref.shape[0] // TOP_K
    slot_rows = stage_ref.shape[0] // 2
    max_bit = tb.bit_length() - 1
    slot = b % 2

    def strip_copy(blk, sl, e, offset, size):
        src = pl.multiple_of(sl * slot_rows + (lb_ref[blk * n_exp + e] + offset) * rpt, SUBLANES)
        dst = pl.multiple_of((hrow_ref[blk * n_exp + e] + offset) * rpt, SUBLANES)
        return pltpu.make_async_copy(stage_ref.at[pl.ds(src, size * rpt), :],
                                     xs_hbm.at[pl.ds(dst, size * rpt), :], sem.at[sl])

    def start_strips(blk, sl):
        _for_strips(nbe_ref, blk, n_exp, max_bit,
                    lambda e, offset, size: strip_copy(blk, sl, e, offset, size).start())

    def wait_strips(blk, sl):
        _for_strips(nbe_ref, blk, n_exp, max_bit,
                    lambda e, offset, size: strip_copy(blk, sl, e, offset, size).wait())

    @pl.when(b >= 2)
    def _():
        wait_strips(b - 2, slot)

    def scatter_block(static_slot):
        def scatter_group(c, carry):
            tok0 = c * GATHER_UNROLL
            idx0 = b * tb + tok0
            for u in range(GATHER_UNROLL):
                row = h2p_ref[pl.ds(pl.multiple_of((tok0 + u) * rpt, rpt), rpt), :]
                for k in range(TOP_K):
                    dst = pl.multiple_of(lidx_ref[idx0 + (k * n_tok + u)], rpt)
                    stage_ref[pl.ds(static_slot * slot_rows + dst, rpt), :] = row
            return carry
        lax.fori_loop(0, tb // GATHER_UNROLL, scatter_group, 0)

    for static_slot in range(2):
        pl.when(slot == static_slot)(functools.partial(scatter_block, static_slot))

    def zero_tail(e, carry):
        n = nbe_ref[b * n_exp + e]

        @pl.when((n & (STRIP_ALIGN - 1)) != 0)
        def _():
            dst = pl.multiple_of(slot * slot_rows + (lb_ref[b * n_exp + e] + n) * rpt, rpt)
            stage_ref[pl.ds(dst, rpt), :] = jnp.zeros((rpt, LANES), jnp.uint32)
        return carry
    lax.fori_loop(0, n_exp, zero_tail, 0)

    start_strips(b, slot)

    @pl.when(b == nblk - 1)
    def _():
        if nblk >= 2:
            wait_strips(b - 1, 1 - slot)
        wait_strips(b, slot)
        zero_ref[...] = jnp.zeros_like(zero_ref)
        pad_bit = ROW_GRANULE.bit_length() - 2

        def pad_copy(e, offset, size):
            dst = pl.multiple_of((padrow_ref[e] + offset) * rpt, SUBLANES)
            return pltpu.make_async_copy(zero_ref.at[pl.ds(0, size * rpt), :],
                                         xs_hbm.at[pl.ds(dst, size * rpt), :], zsem)

        def pad_start(e, carry):
            _for_set_bits(padlen_ref[e], pad_bit, lambda o, s: pad_copy(e, o, s).start())
            return carry

        def pad_wait(e, carry):
            _for_set_bits(padlen_ref[e], pad_bit, lambda o, s: pad_copy(e, o, s).wait())
            return carry
        lax.fori_loop(0, n_exp, pad_start, 0)

        def tail_copy(i):
            dst = pl.multiple_of(i * (ROW_GRANULE * rpt), SUBLANES)
            return pltpu.make_async_copy(zero_ref, xs_hbm.at[pl.ds(dst, ROW_GRANULE * rpt), :], zsem)

        def tail_start(i, carry):
            tail_copy(i).start()
            return carry

        def tail_wait(i, carry):
            tail_copy(i).wait()
            return carry
        n_gran = xs_hbm.shape[0] // (ROW_GRANULE * rpt)
        lax.fori_loop(nused_ref[0], n_gran, tail_start, 0)
        lax.fori_loop(0, n_exp, pad_wait, 0)
        lax.fori_loop(nused_ref[0], n_gran, tail_wait, 0)


def _dispatch(lidx, nbe, lb, hrow, padrow, padlen, n_used, h2p, n_rows, d_model):
    tb = TOK_BLOCK
    rpt = d_model // PACK // LANES
    n_tok = h2p.shape[0] // rpt
    n_exp = padrow.shape[0]
    slot_rows = (tb * TOP_K + n_exp * (STRIP_ALIGN - 1)) * rpt
    grid_spec = pltpu.PrefetchScalarGridSpec(
        num_scalar_prefetch=7,
        grid=(n_tok // tb,),
        in_specs=[pl.BlockSpec((tb * rpt, LANES), lambda i, *_: (i, 0))],
        out_specs=pl.BlockSpec(memory_space=pl.ANY),
        scratch_shapes=[pltpu.VMEM((2 * slot_rows, LANES), jnp.uint32),
                        pltpu.VMEM((ROW_GRANULE * rpt, LANES), jnp.uint32),
                        pltpu.SemaphoreType.DMA((2,)),
                        pltpu.SemaphoreType.DMA],
    )
    return pl.pallas_call(
        functools.partial(_dispatch_kernel, nblk=n_tok // tb),
        grid_spec=grid_spec,
        out_shape=jax.ShapeDtypeStruct((n_rows * rpt, LANES), jnp.uint32),
        compiler_params=pltpu.CompilerParams(
            dimension_semantics=("arbitrary",),
            vmem_limit_bytes=VMEM_LIMIT_BYTES),
        name="dispatch",
    )(lidx, nbe, lb, hrow, padrow, padlen, n_used, h2p)


def _experts_kernel(tstart_ref, tcount_ref, wgu_hbm, bgu_ref, wd_hbm, bd_ref, xs_hbm,
                    y_hbm, wgu_f, wd_f, wgu_b, wd_b, xbuf, ybuf, pend_ref, wsem, xsem, ysem, *, n_exp):
    e = pl.program_id(0)
    d_ff = wd_f.shape[0]
    d_model = wd_f.shape[1]
    rpt = d_model // PACK // LANES
    gran_rows = ROW_GRANULE * rpt
    gpt = ROW_TILE // ROW_GRANULE
    n_gran = y_hbm.shape[0] // gran_rows
    g0 = tstart_ref[e]
    ng = tcount_ref[e]
    nt = ng // gpt
    tail = ng - nt * gpt

    def hbm_rows(ref, gran, count):
        return ref.at[pl.ds(pl.multiple_of(gran * gran_rows, SUBLANES), count * gran_rows), :]

    def x_copy(gran, sl, count=gpt):
        return pltpu.make_async_copy(hbm_rows(xs_hbm, gran, count),
                                     xbuf.at[sl, pl.ds(0, count * gran_rows), :], xsem.at[sl])

    def y_copy(gran, sl, count=gpt):
        return pltpu.make_async_copy(ybuf.at[sl, pl.ds(0, count * gran_rows), :],
                                     hbm_rows(y_hbm, gran, count), ysem.at[sl])

    def mlp(sl, tm):
        x = _unpack_rows(xbuf.at[sl], tm, rpt)
        gu = jnp.dot(x, wgu_b[...], preferred_element_type=jnp.float32) + bgu_ref[0]
        x_glu = jnp.minimum(gu[:, :d_ff], SWIGLU_LIMIT)
        x_lin = jnp.clip(gu[:, d_ff:], -SWIGLU_LIMIT, SWIGLU_LIMIT)
        act = x_glu * (1.0 / (1.0 + jnp.exp(-SWIGLU_ALPHA * x_glu))) * (x_lin + 1.0)
        y = jnp.dot(act.astype(jnp.bfloat16), wd_b[...], preferred_element_type=jnp.float32) + bd_ref[0]
        _pack_rows(y, ybuf.at[sl], tm)

    def weight_copies(ex):
        return (pltpu.make_async_copy(wgu_hbm.at[ex], wgu_f, wsem.at[0]),
                pltpu.make_async_copy(wd_hbm.at[ex], wd_f, wsem.at[1]))

    tail_slot = 2
    tail_gran = g0 + nt * gpt

    def wait_pending_tail_y():
        for count in range(1, gpt):
            @pl.when(pend_ref[0] == count)
            def _():
                y_copy(0, tail_slot, count).wait()

    @pl.when(e == 0)
    def _():
        pend_ref[0] = 0

        @pl.when(nt > 0)
        def _():
            x_copy(g0, 0).start()

    for count in range(1, gpt):
        @pl.when(tail == count)
        def _():
            x_copy(tail_gran, tail_slot, count).start()

    @pl.when(e == 0)
    def _():
        for c in weight_copies(0):
            c.start(priority=1)
    for c in weight_copies(e):
        c.wait()
    wgu_b[...] = wgu_f[...].astype(jnp.bfloat16)
    wd_b[...] = wd_f[...].astype(jnp.bfloat16)

    @pl.when(e + 1 < n_exp)
    def _():
        for c in weight_copies(e + 1):
            c.start(priority=1)

    @pl.when(ng > 0)
    def _():
        def tile_step(j, carry):
            sl = j % 2
            gran = g0 + j * gpt

            @pl.when(j + 1 < nt)
            def _():
                x_copy(gran + gpt, 1 - sl).start()
            x_copy(gran, sl).wait()

            @pl.when(j >= 2)
            def _():
                y_copy(gran - 2 * gpt, sl).wait()
            mlp(sl, ROW_TILE)
            y_copy(gran, sl).start()
            return carry
        lax.fori_loop(0, nt, tile_step, 0)

        @pl.when(nt >= 2)
        def _():
            y_copy(g0 + (nt - 2) * gpt, nt % 2).wait()

        @pl.when(nt >= 1)
        def _():
            y_copy(g0 + (nt - 1) * gpt, (nt - 1) % 2).wait()

    @pl.when(e + 1 < n_exp)
    def _():
        nxt = jnp.minimum(e + 1, n_exp - 1)

        @pl.when(tcount_ref[nxt] >= gpt)
        def _():
            x_copy(tstart_ref[nxt], 0).start()

    for count in range(1, gpt):
        @pl.when(tail == count)
        def _():
            x_copy(tail_gran, tail_slot, count).wait()
            wait_pending_tail_y()
            mlp(tail_slot, count * ROW_GRANULE)
            y_copy(tail_gran, tail_slot, count).start()
            pend_ref[0] = count

    @pl.when(e == n_exp - 1)
    def _():
        wait_pending_tail_y()
        ybuf[0, 0:gran_rows, :] = jnp.zeros((gran_rows, LANES), ybuf.dtype)

        def tail_start(i, carry):
            y_copy(i, 0, 1).start()
            return carry

        def tail_wait(i, carry):
            y_copy(i, 0, 1).wait()
            return carry
        lax.fori_loop(g0 + ng, n_gran, tail_start, 0)
        lax.fori_loop(g0 + ng, n_gran, tail_wait, 0)


def _experts(tile_start, tile_count, xs, w_gu, b_gu, w_down, b_down):
    tm = ROW_TILE
    n_exp, d_model, d_gu = w_gu.shape
    d_ff = w_down.shape[1]
    rpt = d_model // PACK // LANES
    grid_spec = pltpu.PrefetchScalarGridSpec(
        num_scalar_prefetch=2,
        grid=(n_exp,),
        in_specs=[
            pl.BlockSpec(memory_space=pl.ANY),
            pl.BlockSpec((1, 1, d_gu), lambda e, *_: (e, 0, 0)),
            pl.BlockSpec(memory_space=pl.ANY),
            pl.BlockSpec((1, 1, d_model), lambda e, *_: (e, 0, 0)),
            pl.BlockSpec(memory_space=pl.ANY),
        ],
        out_specs=pl.BlockSpec(memory_space=pl.ANY),
        scratch_shapes=[pltpu.VMEM((d_model, d_gu), jnp.float32),
                        pltpu.VMEM((d_ff, d_model), jnp.float32),
                        pltpu.VMEM((d_model, d_gu), jnp.bfloat16),
                        pltpu.VMEM((d_ff, d_model), jnp.bfloat16),
                        pltpu.VMEM((3, tm * rpt, LANES), jnp.uint32),
                        pltpu.VMEM((3, tm * rpt, LANES), jnp.uint32),
                        pltpu.SMEM((1,), jnp.int32),
                        pltpu.SemaphoreType.DMA((2,)),
                        pltpu.SemaphoreType.DMA((3,)),
                        pltpu.SemaphoreType.DMA((3,))],
    )
    return pl.pallas_call(
        functools.partial(_experts_kernel, n_exp=n_exp),
        grid_spec=grid_spec,
        out_shape=jax.ShapeDtypeStruct(xs.shape, jnp.uint32),
        compiler_params=pltpu.CompilerParams(
            dimension_semantics=("arbitrary",),
            vmem_limit_bytes=VMEM_LIMIT_BYTES),
        name="experts",
    )(tile_start, tile_count, w_gu, b_gu.reshape(n_exp, 1, d_gu), w_down,
      b_down.reshape(n_exp, 1, d_model), xs)


def _combine_kernel(lidx_ref, gate_ref, nbe_ref, lb_ref, hrow_ref,
                    x1_ref, nw_ref, y_hbm, out_ref, stage_ref, alo_ref, ahi_ref, sem, *, nblk):
    b = pl.program_id(0)
    tb = x1_ref.shape[0]
    rpt = alo_ref.shape[0] // tb
    n_exp = nbe_ref.shape[0] // nblk
    n_tok = lidx_ref.shape[0] // TOP_K
    slot_rows = stage_ref.shape[0] // 2
    max_bit = tb.bit_length() - 1
    slot = b % 2

    def strip_copy(blk, sl, e, offset, size):
        src = pl.multiple_of((hrow_ref[blk * n_exp + e] + offset) * rpt, SUBLANES)
        dst = pl.multiple_of(sl * slot_rows + (lb_ref[blk * n_exp + e] + offset) * rpt, SUBLANES)
        return pltpu.make_async_copy(y_hbm.at[pl.ds(src, size * rpt), :],
                                     stage_ref.at[pl.ds(dst, size * rpt), :], sem.at[sl])

    def start_strips(blk, sl):
        _for_strips(nbe_ref, blk, n_exp, max_bit,
                    lambda e, offset, size: strip_copy(blk, sl, e, offset, size).start())

    @pl.when(b == 0)
    def _():
        start_strips(0, 0)

    @pl.when(b + 1 < nblk)
    def _():
        start_strips(b + 1, 1 - slot)

    _for_strips(nbe_ref, b, n_exp, max_bit,
                lambda e, offset, size: strip_copy(b, slot, e, offset, size).wait())

    def gather_block(static_slot):
        def gather_group(c, carry):
            tok0 = c * GATHER_UNROLL
            idx0 = b * tb + tok0
            for u in range(GATHER_UNROLL):
                lo = hi = None
                for k in range(TOP_K):
                    src = pl.multiple_of(lidx_ref[idx0 + (k * n_tok + u)], rpt)
                    g = gate_ref[idx0 + (k * n_tok + u)]
                    l, h = _unpack_words(stage_ref[pl.ds(static_slot * slot_rows + src, rpt), :])
                    lo = g * l if lo is None else lo + g * l
                    hi = g * h if hi is None else hi + g * h
                dst = pl.multiple_of((tok0 + u) * rpt, rpt)
                alo_ref[pl.ds(dst, rpt), :] = lo
                ahi_ref[pl.ds(dst, rpt), :] = hi
            return carry
        lax.fori_loop(0, tb // GATHER_UNROLL, gather_group, 0)

    for static_slot in range(2):
        pl.when(slot == static_slot)(functools.partial(gather_block, static_slot))

    parts = [alo_ref[pl.ds(si, tb, stride=rpt), :] for si in range(rpt)]
    parts += [ahi_ref[pl.ds(si, tb, stride=rpt), :] for si in range(rpt)]
    x2 = x1_ref[...] + jnp.concatenate(parts, axis=-1)
    out_ref[...] = _rms(x2, nw_ref[...])


def _combine(lidx, gates, nbe, lb, hrow, x1, norm_w, y_rows):
    tb = TOK_BLOCK
    n_tok, d_model = x1.shape
    rpt = d_model // PACK // LANES
    n_exp = nbe.shape[0] // (n_tok // tb)
    slot_rows = (tb * TOP_K + n_exp * (STRIP_ALIGN - 1)) * rpt
    grid_spec = pltpu.PrefetchScalarGridSpec(
        num_scalar_prefetch=5,
        grid=(n_tok // tb,),
        in_specs=[
            pl.BlockSpec((tb, d_model), lambda i, *_: (i, 0)),
            pl.BlockSpec((1, d_model), lambda i, *_: (0, 0)),
            pl.BlockSpec(memory_space=pl.ANY),
        ],
        out_specs=pl.BlockSpec((tb, d_model), lambda i, *_: (i, 0)),
        scratch_shapes=[pltpu.VMEM((2 * slot_rows, LANES), jnp.uint32),
                        pltpu.VMEM((tb * rpt, LANES), jnp.float32),
                        pltpu.VMEM((tb * rpt, LANES), jnp.float32),
                        pltpu.SemaphoreType.DMA((2,))],
    )
    return pl.pallas_call(
        functools.partial(_combine_kernel, nblk=n_tok // tb),
        grid_spec=grid_spec,
        out_shape=jax.ShapeDtypeStruct((n_tok, d_model), jnp.float32),
        compiler_params=pltpu.CompilerParams(
            dimension_semantics=("arbitrary",),
            vmem_limit_bytes=VMEM_LIMIT_BYTES),
        name="combine",
    )(lidx, gates, nbe, lb, hrow, x1, norm_w, y_rows)


def _retention_tables(seq, hd, ts):
    half = hd // 2
    inv = ROPE_BASE ** (-jnp.arange(half, dtype=jnp.float32) / half)
    ang = jnp.arange(seq, dtype=jnp.float32)[:, None] * inv[None, :]
    cos, sin = jnp.cos(ang), jnp.sin(ang)
    cos2 = jnp.concatenate([cos, cos], axis=-1)
    sin2 = jnp.concatenate([-sin, sin], axis=-1)
    log_g = jnp.log1p(-jnp.exp2(-5.0 - jnp.arange(RET_HEADS, dtype=jnp.float32)))
    pos = jnp.arange(ts, dtype=jnp.float32)
    dist = jnp.abs(pos[:, None] - pos[None, :])
    chunk = jnp.arange(ts, dtype=jnp.int32) // CHUNK
    visible = chunk[None, :] <= chunk[:, None]
    dmat = jnp.where(visible[None], jnp.exp(log_g[:, None, None] * dist[None]), 0.0)
    qdec = jnp.exp(log_g[:, None] * (pos + 1.0)[None, :])
    kdec = jnp.exp(log_g[:, None] * (ts - 1.0 - pos)[None, :])
    qdec = jnp.broadcast_to(qdec[:, :, None], (RET_HEADS, ts, hd))
    kdec = jnp.broadcast_to(kdec[:, :, None], (RET_HEADS, ts, hd))
    tile_decay = tuple(math.exp(math.log1p(-2.0 ** (-5.0 - h)) * ts) for h in range(RET_HEADS))
    return cos2, sin2, dmat, qdec, kdec, tile_decay


def _routing_tables(eid, rank, before, counts, n_tok, n_exp, rows_per_tok):
    tm, tb, ts = ROW_GRANULE, TOK_BLOCK, SEQ_TILE
    cb = before[::tb // ts]
    nbe = jnp.concatenate([cb[1:], counts[None]], axis=0) - cb
    nal = nbe + (nbe & (STRIP_ALIGN - 1))
    region = jnp.sum(nal, axis=0)
    tiles_per = (region + tm - 1) // tm
    tile_end = jnp.cumsum(tiles_per)
    offs = (tile_end - tiles_per) * tm
    n_tiles = (n_tok * TOP_K + (n_tok // tb) * n_exp * (STRIP_ALIGN - 1)) // tm + n_exp
    hrow = offs[None, :] + jnp.cumsum(nal, axis=0) - nal
    lb = jnp.cumsum(nal, axis=1) - nal
    delta = jnp.repeat(lb - cb, tb, axis=0)
    onehot = eid[:, :, None] == jnp.arange(n_exp, dtype=jnp.int32)
    lidx = rank + jnp.sum(jnp.where(onehot, delta[None], 0), axis=-1)
    lidx = lidx * rows_per_tok
    padrow = offs + region
    padlen = tiles_per * tm - region
    as_i32 = lambda a: a.reshape(-1).astype(jnp.int32)
    return (n_tiles, as_i32(tile_end - tiles_per), as_i32(tiles_per), as_i32(tile_end[-1:]),
            as_i32(lidx), as_i32(nbe), as_i32(lb), as_i32(hrow), as_i32(padrow), as_i32(padlen))


def _layer(x, norm_mix_w, w_in, ret_norm_w, w_pool, pool_scale, w_out, norm_ffn_w,
           w_router, b_router, w_gu, b_gu, w_down, b_down, final_w):
    bsz, seq, d_model = x.shape
    n_tok = bsz * seq
    n_exp = w_router.shape[1]
    rw = ret_norm_w.shape[0]
    hd = rw // RET_HEADS
    ts = SEQ_TILE
    bf = jnp.bfloat16
    rpt = d_model // PACK // LANES
    assert d_model % (PACK * LANES) == 0 and (STRIP_ALIGN * rpt) % SUBLANES == 0
    assert seq % ts == 0 and ts % CHUNK == 0 and n_tok % TOK_BLOCK == 0 and TOK_BLOCK % ts == 0
    assert TOK_BLOCK & (TOK_BLOCK - 1) == 0 and ROW_GRANULE & (ROW_GRANULE - 1) == 0
    assert ROW_TILE % ROW_GRANULE == 0

    cos2, sin2, dmat, qdec, kdec, tile_decay = _retention_tables(seq, hd, ts)
    tri = jnp.arange(ts, dtype=jnp.int32)
    utri = (tri[:, None] < tri[None, :]).astype(bf)

    x1, h2p, eid, gate, rank, before, cnt = _mix_route(
        x, norm_mix_w[None], w_in.astype(bf), cos2, sin2, dmat, qdec, kdec, ret_norm_w[None],
        w_pool.astype(bf), pool_scale[None], w_out.astype(bf), norm_ffn_w[None],
        w_router.T.astype(bf), b_router[:, None], utri, tile_decay)

    (n_tiles, tile_start, tile_count, n_used, lidx, nbe, lb, hrow, padrow, padlen) = _routing_tables(
        eid, rank, before[:, :, 0].astype(jnp.int32), cnt[:, 0].astype(jnp.int32), n_tok, n_exp, rpt)

    xs = _dispatch(lidx, nbe, lb, hrow, padrow, padlen, n_used, h2p, n_tiles * ROW_GRANULE, d_model)
    y_rows = _experts(tile_start, tile_count, xs, w_gu, b_gu, w_down, b_down)
    out = _combine(lidx, gate.reshape(-1), nbe, lb, hrow, x1.reshape(n_tok, d_model), final_w[None], y_rows)
    return out.reshape(bsz, seq, d_model)


def kernel(x, norm_mix_w, w_in, ret_norm_w, w_pool, pool_scale, w_out, norm_ffn_w, w_router, b_router,
           w_gu, b_gu, w_down, b_down, norm_final_w):
    depth = norm_mix_w.shape[0]
    for l in range(depth):
        last = l == depth - 1
        assert last, "stacked layers need an un-normalised combine output"
        x = _layer(x, norm_mix_w[l], w_in[l], ret_norm_w[l], w_pool[l], pool_scale[l], w_out[l],
                   norm_ffn_w[l], w_router[l], b_router[l], w_gu[l], b_gu[l], w_down[l], b_down[l],
                   norm_final_w)
    return x
```

```python
import functools
import math

import jax
import jax.numpy as jnp
from jax import lax
from jax.experimental import pallas as pl
from jax.experimental.pallas import tpu as pltpu

CHUNK = 64
RET_HEADS = 4
POOL_WINDOWS = (2, 4, 8, 16)
ROPE_BASE = 10000.0
TOP_K = 4
SWIGLU_LIMIT = 7.0
SWIGLU_ALPHA = 1.702
EPS = 1e-6

LANES = 128
SUBLANES = 8
VMEM_LIMIT_BYTES = 56 * 1024 * 1024

SEQ_TILE = 256
ROW_GRANULE = 128
ROW_TILE = 512
TOK_BLOCK = 1024
POOL_HALO = 16
PACK = 2
STRIP_ALIGN = 2
GATHER_UNROLL = 8
RARE_STRIP_BIT = 8


def _rms(x, w):
    return x * lax.rsqrt(jnp.mean(x * x, axis=-1, keepdims=True) + EPS) * w


def _bf16_bits_hi(x):
    return lax.bitcast_convert_type(x.astype(jnp.bfloat16).astype(jnp.float32), jnp.uint32)


def _pack_rows(x, out_ref, n_rows):
    half = x.shape[1] // PACK
    packed = (_bf16_bits_hi(x[:, :half]) >> 16) | _bf16_bits_hi(x[:, half:])
    rpt = half // LANES
    for si in range(rpt):
        out_ref[pl.ds(si, n_rows, stride=rpt), :] = packed[:, si * LANES:(si + 1) * LANES]


def _unpack_words(u):
    lo = lax.bitcast_convert_type(u << 16, jnp.float32)
    hi = lax.bitcast_convert_type(u & jnp.uint32(0xFFFF0000), jnp.float32)
    return lo, hi


def _unpack_rows(g_ref, n_rows, rpt):
    lo, hi = [], []
    for si in range(rpt):
        l, h = _unpack_words(g_ref[pl.ds(si, n_rows, stride=rpt), :])
        lo.append(l.astype(jnp.bfloat16))
        hi.append(h.astype(jnp.bfloat16))
    return jnp.concatenate(lo + hi, axis=-1)


def _mix_route_kernel(x_ref, nmw_ref, win_ref, cos_ref, sin_ref, dmat_ref, qdec_ref, kdec_ref,
                      rnw_ref, wpool_ref, pscale_ref, wout_ref, nfw_ref, wrt_ref, brt_ref, utri_ref,
                      x1_ref, h2p_ref, eid_ref, gate_ref, rank_ref, before_ref, cnt_ref,
                      state_ref, p1_ref, p2_ref, p4_ref, p8_ref, carry_ref,
                      *, tile_decay):
    b = pl.program_id(0)
    j = pl.program_id(1)
    ts = x_ref.shape[1]
    n_heads = RET_HEADS
    rw = rnw_ref.shape[1]
    hd = rw // n_heads
    pgw = wpool_ref.shape[1]
    n_exp = wrt_ref.shape[0]
    halo = POOL_HALO

    @pl.when(jnp.logical_and(b == 0, j == 0))
    def _():
        carry_ref[...] = jnp.zeros_like(carry_ref)

    @pl.when(j == 0)
    def _():
        state_ref[...] = jnp.zeros_like(state_ref)
        for r in (p1_ref, p2_ref, p4_ref, p8_ref):
            r[0:halo, :] = jnp.zeros((halo, r.shape[1]), jnp.float32)

    @pl.when(j > 0)
    def _():
        for r in (p1_ref, p2_ref, p4_ref, p8_ref):
            r[0:halo, :] = r[ts:ts + halo, :]

    x = x_ref[0]
    h = _rms(x, nmw_ref[...]).astype(jnp.bfloat16)
    z = jnp.dot(h, win_ref[...], preferred_element_type=jnp.float32)

    cos = cos_ref[...]
    sin = sin_ref[...]
    k_scale = hd ** -0.5

    ret_parts = []
    for hh in range(n_heads):
        q = z[:, hh * hd:(hh + 1) * hd]
        k = z[:, rw + hh * hd:rw + (hh + 1) * hd]
        v = z[:, 2 * rw + hh * hd:2 * rw + (hh + 1) * hd]
        g = z[:, 3 * rw + hh * hd:3 * rw + (hh + 1) * hd]
        q = q * cos + pltpu.roll(q, hd // 2, 1) * sin
        k = (k * cos + pltpu.roll(k, hd // 2, 1) * sin) * k_scale
        qb = q.astype(jnp.bfloat16)
        kb = k.astype(jnp.bfloat16)
        vb = v.astype(jnp.bfloat16)
        s = lax.dot_general(qb, kb, (((1,), (1,)), ((), ())), preferred_element_type=jnp.float32)
        pmat = (s * dmat_ref[hh]).astype(jnp.bfloat16)
        o = jnp.dot(pmat, vb, preferred_element_type=jnp.float32)
        state = state_ref[hh]
        qd = (q * qdec_ref[hh]).astype(jnp.bfloat16)
        o = o + jnp.dot(qd, state.astype(jnp.bfloat16), preferred_element_type=jnp.float32)
        kd = (k * kdec_ref[hh]).astype(jnp.bfloat16)
        u = lax.dot_general(kd, vb, (((0,), (0,)), ((), ())), preferred_element_type=jnp.float32)
        state_ref[hh] = state * tile_decay[hh] + u
        o = o * lax.rsqrt(jnp.mean(o * o, axis=-1, keepdims=True) + EPS)
        o = o * rnw_ref[:, hh * hd:(hh + 1) * hd] * (g * (1.0 / (1.0 + jnp.exp(-g))))
        ret_parts.append(o.astype(jnp.bfloat16))

    p = z[:, 4 * rw:]
    p1_ref[halo:halo + ts, :] = p
    s2 = p + p1_ref[halo - 1:halo - 1 + ts, :]
    p2_ref[halo:halo + ts, :] = s2[:, pgw:]
    s4 = s2[:, pgw:] + p2_ref[halo - 2:halo - 2 + ts, :]
    p4_ref[halo:halo + ts, :] = s4[:, pgw:]
    s8 = s4[:, pgw:] + p4_ref[halo - 4:halo - 4 + ts, :]
    p8_ref[halo:halo + ts, :] = s8[:, pgw:]
    s16 = s8[:, pgw:] + p8_ref[halo - 8:halo - 8 + ts, :]
    sums = (s2[:, :pgw], s4[:, :pgw], s8[:, :pgw], s16)
    tpos = (j * ts + lax.broadcasted_iota(jnp.int32, (ts, pgw), 0) + 1).astype(jnp.float32)
    pool_parts = []
    for gi, w in enumerate(POOL_WINDOWS):
        cnt = jnp.minimum(tpos, float(w))
        pooled = sums[gi] / cnt - p[:, gi * pgw:(gi + 1) * pgw]
        mixed = jnp.dot(pooled.astype(jnp.bfloat16), wpool_ref[gi], preferred_element_type=jnp.float32)
        mixed = mixed * pscale_ref[:, gi * pgw:(gi + 1) * pgw]
        pool_parts.append(mixed.astype(jnp.bfloat16))

    mix = jnp.concatenate(ret_parts + pool_parts, axis=-1)
    x1 = x + jnp.dot(mix, wout_ref[...], preferred_element_type=jnp.float32)
    x1_ref[0] = x1

    h2 = _rms(x1, nfw_ref[...])
    _pack_rows(h2, h2p_ref, ts)

    h2b = h2.astype(jnp.bfloat16)
    logits = lax.dot_general(wrt_ref[...], h2b, (((1,), (1,)), ((), ())),
                             preferred_element_type=jnp.float32) + brt_ref[...]
    eiota = lax.broadcasted_iota(jnp.int32, (n_exp, ts), 0)
    work = logits
    vals, ids, sels = [], [], []
    for _ in range(TOP_K):
        m = jnp.max(work, axis=0, keepdims=True)
        idx = jnp.min(jnp.where(work == m, eiota, n_exp), axis=0, keepdims=True)
        sel = eiota == idx
        vals.append(m)
        ids.append(idx)
        sels.append(sel)
        work = jnp.where(sel, -jnp.inf, work)
    ex = [jnp.exp(vk - vals[0]) for vk in vals]
    den = ex[0] + ex[1] + ex[2] + ex[3]
    gate_ref[...] = jnp.concatenate([e / den for e in ex], axis=0)
    eid_ref[...] = jnp.concatenate(ids, axis=0)

    member = jnp.zeros((n_exp, ts), jnp.float32)
    for sel in sels:
        member = member + sel.astype(jnp.float32)
    before_ref[0] = carry_ref[...]
    before = jnp.dot(member.astype(jnp.bfloat16), utri_ref[...], preferred_element_type=jnp.float32)
    before = before + carry_ref[:, 0:1]
    ranks = [jnp.sum(jnp.where(sel, before, 0.0), axis=0, keepdims=True) for sel in sels]
    rank_ref[...] = jnp.concatenate(ranks, axis=0).astype(jnp.int32)
    new_carry = carry_ref[...] + jnp.sum(member, axis=1, keepdims=True)
    carry_ref[...] = new_carry
    cnt_ref[...] = new_carry


def _mix_route(x, nmw, win_b, cos2, sin2, dmat, qdec, kdec, rnw, wpool_b, pscale, wout_b, nfw,
               wrt_b, brt, utri, tile_decay):
    bsz, seq, d_model = x.shape
    ts = SEQ_TILE
    n_tok = bsz * seq
    n_exp = wrt_b.shape[0]
    rw = rnw.shape[1]
    hd = rw // RET_HEADS
    pw = pscale.shape[1]
    pgw = pw // len(POOL_WINDOWS)
    nj = seq // ts
    rows_per_tok = d_model // PACK // LANES

    def const(shape):
        return pl.BlockSpec(shape, lambda b, j: (0,) * len(shape))

    in_specs = [
        pl.BlockSpec((1, ts, d_model), lambda b, j: (b, j, 0)),
        const(nmw.shape), const(win_b.shape),
        pl.BlockSpec((ts, hd), lambda b, j: (j, 0)),
        pl.BlockSpec((ts, hd), lambda b, j: (j, 0)),
        const(dmat.shape), const(qdec.shape), const(kdec.shape),
        const(rnw.shape), const(wpool_b.shape), const(pscale.shape), const(wout_b.shape),
        const(nfw.shape), const(wrt_b.shape), const(brt.shape), const(utri.shape),
    ]
    out_shape = [
        jax.ShapeDtypeStruct((bsz, seq, d_model), jnp.float32),
        jax.ShapeDtypeStruct((n_tok * rows_per_tok, LANES), jnp.uint32),
        jax.ShapeDtypeStruct((TOP_K, n_tok), jnp.int32),
        jax.ShapeDtypeStruct((TOP_K, n_tok), jnp.float32),
        jax.ShapeDtypeStruct((TOP_K, n_tok), jnp.int32),
        jax.ShapeDtypeStruct((n_tok // ts, n_exp, LANES), jnp.float32),
        jax.ShapeDtypeStruct((n_exp, LANES), jnp.float32),
    ]
    out_specs = [
        pl.BlockSpec((1, ts, d_model), lambda b, j: (b, j, 0)),
        pl.BlockSpec((ts * rows_per_tok, LANES), lambda b, j: (b * nj + j, 0)),
        pl.BlockSpec((TOP_K, ts), lambda b, j: (0, b * nj + j)),
        pl.BlockSpec((TOP_K, ts), lambda b, j: (0, b * nj + j)),
        pl.BlockSpec((TOP_K, ts), lambda b, j: (0, b * nj + j)),
        pl.BlockSpec((1, n_exp, LANES), lambda b, j: (b * nj + j, 0, 0)),
        pl.BlockSpec((n_exp, LANES), lambda b, j: (0, 0)),
    ]
    scratch = [
        pltpu.VMEM((RET_HEADS, hd, hd), jnp.float32),
        pltpu.VMEM((POOL_HALO + ts, pw), jnp.float32),
        pltpu.VMEM((POOL_HALO + ts, pw - pgw), jnp.float32),
        pltpu.VMEM((POOL_HALO + ts, pw - 2 * pgw), jnp.float32),
        pltpu.VMEM((POOL_HALO + ts, pw - 3 * pgw), jnp.float32),
        pltpu.VMEM((n_exp, LANES), jnp.float32),
    ]
    return pl.pallas_call(
        functools.partial(_mix_route_kernel, tile_decay=tile_decay),
        grid=(bsz, nj),
        in_specs=in_specs,
        out_specs=out_specs,
        out_shape=out_shape,
        scratch_shapes=scratch,
        compiler_params=pltpu.CompilerParams(
            dimension_semantics=("arbitrary", "arbitrary"),
            vmem_limit_bytes=VMEM_LIMIT_BYTES),
        name="mix_route",
    )(x, nmw, win_b, cos2, sin2, dmat, qdec, kdec, rnw, wpool_b, pscale, wout_b, nfw, wrt_b, brt, utri)


def _for_set_bits(n, max_bit, fn):
    def piece(bit):
        size = 1 << bit
        offset = (n >> (bit + 1)) << (bit + 1)

        @pl.when((n & size) != 0)
        def _():
            fn(offset, size)

    min_bit = STRIP_ALIGN.bit_length() - 1
    rare_bit = min(max_bit + 1, RARE_STRIP_BIT)

    @pl.when(n >= (1 << rare_bit))
    def _():
        for bit in range(max_bit, rare_bit - 1, -1):
            piece(bit)
    for bit in range(rare_bit - 1, min_bit - 1, -1):
        piece(bit)


def _aligned_len(n):
    return n + (n & (STRIP_ALIGN - 1))


def _for_strips(nbe_ref, blk, n_exp, max_bit, fn):
    def body(e, carry):
        n = _aligned_len(nbe_ref[blk * n_exp + e])
        _for_set_bits(n, max_bit, functools.partial(fn, e))
        return carry
    lax.fori_loop(0, n_exp, body, 0)


def _dispatch_kernel(lidx_ref, nbe_ref, lb_ref, hrow_ref, extra_ref, padrow_ref, padlen_ref, nused_ref,
                     h2p_ref, xs_hbm, stage_ref, zero_ref, sem, zsem, *, nblk):
    b = pl.program_id(0)
    rpt = zero_ref.shape[0] // ROW_GRANULE
    tb = h2p_ref.shape[0] // rpt
    n_exp = padrow_ref.shape[0]
    n_tok = lidx_ref.shape[0] // TOP_K
    slot_rows = stage_ref.shape[0] // 2
    max_bit = tb.bit_length() - 1
    slot = b % 2

    def strip_copy(blk, sl, e, offset, size):
        src = pl.multiple_of(sl * slot_rows + (lb_ref[blk * n_exp + e] + offset) * rpt, SUBLANES)
        dst = pl.multiple_of((hrow_ref[blk * n_exp + e] + offset) * rpt, SUBLANES)
        return pltpu.make_async_copy(stage_ref.at[pl.ds(src, size * rpt), :],
                                     xs_hbm.at[pl.ds(dst, size * rpt), :], sem.at[sl])

    def start_strips(blk, sl):
        _for_strips(nbe_ref, blk, n_exp, max_bit,
                    lambda e, offset, size: strip_copy(blk, sl, e, offset, size).start())

    def wait_strips(blk, sl):
        def sized_copy(offset, size):
            return pltpu.make_async_copy(stage_ref.at[pl.ds(sl * slot_rows, size * rpt), :],
                                         xs_hbm.at[pl.ds(0, size * rpt), :], sem.at[sl])
        sized_copy(0, tb * TOP_K).wait()
        _for_set_bits(extra_ref[blk], n_exp.bit_length() - 1, lambda o, s: sized_copy(o, s).wait())

    @pl.when(b >= 2)
    def _():
        wait_strips(b - 2, slot)

    def scatter_block(static_slot):
        def scatter_group(c, carry):
            tok0 = c * GATHER_UNROLL
            idx0 = b * tb + tok0
            for u in range(GATHER_UNROLL):
                row = h2p_ref[pl.ds(pl.multiple_of((tok0 + u) * rpt, rpt), rpt), :]
                for k in range(TOP_K):
                    dst = pl.multiple_of(lidx_ref[idx0 + (k * n_tok + u)], rpt)
                    stage_ref[pl.ds(static_slot * slot_rows + dst, rpt), :] = row
            return carry
        lax.fori_loop(0, tb // GATHER_UNROLL, scatter_group, 0)

    for static_slot in range(2):
        pl.when(slot == static_slot)(functools.partial(scatter_block, static_slot))

    def zero_tail(e, carry):
        n = nbe_ref[b * n_exp + e]

        @pl.when((n & (STRIP_ALIGN - 1)) != 0)
        def _():
            dst = pl.multiple_of(slot * slot_rows + (lb_ref[b * n_exp + e] + n) * rpt, rpt)
            stage_ref[pl.ds(dst, rpt), :] = jnp.zeros((rpt, LANES), jnp.uint32)
        return carry
    lax.fori_loop(0, n_exp, zero_tail, 0)

    start_strips(b, slot)

    @pl.when(b == nblk - 1)
    def _():
        if nblk >= 2:
            wait_strips(b - 1, 1 - slot)
        wait_strips(b, slot)
        zero_ref[...] = jnp.zeros_like(zero_ref)
        pad_bit = ROW_GRANULE.bit_length() - 2

        def pad_copy(e, offset, size):
            dst = pl.multiple_of((padrow_ref[e] + offset) * rpt, SUBLANES)
            return pltpu.make_async_copy(zero_ref.at[pl.ds(0, size * rpt), :],
                                         xs_hbm.at[pl.ds(dst, size * rpt), :], zsem)

        def pad_start(e, carry):
            _for_set_bits(padlen_ref[e], pad_bit, lambda o, s: pad_copy(e, o, s).start())
            return carry

        def pad_wait(e, carry):
            _for_set_bits(padlen_ref[e], pad_bit, lambda o, s: pad_copy(e, o, s).wait())
            return carry
        lax.fori_loop(0, n_exp, pad_start, 0)

        def tail_copy(i):
            dst = pl.multiple_of(i * (ROW_GRANULE * rpt), SUBLANES)
            return pltpu.make_async_copy(zero_ref, xs_hbm.at[pl.ds(dst, ROW_GRANULE * rpt), :], zsem)

        def tail_start(i, carry):
            tail_copy(i).start()
            return carry

        def tail_wait(i, carry):
            tail_copy(i).wait()
            return carry
        n_gran = xs_hbm.shape[0] // (ROW_GRANULE * rpt)
        lax.fori_loop(nused_ref[0], n_gran, tail_start, 0)
        lax.fori_loop(0, n_exp, pad_wait, 0)
        lax.fori_loop(nused_ref[0], n_gran, tail_wait, 0)


def _dispatch(lidx, nbe, lb, hrow, extra, padrow, padlen, n_used, h2p, n_rows, d_model):
    tb = TOK_BLOCK
    rpt = d_model // PACK // LANES
    n_tok = h2p.shape[0] // rpt
    n_exp = padrow.shape[0]
    slot_rows = (tb * TOP_K + n_exp * (STRIP_ALIGN - 1)) * rpt
    grid_spec = pltpu.PrefetchScalarGridSpec(
        num_scalar_prefetch=8,
        grid=(n_tok // tb,),
        in_specs=[pl.BlockSpec((tb * rpt, LANES), lambda i, *_: (i, 0))],
        out_specs=pl.BlockSpec(memory_space=pl.ANY),
        scratch_shapes=[pltpu.VMEM((2 * slot_rows, LANES), jnp.uint32),
                        pltpu.VMEM((ROW_GRANULE * rpt, LANES), jnp.uint32),
                        pltpu.SemaphoreType.DMA((2,)),
                        pltpu.SemaphoreType.DMA],
    )
    return pl.pallas_call(
        functools.partial(_dispatch_kernel, nblk=n_tok // tb),
        grid_spec=grid_spec,
        out_shape=jax.ShapeDtypeStruct((n_rows * rpt, LANES), jnp.uint32),
        compiler_params=pltpu.CompilerParams(
            dimension_semantics=("arbitrary",),
            vmem_limit_bytes=VMEM_LIMIT_BYTES),
        name="dispatch",
    )(lidx, nbe, lb, hrow, extra, padrow, padlen, n_used, h2p)


def _experts_kernel(tstart_ref, tcount_ref, wgu_hbm, bgu_ref, wd_hbm, bd_ref, xs_hbm,
                    y_hbm, wgu_f, wd_f, wgu_b, wd_b, xbuf, ybuf, pend_ref, wsem, xsem, ysem, *, n_exp):
    e = pl.program_id(0)
    d_ff = wd_f.shape[0]
    d_model = wd_f.shape[1]
    rpt = d_model // PACK // LANES
    gran_rows = ROW_GRANULE * rpt
    gpt = ROW_TILE // ROW_GRANULE
    n_gran = y_hbm.shape[0] // gran_rows
    g0 = tstart_ref[e]
    ng = tcount_ref[e]
    nt = ng // gpt
    tail = ng - nt * gpt

    def hbm_rows(ref, gran, count):
        return ref.at[pl.ds(pl.multiple_of(gran * gran_rows, SUBLANES), count * gran_rows), :]

    def x_copy(gran, sl, count=gpt):
        return pltpu.make_async_copy(hbm_rows(xs_hbm, gran, count),
                                     xbuf.at[sl, pl.ds(0, count * gran_rows), :], xsem.at[sl])

    def y_copy(gran, sl, count=gpt):
        return pltpu.make_async_copy(ybuf.at[sl, pl.ds(0, count * gran_rows), :],
                                     hbm_rows(y_hbm, gran, count), ysem.at[sl])

    def mlp(sl, tm):
        x = _unpack_rows(xbuf.at[sl], tm, rpt)
        gu = jnp.dot(x, wgu_b[...], preferred_element_type=jnp.float32) + bgu_ref[0]
        x_glu = jnp.minimum(gu[:, :d_ff], SWIGLU_LIMIT)
        x_lin = jnp.clip(gu[:, d_ff:], -SWIGLU_LIMIT, SWIGLU_LIMIT)
        act = x_glu * (1.0 / (1.0 + jnp.exp(-SWIGLU_ALPHA * x_glu))) * (x_lin + 1.0)
        y = jnp.dot(act.astype(jnp.bfloat16), wd_b[...], preferred_element_type=jnp.float32) + bd_ref[0]
        _pack_rows(y, ybuf.at[sl], tm)

    def weight_copies(ex):
        return (pltpu.make_async_copy(wgu_hbm.at[ex], wgu_f, wsem.at[0]),
                pltpu.make_async_copy(wd_hbm.at[ex], wd_f, wsem.at[1]))

    tail_slot = 2
    tail_gran = g0 + nt * gpt

    def wait_pending_tail_y():
        for count in range(1, gpt):
            @pl.when(pend_ref[0] == count)
            def _():
                y_copy(0, tail_slot, count).wait()

    @pl.when(e == 0)
    def _():
        pend_ref[0] = 0

        @pl.when(nt > 0)
        def _():
            x_copy(g0, 0).start()

    for count in range(1, gpt):
        @pl.when(tail == count)
        def _():
            x_copy(tail_gran, tail_slot, count).start()

    @pl.when(e == 0)
    def _():
        for c in weight_copies(0):
            c.start(priority=1)
    for c in weight_copies(e):
        c.wait()
    wgu_b[...] = wgu_f[...].astype(jnp.bfloat16)
    wd_b[...] = wd_f[...].astype(jnp.bfloat16)

    @pl.when(e + 1 < n_exp)
    def _():
        for c in weight_copies(e + 1):
            c.start(priority=1)

    @pl.when(ng > 0)
    def _():
        def tile_step(j, carry):
            sl = j % 2
            gran = g0 + j * gpt

            @pl.when(j + 1 < nt)
            def _():
                x_copy(gran + gpt, 1 - sl).start()
            x_copy(gran, sl).wait()

            @pl.when(j >= 2)
            def _():
                y_copy(gran - 2 * gpt, sl).wait()
            mlp(sl, ROW_TILE)
            y_copy(gran, sl).start()
            return carry
        lax.fori_loop(0, nt, tile_step, 0)

        @pl.when(nt >= 2)
        def _():
            y_copy(g0 + (nt - 2) * gpt, nt % 2).wait()

        @pl.when(nt >= 1)
        def _():
            y_copy(g0 + (nt - 1) * gpt, (nt - 1) % 2).wait()

    @pl.when(e + 1 < n_exp)
    def _():
        nxt = jnp.minimum(e + 1, n_exp - 1)

        @pl.when(tcount_ref[nxt] >= gpt)
        def _():
            x_copy(tstart_ref[nxt], 0).start()

    for count in range(1, gpt):
        @pl.when(tail == count)
        def _():
            x_copy(tail_gran, tail_slot, count).wait()
            wait_pending_tail_y()
            mlp(tail_slot, count * ROW_GRANULE)
            y_copy(tail_gran, tail_slot, count).start()
            pend_ref[0] = count

    @pl.when(e == n_exp - 1)
    def _():
        wait_pending_tail_y()
        ybuf[0, 0:gran_rows, :] = jnp.zeros((gran_rows, LANES), ybuf.dtype)

        def tail_start(i, carry):
            y_copy(i, 0, 1).start()
            return carry

        def tail_wait(i, carry):
            y_copy(i, 0, 1).wait()
            return carry
        lax.fori_loop(g0 + ng, n_gran, tail_start, 0)
        lax.fori_loop(g0 + ng, n_gran, tail_wait, 0)


def _experts(tile_start, tile_count, xs, w_gu, b_gu, w_down, b_down):
    tm = ROW_TILE
    n_exp, d_model, d_gu = w_gu.shape
    d_ff = w_down.shape[1]
    rpt = d_model // PACK // LANES
    grid_spec = pltpu.PrefetchScalarGridSpec(
        num_scalar_prefetch=2,
        grid=(n_exp,),
        in_specs=[
            pl.BlockSpec(memory_space=pl.ANY),
            pl.BlockSpec((1, 1, d_gu), lambda e, *_: (e, 0, 0)),
            pl.BlockSpec(memory_space=pl.ANY),
            pl.BlockSpec((1, 1, d_model), lambda e, *_: (e, 0, 0)),
            pl.BlockSpec(memory_space=pl.ANY),
        ],
        out_specs=pl.BlockSpec(memory_space=pl.ANY),
        scratch_shapes=[pltpu.VMEM((d_model, d_gu), jnp.float32),
                        pltpu.VMEM((d_ff, d_model), jnp.float32),
                        pltpu.VMEM((d_model, d_gu), jnp.bfloat16),
                        pltpu.VMEM((d_ff, d_model), jnp.bfloat16),
                        pltpu.VMEM((3, tm * rpt, LANES), jnp.uint32),
                        pltpu.VMEM((3, tm * rpt, LANES), jnp.uint32),
                        pltpu.SMEM((1,), jnp.int32),
                        pltpu.SemaphoreType.DMA((2,)),
                        pltpu.SemaphoreType.DMA((3,)),
                        pltpu.SemaphoreType.DMA((3,))],
    )
    return pl.pallas_call(
        functools.partial(_experts_kernel, n_exp=n_exp),
        grid_spec=grid_spec,
        out_shape=jax.ShapeDtypeStruct(xs.shape, jnp.uint32),
        compiler_params=pltpu.CompilerParams(
            dimension_semantics=("arbitrary",),
            vmem_limit_bytes=VMEM_LIMIT_BYTES),
        name="experts",
    )(tile_start, tile_count, w_gu, b_gu.reshape(n_exp, 1, d_gu), w_down,
      b_down.reshape(n_exp, 1, d_model), xs)


def _combine_kernel(lidx_ref, gate_ref, nbe_ref, lb_ref, hrow_ref, extra_ref,
                    x1_ref, nw_ref, y_hbm, out_ref, stage_ref, alo_ref, ahi_ref, sem, *, nblk):
    b = pl.program_id(0)
    tb = x1_ref.shape[0]
    rpt = alo_ref.shape[0] // tb
    n_exp = nbe_ref.shape[0] // nblk
    n_tok = lidx_ref.shape[0] // TOP_K
    slot_rows = stage_ref.shape[0] // 2
    max_bit = tb.bit_length() - 1
    slot = b % 2

    def strip_copy(blk, sl, e, offset, size):
        src = pl.multiple_of((hrow_ref[blk * n_exp + e] + offset) * rpt, SUBLANES)
        dst = pl.multiple_of(sl * slot_rows + (lb_ref[blk * n_exp + e] + offset) * rpt, SUBLANES)
        return pltpu.make_async_copy(y_hbm.at[pl.ds(src, size * rpt), :],
                                     stage_ref.at[pl.ds(dst, size * rpt), :], sem.at[sl])

    def start_strips(blk, sl):
        _for_strips(nbe_ref, blk, n_exp, max_bit,
                    lambda e, offset, size: strip_copy(blk, sl, e, offset, size).start())

    @pl.when(b == 0)
    def _():
        start_strips(0, 0)

    @pl.when(b + 1 < nblk)
    def _():
        start_strips(b + 1, 1 - slot)

    def sized_copy(offset, size):
        return pltpu.make_async_copy(y_hbm.at[pl.ds(0, size * rpt), :],
                                     stage_ref.at[pl.ds(slot * slot_rows, size * rpt), :], sem.at[slot])
    sized_copy(0, tb * TOP_K).wait()
    _for_set_bits(extra_ref[b], n_exp.bit_length() - 1, lambda o, s: sized_copy(o, s).wait())

    def gather_block(static_slot):
        def gather_group(c, carry):
            tok0 = c * GATHER_UNROLL
            idx0 = b * tb + tok0
            for u in range(GATHER_UNROLL):
                lo = hi = None
                for k in range(TOP_K):
                    src = pl.multiple_of(lidx_ref[idx0 + (k * n_tok + u)], rpt)
                    g = gate_ref[idx0 + (k * n_tok + u)]
                    l, h = _unpack_words(stage_ref[pl.ds(static_slot * slot_rows + src, rpt), :])
                    lo = g * l if lo is None else lo + g * l
                    hi = g * h if hi is None else hi + g * h
                dst = pl.multiple_of((tok0 + u) * rpt, rpt)
                alo_ref[pl.ds(dst, rpt), :] = lo
                ahi_ref[pl.ds(dst, rpt), :] = hi
            return carry
        lax.fori_loop(0, tb // GATHER_UNROLL, gather_group, 0)

    for static_slot in range(2):
        pl.when(slot == static_slot)(functools.partial(gather_block, static_slot))

    parts = [alo_ref[pl.ds(si, tb, stride=rpt), :] for si in range(rpt)]
    parts += [ahi_ref[pl.ds(si, tb, stride=rpt), :] for si in range(rpt)]
    x2 = x1_ref[...] + jnp.concatenate(parts, axis=-1)
    out_ref[...] = _rms(x2, nw_ref[...])


def _combine(lidx, gates, nbe, lb, hrow, extra, x1, norm_w, y_rows):
    tb = TOK_BLOCK
    n_tok, d_model = x1.shape
    rpt = d_model // PACK // LANES
    n_exp = nbe.shape[0] // (n_tok // tb)
    slot_rows = (tb * TOP_K + n_exp * (STRIP_ALIGN - 1)) * rpt
    grid_spec = pltpu.PrefetchScalarGridSpec(
        num_scalar_prefetch=6,
        grid=(n_tok // tb,),
        in_specs=[
            pl.BlockSpec((tb, d_model), lambda i, *_: (i, 0)),
            pl.BlockSpec((1, d_model), lambda i, *_: (0, 0)),
            pl.BlockSpec(memory_space=pl.ANY),
        ],
        out_specs=pl.BlockSpec((tb, d_model), lambda i, *_: (i, 0)),
        scratch_shapes=[pltpu.VMEM((2 * slot_rows, LANES), jnp.uint32),
                        pltpu.VMEM((tb * rpt, LANES), jnp.float32),
                        pltpu.VMEM((tb * rpt, LANES), jnp.float32),
                        pltpu.SemaphoreType.DMA((2,))],
    )
    return pl.pallas_call(
        functools.partial(_combine_kernel, nblk=n_tok // tb),
        grid_spec=grid_spec,
        out_shape=jax.ShapeDtypeStruct((n_tok, d_model), jnp.float32),
        compiler_params=pltpu.CompilerParams(
            dimension_semantics=("arbitrary",),
            vmem_limit_bytes=VMEM_LIMIT_BYTES),
        name="combine",
    )(lidx, gates, nbe, lb, hrow, extra, x1, norm_w, y_rows)


def _retention_tables(seq, hd, ts):
    half = hd // 2
    inv = ROPE_BASE ** (-jnp.arange(half, dtype=jnp.float32) / half)
    ang = jnp.arange(seq, dtype=jnp.float32)[:, None] * inv[None, :]
    cos, sin = jnp.cos(ang), jnp.sin(ang)
    cos2 = jnp.concatenate([cos, cos], axis=-1)
    sin2 = jnp.concatenate([-sin, sin], axis=-1)
    log_g = jnp.log1p(-jnp.exp2(-5.0 - jnp.arange(RET_HEADS, dtype=jnp.float32)))
    pos = jnp.arange(ts, dtype=jnp.float32)
    dist = jnp.abs(pos[:, None] - pos[None, :])
    chunk = jnp.arange(ts, dtype=jnp.int32) // CHUNK
    visible = chunk[None, :] <= chunk[:, None]
    dmat = jnp.where(visible[None], jnp.exp(log_g[:, None, None] * dist[None]), 0.0)
    qdec = jnp.exp(log_g[:, None] * (pos + 1.0)[None, :])
    kdec = jnp.exp(log_g[:, None] * (ts - 1.0 - pos)[None, :])
    qdec = jnp.broadcast_to(qdec[:, :, None], (RET_HEADS, ts, hd))
    kdec = jnp.broadcast_to(kdec[:, :, None], (RET_HEADS, ts, hd))
    tile_decay = tuple(math.exp(math.log1p(-2.0 ** (-5.0 - h)) * ts) for h in range(RET_HEADS))
    return cos2, sin2, dmat, qdec, kdec, tile_decay


def _routing_tables(eid, rank, before, counts, n_tok, n_exp, rows_per_tok):
    tm, tb, ts = ROW_GRANULE, TOK_BLOCK, SEQ_TILE
    cb = before[::tb // ts]
    nbe = jnp.concatenate([cb[1:], counts[None]], axis=0) - cb
    nal = nbe + (nbe & (STRIP_ALIGN - 1))
    region = jnp.sum(nal, axis=0)
    tiles_per = (region + tm - 1) // tm
    tile_end = jnp.cumsum(tiles_per)
    offs = (tile_end - tiles_per) * tm
    n_tiles = (n_tok * TOP_K + (n_tok // tb) * n_exp * (STRIP_ALIGN - 1)) // tm + n_exp
    hrow = offs[None, :] + jnp.cumsum(nal, axis=0) - nal
    lb = jnp.cumsum(nal, axis=1) - nal
    delta = jnp.repeat(lb - cb, tb, axis=0)
    onehot = eid[:, :, None] == jnp.arange(n_exp, dtype=jnp.int32)
    lidx = rank + jnp.sum(jnp.where(onehot, delta[None], 0), axis=-1)
    lidx = lidx * rows_per_tok
    padrow = offs + region
    padlen = tiles_per * tm - region
    as_i32 = lambda a: a.reshape(-1).astype(jnp.int32)
    return (n_tiles, as_i32(tile_end - tiles_per), as_i32(tiles_per), as_i32(tile_end[-1:]),
            as_i32(lidx), as_i32(nbe), as_i32(lb), as_i32(hrow),
            as_i32(jnp.sum(nal, axis=1) - tb * TOP_K), as_i32(padrow), as_i32(padlen))


def _layer(x, norm_mix_w, w_in, ret_norm_w, w_pool, pool_scale, w_out, norm_ffn_w,
           w_router, b_router, w_gu, b_gu, w_down, b_down, final_w):
    bsz, seq, d_model = x.shape
    n_tok = bsz * seq
    n_exp = w_router.shape[1]
    rw = ret_norm_w.shape[0]
    hd = rw // RET_HEADS
    ts = SEQ_TILE
    bf = jnp.bfloat16
    rpt = d_model // PACK // LANES
    assert d_model % (PACK * LANES) == 0 and (STRIP_ALIGN * rpt) % SUBLANES == 0
    assert seq % ts == 0 and ts % CHUNK == 0 and n_tok % TOK_BLOCK == 0 and TOK_BLOCK % ts == 0
    assert TOK_BLOCK & (TOK_BLOCK - 1) == 0 and ROW_GRANULE & (ROW_GRANULE - 1) == 0
    assert ROW_TILE % ROW_GRANULE == 0

    cos2, sin2, dmat, qdec, kdec, tile_decay = _retention_tables(seq, hd, ts)
    tri = jnp.arange(ts, dtype=jnp.int32)
    utri = (tri[:, None] < tri[None, :]).astype(bf)

    x1, h2p, eid, gate, rank, before, cnt = _mix_route(
        x, norm_mix_w[None], w_in.astype(bf), cos2, sin2, dmat, qdec, kdec, ret_norm_w[None],
        w_pool.astype(bf), pool_scale[None], w_out.astype(bf), norm_ffn_w[None],
        w_router.T.astype(bf), b_router[:, None], utri, tile_decay)

    (n_tiles, tile_start, tile_count, n_used, lidx, nbe, lb, hrow, extra, padrow, padlen) = _routing_tables(
        eid, rank, before[:, :, 0].astype(jnp.int32), cnt[:, 0].astype(jnp.int32), n_tok, n_exp, rpt)

    xs = _dispatch(lidx, nbe, lb, hrow, extra, padrow, padlen, n_used, h2p, n_tiles * ROW_GRANULE, d_model)
    y_rows = _experts(tile_start, tile_count, xs, w_gu, b_gu, w_down, b_down)
    out = _combine(lidx, gate.reshape(-1), nbe, lb, hrow, extra, x1.reshape(n_tok, d_model), final_w[None], y_rows)
    return out.reshape(bsz, seq, d_model)


def kernel(x, norm_mix_w, w_in, ret_norm_w, w_pool, pool_scale, w_out, norm_ffn_w, w_router, b_router,
           w_gu, b_gu, w_down, b_down, norm_final_w):
    depth = norm_mix_w.shape[0]
    for l in range(depth):
        last = l == depth - 1
        assert last, "stacked layers need an un-normalised combine output"
        x = _layer(x, norm_mix_w[l], w_in[l], ret_norm_w[l], w_pool[l], pool_scale[l], w_out[l],
                   norm_ffn_w[l], w_router[l], b_router[l], w_gu[l], b_gu[l], w_down[l], b_down[l],
                   norm_final_w)
    return x
```

```python
import functools
import math

import jax
import jax.numpy as jnp
from jax import lax
from jax.experimental import pallas as pl
from jax.experimental.pallas import tpu as pltpu

CHUNK = 64
RET_HEADS = 4
POOL_WINDOWS = (2, 4, 8, 16)
ROPE_BASE = 10000.0
TOP_K = 4
SWIGLU_LIMIT = 7.0
SWIGLU_ALPHA = 1.702
EPS = 1e-6

LANES = 128
SUBLANES = 8
VMEM_LIMIT_BYTES = 56 * 1024 * 1024

SEQ_TILE = 256
ROUTE_TILE = 2048
ROUTE_CHUNK = 256
ROW_GRANULE = 128
ROW_TILE = 512
TOK_BLOCK = 1024
POOL_HALO = 16
PACK = 2
STRIP_ALIGN = 2
GATHER_UNROLL = 8
RARE_STRIP_BIT = 8


def _rms(x, w):
    return x * lax.rsqrt(jnp.mean(x * x, axis=-1, keepdims=True) + EPS) * w


def _bf16_bits_hi(x):
    return lax.bitcast_convert_type(x.astype(jnp.bfloat16).astype(jnp.float32), jnp.uint32)


def _pack_rows(x, out_ref, n_rows):
    half = x.shape[1] // PACK
    packed = (_bf16_bits_hi(x[:, :half]) >> 16) | _bf16_bits_hi(x[:, half:])
    rpt = half // LANES
    for si in range(rpt):
        out_ref[pl.ds(si, n_rows, stride=rpt), :] = packed[:, si * LANES:(si + 1) * LANES]


def _unpack_words(u):
    lo = lax.bitcast_convert_type(u << 16, jnp.float32)
    hi = lax.bitcast_convert_type(u & jnp.uint32(0xFFFF0000), jnp.float32)
    return lo, hi


def _unpack_rows(g_ref, n_rows, rpt):
    lo, hi = [], []
    for si in range(rpt):
        l, h = _unpack_words(g_ref[pl.ds(si, n_rows, stride=rpt), :])
        lo.append(l.astype(jnp.bfloat16))
        hi.append(h.astype(jnp.bfloat16))
    return jnp.concatenate(lo + hi, axis=-1)


def _mix_kernel(x_ref, nmw_ref, win_ref, cos_ref, sin_ref, dmat_ref, qdec_ref, kdec_ref,
                rnw_ref, wpool_ref, pscale_ref, wout_ref, nfw_ref,
                x1_ref, h2p_ref,
                state_ref, p1_ref, p2_ref, p4_ref, p8_ref,
                *, tile_decay):
    j = pl.program_id(1)
    ts = x_ref.shape[1]
    n_heads = RET_HEADS
    rw = rnw_ref.shape[1]
    hd = rw // n_heads
    pgw = wpool_ref.shape[1]
    halo = POOL_HALO

    @pl.when(j == 0)
    def _():
        state_ref[...] = jnp.zeros_like(state_ref)
        for r in (p1_ref, p2_ref, p4_ref, p8_ref):
            r[0:halo, :] = jnp.zeros((halo, r.shape[1]), jnp.float32)

    @pl.when(j > 0)
    def _():
        for r in (p1_ref, p2_ref, p4_ref, p8_ref):
            r[0:halo, :] = r[ts:ts + halo, :]

    x = x_ref[0]
    h = _rms(x, nmw_ref[...]).astype(jnp.bfloat16)
    z = jnp.dot(h, win_ref[...], preferred_element_type=jnp.float32)

    cos = cos_ref[...]
    sin = sin_ref[...]
    k_scale = hd ** -0.5

    ret_parts = []
    for hh in range(n_heads):
        q = z[:, hh * hd:(hh + 1) * hd]
        k = z[:, rw + hh * hd:rw + (hh + 1) * hd]
        v = z[:, 2 * rw + hh * hd:2 * rw + (hh + 1) * hd]
        g = z[:, 3 * rw + hh * hd:3 * rw + (hh + 1) * hd]
        q = q * cos + pltpu.roll(q, hd // 2, 1) * sin
        k = (k * cos + pltpu.roll(k, hd // 2, 1) * sin) * k_scale
        qb = q.astype(jnp.bfloat16)
        kb = k.astype(jnp.bfloat16)
        vb = v.astype(jnp.bfloat16)
        s = lax.dot_general(qb, kb, (((1,), (1,)), ((), ())), preferred_element_type=jnp.float32)
        pmat = (s * dmat_ref[hh]).astype(jnp.bfloat16)
        o = jnp.dot(pmat, vb, preferred_element_type=jnp.float32)
        state = state_ref[hh]
        qd = (q * qdec_ref[hh]).astype(jnp.bfloat16)
        o = o + jnp.dot(qd, state.astype(jnp.bfloat16), preferred_element_type=jnp.float32)
        kd = (k * kdec_ref[hh]).astype(jnp.bfloat16)
        u = lax.dot_general(kd, vb, (((0,), (0,)), ((), ())), preferred_element_type=jnp.float32)
        state_ref[hh] = state * tile_decay[hh] + u
        o = o * lax.rsqrt(jnp.mean(o * o, axis=-1, keepdims=True) + EPS)
        o = o * rnw_ref[:, hh * hd:(hh + 1) * hd] * (g * (1.0 / (1.0 + jnp.exp(-g))))
        ret_parts.append(o.astype(jnp.bfloat16))

    p = z[:, 4 * rw:]
    p1_ref[halo:halo + ts, :] = p
    s2 = p + p1_ref[halo - 1:halo - 1 + ts, :]
    p2_ref[halo:halo + ts, :] = s2[:, pgw:]
    s4 = s2[:, pgw:] + p2_ref[halo - 2:halo - 2 + ts, :]
    p4_ref[halo:halo + ts, :] = s4[:, pgw:]
    s8 = s4[:, pgw:] + p4_ref[halo - 4:halo - 4 + ts, :]
    p8_ref[halo:halo + ts, :] = s8[:, pgw:]
    s16 = s8[:, pgw:] + p8_ref[halo - 8:halo - 8 + ts, :]
    sums = (s2[:, :pgw], s4[:, :pgw], s8[:, :pgw], s16)
    tpos = (j * ts + lax.broadcasted_iota(jnp.int32, (ts, pgw), 0) + 1).astype(jnp.float32)
    pool_parts = []
    for gi, w in enumerate(POOL_WINDOWS):
        cnt = jnp.minimum(tpos, float(w))
        pooled = sums[gi] / cnt - p[:, gi * pgw:(gi + 1) * pgw]
        mixed = jnp.dot(pooled.astype(jnp.bfloat16), wpool_ref[gi], preferred_element_type=jnp.float32)
        mixed = mixed * pscale_ref[:, gi * pgw:(gi + 1) * pgw]
        pool_parts.append(mixed.astype(jnp.bfloat16))

    mix = jnp.concatenate(ret_parts + pool_parts, axis=-1)
    x1 = x + jnp.dot(mix, wout_ref[...], preferred_element_type=jnp.float32)
    x1_ref[0] = x1

    h2 = _rms(x1, nfw_ref[...])
    _pack_rows(h2, h2p_ref, ts)


def _mix(x, nmw, win_b, cos2, sin2, dmat, qdec, kdec, rnw, wpool_b, pscale, wout_b, nfw, tile_decay):
    bsz, seq, d_model = x.shape
    ts = SEQ_TILE
    n_tok = bsz * seq
    rw = rnw.shape[1]
    hd = rw // RET_HEADS
    pw = pscale.shape[1]
    pgw = pw // len(POOL_WINDOWS)
    nj = seq // ts
    rows_per_tok = d_model // PACK // LANES

    def const(shape):
        return pl.BlockSpec(shape, lambda b, j: (0,) * len(shape))

    in_specs = [
        pl.BlockSpec((1, ts, d_model), lambda b, j: (b, j, 0)),
        const(nmw.shape), const(win_b.shape),
        pl.BlockSpec((ts, hd), lambda b, j: (j, 0)),
        pl.BlockSpec((ts, hd), lambda b, j: (j, 0)),
        const(dmat.shape), const(qdec.shape), const(kdec.shape),
        const(rnw.shape), const(wpool_b.shape), const(pscale.shape), const(wout_b.shape),
        const(nfw.shape),
    ]
    out_shape = [
        jax.ShapeDtypeStruct((bsz, seq, d_model), jnp.float32),
        jax.ShapeDtypeStruct((n_tok * rows_per_tok, LANES), jnp.uint32),
    ]
    out_specs = [
        pl.BlockSpec((1, ts, d_model), lambda b, j: (b, j, 0)),
        pl.BlockSpec((ts * rows_per_tok, LANES), lambda b, j: (b * nj + j, 0)),
    ]
    scratch = [
        pltpu.VMEM((RET_HEADS, hd, hd), jnp.float32),
        pltpu.VMEM((POOL_HALO + ts, pw), jnp.float32),
        pltpu.VMEM((POOL_HALO + ts, pw - pgw), jnp.float32),
        pltpu.VMEM((POOL_HALO + ts, pw - 2 * pgw), jnp.float32),
        pltpu.VMEM((POOL_HALO + ts, pw - 3 * pgw), jnp.float32),
    ]
    return pl.pallas_call(
        functools.partial(_mix_kernel, tile_decay=tile_decay),
        grid=(bsz, nj),
        in_specs=in_specs,
        out_specs=out_specs,
        out_shape=out_shape,
        scratch_shapes=scratch,
        compiler_params=pltpu.CompilerParams(
            dimension_semantics=("arbitrary", "arbitrary"),
            vmem_limit_bytes=VMEM_LIMIT_BYTES),
        name="mix",
    )(x, nmw, win_b, cos2, sin2, dmat, qdec, kdec, rnw, wpool_b, pscale, wout_b, nfw)


def _route_kernel(h2p_ref, wrt_ref, brt_ref, utri_ref,
                  eid_ref, gate_ref, rank_ref, before_ref, cnt_ref, carry_ref):
    i = pl.program_id(0)
    n_exp, d_model = wrt_ref.shape
    rpt = d_model // PACK // LANES
    tr = h2p_ref.shape[0] // rpt
    sub = utri_ref.shape[0]

    @pl.when(i == 0)
    def _():
        carry_ref[...] = jnp.zeros_like(carry_ref)

    h2b = _unpack_rows(h2p_ref, tr, rpt)
    logits = lax.dot_general(wrt_ref[...], h2b, (((1,), (1,)), ((), ())),
                             preferred_element_type=jnp.float32) + brt_ref[...]
    eiota = lax.broadcasted_iota(jnp.int32, (n_exp, tr), 0)
    work = logits
    vals, ids, sels = [], [], []
    for _ in range(TOP_K):
        m = jnp.max(work, axis=0, keepdims=True)
        idx = jnp.min(jnp.where(work == m, eiota, n_exp), axis=0, keepdims=True)
        sel = eiota == idx
        vals.append(m)
        ids.append(idx)
        sels.append(sel)
        work = jnp.where(sel, -jnp.inf, work)
    ex = [jnp.exp(vk - vals[0]) for vk in vals]
    den = ex[0] + ex[1] + ex[2] + ex[3]
    gate_ref[...] = jnp.concatenate([e / den for e in ex], axis=0)
    eid_ref[...] = jnp.concatenate(ids, axis=0)

    member = jnp.zeros((n_exp, tr), jnp.float32)
    for sel in sels:
        member = member + sel.astype(jnp.float32)
    carry = carry_ref[...]
    befores = []
    for c in range(tr // sub):
        before_ref[c] = carry
        mc = member[:, c * sub:(c + 1) * sub]
        bc = jnp.dot(mc.astype(jnp.bfloat16), utri_ref[...], preferred_element_type=jnp.float32)
        befores.append(bc + carry[:, 0:1])
        carry = carry + jnp.sum(mc, axis=1, keepdims=True)
    before = jnp.concatenate(befores, axis=1)
    ranks = [jnp.sum(jnp.where(sel, before, 0.0), axis=0, keepdims=True) for sel in sels]
    rank_ref[...] = jnp.concatenate(ranks, axis=0).astype(jnp.int32)
    carry_ref[...] = carry
    cnt_ref[...] = carry


def _route(h2p, wrt_b, brt, utri):
    n_exp, d_model = wrt_b.shape
    rpt = d_model // PACK // LANES
    n_tok = h2p.shape[0] // rpt
    tr = ROUTE_TILE
    sub = utri.shape[0]

    def const(shape):
        return pl.BlockSpec(shape, lambda i: (0,) * len(shape))

    out_shape = [
        jax.ShapeDtypeStruct((TOP_K, n_tok), jnp.int32),
        jax.ShapeDtypeStruct((TOP_K, n_tok), jnp.float32),
        jax.ShapeDtypeStruct((TOP_K, n_tok), jnp.int32),
        jax.ShapeDtypeStruct((n_tok // sub, n_exp, LANES), jnp.float32),
        jax.ShapeDtypeStruct((n_exp, LANES), jnp.float32),
    ]
    out_specs = [
        pl.BlockSpec((TOP_K, tr), lambda i: (0, i)),
        pl.BlockSpec((TOP_K, tr), lambda i: (0, i)),
        pl.BlockSpec((TOP_K, tr), lambda i: (0, i)),
        pl.BlockSpec((tr // sub, n_exp, LANES), lambda i: (i, 0, 0)),
        pl.BlockSpec((n_exp, LANES), lambda i: (0, 0)),
    ]
    return pl.pallas_call(
        _route_kernel,
        grid=(n_tok // tr,),
        in_specs=[pl.BlockSpec((tr * rpt, LANES), lambda i: (i, 0)),
                  const(wrt_b.shape), const(brt.shape), const(utri.shape)],
        out_specs=out_specs,
        out_shape=out_shape,
        scratch_shapes=[pltpu.VMEM((n_exp, LANES), jnp.float32)],
        compiler_params=pltpu.CompilerParams(
            dimension_semantics=("arbitrary",),
            vmem_limit_bytes=VMEM_LIMIT_BYTES),
        name="route",
    )(h2p, wrt_b, brt, utri)


def _for_set_bits(n, max_bit, fn):
    def piece(bit):
        size = 1 << bit
        offset = (n >> (bit + 1)) << (bit + 1)

        @pl.when((n & size) != 0)
        def _():
            fn(offset, size)

    min_bit = STRIP_ALIGN.bit_length() - 1
    rare_bit = min(max_bit + 1, RARE_STRIP_BIT)

    @pl.when(n >= (1 << rare_bit))
    def _():
        for bit in range(max_bit, rare_bit - 1, -1):
            piece(bit)
    for bit in range(rare_bit - 1, min_bit - 1, -1):
        piece(bit)


def _aligned_len(n):
    return n + (n & (STRIP_ALIGN - 1))


def _for_strips(nbe_ref, blk, n_exp, max_bit, fn):
    def body(e, carry):
        n = _aligned_len(nbe_ref[blk * n_exp + e])
        _for_set_bits(n, max_bit, functools.partial(fn, e))
        return carry
    lax.fori_loop(0, n_exp, body, 0)


def _dispatch_kernel(lidx_ref, nbe_ref, lb_ref, hrow_ref, extra_ref, padrow_ref, padlen_ref, nused_ref,
                     h2p_ref, xs_hbm, stage_ref, zero_ref, sem, zsem, *, nblk):
    b = pl.program_id(0)
    rpt = zero_ref.shape[0] // ROW_GRANULE
    tb = h2p_ref.shape[0] // rpt
    n_exp = padrow_ref.shape[0]
    n_tok = lidx_ref.shape[0] // TOP_K
    slot_rows = stage_ref.shape[0] // 2
    max_bit = tb.bit_length() - 1
    slot = b % 2

    def strip_copy(blk, sl, e, offset, size):
        src = pl.multiple_of(sl * slot_rows + (lb_ref[blk * n_exp + e] + offset) * rpt, SUBLANES)
        dst = pl.multiple_of((hrow_ref[blk * n_exp + e] + offset) * rpt, SUBLANES)
        return pltpu.make_async_copy(stage_ref.at[pl.ds(src, size * rpt), :],
                                     xs_hbm.at[pl.ds(dst, size * rpt), :], sem.at[sl])

    def start_strips(blk, sl):
        _for_strips(nbe_ref, blk, n_exp, max_bit,
                    lambda e, offset, size: strip_copy(blk, sl, e, offset, size).start())

    def wait_strips(blk, sl):
        def sized_copy(offset, size):
            return pltpu.make_async_copy(stage_ref.at[pl.ds(sl * slot_rows, size * rpt), :],
                                         xs_hbm.at[pl.ds(0, size * rpt), :], sem.at[sl])
        sized_copy(0, tb * TOP_K).wait()
        _for_set_bits(extra_ref[blk], n_exp.bit_length() - 1, lambda o, s: sized_copy(o, s).wait())

    @pl.when(b >= 2)
    def _():
        wait_strips(b - 2, slot)

    def scatter_block(static_slot):
        def scatter_group(c, carry):
            tok0 = c * GATHER_UNROLL
            idx0 = b * tb + tok0
            for u in range(GATHER_UNROLL):
                row = h2p_ref[pl.ds(pl.multiple_of((tok0 + u) * rpt, rpt), rpt), :]
                for k in range(TOP_K):
                    dst = pl.multiple_of(lidx_ref[idx0 + (k * n_tok + u)], rpt)
                    stage_ref[pl.ds(static_slot * slot_rows + dst, rpt), :] = row
            return carry
        lax.fori_loop(0, tb // GATHER_UNROLL, scatter_group, 0)

    for static_slot in range(2):
        pl.when(slot == static_slot)(functools.partial(scatter_block, static_slot))

    def zero_tail(e, carry):
        n = nbe_ref[b * n_exp + e]

        @pl.when((n & (STRIP_ALIGN - 1)) != 0)
        def _():
            dst = pl.multiple_of(slot * slot_rows + (lb_ref[b * n_exp + e] + n) * rpt, rpt)
            stage_ref[pl.ds(dst, rpt), :] = jnp.zeros((rpt, LANES), jnp.uint32)
        return carry
    lax.fori_loop(0, n_exp, zero_tail, 0)

    start_strips(b, slot)

    @pl.when(b == nblk - 1)
    def _():
        if nblk >= 2:
            wait_strips(b - 1, 1 - slot)
        wait_strips(b, slot)
        zero_ref[...] = jnp.zeros_like(zero_ref)
        pad_bit = ROW_GRANULE.bit_length() - 2

        def pad_copy(e, offset, size):
            dst = pl.multiple_of((padrow_ref[e] + offset) * rpt, SUBLANES)
            return pltpu.make_async_copy(zero_ref.at[pl.ds(0, size * rpt), :],
                                         xs_hbm.at[pl.ds(dst, size * rpt), :], zsem)

        def pad_start(e, carry):
            _for_set_bits(padlen_ref[e], pad_bit, lambda o, s: pad_copy(e, o, s).start())
            return carry

        def pad_wait(e, carry):
            _for_set_bits(padlen_ref[e], pad_bit, lambda o, s: pad_copy(e, o, s).wait())
            return carry
        lax.fori_loop(0, n_exp, pad_start, 0)

        def tail_copy(i):
            dst = pl.multiple_of(i * (ROW_GRANULE * rpt), SUBLANES)
            return pltpu.make_async_copy(zero_ref, xs_hbm.at[pl.ds(dst, ROW_GRANULE * rpt), :], zsem)

        def tail_start(i, carry):
            tail_copy(i).start()
            return carry

        def tail_wait(i, carry):
            tail_copy(i).wait()
            return carry
        n_gran = xs_hbm.shape[0] // (ROW_GRANULE * rpt)
        lax.fori_loop(nused_ref[0], n_gran, tail_start, 0)
        lax.fori_loop(0, n_exp, pad_wait, 0)
        lax.fori_loop(nused_ref[0], n_gran, tail_wait, 0)


def _dispatch(lidx, nbe, lb, hrow, extra, padrow, padlen, n_used, h2p, n_rows, d_model):
    tb = TOK_BLOCK
    rpt = d_model // PACK // LANES
    n_tok = h2p.shape[0] // rpt
    n_exp = padrow.shape[0]
    slot_rows = (tb * TOP_K + n_exp * (STRIP_ALIGN - 1)) * rpt
    grid_spec = pltpu.PrefetchScalarGridSpec(
        num_scalar_prefetch=8,
        grid=(n_tok // tb,),
        in_specs=[pl.BlockSpec((tb * rpt, LANES), lambda i, *_: (i, 0))],
        out_specs=pl.BlockSpec(memory_space=pl.ANY),
        scratch_shapes=[pltpu.VMEM((2 * slot_rows, LANES), jnp.uint32),
                        pltpu.VMEM((ROW_GRANULE * rpt, LANES), jnp.uint32),
                        pltpu.SemaphoreType.DMA((2,)),
                        pltpu.SemaphoreType.DMA],
    )
    return pl.pallas_call(
        functools.partial(_dispatch_kernel, nblk=n_tok // tb),
        grid_spec=grid_spec,
        out_shape=jax.ShapeDtypeStruct((n_rows * rpt, LANES), jnp.uint32),
        compiler_params=pltpu.CompilerParams(
            dimension_semantics=("arbitrary",),
            vmem_limit_bytes=VMEM_LIMIT_BYTES),
        name="dispatch",
    )(lidx, nbe, lb, hrow, extra, padrow, padlen, n_used, h2p)


def _experts_kernel(tstart_ref, tcount_ref, wgu_hbm, bgu_ref, wd_hbm, bd_ref, xs_hbm,
                    y_hbm, wgu_f, wd_f, wgu_b, wd_b, xbuf, ybuf, pend_ref, wsem, xsem, ysem, *, n_exp):
    e = pl.program_id(0)
    d_ff = wd_f.shape[0]
    d_model = wd_f.shape[1]
    rpt = d_model // PACK // LANES
    gran_rows = ROW_GRANULE * rpt
    gpt = ROW_TILE // ROW_GRANULE
    n_gran = y_hbm.shape[0] // gran_rows
    g0 = tstart_ref[e]
    ng = tcount_ref[e]
    nt = ng // gpt
    tail = ng - nt * gpt

    def hbm_rows(ref, gran, count):
        return ref.at[pl.ds(pl.multiple_of(gran * gran_rows, SUBLANES), count * gran_rows), :]

    def x_copy(gran, sl, count=gpt):
        return pltpu.make_async_copy(hbm_rows(xs_hbm, gran, count),
                                     xbuf.at[sl, pl.ds(0, count * gran_rows), :], xsem.at[sl])

    def y_copy(gran, sl, count=gpt):
        return pltpu.make_async_copy(ybuf.at[sl, pl.ds(0, count * gran_rows), :],
                                     hbm_rows(y_hbm, gran, count), ysem.at[sl])

    def mlp(sl, tm):
        x = _unpack_rows(xbuf.at[sl], tm, rpt)
        gu = jnp.dot(x, wgu_b[...], preferred_element_type=jnp.float32) + bgu_ref[0]
        x_glu = jnp.minimum(gu[:, :d_ff], SWIGLU_LIMIT)
        x_lin = jnp.clip(gu[:, d_ff:], -SWIGLU_LIMIT, SWIGLU_LIMIT)
        act = x_glu * (1.0 / (1.0 + jnp.exp(-SWIGLU_ALPHA * x_glu))) * (x_lin + 1.0)
        y = jnp.dot(act.astype(jnp.bfloat16), wd_b[...], preferred_element_type=jnp.float32) + bd_ref[0]
        _pack_rows(y, ybuf.at[sl], tm)

    def weight_copies(ex):
        return (pltpu.make_async_copy(wgu_hbm.at[ex], wgu_f, wsem.at[0]),
                pltpu.make_async_copy(wd_hbm.at[ex], wd_f, wsem.at[1]))

    tail_slot = 2
    tail_gran = g0 + nt * gpt

    def wait_pending_tail_y():
        for count in range(1, gpt):
            @pl.when(pend_ref[0] == count)
            def _():
                y_copy(0, tail_slot, count).wait()

    @pl.when(e == 0)
    def _():
        pend_ref[0] = 0

        @pl.when(nt > 0)
        def _():
            x_copy(g0, 0).start()

    for count in range(1, gpt):
        @pl.when(tail == count)
        def _():
            x_copy(tail_gran, tail_slot, count).start()

    @pl.when(e == 0)
    def _():
        for c in weight_copies(0):
            c.start(priority=1)
    for c in weight_copies(e):
        c.wait()
    wgu_b[...] = wgu_f[...].astype(jnp.bfloat16)
    wd_b[...] = wd_f[...].astype(jnp.bfloat16)

    @pl.when(e + 1 < n_exp)
    def _():
        for c in weight_copies(e + 1):
            c.start(priority=1)

    @pl.when(ng > 0)
    def _():
        def tile_step(j, carry):
            sl = j % 2
            gran = g0 + j * gpt

            @pl.when(j + 1 < nt)
            def _():
                x_copy(gran + gpt, 1 - sl).start()
            x_copy(gran, sl).wait()

            @pl.when(j >= 2)
            def _():
                y_copy(gran - 2 * gpt, sl).wait()
            mlp(sl, ROW_TILE)
            y_copy(gran, sl).start()
            return carry
        lax.fori_loop(0, nt, tile_step, 0)

        @pl.when(nt >= 2)
        def _():
            y_copy(g0 + (nt - 2) * gpt, nt % 2).wait()

        @pl.when(nt >= 1)
        def _():
            y_copy(g0 + (nt - 1) * gpt, (nt - 1) % 2).wait()

    @pl.when(e + 1 < n_exp)
    def _():
        nxt = jnp.minimum(e + 1, n_exp - 1)

        @pl.when(tcount_ref[nxt] >= gpt)
        def _():
            x_copy(tstart_ref[nxt], 0).start()

    for count in range(1, gpt):
        @pl.when(tail == count)
        def _():
            x_copy(tail_gran, tail_slot, count).wait()
            wait_pending_tail_y()
            mlp(tail_slot, count * ROW_GRANULE)
            y_copy(tail_gran, tail_slot, count).start()
            pend_ref[0] = count

    @pl.when(e == n_exp - 1)
    def _():
        wait_pending_tail_y()
        ybuf[0, 0:gran_rows, :] = jnp.zeros((gran_rows, LANES), ybuf.dtype)

        def tail_start(i, carry):
            y_copy(i, 0, 1).start()
            return carry

        def tail_wait(i, carry):
            y_copy(i, 0, 1).wait()
            return carry
        lax.fori_loop(g0 + ng, n_gran, tail_start, 0)
        lax.fori_loop(g0 + ng, n_gran, tail_wait, 0)


def _experts(tile_start, tile_count, xs, w_gu, b_gu, w_down, b_down):
    tm = ROW_TILE
    n_exp, d_model, d_gu = w_gu.shape
    d_ff = w_down.shape[1]
    rpt = d_model // PACK // LANES
    grid_spec = pltpu.PrefetchScalarGridSpec(
        num_scalar_prefetch=2,
        grid=(n_exp,),
        in_specs=[
            pl.BlockSpec(memory_space=pl.ANY),
            pl.BlockSpec((1, 1, d_gu), lambda e, *_: (e, 0, 0)),
            pl.BlockSpec(memory_space=pl.ANY),
            pl.BlockSpec((1, 1, d_model), lambda e, *_: (e, 0, 0)),
            pl.BlockSpec(memory_space=pl.ANY),
        ],
        out_specs=pl.BlockSpec(memory_space=pl.ANY),
        scratch_shapes=[pltpu.VMEM((d_model, d_gu), jnp.float32),
                        pltpu.VMEM((d_ff, d_model), jnp.float32),
                        pltpu.VMEM((d_model, d_gu), jnp.bfloat16),
                        pltpu.VMEM((d_ff, d_model), jnp.bfloat16),
                        pltpu.VMEM((3, tm * rpt, LANES), jnp.uint32),
                        pltpu.VMEM((3, tm * rpt, LANES), jnp.uint32),
                        pltpu.SMEM((1,), jnp.int32),
                        pltpu.SemaphoreType.DMA((2,)),
                        pltpu.SemaphoreType.DMA((3,)),
                        pltpu.SemaphoreType.DMA((3,))],
    )
    return pl.pallas_call(
        functools.partial(_experts_kernel, n_exp=n_exp),
        grid_spec=grid_spec,
        out_shape=jax.ShapeDtypeStruct(xs.shape, jnp.uint32),
        compiler_params=pltpu.CompilerParams(
            dimension_semantics=("arbitrary",),
            vmem_limit_bytes=VMEM_LIMIT_BYTES),
        name="experts",
    )(tile_start, tile_count, w_gu, b_gu.reshape(n_exp, 1, d_gu), w_down,
      b_down.reshape(n_exp, 1, d_model), xs)


def _combine_kernel(lidx_ref, gate_ref, nbe_ref, lb_ref, hrow_ref, extra_ref,
                    x1_ref, nw_ref, y_hbm, out_ref, stage_ref, alo_ref, ahi_ref, sem, *, nblk):
    b = pl.program_id(0)
    tb = x1_ref.shape[0]
    rpt = alo_ref.shape[0] // tb
    n_exp = nbe_ref.shape[0] // nblk
    n_tok = lidx_ref.shape[0] // TOP_K
    slot_rows = stage_ref.shape[0] // 2
    max_bit = tb.bit_length() - 1
    slot = b % 2

    def strip_copy(blk, sl, e, offset, size):
        src = pl.multiple_of((hrow_ref[blk * n_exp + e] + offset) * rpt, SUBLANES)
        dst = pl.multiple_of(sl * slot_rows + (lb_ref[blk * n_exp + e] + offset) * rpt, SUBLANES)
        return pltpu.make_async_copy(y_hbm.at[pl.ds(src, size * rpt), :],
                                     stage_ref.at[pl.ds(dst, size * rpt), :], sem.at[sl])

    def start_strips(blk, sl):
        _for_strips(nbe_ref, blk, n_exp, max_bit,
                    lambda e, offset, size: strip_copy(blk, sl, e, offset, size).start())

    @pl.when(b == 0)
    def _():
        start_strips(0, 0)

    @pl.when(b + 1 < nblk)
    def _():
        start_strips(b + 1, 1 - slot)

    def sized_copy(offset, size):
        return pltpu.make_async_copy(y_hbm.at[pl.ds(0, size * rpt), :],
                                     stage_ref.at[pl.ds(slot * slot_rows, size * rpt), :], sem.at[slot])
    sized_copy(0, tb * TOP_K).wait()
    _for_set_bits(extra_ref[b], n_exp.bit_length() - 1, lambda o, s: sized_copy(o, s).wait())

    def gather_block(static_slot):
        def gather_group(c, carry):
            tok0 = c * GATHER_UNROLL
            idx0 = b * tb + tok0
            for u in range(GATHER_UNROLL):
                lo = hi = None
                for k in range(TOP_K):
                    src = pl.multiple_of(lidx_ref[idx0 + (k * n_tok + u)], rpt)
                    g = gate_ref[idx0 + (k * n_tok + u)]
                    l, h = _unpack_words(stage_ref[pl.ds(static_slot * slot_rows + src, rpt), :])
                    lo = g * l if lo is None else lo + g * l
                    hi = g * h if hi is None else hi + g * h
                dst = pl.multiple_of((tok0 + u) * rpt, rpt)
                alo_ref[pl.ds(dst, rpt), :] = lo
                ahi_ref[pl.ds(dst, rpt), :] = hi
            return carry
        lax.fori_loop(0, tb // GATHER_UNROLL, gather_group, 0)

    for static_slot in range(2):
        pl.when(slot == static_slot)(functools.partial(gather_block, static_slot))

    parts = [alo_ref[pl.ds(si, tb, stride=rpt), :] for si in range(rpt)]
    parts += [ahi_ref[pl.ds(si, tb, stride=rpt), :] for si in range(rpt)]
    x2 = x1_ref[...] + jnp.concatenate(parts, axis=-1)
    out_ref[...] = _rms(x2, nw_ref[...])


def _combine(lidx, gates, nbe, lb, hrow, extra, x1, norm_w, y_rows):
    tb = TOK_BLOCK
    n_tok, d_model = x1.shape
    rpt = d_model // PACK // LANES
    n_exp = nbe.shape[0] // (n_tok // tb)
    slot_rows = (tb * TOP_K + n_exp * (STRIP_ALIGN - 1)) * rpt
    grid_spec = pltpu.PrefetchScalarGridSpec(
        num_scalar_prefetch=6,
        grid=(n_tok // tb,),
        in_specs=[
            pl.BlockSpec((tb, d_model), lambda i, *_: (i, 0)),
            pl.BlockSpec((1, d_model), lambda i, *_: (0, 0)),
            pl.BlockSpec(memory_space=pl.ANY),
        ],
        out_specs=pl.BlockSpec((tb, d_model), lambda i, *_: (i, 0)),
        scratch_shapes=[pltpu.VMEM((2 * slot_rows, LANES), jnp.uint32),
                        pltpu.VMEM((tb * rpt, LANES), jnp.float32),
                        pltpu.VMEM((tb * rpt, LANES), jnp.float32),
                        pltpu.SemaphoreType.DMA((2,))],
    )
    return pl.pallas_call(
        functools.partial(_combine_kernel, nblk=n_tok // tb),
        grid_spec=grid_spec,
        out_shape=jax.ShapeDtypeStruct((n_tok, d_model), jnp.float32),
        compiler_params=pltpu.CompilerParams(
            dimension_semantics=("arbitrary",),
            vmem_limit_bytes=VMEM_LIMIT_BYTES),
        name="combine",
    )(lidx, gates, nbe, lb, hrow, extra, x1, norm_w, y_rows)


def _retention_tables(seq, hd, ts):
    half = hd // 2
    inv = ROPE_BASE ** (-jnp.arange(half, dtype=jnp.float32) / half)
    ang = jnp.arange(seq, dtype=jnp.float32)[:, None] * inv[None, :]
    cos, sin = jnp.cos(ang), jnp.sin(ang)
    cos2 = jnp.concatenate([cos, cos], axis=-1)
    sin2 = jnp.concatenate([-sin, sin], axis=-1)
    log_g = jnp.log1p(-jnp.exp2(-5.0 - jnp.arange(RET_HEADS, dtype=jnp.float32)))
    pos = jnp.arange(ts, dtype=jnp.float32)
    dist = jnp.abs(pos[:, None] - pos[None, :])
    chunk = jnp.arange(ts, dtype=jnp.int32) // CHUNK
    visible = chunk[None, :] <= chunk[:, None]
    dmat = jnp.where(visible[None], jnp.exp(log_g[:, None, None] * dist[None]), 0.0)
    qdec = jnp.exp(log_g[:, None] * (pos + 1.0)[None, :])
    kdec = jnp.exp(log_g[:, None] * (ts - 1.0 - pos)[None, :])
    qdec = jnp.broadcast_to(qdec[:, :, None], (RET_HEADS, ts, hd))
    kdec = jnp.broadcast_to(kdec[:, :, None], (RET_HEADS, ts, hd))
    tile_decay = tuple(math.exp(math.log1p(-2.0 ** (-5.0 - h)) * ts) for h in range(RET_HEADS))
    return cos2, sin2, dmat, qdec, kdec, tile_decay


def _routing_tables(eid, rank, before, counts, n_tok, n_exp, rows_per_tok):
    tm, tb, ts = ROW_GRANULE, TOK_BLOCK, ROUTE_CHUNK
    cb = before[::tb // ts]
    nbe = jnp.concatenate([cb[1:], counts[None]], axis=0) - cb
    nal = nbe + (nbe & (STRIP_ALIGN - 1))
    region = jnp.sum(nal, axis=0)
    tiles_per = (region + tm - 1) // tm
    tile_end = jnp.cumsum(tiles_per)
    offs = (tile_end - tiles_per) * tm
    n_tiles = (n_tok * TOP_K + (n_tok // tb) * n_exp * (STRIP_ALIGN - 1)) // tm + n_exp
    hrow = offs[None, :] + jnp.cumsum(nal, axis=0) - nal
    lb = jnp.cumsum(nal, axis=1) - nal
    delta = jnp.repeat(lb - cb, tb, axis=0)
    onehot = eid[:, :, None] == jnp.arange(n_exp, dtype=jnp.int32)
    lidx = rank + jnp.sum(jnp.where(onehot, delta[None], 0), axis=-1)
    lidx = lidx * rows_per_tok
    padrow = offs + region
    padlen = tiles_per * tm - region
    as_i32 = lambda a: a.reshape(-1).astype(jnp.int32)
    return (n_tiles, as_i32(tile_end - tiles_per), as_i32(tiles_per), as_i32(tile_end[-1:]),
            as_i32(lidx), as_i32(nbe), as_i32(lb), as_i32(hrow),
            as_i32(jnp.sum(nal, axis=1) - tb * TOP_K), as_i32(padrow), as_i32(padlen))


def _layer(x, norm_mix_w, w_in, ret_norm_w, w_pool, pool_scale, w_out, norm_ffn_w,
           w_router, b_router, w_gu, b_gu, w_down, b_down, final_w):
    bsz, seq, d_model = x.shape
    n_tok = bsz * seq
    n_exp = w_router.shape[1]
    rw = ret_norm_w.shape[0]
    hd = rw // RET_HEADS
    ts = SEQ_TILE
    bf = jnp.bfloat16
    rpt = d_model // PACK // LANES
    assert d_model % (PACK * LANES) == 0 and (STRIP_ALIGN * rpt) % SUBLANES == 0
    assert seq % ts == 0 and ts % CHUNK == 0 and n_tok % TOK_BLOCK == 0
    assert n_tok % ROUTE_TILE == 0 and ROUTE_TILE % ROUTE_CHUNK == 0 and TOK_BLOCK % ROUTE_CHUNK == 0
    assert TOK_BLOCK & (TOK_BLOCK - 1) == 0 and ROW_GRANULE & (ROW_GRANULE - 1) == 0
    assert ROW_TILE % ROW_GRANULE == 0

    cos2, sin2, dmat, qdec, kdec, tile_decay = _retention_tables(seq, hd, ts)
    tri = jnp.arange(ROUTE_CHUNK, dtype=jnp.int32)
    utri = (tri[:, None] < tri[None, :]).astype(bf)

    x1, h2p = _mix(
        x, norm_mix_w[None], w_in.astype(bf), cos2, sin2, dmat, qdec, kdec, ret_norm_w[None],
        w_pool.astype(bf), pool_scale[None], w_out.astype(bf), norm_ffn_w[None], tile_decay)
    eid, gate, rank, before, cnt = _route(h2p, w_router.T.astype(bf), b_router[:, None], utri)

    (n_tiles, tile_start, tile_count, n_used, lidx, nbe, lb, hrow, extra, padrow, padlen) = _routing_tables(
        eid, rank, before[:, :, 0].astype(jnp.int32), cnt[:, 0].astype(jnp.int32), n_tok, n_exp, rpt)

    xs = _dispatch(lidx, nbe, lb, hrow, extra, padrow, padlen, n_used, h2p, n_tiles * ROW_GRANULE, d_model)
    y_rows = _experts(tile_start, tile_count, xs, w_gu, b_gu, w_down, b_down)
    out = _combine(lidx, gate.reshape(-1), nbe, lb, hrow, extra, x1.reshape(n_tok, d_model), final_w[None], y_rows)
    return out.reshape(bsz, seq, d_model)


def kernel(x, norm_mix_w, w_in, ret_norm_w, w_pool, pool_scale, w_out, norm_ffn_w, w_router, b_router,
           w_gu, b_gu, w_down, b_down, norm_final_w):
    depth = norm_mix_w.shape[0]
    for l in range(depth):
        last = l == depth - 1
        assert last, "stacked layers need an un-normalised combine output"
        x = _layer(x, norm_mix_w[l], w_in[l], ret_norm_w[l], w_pool[l], pool_scale[l], w_out[l],
                   norm_ffn_w[l], w_router[l], b_router[l], w_gu[l], b_gu[l], w_down[l], b_down[l],
                   norm_final_w)
    return x
```

```python
import functools
import math

import jax
import jax.numpy as jnp
from jax import lax
from jax.experimental import pallas as pl
from jax.experimental.pallas import tpu as pltpu

CHUNK = 64
RET_HEADS = 4
POOL_WINDOWS = (2, 4, 8, 16)
ROPE_BASE = 10000.0
TOP_K = 4
SWIGLU_LIMIT = 7.0
SWIGLU_ALPHA = 1.702
EPS = 1e-6

LANES = 128
SUBLANES = 8
VMEM_LIMIT_BYTES = 56 * 1024 * 1024

SEQ_TILE = 256
ROUTE_TILE = 2048
ROUTE_CHUNK = 256
ROW_GRANULE = 128
ROW_TILE = 512
TOK_BLOCK = 1024
POOL_HALO = 16
PACK = 2
STRIP_ALIGN = 2
GATHER_UNROLL = 16
STRIP_UNROLL = 4
RARE_STRIP_BIT = 8


def _rms(x, w):
    return x * lax.rsqrt(jnp.mean(x * x, axis=-1, keepdims=True) + EPS) * w


def _bf16_bits_hi(x):
    return lax.bitcast_convert_type(x.astype(jnp.bfloat16).astype(jnp.float32), jnp.uint32)


def _pack_rows(x, out_ref, n_rows):
    half = x.shape[1] // PACK
    packed = (_bf16_bits_hi(x[:, :half]) >> 16) | _bf16_bits_hi(x[:, half:])
    rpt = half // LANES
    for si in range(rpt):
        out_ref[pl.ds(si, n_rows, stride=rpt), :] = packed[:, si * LANES:(si + 1) * LANES]


def _unpack_words(u):
    lo = lax.bitcast_convert_type(u << 16, jnp.float32)
    hi = lax.bitcast_convert_type(u & jnp.uint32(0xFFFF0000), jnp.float32)
    return lo, hi


def _unpack_rows(g_ref, n_rows, rpt):
    lo, hi = [], []
    for si in range(rpt):
        l, h = _unpack_words(g_ref[pl.ds(si, n_rows, stride=rpt), :])
        lo.append(l.astype(jnp.bfloat16))
        hi.append(h.astype(jnp.bfloat16))
    return jnp.concatenate(lo + hi, axis=-1)


def _mix_kernel(x_ref, nmw_ref, win_ref, cos_ref, sin_ref, dmat_ref, qdec_ref, kdec_ref,
                rnw_ref, wpool_ref, pscale_ref, wout_ref, nfw_ref,
                x1_ref, h2p_ref,
                state_ref, p1_ref, p2_ref, p4_ref, p8_ref,
                *, tile_decay):
    j = pl.program_id(1)
    ts = x_ref.shape[1]
    n_heads = RET_HEADS
    rw = rnw_ref.shape[1]
    hd = rw // n_heads
    pgw = wpool_ref.shape[1]
    halo = POOL_HALO

    @pl.when(j == 0)
    def _():
        state_ref[...] = jnp.zeros_like(state_ref)
        for r in (p1_ref, p2_ref, p4_ref, p8_ref):
            r[0:halo, :] = jnp.zeros((halo, r.shape[1]), jnp.float32)

    @pl.when(j > 0)
    def _():
        for r in (p1_ref, p2_ref, p4_ref, p8_ref):
            r[0:halo, :] = r[ts:ts + halo, :]

    x = x_ref[0]
    h = _rms(x, nmw_ref[...]).astype(jnp.bfloat16)
    z = jnp.dot(h, win_ref[...], preferred_element_type=jnp.float32)

    cos = cos_ref[...]
    sin = sin_ref[...]
    k_scale = hd ** -0.5

    ret_parts = []
    for hh in range(n_heads):
        q = z[:, hh * hd:(hh + 1) * hd]
        k = z[:, rw + hh * hd:rw + (hh + 1) * hd]
        v = z[:, 2 * rw + hh * hd:2 * rw + (hh + 1) * hd]
        g = z[:, 3 * rw + hh * hd:3 * rw + (hh + 1) * hd]
        q = q * cos + pltpu.roll(q, hd // 2, 1) * sin
        k = (k * cos + pltpu.roll(k, hd // 2, 1) * sin) * k_scale
        qb = q.astype(jnp.bfloat16)
        kb = k.astype(jnp.bfloat16)
        vb = v.astype(jnp.bfloat16)
        s = lax.dot_general(qb, kb, (((1,), (1,)), ((), ())), preferred_element_type=jnp.float32)
        pmat = (s * dmat_ref[hh]).astype(jnp.bfloat16)
        o = jnp.dot(pmat, vb, preferred_element_type=jnp.float32)
        state = state_ref[hh]
        qd = (q * qdec_ref[hh]).astype(jnp.bfloat16)
        o = o + jnp.dot(qd, state.astype(jnp.bfloat16), preferred_element_type=jnp.float32)
        kd = (k * kdec_ref[hh]).astype(jnp.bfloat16)
        u = lax.dot_general(kd, vb, (((0,), (0,)), ((), ())), preferred_element_type=jnp.float32)
        state_ref[hh] = state * tile_decay[hh] + u
        o = o * lax.rsqrt(jnp.mean(o * o, axis=-1, keepdims=True) + EPS)
        o = o * rnw_ref[:, hh * hd:(hh + 1) * hd] * (g * (1.0 / (1.0 + jnp.exp(-g))))
        ret_parts.append(o.astype(jnp.bfloat16))

    p = z[:, 4 * rw:]
    p1_ref[halo:halo + ts, :] = p
    s2 = p + p1_ref[halo - 1:halo - 1 + ts, :]
    p2_ref[halo:halo + ts, :] = s2[:, pgw:]
    s4 = s2[:, pgw:] + p2_ref[halo - 2:halo - 2 + ts, :]
    p4_ref[halo:halo + ts, :] = s4[:, pgw:]
    s8 = s4[:, pgw:] + p4_ref[halo - 4:halo - 4 + ts, :]
    p8_ref[halo:halo + ts, :] = s8[:, pgw:]
    s16 = s8[:, pgw:] + p8_ref[halo - 8:halo - 8 + ts, :]
    sums = (s2[:, :pgw], s4[:, :pgw], s8[:, :pgw], s16)
    tpos = (j * ts + lax.broadcasted_iota(jnp.int32, (ts, pgw), 0) + 1).astype(jnp.float32)
    pool_parts = []
    for gi, w in enumerate(POOL_WINDOWS):
        cnt = jnp.minimum(tpos, float(w))
        pooled = sums[gi] / cnt - p[:, gi * pgw:(gi + 1) * pgw]
        mixed = jnp.dot(pooled.astype(jnp.bfloat16), wpool_ref[gi], preferred_element_type=jnp.float32)
        mixed = mixed * pscale_ref[:, gi * pgw:(gi + 1) * pgw]
        pool_parts.append(mixed.astype(jnp.bfloat16))

    mix = jnp.concatenate(ret_parts + pool_parts, axis=-1)
    x1 = x + jnp.dot(mix, wout_ref[...], preferred_element_type=jnp.float32)
    x1_ref[0] = x1

    h2 = _rms(x1, nfw_ref[...])
    _pack_rows(h2, h2p_ref, ts)


def _mix(x, nmw, win_b, cos2, sin2, dmat, qdec, kdec, rnw, wpool_b, pscale, wout_b, nfw, tile_decay):
    bsz, seq, d_model = x.shape
    ts = SEQ_TILE
    n_tok = bsz * seq
    rw = rnw.shape[1]
    hd = rw // RET_HEADS
    pw = pscale.shape[1]
    pgw = pw // len(POOL_WINDOWS)
    nj = seq // ts
    rows_per_tok = d_model // PACK // LANES

    def const(shape):
        return pl.BlockSpec(shape, lambda b, j: (0,) * len(shape))

    in_specs = [
        pl.BlockSpec((1, ts, d_model), lambda b, j: (b, j, 0)),
        const(nmw.shape), const(win_b.shape),
        pl.BlockSpec((ts, hd), lambda b, j: (j, 0)),
        pl.BlockSpec((ts, hd), lambda b, j: (j, 0)),
        const(dmat.shape), const(qdec.shape), const(kdec.shape),
        const(rnw.shape), const(wpool_b.shape), const(pscale.shape), const(wout_b.shape),
        const(nfw.shape),
    ]
    out_shape = [
        jax.ShapeDtypeStruct((bsz, seq, d_model), jnp.float32),
        jax.ShapeDtypeStruct((n_tok * rows_per_tok, LANES), jnp.uint32),
    ]
    out_specs = [
        pl.BlockSpec((1, ts, d_model), lambda b, j: (b, j, 0)),
        pl.BlockSpec((ts * rows_per_tok, LANES), lambda b, j: (b * nj + j, 0)),
    ]
    scratch = [
        pltpu.VMEM((RET_HEADS, hd, hd), jnp.float32),
        pltpu.VMEM((POOL_HALO + ts, pw), jnp.float32),
        pltpu.VMEM((POOL_HALO + ts, pw - pgw), jnp.float32),
        pltpu.VMEM((POOL_HALO + ts, pw - 2 * pgw), jnp.float32),
        pltpu.VMEM((POOL_HALO + ts, pw - 3 * pgw), jnp.float32),
    ]
    return pl.pallas_call(
        functools.partial(_mix_kernel, tile_decay=tile_decay),
        grid=(bsz, nj),
        in_specs=in_specs,
        out_specs=out_specs,
        out_shape=out_shape,
        scratch_shapes=scratch,
        compiler_params=pltpu.CompilerParams(
            dimension_semantics=("arbitrary", "arbitrary"),
            vmem_limit_bytes=VMEM_LIMIT_BYTES),
        name="mix",
    )(x, nmw, win_b, cos2, sin2, dmat, qdec, kdec, rnw, wpool_b, pscale, wout_b, nfw)


def _route_kernel(h2p_ref, wrt_ref, brt_ref, utri_ref,
                  eid_ref, gate_ref, rank_ref, before_ref, cnt_ref, carry_ref):
    i = pl.program_id(0)
    n_exp, d_model = wrt_ref.shape
    rpt = d_model // PACK // LANES
    tr = h2p_ref.shape[0] // rpt
    sub = utri_ref.shape[0]

    @pl.when(i == 0)
    def _():
        carry_ref[...] = jnp.zeros_like(carry_ref)

    h2b = _unpack_rows(h2p_ref, tr, rpt)
    logits = lax.dot_general(wrt_ref[...], h2b, (((1,), (1,)), ((), ())),
                             preferred_element_type=jnp.float32) + brt_ref[...]
    eiota = lax.broadcasted_iota(jnp.int32, (n_exp, tr), 0)
    work = logits
    vals, ids, sels = [], [], []
    for _ in range(TOP_K):
        m = jnp.max(work, axis=0, keepdims=True)
        idx = jnp.min(jnp.where(work == m, eiota, n_exp), axis=0, keepdims=True)
        sel = eiota == idx
        vals.append(m)
        ids.append(idx)
        sels.append(sel)
        work = jnp.where(sel, -jnp.inf, work)
    ex = [jnp.exp(vk - vals[0]) for vk in vals]
    den = ex[0] + ex[1] + ex[2] + ex[3]
    gate_ref[...] = jnp.concatenate([e / den for e in ex], axis=0)
    eid_ref[...] = jnp.concatenate(ids, axis=0)

    member = jnp.zeros((n_exp, tr), jnp.float32)
    for sel in sels:
        member = member + sel.astype(jnp.float32)
    carry = carry_ref[...]
    befores = []
    for c in range(tr // sub):
        before_ref[c] = carry
        mc = member[:, c * sub:(c + 1) * sub]
        bc = jnp.dot(mc.astype(jnp.bfloat16), utri_ref[...], preferred_element_type=jnp.float32)
        befores.append(bc + carry[:, 0:1])
        carry = carry + jnp.sum(mc, axis=1, keepdims=True)
    before = jnp.concatenate(befores, axis=1)
    ranks = [jnp.sum(jnp.where(sel, before, 0.0), axis=0, keepdims=True) for sel in sels]
    rank_ref[...] = jnp.concatenate(ranks, axis=0).astype(jnp.int32)
    carry_ref[...] = carry
    cnt_ref[...] = carry


def _route(h2p, wrt_b, brt, utri):
    n_exp, d_model = wrt_b.shape
    rpt = d_model // PACK // LANES
    n_tok = h2p.shape[0] // rpt
    tr = ROUTE_TILE
    sub = utri.shape[0]

    def const(shape):
        return pl.BlockSpec(shape, lambda i: (0,) * len(shape))

    out_shape = [
        jax.ShapeDtypeStruct((TOP_K, n_tok), jnp.int32),
        jax.ShapeDtypeStruct((TOP_K, n_tok), jnp.float32),
        jax.ShapeDtypeStruct((TOP_K, n_tok), jnp.int32),
        jax.ShapeDtypeStruct((n_tok // sub, n_exp, LANES), jnp.float32),
        jax.ShapeDtypeStruct((n_exp, LANES), jnp.float32),
    ]
    out_specs = [
        pl.BlockSpec((TOP_K, tr), lambda i: (0, i)),
        pl.BlockSpec((TOP_K, tr), lambda i: (0, i)),
        pl.BlockSpec((TOP_K, tr), lambda i: (0, i)),
        pl.BlockSpec((tr // sub, n_exp, LANES), lambda i: (i, 0, 0)),
        pl.BlockSpec((n_exp, LANES), lambda i: (0, 0)),
    ]
    return pl.pallas_call(
        _route_kernel,
        grid=(n_tok // tr,),
        in_specs=[pl.BlockSpec((tr * rpt, LANES), lambda i: (i, 0)),
                  const(wrt_b.shape), const(brt.shape), const(utri.shape)],
        out_specs=out_specs,
        out_shape=out_shape,
        scratch_shapes=[pltpu.VMEM((n_exp, LANES), jnp.float32)],
        compiler_params=pltpu.CompilerParams(
            dimension_semantics=("arbitrary",),
            vmem_limit_bytes=VMEM_LIMIT_BYTES),
        name="route",
    )(h2p, wrt_b, brt, utri)


def _for_set_bits(n, max_bit, fn):
    def piece(bit):
        size = 1 << bit
        offset = (n >> (bit + 1)) << (bit + 1)

        @pl.when((n & size) != 0)
        def _():
            fn(offset, size)

    min_bit = STRIP_ALIGN.bit_length() - 1
    rare_bit = min(max_bit + 1, RARE_STRIP_BIT)

    @pl.when(n >= (1 << rare_bit))
    def _():
        for bit in range(max_bit, rare_bit - 1, -1):
            piece(bit)
    for bit in range(rare_bit - 1, min_bit - 1, -1):
        piece(bit)


def _aligned_len(n):
    return n + (n & (STRIP_ALIGN - 1))


def _for_strips(nbe_ref, lb_ref, hrow_ref, blk, n_exp, max_bit, fn):
    def body(i, carry):
        for u in range(STRIP_UNROLL):
            k = blk * n_exp + i * STRIP_UNROLL + u
            n = _aligned_len(nbe_ref[k])
            lb = lb_ref[k]
            hrow = hrow_ref[k]
            _for_set_bits(n, max_bit, lambda offset, size: fn(lb + offset, hrow + offset, size))
        return carry
    assert n_exp % STRIP_UNROLL == 0
    lax.fori_loop(0, n_exp // STRIP_UNROLL, body, 0)


def _dispatch_kernel(lidx_ref, nbe_ref, lb_ref, hrow_ref, extra_ref, padrow_ref, padlen_ref, nused_ref,
                     h2p_ref, xs_hbm, stage_ref, zero_ref, sem, zsem, *, nblk):
    b = pl.program_id(0)
    rpt = zero_ref.shape[0] // ROW_GRANULE
    tb = h2p_ref.shape[0] // rpt
    n_exp = padrow_ref.shape[0]
    n_tok = lidx_ref.shape[0] // TOP_K
    slot_rows = stage_ref.shape[0] // 2
    max_bit = tb.bit_length() - 1
    slot = b % 2

    def strip_copy(sl, staging_row, sorted_row, size):
        src = pl.multiple_of(sl * slot_rows + staging_row * rpt, SUBLANES)
        dst = pl.multiple_of(sorted_row * rpt, SUBLANES)
        return pltpu.make_async_copy(stage_ref.at[pl.ds(src, size * rpt), :],
                                     xs_hbm.at[pl.ds(dst, size * rpt), :], sem.at[sl])

    def start_strips(blk, sl):
        _for_strips(nbe_ref, lb_ref, hrow_ref, blk, n_exp, max_bit,
                    lambda srow, hrow, size: strip_copy(sl, srow, hrow, size).start())

    def wait_strips(blk, sl):
        def sized_copy(offset, size):
            return pltpu.make_async_copy(stage_ref.at[pl.ds(sl * slot_rows, size * rpt), :],
                                         xs_hbm.at[pl.ds(0, size * rpt), :], sem.at[sl])
        sized_copy(0, tb * TOP_K).wait()
        _for_set_bits(extra_ref[blk], n_exp.bit_length() - 1, lambda o, s: sized_copy(o, s).wait())

    @pl.when(b >= 2)
    def _():
        wait_strips(b - 2, slot)

    def scatter_block(static_slot):
        def scatter_group(c, carry):
            tok0 = c * GATHER_UNROLL
            idx0 = b * tb + tok0
            for u in range(GATHER_UNROLL):
                row = h2p_ref[pl.ds(pl.multiple_of((tok0 + u) * rpt, rpt), rpt), :]
                for k in range(TOP_K):
                    dst = pl.multiple_of(lidx_ref[idx0 + (k * n_tok + u)], rpt)
                    stage_ref[pl.ds(static_slot * slot_rows + dst, rpt), :] = row
            return carry
        lax.fori_loop(0, tb // GATHER_UNROLL, scatter_group, 0)

    for static_slot in range(2):
        pl.when(slot == static_slot)(functools.partial(scatter_block, static_slot))

    def zero_tail(e, carry):
        n = nbe_ref[b * n_exp + e]

        @pl.when((n & (STRIP_ALIGN - 1)) != 0)
        def _():
            dst = pl.multiple_of(slot * slot_rows + (lb_ref[b * n_exp + e] + n) * rpt, rpt)
            stage_ref[pl.ds(dst, rpt), :] = jnp.zeros((rpt, LANES), jnp.uint32)
        return carry
    lax.fori_loop(0, n_exp, zero_tail, 0)

    start_strips(b, slot)

    @pl.when(b == nblk - 1)
    def _():
        if nblk >= 2:
            wait_strips(b - 1, 1 - slot)
        wait_strips(b, slot)
        zero_ref[...] = jnp.zeros_like(zero_ref)
        pad_bit = ROW_GRANULE.bit_length() - 2

        def pad_copy(e, offset, size):
            dst = pl.multiple_of((padrow_ref[e] + offset) * rpt, SUBLANES)
            return pltpu.make_async_copy(zero_ref.at[pl.ds(0, size * rpt), :],
                                         xs_hbm.at[pl.ds(dst, size * rpt), :], zsem)

        def pad_start(e, carry):
            _for_set_bits(padlen_ref[e], pad_bit, lambda o, s: pad_copy(e, o, s).start())
            return carry

        def pad_wait(e, carry):
            _for_set_bits(padlen_ref[e], pad_bit, lambda o, s: pad_copy(e, o, s).wait())
            return carry
        lax.fori_loop(0, n_exp, pad_start, 0)

        def tail_copy(i):
            dst = pl.multiple_of(i * (ROW_GRANULE * rpt), SUBLANES)
            return pltpu.make_async_copy(zero_ref, xs_hbm.at[pl.ds(dst, ROW_GRANULE * rpt), :], zsem)

        def tail_start(i, carry):
            tail_copy(i).start()
            return carry

        def tail_wait(i, carry):
            tail_copy(i).wait()
            return carry
        n_gran = xs_hbm.shape[0] // (ROW_GRANULE * rpt)
        lax.fori_loop(nused_ref[0], n_gran, tail_start, 0)
        lax.fori_loop(0, n_exp, pad_wait, 0)
        lax.fori_loop(nused_ref[0], n_gran, tail_wait, 0)


def _dispatch(lidx, nbe, lb, hrow, extra, padrow, padlen, n_used, h2p, n_rows, d_model):
    tb = TOK_BLOCK
    rpt = d_model // PACK // LANES
    n_tok = h2p.shape[0] // rpt
    n_exp = padrow.shape[0]
    slot_rows = (tb * TOP_K + n_exp * (STRIP_ALIGN - 1)) * rpt
    grid_spec = pltpu.PrefetchScalarGridSpec(
        num_scalar_prefetch=8,
        grid=(n_tok // tb,),
        in_specs=[pl.BlockSpec((tb * rpt, LANES), lambda i, *_: (i, 0))],
        out_specs=pl.BlockSpec(memory_space=pl.ANY),
        scratch_shapes=[pltpu.VMEM((2 * slot_rows, LANES), jnp.uint32),
                        pltpu.VMEM((ROW_GRANULE * rpt, LANES), jnp.uint32),
                        pltpu.SemaphoreType.DMA((2,)),
                        pltpu.SemaphoreType.DMA],
    )
    return pl.pallas_call(
        functools.partial(_dispatch_kernel, nblk=n_tok // tb),
        grid_spec=grid_spec,
        out_shape=jax.ShapeDtypeStruct((n_rows * rpt, LANES), jnp.uint32),
        compiler_params=pltpu.CompilerParams(
            dimension_semantics=("arbitrary",),
            vmem_limit_bytes=VMEM_LIMIT_BYTES),
        name="dispatch",
    )(lidx, nbe, lb, hrow, extra, padrow, padlen, n_used, h2p)


def _experts_kernel(tstart_ref, tcount_ref, wgu_hbm, bgu_ref, wd_hbm, bd_ref, xs_hbm,
                    y_hbm, wgu_f, wd_f, wgu_b, wd_b, xbuf, ybuf, pend_ref, wsem, xsem, ysem, *, n_exp):
    e = pl.program_id(0)
    d_ff = wd_f.shape[0]
    d_model = wd_f.shape[1]
    rpt = d_model // PACK // LANES
    gran_rows = ROW_GRANULE * rpt
    gpt = ROW_TILE // ROW_GRANULE
    n_gran = y_hbm.shape[0] // gran_rows
    g0 = tstart_ref[e]
    ng = tcount_ref[e]
    nt = ng // gpt
    tail = ng - nt * gpt

    def hbm_rows(ref, gran, count):
        return ref.at[pl.ds(pl.multiple_of(gran * gran_rows, SUBLANES), count * gran_rows), :]

    def x_copy(gran, sl, count=gpt):
        return pltpu.make_async_copy(hbm_rows(xs_hbm, gran, count),
                                     xbuf.at[sl, pl.ds(0, count * gran_rows), :], xsem.at[sl])

    def y_copy(gran, sl, count=gpt):
        return pltpu.make_async_copy(ybuf.at[sl, pl.ds(0, count * gran_rows), :],
                                     hbm_rows(y_hbm, gran, count), ysem.at[sl])

    def mlp(sl, tm):
        x = _unpack_rows(xbuf.at[sl], tm, rpt)
        gu = jnp.dot(x, wgu_b[...], preferred_element_type=jnp.float32) + bgu_ref[0]
        x_glu = jnp.minimum(gu[:, :d_ff], SWIGLU_LIMIT)
        x_lin = jnp.clip(gu[:, d_ff:], -SWIGLU_LIMIT, SWIGLU_LIMIT)
        act = x_glu * (1.0 / (1.0 + jnp.exp(-SWIGLU_ALPHA * x_glu))) * (x_lin + 1.0)
        y = jnp.dot(act.astype(jnp.bfloat16), wd_b[...], preferred_element_type=jnp.float32) + bd_ref[0]
        _pack_rows(y, ybuf.at[sl], tm)

    def weight_copies(ex):
        return (pltpu.make_async_copy(wgu_hbm.at[ex], wgu_f, wsem.at[0]),
                pltpu.make_async_copy(wd_hbm.at[ex], wd_f, wsem.at[1]))

    tail_slot = 2
    tail_gran = g0 + nt * gpt

    def wait_pending_tail_y():
        for count in range(1, gpt):
            @pl.when(pend_ref[0] == count)
            def _():
                y_copy(0, tail_slot, count).wait()

    @pl.when(e == 0)
    def _():
        pend_ref[0] = 0

        @pl.when(nt > 0)
        def _():
            x_copy(g0, 0).start()

    for count in range(1, gpt):
        @pl.when(tail == count)
        def _():
            x_copy(tail_gran, tail_slot, count).start()

    @pl.when(e == 0)
    def _():
        for c in weight_copies(0):
            c.start(priority=1)
    for c in weight_copies(e):
        c.wait()
    wgu_b[...] = wgu_f[...].astype(jnp.bfloat16)
    wd_b[...] = wd_f[...].astype(jnp.bfloat16)

    @pl.when(e + 1 < n_exp)
    def _():
        for c in weight_copies(e + 1):
            c.start(priority=1)

    @pl.when(ng > 0)
    def _():
        def tile_step(j, carry):
            sl = j % 2
            gran = g0 + j * gpt

            @pl.when(j + 1 < nt)
            def _():
                x_copy(gran + gpt, 1 - sl).start()
            x_copy(gran, sl).wait()

            @pl.when(j >= 2)
            def _():
                y_copy(gran - 2 * gpt, sl).wait()
            mlp(sl, ROW_TILE)
            y_copy(gran, sl).start()
            return carry
        lax.fori_loop(0, nt, tile_step, 0)

        @pl.when(nt >= 2)
        def _():
            y_copy(g0 + (nt - 2) * gpt, nt % 2).wait()

        @pl.when(nt >= 1)
        def _():
            y_copy(g0 + (nt - 1) * gpt, (nt - 1) % 2).wait()

    @pl.when(e + 1 < n_exp)
    def _():
        nxt = jnp.minimum(e + 1, n_exp - 1)

        @pl.when(tcount_ref[nxt] >= gpt)
        def _():
            x_copy(tstart_ref[nxt], 0).start()

    for count in range(1, gpt):
        @pl.when(tail == count)
        def _():
            x_copy(tail_gran, tail_slot, count).wait()
            wait_pending_tail_y()
            mlp(tail_slot, count * ROW_GRANULE)
            y_copy(tail_gran, tail_slot, count).start()
            pend_ref[0] = count

    @pl.when(e == n_exp - 1)
    def _():
        wait_pending_tail_y()
        ybuf[0, 0:gran_rows, :] = jnp.zeros((gran_rows, LANES), ybuf.dtype)

        def tail_start(i, carry):
            y_copy(i, 0, 1).start()
            return carry

        def tail_wait(i, carry):
            y_copy(i, 0, 1).wait()
            return carry
        lax.fori_loop(g0 + ng, n_gran, tail_start, 0)
        lax.fori_loop(g0 + ng, n_gran, tail_wait, 0)


def _experts(tile_start, tile_count, xs, w_gu, b_gu, w_down, b_down):
    tm = ROW_TILE
    n_exp, d_model, d_gu = w_gu.shape
    d_ff = w_down.shape[1]
    rpt = d_model // PACK // LANES
    grid_spec = pltpu.PrefetchScalarGridSpec(
        num_scalar_prefetch=2,
        grid=(n_exp,),
        in_specs=[
            pl.BlockSpec(memory_space=pl.ANY),
            pl.BlockSpec((1, 1, d_gu), lambda e, *_: (e, 0, 0)),
            pl.BlockSpec(memory_space=pl.ANY),
            pl.BlockSpec((1, 1, d_model), lambda e, *_: (e, 0, 0)),
            pl.BlockSpec(memory_space=pl.ANY),
        ],
        out_specs=pl.BlockSpec(memory_space=pl.ANY),
        scratch_shapes=[pltpu.VMEM((d_model, d_gu), jnp.float32),
                        pltpu.VMEM((d_ff, d_model), jnp.float32),
                        pltpu.VMEM((d_model, d_gu), jnp.bfloat16),
                        pltpu.VMEM((d_ff, d_model), jnp.bfloat16),
                        pltpu.VMEM((3, tm * rpt, LANES), jnp.uint32),
                        pltpu.VMEM((3, tm * rpt, LANES), jnp.uint32),
                        pltpu.SMEM((1,), jnp.int32),
                        pltpu.SemaphoreType.DMA((2,)),
                        pltpu.SemaphoreType.DMA((3,)),
                        pltpu.SemaphoreType.DMA((3,))],
    )
    return pl.pallas_call(
        functools.partial(_experts_kernel, n_exp=n_exp),
        grid_spec=grid_spec,
        out_shape=jax.ShapeDtypeStruct(xs.shape, jnp.uint32),
        compiler_params=pltpu.CompilerParams(
            dimension_semantics=("arbitrary",),
            vmem_limit_bytes=VMEM_LIMIT_BYTES),
        name="experts",
    )(tile_start, tile_count, w_gu, b_gu.reshape(n_exp, 1, d_gu), w_down,
      b_down.reshape(n_exp, 1, d_model), xs)


def _combine_kernel(lidx_ref, gate_ref, nbe_ref, lb_ref, hrow_ref, extra_ref,
                    x1_ref, nw_ref, y_hbm, out_ref, stage_ref, alo_ref, ahi_ref, sem, *, nblk):
    b = pl.program_id(0)
    tb = x1_ref.shape[0]
    rpt = alo_ref.shape[0] // tb
    n_exp = nbe_ref.shape[0] // nblk
    n_tok = lidx_ref.shape[0] // TOP_K
    slot_rows = stage_ref.shape[0] // 2
    max_bit = tb.bit_length() - 1
    slot = b % 2

    def strip_copy(sl, staging_row, sorted_row, size):
        src = pl.multiple_of(sorted_row * rpt, SUBLANES)
        dst = pl.multiple_of(sl * slot_rows + staging_row * rpt, SUBLANES)
        return pltpu.make_async_copy(y_hbm.at[pl.ds(src, size * rpt), :],
                                     stage_ref.at[pl.ds(dst, size * rpt), :], sem.at[sl])

    def start_strips(blk, sl):
        _for_strips(nbe_ref, lb_ref, hrow_ref, blk, n_exp, max_bit,
                    lambda srow, hrow, size: strip_copy(sl, srow, hrow, size).start())

    @pl.when(b == 0)
    def _():
        start_strips(0, 0)

    @pl.when(b + 1 < nblk)
    def _():
        start_strips(b + 1, 1 - slot)

    def sized_copy(offset, size):
        return pltpu.make_async_copy(y_hbm.at[pl.ds(0, size * rpt), :],
                                     stage_ref.at[pl.ds(slot * slot_rows, size * rpt), :], sem.at[slot])
    sized_copy(0, tb * TOP_K).wait()
    _for_set_bits(extra_ref[b], n_exp.bit_length() - 1, lambda o, s: sized_copy(o, s).wait())

    def gather_block(static_slot):
        def gather_group(c, carry):
            tok0 = c * GATHER_UNROLL
            idx0 = b * tb + tok0
            for u in range(GATHER_UNROLL):
                lo = hi = None
                for k in range(TOP_K):
                    src = pl.multiple_of(lidx_ref[idx0 + (k * n_tok + u)], rpt)
                    g = gate_ref[idx0 + (k * n_tok + u)]
                    l, h = _unpack_words(stage_ref[pl.ds(static_slot * slot_rows + src, rpt), :])
                    lo = g * l if lo is None else lo + g * l
                    hi = g * h if hi is None else hi + g * h
                dst = pl.multiple_of((tok0 + u) * rpt, rpt)
                alo_ref[pl.ds(dst, rpt), :] = lo
                ahi_ref[pl.ds(dst, rpt), :] = hi
            return carry
        lax.fori_loop(0, tb // GATHER_UNROLL, gather_group, 0)

    for static_slot in range(2):
        pl.when(slot == static_slot)(functools.partial(gather_block, static_slot))

    parts = [alo_ref[pl.ds(si, tb, stride=rpt), :] for si in range(rpt)]
    parts += [ahi_ref[pl.ds(si, tb, stride=rpt), :] for si in range(rpt)]
    x2 = x1_ref[...] + jnp.concatenate(parts, axis=-1)
    out_ref[...] = _rms(x2, nw_ref[...])


def _combine(lidx, gates, nbe, lb, hrow, extra, x1, norm_w, y_rows):
    tb = TOK_BLOCK
    n_tok, d_model = x1.shape
    rpt = d_model // PACK // LANES
    n_exp = nbe.shape[0] // (n_tok // tb)
    slot_rows = (tb * TOP_K + n_exp * (STRIP_ALIGN - 1)) * rpt
    grid_spec = pltpu.PrefetchScalarGridSpec(
        num_scalar_prefetch=6,
        grid=(n_tok // tb,),
        in_specs=[
            pl.BlockSpec((tb, d_model), lambda i, *_: (i, 0)),
            pl.BlockSpec((1, d_model), lambda i, *_: (0, 0)),
            pl.BlockSpec(memory_space=pl.ANY),
        ],
        out_specs=pl.BlockSpec((tb, d_model), lambda i, *_: (i, 0)),
        scratch_shapes=[pltpu.VMEM((2 * slot_rows, LANES), jnp.uint32),
                        pltpu.VMEM((tb * rpt, LANES), jnp.float32),
                        pltpu.VMEM((tb * rpt, LANES), jnp.float32),
                        pltpu.SemaphoreType.DMA((2,))],
    )
    return pl.pallas_call(
        functools.partial(_combine_kernel, nblk=n_tok // tb),
        grid_spec=grid_spec,
        out_shape=jax.ShapeDtypeStruct((n_tok, d_model), jnp.float32),
        compiler_params=pltpu.CompilerParams(
            dimension_semantics=("arbitrary",),
            vmem_limit_bytes=VMEM_LIMIT_BYTES),
        name="combine",
    )(lidx, gates, nbe, lb, hrow, extra, x1, norm_w, y_rows)


def _retention_tables(seq, hd, ts):
    half = hd // 2
    inv = ROPE_BASE ** (-jnp.arange(half, dtype=jnp.float32) / half)
    ang = jnp.arange(seq, dtype=jnp.float32)[:, None] * inv[None, :]
    cos, sin = jnp.cos(ang), jnp.sin(ang)
    cos2 = jnp.concatenate([cos, cos], axis=-1)
    sin2 = jnp.concatenate([-sin, sin], axis=-1)
    log_g = jnp.log1p(-jnp.exp2(-5.0 - jnp.arange(RET_HEADS, dtype=jnp.float32)))
    pos = jnp.arange(ts, dtype=jnp.float32)
    dist = jnp.abs(pos[:, None] - pos[None, :])
    chunk = jnp.arange(ts, dtype=jnp.int32) // CHUNK
    visible = chunk[None, :] <= chunk[:, None]
    dmat = jnp.where(visible[None], jnp.exp(log_g[:, None, None] * dist[None]), 0.0)
    qdec = jnp.exp(log_g[:, None] * (pos + 1.0)[None, :])
    kdec = jnp.exp(log_g[:, None] * (ts - 1.0 - pos)[None, :])
    qdec = jnp.broadcast_to(qdec[:, :, None], (RET_HEADS, ts, hd))
    kdec = jnp.broadcast_to(kdec[:, :, None], (RET_HEADS, ts, hd))
    tile_decay = tuple(math.exp(math.log1p(-2.0 ** (-5.0 - h)) * ts) for h in range(RET_HEADS))
    return cos2, sin2, dmat, qdec, kdec, tile_decay


def _routing_tables(eid, rank, before, counts, n_tok, n_exp, rows_per_tok):
    tm, tb, ts = ROW_GRANULE, TOK_BLOCK, ROUTE_CHUNK
    cb = before[::tb // ts]
    nbe = jnp.concatenate([cb[1:], counts[None]], axis=0) - cb
    nal = nbe + (nbe & (STRIP_ALIGN - 1))
    region = jnp.sum(nal, axis=0)
    tiles_per = (region + tm - 1) // tm
    tile_end = jnp.cumsum(tiles_per)
    offs = (tile_end - tiles_per) * tm
    n_tiles = (n_tok * TOP_K + (n_tok // tb) * n_exp * (STRIP_ALIGN - 1)) // tm + n_exp
    hrow = offs[None, :] + jnp.cumsum(nal, axis=0) - nal
    lb = jnp.cumsum(nal, axis=1) - nal
    delta = jnp.repeat(lb - cb, tb, axis=0)
    onehot = eid[:, :, None] == jnp.arange(n_exp, dtype=jnp.int32)
    lidx = rank + jnp.sum(jnp.where(onehot, delta[None], 0), axis=-1)
    lidx = lidx * rows_per_tok
    padrow = offs + region
    padlen = tiles_per * tm - region
    as_i32 = lambda a: a.reshape(-1).astype(jnp.int32)
    return (n_tiles, as_i32(tile_end - tiles_per), as_i32(tiles_per), as_i32(tile_end[-1:]),
            as_i32(lidx), as_i32(nbe), as_i32(lb), as_i32(hrow),
            as_i32(jnp.sum(nal, axis=1) - tb * TOP_K), as_i32(padrow), as_i32(padlen))


def _layer(x, norm_mix_w, w_in, ret_norm_w, w_pool, pool_scale, w_out, norm_ffn_w,
           w_router, b_router, w_gu, b_gu, w_down, b_down, final_w):
    bsz, seq, d_model = x.shape
    n_tok = bsz * seq
    n_exp = w_router.shape[1]
    rw = ret_norm_w.shape[0]
    hd = rw // RET_HEADS
    ts = SEQ_TILE
    bf = jnp.bfloat16
    rpt = d_model // PACK // LANES
    assert d_model % (PACK * LANES) == 0 and (STRIP_ALIGN * rpt) % SUBLANES == 0
    assert seq % ts == 0 and ts % CHUNK == 0 and n_tok % TOK_BLOCK == 0
    assert n_tok % ROUTE_TILE == 0 and ROUTE_TILE % ROUTE_CHUNK == 0 and TOK_BLOCK % ROUTE_CHUNK == 0
    assert TOK_BLOCK & (TOK_BLOCK - 1) == 0 and ROW_GRANULE & (ROW_GRANULE - 1) == 0
    assert ROW_TILE % ROW_GRANULE == 0

    cos2, sin2, dmat, qdec, kdec, tile_decay = _retention_tables(seq, hd, ts)
    tri = jnp.arange(ROUTE_CHUNK, dtype=jnp.int32)
    utri = (tri[:, None] < tri[None, :]).astype(bf)

    x1, h2p = _mix(
        x, norm_mix_w[None], w_in.astype(bf), cos2, sin2, dmat, qdec, kdec, ret_norm_w[None],
        w_pool.astype(bf), pool_scale[None], w_out.astype(bf), norm_ffn_w[None], tile_decay)
    eid, gate, rank, before, cnt = _route(h2p, w_router.T.astype(bf), b_router[:, None], utri)

    (n_tiles, tile_start, tile_count, n_used, lidx, nbe, lb, hrow, extra, padrow, padlen) = _routing_tables(
        eid, rank, before[:, :, 0].astype(jnp.int32), cnt[:, 0].astype(jnp.int32), n_tok, n_exp, rpt)

    xs = _dispatch(lidx, nbe, lb, hrow, extra, padrow, padlen, n_used, h2p, n_tiles * ROW_GRANULE, d_model)
    y_rows = _experts(tile_start, tile_count, xs, w_gu, b_gu, w_down, b_down)
    out = _combine(lidx, gate.reshape(-1), nbe, lb, hrow, extra, x1.reshape(n_tok, d_model), final_w[None], y_rows)
    return out.reshape(bsz, seq, d_model)


def kernel(x, norm_mix_w, w_in, ret_norm_w, w_pool, pool_scale, w_out, norm_ffn_w, w_router, b_router,
           w_gu, b_gu, w_down, b_down, norm_final_w):
    depth = norm_mix_w.shape[0]
    for l in range(depth):
        last = l == depth - 1
        assert last, "stacked layers need an un-normalised combine output"
        x = _layer(x, norm_mix_w[l], w_in[l], ret_norm_w[l], w_pool[l], pool_scale[l], w_out[l],
                   norm_ffn_w[l], w_router[l], b_router[l], w_gu[l], b_gu[l], w_down[l], b_down[l],
                   norm_final_w)
    return x
```

```python
import functools
import math

import jax
import jax.numpy as jnp
from jax import lax
from jax.experimental import pallas as pl
from jax.experimental.pallas import tpu as pltpu

CHUNK = 64
RET_HEADS = 4
POOL_WINDOWS = (2, 4, 8, 16)
ROPE_BASE = 10000.0
TOP_K = 4
SWIGLU_LIMIT = 7.0
SWIGLU_ALPHA = 1.702
EPS = 1e-6

LANES = 128
SUBLANES = 8
VMEM_LIMIT_BYTES = 56 * 1024 * 1024

SEQ_TILE = 256
MIX_STREAMS = 2
ROUTE_TILE = 2048
ROUTE_CHUNK = 256
ROW_GRANULE = 128
ROW_TILE = 512
TOK_BLOCK = 1024
POOL_HALO = 16
PACK = 2
STRIP_ALIGN = 2
GATHER_UNROLL = 16
STRIP_UNROLL = 4
RARE_STRIP_BIT = 8


def _rms(x, w):
    return x * lax.rsqrt(jnp.mean(x * x, axis=-1, keepdims=True) + EPS) * w


def _bf16_bits_hi(x):
    return lax.bitcast_convert_type(x.astype(jnp.bfloat16).astype(jnp.float32), jnp.uint32)


def _pack_rows(x, out_ref, n_rows):
    half = x.shape[1] // PACK
    packed = (_bf16_bits_hi(x[:, :half]) >> 16) | _bf16_bits_hi(x[:, half:])
    rpt = half // LANES
    for si in range(rpt):
        out_ref[pl.ds(si, n_rows, stride=rpt), :] = packed[:, si * LANES:(si + 1) * LANES]


def _unpack_words(u):
    lo = lax.bitcast_convert_type(u << 16, jnp.float32)
    hi = lax.bitcast_convert_type(u & jnp.uint32(0xFFFF0000), jnp.float32)
    return lo, hi


def _unpack_rows(g_ref, n_rows, rpt):
    lo, hi = [], []
    for si in range(rpt):
        l, h = _unpack_words(g_ref[pl.ds(si, n_rows, stride=rpt), :])
        lo.append(l.astype(jnp.bfloat16))
        hi.append(h.astype(jnp.bfloat16))
    return jnp.concatenate(lo + hi, axis=-1)


def _mix_kernel(x_ref, nmw_ref, win_ref, cos_ref, sin_ref, dmat_ref, qdec_ref, kdec_ref,
                rnw_ref, wpool_ref, pscale_ref, wout_ref, nfw_ref,
                x1_ref, h2p_ref,
                state_ref, p1_ref, p2_ref, p4_ref, p8_ref,
                *, tile_decay):
    j = pl.program_id(1)
    n_str = x_ref.shape[0]
    ts = x_ref.shape[2]
    n_heads = RET_HEADS
    rw = rnw_ref.shape[1]
    hd = rw // n_heads
    pgw = wpool_ref.shape[1]
    halo = POOL_HALO
    pool_refs = (p1_ref, p2_ref, p4_ref, p8_ref)

    @pl.when(j == 0)
    def _():
        state_ref[...] = jnp.zeros_like(state_ref)
        for r in pool_refs:
            r[:, 0:halo, :] = jnp.zeros((n_str, halo, r.shape[2]), jnp.float32)

    @pl.when(j > 0)
    def _():
        for r in pool_refs:
            r[:, 0:halo, :] = r[:, ts:ts + halo, :]

    x = jnp.concatenate([x_ref[si, 0] for si in range(n_str)], axis=0)
    h = _rms(x, nmw_ref[...]).astype(jnp.bfloat16)
    z_all = jnp.dot(h, win_ref[...], preferred_element_type=jnp.float32)

    cos = cos_ref[...]
    sin = sin_ref[...]
    k_scale = hd ** -0.5
    tpos = (j * ts + lax.broadcasted_iota(jnp.int32, (ts, pgw), 0) + 1).astype(jnp.float32)

    mix_rows = []
    for si in range(n_str):
        z = z_all[si * ts:(si + 1) * ts]
        parts = []
        for hh in range(n_heads):
            q = z[:, hh * hd:(hh + 1) * hd]
            k = z[:, rw + hh * hd:rw + (hh + 1) * hd]
            v = z[:, 2 * rw + hh * hd:2 * rw + (hh + 1) * hd]
            g = z[:, 3 * rw + hh * hd:3 * rw + (hh + 1) * hd]
            q = q * cos + pltpu.roll(q, hd // 2, 1) * sin
            k = (k * cos + pltpu.roll(k, hd // 2, 1) * sin) * k_scale
            qb = q.astype(jnp.bfloat16)
            kb = k.astype(jnp.bfloat16)
            vb = v.astype(jnp.bfloat16)
            s = lax.dot_general(qb, kb, (((1,), (1,)), ((), ())), preferred_element_type=jnp.float32)
            pmat = (s * dmat_ref[hh]).astype(jnp.bfloat16)
            o = jnp.dot(pmat, vb, preferred_element_type=jnp.float32)
            state = state_ref[si, hh]
            qd = (q * qdec_ref[hh]).astype(jnp.bfloat16)
            o = o + jnp.dot(qd, state.astype(jnp.bfloat16), preferred_element_type=jnp.float32)
            kd = (k * kdec_ref[hh]).astype(jnp.bfloat16)
            u = lax.dot_general(kd, vb, (((0,), (0,)), ((), ())), preferred_element_type=jnp.float32)
            state_ref[si, hh] = state * tile_decay[hh] + u
            o = o * lax.rsqrt(jnp.mean(o * o, axis=-1, keepdims=True) + EPS)
            o = o * rnw_ref[:, hh * hd:(hh + 1) * hd] * (g * (1.0 / (1.0 + jnp.exp(-g))))
            parts.append(o.astype(jnp.bfloat16))

        p = z[:, 4 * rw:]
        p1_ref[si, halo:halo + ts, :] = p
        s2 = p + p1_ref[si, halo - 1:halo - 1 + ts, :]
        p2_ref[si, halo:halo + ts, :] = s2[:, pgw:]
        s4 = s2[:, pgw:] + p2_ref[si, halo - 2:halo - 2 + ts, :]
        p4_ref[si, halo:halo + ts, :] = s4[:, pgw:]
        s8 = s4[:, pgw:] + p4_ref[si, halo - 4:halo - 4 + ts, :]
        p8_ref[si, halo:halo + ts, :] = s8[:, pgw:]
        s16 = s8[:, pgw:] + p8_ref[si, halo - 8:halo - 8 + ts, :]
        sums = (s2[:, :pgw], s4[:, :pgw], s8[:, :pgw], s16)
        for gi, w in enumerate(POOL_WINDOWS):
            cnt = jnp.minimum(tpos, float(w))
            pooled = sums[gi] / cnt - p[:, gi * pgw:(gi + 1) * pgw]
            mixed = jnp.dot(pooled.astype(jnp.bfloat16), wpool_ref[gi], preferred_element_type=jnp.float32)
            mixed = mixed * pscale_ref[:, gi * pgw:(gi + 1) * pgw]
            parts.append(mixed.astype(jnp.bfloat16))
        mix_rows.append(jnp.concatenate(parts, axis=-1))

    mix = jnp.concatenate(mix_rows, axis=0)
    x1 = x + jnp.dot(mix, wout_ref[...], preferred_element_type=jnp.float32)
    h2 = _rms(x1, nfw_ref[...])
    for si in range(n_str):
        x1_ref[si, 0] = x1[si * ts:(si + 1) * ts]
        _pack_rows(h2[si * ts:(si + 1) * ts], h2p_ref.at[si, 0], ts)


def _mix(x, nmw, win_b, cos2, sin2, dmat, qdec, kdec, rnw, wpool_b, pscale, wout_b, nfw, tile_decay):
    bsz, seq, d_model = x.shape
    ts = SEQ_TILE
    n_tok = bsz * seq
    rw = rnw.shape[1]
    hd = rw // RET_HEADS
    pw = pscale.shape[1]
    pgw = pw // len(POOL_WINDOWS)
    nj = seq // ts
    rows_per_tok = d_model // PACK // LANES

    def const(shape):
        return pl.BlockSpec(shape, lambda b, j: (0,) * len(shape))

    n_str = MIX_STREAMS if bsz % MIX_STREAMS == 0 else 1
    per = bsz // n_str
    in_specs = [
        pl.BlockSpec((n_str, 1, ts, d_model), lambda q, j: (0, q, j, 0)),
        const(nmw.shape), const(win_b.shape),
        pl.BlockSpec((ts, hd), lambda q, j: (j, 0)),
        pl.BlockSpec((ts, hd), lambda q, j: (j, 0)),
        const(dmat.shape), const(qdec.shape), const(kdec.shape),
        const(rnw.shape), const(wpool_b.shape), const(pscale.shape), const(wout_b.shape),
        const(nfw.shape),
    ]
    out_shape = [
        jax.ShapeDtypeStruct((n_str, per, seq, d_model), jnp.float32),
        jax.ShapeDtypeStruct((n_str, per, seq * rows_per_tok, LANES), jnp.uint32),
    ]
    out_specs = [
        pl.BlockSpec((n_str, 1, ts, d_model), lambda q, j: (0, q, j, 0)),
        pl.BlockSpec((n_str, 1, ts * rows_per_tok, LANES), lambda q, j: (0, q, j, 0)),
    ]
    scratch = [
        pltpu.VMEM((n_str, RET_HEADS, hd, hd), jnp.float32),
        pltpu.VMEM((n_str, POOL_HALO + ts, pw), jnp.float32),
        pltpu.VMEM((n_str, POOL_HALO + ts, pw - pgw), jnp.float32),
        pltpu.VMEM((n_str, POOL_HALO + ts, pw - 2 * pgw), jnp.float32),
        pltpu.VMEM((n_str, POOL_HALO + ts, pw - 3 * pgw), jnp.float32),
    ]
    x1, h2p = pl.pallas_call(
        functools.partial(_mix_kernel, tile_decay=tile_decay),
        grid=(per, nj),
        in_specs=in_specs,
        out_specs=out_specs,
        out_shape=out_shape,
        scratch_shapes=scratch,
        compiler_params=pltpu.CompilerParams(
            dimension_semantics=("arbitrary", "arbitrary"),
            vmem_limit_bytes=VMEM_LIMIT_BYTES),
        name="mix",
    )(x.reshape(n_str, per, seq, d_model), nmw, win_b, cos2, sin2, dmat, qdec, kdec, rnw, wpool_b,
      pscale, wout_b, nfw)
    return x1.reshape(bsz, seq, d_model), h2p.reshape(n_tok * rows_per_tok, LANES)


def _route_kernel(h2p_ref, wrt_ref, brt_ref, utri_ref,
                  eid_ref, gate_ref, rank_ref, before_ref, cnt_ref, carry_ref):
    i = pl.program_id(0)
    n_exp, d_model = wrt_ref.shape
    rpt = d_model // PACK // LANES
    tr = h2p_ref.shape[0] // rpt
    sub = utri_ref.shape[0]

    @pl.when(i == 0)
    def _():
        carry_ref[...] = jnp.zeros_like(carry_ref)

    h2b = _unpack_rows(h2p_ref, tr, rpt)
    logits = lax.dot_general(wrt_ref[...], h2b, (((1,), (1,)), ((), ())),
                             preferred_element_type=jnp.float32) + brt_ref[...]
    eiota = lax.broadcasted_iota(jnp.int32, (n_exp, tr), 0)
    work = logits
    vals, ids, sels = [], [], []
    for _ in range(TOP_K):
        m = jnp.max(work, axis=0, keepdims=True)
        idx = jnp.min(jnp.where(work == m, eiota, n_exp), axis=0, keepdims=True)
        sel = eiota == idx
        vals.append(m)
        ids.append(idx)
        sels.append(sel)
        work = jnp.where(sel, -jnp.inf, work)
    ex = [jnp.exp(vk - vals[0]) for vk in vals]
    den = ex[0] + ex[1] + ex[2] + ex[3]
    gate_ref[...] = jnp.concatenate([e / den for e in ex], axis=0)
    eid_ref[...] = jnp.concatenate(ids, axis=0)

    member = jnp.zeros((n_exp, tr), jnp.float32)
    for sel in sels:
        member = member + sel.astype(jnp.float32)
    carry = carry_ref[...]
    befores = []
    for c in range(tr // sub):
        before_ref[c] = carry
        mc = member[:, c * sub:(c + 1) * sub]
        bc = jnp.dot(mc.astype(jnp.bfloat16), utri_ref[...], preferred_element_type=jnp.float32)
        befores.append(bc + carry[:, 0:1])
        carry = carry + jnp.sum(mc, axis=1, keepdims=True)
    before = jnp.concatenate(befores, axis=1)
    ranks = [jnp.sum(jnp.where(sel, before, 0.0), axis=0, keepdims=True) for sel in sels]
    rank_ref[...] = jnp.concatenate(ranks, axis=0).astype(jnp.int32)
    carry_ref[...] = carry
    cnt_ref[...] = carry


def _route(h2p, wrt_b, brt, utri):
    n_exp, d_model = wrt_b.shape
    rpt = d_model // PACK // LANES
    n_tok = h2p.shape[0] // rpt
    tr = ROUTE_TILE
    sub = utri.shape[0]

    def const(shape):
        return pl.BlockSpec(shape, lambda i: (0,) * len(shape))

    out_shape = [
        jax.ShapeDtypeStruct((TOP_K, n_tok), jnp.int32),
        jax.ShapeDtypeStruct((TOP_K, n_tok), jnp.float32),
        jax.ShapeDtypeStruct((TOP_K, n_tok), jnp.int32),
        jax.ShapeDtypeStruct((n_tok // sub, n_exp, LANES), jnp.float32),
        jax.ShapeDtypeStruct((n_exp, LANES), jnp.float32),
    ]
    out_specs = [
        pl.BlockSpec((TOP_K, tr), lambda i: (0, i)),
        pl.BlockSpec((TOP_K, tr), lambda i: (0, i)),
        pl.BlockSpec((TOP_K, tr), lambda i: (0, i)),
        pl.BlockSpec((tr // sub, n_exp, LANES), lambda i: (i, 0, 0)),
        pl.BlockSpec((n_exp, LANES), lambda i: (0, 0)),
    ]
    return pl.pallas_call(
        _route_kernel,
        grid=(n_tok // tr,),
        in_specs=[pl.BlockSpec((tr * rpt, LANES), lambda i: (i, 0)),
                  const(wrt_b.shape), const(brt.shape), const(utri.shape)],
        out_specs=out_specs,
        out_shape=out_shape,
        scratch_shapes=[pltpu.VMEM((n_exp, LANES), jnp.float32)],
        compiler_params=pltpu.CompilerParams(
            dimension_semantics=("arbitrary",),
            vmem_limit_bytes=VMEM_LIMIT_BYTES),
        name="route",
    )(h2p, wrt_b, brt, utri)


def _for_set_bits(n, max_bit, fn):
    def piece(bit):
        size = 1 << bit
        offset = (n >> (bit + 1)) << (bit + 1)

        @pl.when((n & size) != 0)
        def _():
            fn(offset, size)

    min_bit = STRIP_ALIGN.bit_length() - 1
    rare_bit = min(max_bit + 1, RARE_STRIP_BIT)

    @pl.when(n >= (1 << rare_bit))
    def _():
        for bit in range(max_bit, rare_bit - 1, -1):
            piece(bit)
    for bit in range(rare_bit - 1, min_bit - 1, -1):
        piece(bit)


def _aligned_len(n):
    return n + (n & (STRIP_ALIGN - 1))


def _for_strips(nbe_ref, lb_ref, hrow_ref, blk, n_exp, max_bit, fn):
    def body(i, carry):
        for u in range(STRIP_UNROLL):
            k = blk * n_exp + i * STRIP_UNROLL + u
            n = _aligned_len(nbe_ref[k])
            lb = lb_ref[k]
            hrow = hrow_ref[k]
            _for_set_bits(n, max_bit, lambda offset, size: fn(lb + offset, hrow + offset, size))
        return carry
    assert n_exp % STRIP_UNROLL == 0
    lax.fori_loop(0, n_exp // STRIP_UNROLL, body, 0)


def _dispatch_kernel(lidx_ref, nbe_ref, lb_ref, hrow_ref, extra_ref, padrow_ref, padlen_ref, nused_ref,
                     h2p_ref, xs_hbm, stage_ref, zero_ref, sem, zsem, *, nblk):
    b = pl.program_id(0)
    rpt = zero_ref.shape[0] // ROW_GRANULE
    tb = h2p_ref.shape[0] // rpt
    n_exp = padrow_ref.shape[0]
    n_tok = lidx_ref.shape[0] // TOP_K
    slot_rows = stage_ref.shape[0] // 2
    max_bit = tb.bit_length() - 1
    slot = b % 2

    def strip_copy(sl, staging_row, sorted_row, size):
        src = pl.multiple_of(sl * slot_rows + staging_row * rpt, SUBLANES)
        dst = pl.multiple_of(sorted_row * rpt, SUBLANES)
        return pltpu.make_async_copy(stage_ref.at[pl.ds(src, size * rpt), :],
                                     xs_hbm.at[pl.ds(dst, size * rpt), :], sem.at[sl])

    def start_strips(blk, sl):
        _for_strips(nbe_ref, lb_ref, hrow_ref, blk, n_exp, max_bit,
                    lambda srow, hrow, size: strip_copy(sl, srow, hrow, size).start())

    def wait_strips(blk, sl):
        def sized_copy(offset, size):
            return pltpu.make_async_copy(stage_ref.at[pl.ds(sl * slot_rows, size * rpt), :],
                                         xs_hbm.at[pl.ds(0, size * rpt), :], sem.at[sl])
        sized_copy(0, tb * TOP_K).wait()
        _for_set_bits(extra_ref[blk], n_exp.bit_length() - 1, lambda o, s: sized_copy(o, s).wait())

    @pl.when(b >= 2)
    def _():
        wait_strips(b - 2, slot)

    def scatter_block(static_slot):
        def scatter_group(c, carry):
            tok0 = c * GATHER_UNROLL
            idx0 = b * tb + tok0
            for u in range(GATHER_UNROLL):
                row = h2p_ref[pl.ds(pl.multiple_of((tok0 + u) * rpt, rpt), rpt), :]
                for k in range(TOP_K):
                    dst = pl.multiple_of(lidx_ref[idx0 + (k * n_tok + u)], rpt)
                    stage_ref[pl.ds(static_slot * slot_rows + dst, rpt), :] = row
            return carry
        lax.fori_loop(0, tb // GATHER_UNROLL, scatter_group, 0)

    for static_slot in range(2):
        pl.when(slot == static_slot)(functools.partial(scatter_block, static_slot))

    def zero_tail(e, carry):
        n = nbe_ref[b * n_exp + e]

        @pl.when((n & (STRIP_ALIGN - 1)) != 0)
        def _():
            dst = pl.multiple_of(slot * slot_rows + (lb_ref[b * n_exp + e] + n) * rpt, rpt)
            stage_ref[pl.ds(dst, rpt), :] = jnp.zeros((rpt, LANES), jnp.uint32)
        return carry
    lax.fori_loop(0, n_exp, zero_tail, 0)

    start_strips(b, slot)

    @pl.when(b == nblk - 1)
    def _():
        if nblk >= 2:
            wait_strips(b - 1, 1 - slot)
        wait_strips(b, slot)
        zero_ref[...] = jnp.zeros_like(zero_ref)
        pad_bit = ROW_GRANULE.bit_length() - 2

        def pad_copy(e, offset, size):
            dst = pl.multiple_of((padrow_ref[e] + offset) * rpt, SUBLANES)
            return pltpu.make_async_copy(zero_ref.at[pl.ds(0, size * rpt), :],
                                         xs_hbm.at[pl.ds(dst, size * rpt), :], zsem)

        def pad_start(e, carry):
            _for_set_bits(padlen_ref[e], pad_bit, lambda o, s: pad_copy(e, o, s).start())
            return carry

        def pad_wait(e, carry):
            _for_set_bits(padlen_ref[e], pad_bit, lambda o, s: pad_copy(e, o, s).wait())
            return carry
        lax.fori_loop(0, n_exp, pad_start, 0)

        def tail_copy(i):
            dst = pl.multiple_of(i * (ROW_GRANULE * rpt), SUBLANES)
            return pltpu.make_async_copy(zero_ref, xs_hbm.at[pl.ds(dst, ROW_GRANULE * rpt), :], zsem)

        def tail_start(i, carry):
            tail_copy(i).start()
            return carry

        def tail_wait(i, carry):
            tail_copy(i).wait()
            return carry
        n_gran = xs_hbm.shape[0] // (ROW_GRANULE * rpt)
        lax.fori_loop(nused_ref[0], n_gran, tail_start, 0)
        lax.fori_loop(0, n_exp, pad_wait, 0)
        lax.fori_loop(nused_ref[0], n_gran, tail_wait, 0)


def _dispatch(lidx, nbe, lb, hrow, extra, padrow, padlen, n_used, h2p, n_rows, d_model):
    tb = TOK_BLOCK
    rpt = d_model // PACK // LANES
    n_tok = h2p.shape[0] // rpt
    n_exp = padrow.shape[0]
    slot_rows = (tb * TOP_K + n_exp * (STRIP_ALIGN - 1)) * rpt
    grid_spec = pltpu.PrefetchScalarGridSpec(
        num_scalar_prefetch=8,
        grid=(n_tok // tb,),
        in_specs=[pl.BlockSpec((tb * rpt, LANES), lambda i, *_: (i, 0))],
        out_specs=pl.BlockSpec(memory_space=pl.ANY),
        scratch_shapes=[pltpu.VMEM((2 * slot_rows, LANES), jnp.uint32),
                        pltpu.VMEM((ROW_GRANULE * rpt, LANES), jnp.uint32),
                        pltpu.SemaphoreType.DMA((2,)),
                        pltpu.SemaphoreType.DMA],
    )
    return pl.pallas_call(
        functools.partial(_dispatch_kernel, nblk=n_tok // tb),
        grid_spec=grid_spec,
        out_shape=jax.ShapeDtypeStruct((n_rows * rpt, LANES), jnp.uint32),
        compiler_params=pltpu.CompilerParams(
            dimension_semantics=("arbitrary",),
            vmem_limit_bytes=VMEM_LIMIT_BYTES),
        name="dispatch",
    )(lidx, nbe, lb, hrow, extra, padrow, padlen, n_used, h2p)


def _experts_kernel(tstart_ref, tcount_ref, wgu_hbm, bgu_ref, wd_hbm, bd_ref, xs_hbm,
                    y_hbm, wgu_f, wd_f, wgu_b, wd_b, xbuf, ybuf, pend_ref, wsem, xsem, ysem, *, n_exp):
    e = pl.program_id(0)
    d_ff = wd_f.shape[0]
    d_model = wd_f.shape[1]
    rpt = d_model // PACK // LANES
    gran_rows = ROW_GRANULE * rpt
    gpt = ROW_TILE // ROW_GRANULE
    n_gran = y_hbm.shape[0] // gran_rows
    g0 = tstart_ref[e]
    ng = tcount_ref[e]
    nt = ng // gpt
    tail = ng - nt * gpt

    def hbm_rows(ref, gran, count):
        return ref.at[pl.ds(pl.multiple_of(gran * gran_rows, SUBLANES), count * gran_rows), :]

    def x_copy(gran, sl, count=gpt):
        return pltpu.make_async_copy(hbm_rows(xs_hbm, gran, count),
                                     xbuf.at[sl, pl.ds(0, count * gran_rows), :], xsem.at[sl])

    def y_copy(gran, sl, count=gpt):
        return pltpu.make_async_copy(ybuf.at[sl, pl.ds(0, count * gran_rows), :],
                                     hbm_rows(y_hbm, gran, count), ysem.at[sl])

    def mlp(sl, tm):
        x = _unpack_rows(xbuf.at[sl], tm, rpt)
        gu = jnp.dot(x, wgu_b[...], preferred_element_type=jnp.float32) + bgu_ref[0]
        x_glu = jnp.minimum(gu[:, :d_ff], SWIGLU_LIMIT)
        x_lin = jnp.clip(gu[:, d_ff:], -SWIGLU_LIMIT, SWIGLU_LIMIT)
        act = x_glu * (1.0 / (1.0 + jnp.exp(-SWIGLU_ALPHA * x_glu))) * (x_lin + 1.0)
        y = jnp.dot(act.astype(jnp.bfloat16), wd_b[...], preferred_element_type=jnp.float32) + bd_ref[0]
        _pack_rows(y, ybuf.at[sl], tm)

    def weight_copies(ex):
        return (pltpu.make_async_copy(wgu_hbm.at[ex], wgu_f, wsem.at[0]),
                pltpu.make_async_copy(wd_hbm.at[ex], wd_f, wsem.at[1]))

    tail_slot = 2
    tail_gran = g0 + nt * gpt

    def wait_pending_tail_y():
        for count in range(1, gpt):
            @pl.when(pend_ref[0] == count)
            def _():
                y_copy(0, tail_slot, count).wait()

    @pl.when(e == 0)
    def _():
        pend_ref[0] = 0

        @pl.when(nt > 0)
        def _():
            x_copy(g0, 0).start()

    for count in range(1, gpt):
        @pl.when(tail == count)
        def _():
            x_copy(tail_gran, tail_slot, count).start()

    @pl.when(e == 0)
    def _():
        for c in weight_copies(0):
            c.start(priority=1)
    for c in weight_copies(e):
        c.wait()
    wgu_b[...] = wgu_f[...].astype(jnp.bfloat16)
    wd_b[...] = wd_f[...].astype(jnp.bfloat16)

    @pl.when(e + 1 < n_exp)
    def _():
        for c in weight_copies(e + 1):
            c.start(priority=1)

    @pl.when(ng > 0)
    def _():
        def tile_step(j, carry):
            sl = j % 2
            gran = g0 + j * gpt

            @pl.when(j + 1 < nt)
            def _():
                x_copy(gran + gpt, 1 - sl).start()
            x_copy(gran, sl).wait()

            @pl.when(j >= 2)
            def _():
                y_copy(gran - 2 * gpt, sl).wait()
            mlp(sl, ROW_TILE)
            y_copy(gran, sl).start()
            return carry
        lax.fori_loop(0, nt, tile_step, 0)

        @pl.when(nt >= 2)
        def _():
            y_copy(g0 + (nt - 2) * gpt, nt % 2).wait()

        @pl.when(nt >= 1)
        def _():
            y_copy(g0 + (nt - 1) * gpt, (nt - 1) % 2).wait()

    @pl.when(e + 1 < n_exp)
    def _():
        nxt = jnp.minimum(e + 1, n_exp - 1)

        @pl.when(tcount_ref[nxt] >= gpt)
        def _():
            x_copy(tstart_ref[nxt], 0).start()

    for count in range(1, gpt):
        @pl.when(tail == count)
        def _():
            x_copy(tail_gran, tail_slot, count).wait()
            wait_pending_tail_y()
            mlp(tail_slot, count * ROW_GRANULE)
            y_copy(tail_gran, tail_slot, count).start()
            pend_ref[0] = count

    @pl.when(e == n_exp - 1)
    def _():
        wait_pending_tail_y()
        ybuf[0, 0:gran_rows, :] = jnp.zeros((gran_rows, LANES), ybuf.dtype)

        def tail_start(i, carry):
            y_copy(i, 0, 1).start()
            return carry

        def tail_wait(i, carry):
            y_copy(i, 0, 1).wait()
            return carry
        lax.fori_loop(g0 + ng, n_gran, tail_start, 0)
        lax.fori_loop(g0 + ng, n_gran, tail_wait, 0)


def _experts(tile_start, tile_count, xs, w_gu, b_gu, w_down, b_down):
    tm = ROW_TILE
    n_exp, d_model, d_gu = w_gu.shape
    d_ff = w_down.shape[1]
    rpt = d_model // PACK // LANES
    grid_spec = pltpu.PrefetchScalarGridSpec(
        num_scalar_prefetch=2,
        grid=(n_exp,),
        in_specs=[
            pl.BlockSpec(memory_space=pl.ANY),
            pl.BlockSpec((1, 1, d_gu), lambda e, *_: (e, 0, 0)),
            pl.BlockSpec(memory_space=pl.ANY),
            pl.BlockSpec((1, 1, d_model), lambda e, *_: (e, 0, 0)),
            pl.BlockSpec(memory_space=pl.ANY),
        ],
        out_specs=pl.BlockSpec(memory_space=pl.ANY),
        scratch_shapes=[pltpu.VMEM((d_model, d_gu), jnp.float32),
                        pltpu.VMEM((d_ff, d_model), jnp.float32),
                        pltpu.VMEM((d_model, d_gu), jnp.bfloat16),
                        pltpu.VMEM((d_ff, d_model), jnp.bfloat16),
                        pltpu.VMEM((3, tm * rpt, LANES), jnp.uint32),
                        pltpu.VMEM((3, tm * rpt, LANES), jnp.uint32),
                        pltpu.SMEM((1,), jnp.int32),
                        pltpu.SemaphoreType.DMA((2,)),
                        pltpu.SemaphoreType.DMA((3,)),
                        pltpu.SemaphoreType.DMA((3,))],
    )
    return pl.pallas_call(
        functools.partial(_experts_kernel, n_exp=n_exp),
        grid_spec=grid_spec,
        out_shape=jax.ShapeDtypeStruct(xs.shape, jnp.uint32),
        compiler_params=pltpu.CompilerParams(
            dimension_semantics=("arbitrary",),
            vmem_limit_bytes=VMEM_LIMIT_BYTES),
        name="experts",
    )(tile_start, tile_count, w_gu, b_gu.reshape(n_exp, 1, d_gu), w_down,
      b_down.reshape(n_exp, 1, d_model), xs)


def _combine_kernel(lidx_ref, gate_ref, nbe_ref, lb_ref, hrow_ref, extra_ref,
                    x1_ref, nw_ref, y_hbm, out_ref, stage_ref, alo_ref, ahi_ref, sem, *, nblk):
    b = pl.program_id(0)
    tb = x1_ref.shape[0]
    rpt = alo_ref.shape[0] // tb
    n_exp = nbe_ref.shape[0] // nblk
    n_tok = lidx_ref.shape[0] // TOP_K
    slot_rows = stage_ref.shape[0] // 2
    max_bit = tb.bit_length() - 1
    slot = b % 2

    def strip_copy(sl, staging_row, sorted_row, size):
        src = pl.multiple_of(sorted_row * rpt, SUBLANES)
        dst = pl.multiple_of(sl * slot_rows + staging_row * rpt, SUBLANES)
        return pltpu.make_async_copy(y_hbm.at[pl.ds(src, size * rpt), :],
                                     stage_ref.at[pl.ds(dst, size * rpt), :], sem.at[sl])

    def start_strips(blk, sl):
        _for_strips(nbe_ref, lb_ref, hrow_ref, blk, n_exp, max_bit,
                    lambda srow, hrow, size: strip_copy(sl, srow, hrow, size).start())

    @pl.when(b == 0)
    def _():
        start_strips(0, 0)

    @pl.when(b + 1 < nblk)
    def _():
        start_strips(b + 1, 1 - slot)

    def sized_copy(offset, size):
        return pltpu.make_async_copy(y_hbm.at[pl.ds(0, size * rpt), :],
                                     stage_ref.at[pl.ds(slot * slot_rows, size * rpt), :], sem.at[slot])
    sized_copy(0, tb * TOP_K).wait()
    _for_set_bits(extra_ref[b], n_exp.bit_length() - 1, lambda o, s: sized_copy(o, s).wait())

    def gather_block(static_slot):
        def gather_group(c, carry):
            tok0 = c * GATHER_UNROLL
            idx0 = b * tb + tok0
            for u in range(GATHER_UNROLL):
                lo = hi = None
                for k in range(TOP_K):
                    src = pl.multiple_of(lidx_ref[idx0 + (k * n_tok + u)], rpt)
                    g = gate_ref[idx0 + (k * n_tok + u)]
                    l, h = _unpack_words(stage_ref[pl.ds(static_slot * slot_rows + src, rpt), :])
                    lo = g * l if lo is None else lo + g * l
                    hi = g * h if hi is None else hi + g * h
                dst = pl.multiple_of((tok0 + u) * rpt, rpt)
                alo_ref[pl.ds(dst, rpt), :] = lo
                ahi_ref[pl.ds(dst, rpt), :] = hi
            return carry
        lax.fori_loop(0, tb // GATHER_UNROLL, gather_group, 0)

    for static_slot in range(2):
        pl.when(slot == static_slot)(functools.partial(gather_block, static_slot))

    parts = [alo_ref[pl.ds(si, tb, stride=rpt), :] for si in range(rpt)]
    parts += [ahi_ref[pl.ds(si, tb, stride=rpt), :] for si in range(rpt)]
    x2 = x1_ref[...] + jnp.concatenate(parts, axis=-1)
    out_ref[...] = _rms(x2, nw_ref[...])


def _combine(lidx, gates, nbe, lb, hrow, extra, x1, norm_w, y_rows):
    tb = TOK_BLOCK
    n_tok, d_model = x1.shape
    rpt = d_model // PACK // LANES
    n_exp = nbe.shape[0] // (n_tok // tb)
    slot_rows = (tb * TOP_K + n_exp * (STRIP_ALIGN - 1)) * rpt
    grid_spec = pltpu.PrefetchScalarGridSpec(
        num_scalar_prefetch=6,
        grid=(n_tok // tb,),
        in_specs=[
            pl.BlockSpec((tb, d_model), lambda i, *_: (i, 0)),
            pl.BlockSpec((1, d_model), lambda i, *_: (0, 0)),
            pl.BlockSpec(memory_space=pl.ANY),
        ],
        out_specs=pl.BlockSpec((tb, d_model), lambda i, *_: (i, 0)),
        scratch_shapes=[pltpu.VMEM((2 * slot_rows, LANES), jnp.uint32),
                        pltpu.VMEM((tb * rpt, LANES), jnp.float32),
                        pltpu.VMEM((tb * rpt, LANES), jnp.float32),
                        pltpu.SemaphoreType.DMA((2,))],
    )
    return pl.pallas_call(
        functools.partial(_combine_kernel, nblk=n_tok // tb),
        grid_spec=grid_spec,
        out_shape=jax.ShapeDtypeStruct((n_tok, d_model), jnp.float32),
        compiler_params=pltpu.CompilerParams(
            dimension_semantics=("arbitrary",),
            vmem_limit_bytes=VMEM_LIMIT_BYTES),
        name="combine",
    )(lidx, gates, nbe, lb, hrow, extra, x1, norm_w, y_rows)


def _retention_tables(seq, hd, ts):
    half = hd // 2
    inv = ROPE_BASE ** (-jnp.arange(half, dtype=jnp.float32) / half)
    ang = jnp.arange(seq, dtype=jnp.float32)[:, None] * inv[None, :]
    cos, sin = jnp.cos(ang), jnp.sin(ang)
    cos2 = jnp.concatenate([cos, cos], axis=-1)
    sin2 = jnp.concatenate([-sin, sin], axis=-1)
    log_g = jnp.log1p(-jnp.exp2(-5.0 - jnp.arange(RET_HEADS, dtype=jnp.float32)))
    pos = jnp.arange(ts, dtype=jnp.float32)
    dist = jnp.abs(pos[:, None] - pos[None, :])
    chunk = jnp.arange(ts, dtype=jnp.int32) // CHUNK
    visible = chunk[None, :] <= chunk[:, None]
    dmat = jnp.where(visible[None], jnp.exp(log_g[:, None, None] * dist[None]), 0.0)
    qdec = jnp.exp(log_g[:, None] * (pos + 1.0)[None, :])
    kdec = jnp.exp(log_g[:, None] * (ts - 1.0 - pos)[None, :])
    qdec = jnp.broadcast_to(qdec[:, :, None], (RET_HEADS, ts, hd))
    kdec = jnp.broadcast_to(kdec[:, :, None], (RET_HEADS, ts, hd))
    tile_decay = tuple(math.exp(math.log1p(-2.0 ** (-5.0 - h)) * ts) for h in range(RET_HEADS))
    return cos2, sin2, dmat, qdec, kdec, tile_decay


def _routing_tables(eid, rank, before, counts, n_tok, n_exp, rows_per_tok):
    tm, tb, ts = ROW_GRANULE, TOK_BLOCK, ROUTE_CHUNK
    cb = before[::tb // ts]
    nbe = jnp.concatenate([cb[1:], counts[None]], axis=0) - cb
    nal = nbe + (nbe & (STRIP_ALIGN - 1))
    region = jnp.sum(nal, axis=0)
    tiles_per = (region + tm - 1) // tm
    tile_end = jnp.cumsum(tiles_per)
    offs = (tile_end - tiles_per) * tm
    n_tiles = (n_tok * TOP_K + (n_tok // tb) * n_exp * (STRIP_ALIGN - 1)) // tm + n_exp
    hrow = offs[None, :] + jnp.cumsum(nal, axis=0) - nal
    lb = jnp.cumsum(nal, axis=1) - nal
    delta = jnp.repeat(lb - cb, tb, axis=0)
    onehot = eid[:, :, None] == jnp.arange(n_exp, dtype=jnp.int32)
    lidx = rank + jnp.sum(jnp.where(onehot, delta[None], 0), axis=-1)
    lidx = lidx * rows_per_tok
    padrow = offs + region
    padlen = tiles_per * tm - region
    as_i32 = lambda a: a.reshape(-1).astype(jnp.int32)
    return (n_tiles, as_i32(tile_end - tiles_per), as_i32(tiles_per), as_i32(tile_end[-1:]),
            as_i32(lidx), as_i32(nbe), as_i32(lb), as_i32(hrow),
            as_i32(jnp.sum(nal, axis=1) - tb * TOP_K), as_i32(padrow), as_i32(padlen))


def _layer(x, norm_mix_w, w_in, ret_norm_w, w_pool, pool_scale, w_out, norm_ffn_w,
           w_router, b_router, w_gu, b_gu, w_down, b_down, final_w):
    bsz, seq, d_model = x.shape
    n_tok = bsz * seq
    n_exp = w_router.shape[1]
    rw = ret_norm_w.shape[0]
    hd = rw // RET_HEADS
    ts = SEQ_TILE
    bf = jnp.bfloat16
    rpt = d_model // PACK // LANES
    assert d_model % (PACK * LANES) == 0 and (STRIP_ALIGN * rpt) % SUBLANES == 0
    assert seq % ts == 0 and ts % CHUNK == 0 and n_tok % TOK_BLOCK == 0
    assert n_tok % ROUTE_TILE == 0 and ROUTE_TILE % ROUTE_CHUNK == 0 and TOK_BLOCK % ROUTE_CHUNK == 0
    assert TOK_BLOCK & (TOK_BLOCK - 1) == 0 and ROW_GRANULE & (ROW_GRANULE - 1) == 0
    assert ROW_TILE % ROW_GRANULE == 0

    cos2, sin2, dmat, qdec, kdec, tile_decay = _retention_tables(seq, hd, ts)
    tri = jnp.arange(ROUTE_CHUNK, dtype=jnp.int32)
    utri = (tri[:, None] < tri[None, :]).astype(bf)

    x1, h2p = _mix(
        x, norm_mix_w[None], w_in.astype(bf), cos2, sin2, dmat, qdec, kdec, ret_norm_w[None],
        w_pool.astype(bf), pool_scale[None], w_out.astype(bf), norm_ffn_w[None], tile_decay)
    eid, gate, rank, before, cnt = _route(h2p, w_router.T.astype(bf), b_router[:, None], utri)

    (n_tiles, tile_start, tile_count, n_used, lidx, nbe, lb, hrow, extra, padrow, padlen) = _routing_tables(
        eid, rank, before[:, :, 0].astype(jnp.int32), cnt[:, 0].astype(jnp.int32), n_tok, n_exp, rpt)

    xs = _dispatch(lidx, nbe, lb, hrow, extra, padrow, padlen, n_used, h2p, n_tiles * ROW_GRANULE, d_model)
    y_rows = _experts(tile_start, tile_count, xs, w_gu, b_gu, w_down, b_down)
    out = _combine(lidx, gate.reshape(-1), nbe, lb, hrow, extra, x1.reshape(n_tok, d_model), final_w[None], y_rows)
    return out.reshape(bsz, seq, d_model)


def kernel(x, norm_mix_w, w_in, ret_norm_w, w_pool, pool_scale, w_out, norm_ffn_w, w_router, b_router,
           w_gu, b_gu, w_down, b_down, norm_final_w):
    depth = norm_mix_w.shape[0]
    for l in range(depth):
        last = l == depth - 1
        assert last, "stacked layers need an un-normalised combine output"
        x = _layer(x, norm_mix_w[l], w_in[l], ret_norm_w[l], w_pool[l], pool_scale[l], w_out[l],
                   norm_ffn_w[l], w_router[l], b_router[l], w_gu[l], b_gu[l], w_down[l], b_down[l],
                   norm_final_w)
    return x
```

```python
import functools
import math

import jax
import jax.numpy as jnp
from jax import lax
from jax.experimental import pallas as pl
from jax.experimental.pallas import tpu as pltpu

CHUNK = 64
RET_HEADS = 4
POOL_WINDOWS = (2, 4, 8, 16)
ROPE_BASE = 10000.0
TOP_K = 4
SWIGLU_LIMIT = 7.0
SWIGLU_ALPHA = 1.702
EPS = 1e-6

LANES = 128
SUBLANES = 8
VMEM_LIMIT_BYTES = 56 * 1024 * 1024

SEQ_TILE = 256
MIX_STREAMS = 4
ROUTE_TILE = 2048
ROUTE_CHUNK = 256
ROW_GRANULE = 128
ROW_TILE = 512
TOK_BLOCK = 1024
POOL_HALO = 16
PACK = 2
STRIP_ALIGN = 2
GATHER_UNROLL = 16
STRIP_UNROLL = 4
RARE_STRIP_BIT = 8


def _rms(x, w):
    return x * lax.rsqrt(jnp.mean(x * x, axis=-1, keepdims=True) + EPS) * w


def _bf16_bits_hi(x):
    return lax.bitcast_convert_type(x.astype(jnp.bfloat16).astype(jnp.float32), jnp.uint32)


def _pack_rows(x, out_ref, n_rows):
    half = x.shape[1] // PACK
    packed = (_bf16_bits_hi(x[:, :half]) >> 16) | _bf16_bits_hi(x[:, half:])
    rpt = half // LANES
    for si in range(rpt):
        out_ref[pl.ds(si, n_rows, stride=rpt), :] = packed[:, si * LANES:(si + 1) * LANES]


def _unpack_words(u):
    lo = lax.bitcast_convert_type(u << 16, jnp.float32)
    hi = lax.bitcast_convert_type(u & jnp.uint32(0xFFFF0000), jnp.float32)
    return lo, hi


def _unpack_rows(g_ref, n_rows, rpt):
    lo, hi = [], []
    for si in range(rpt):
        l, h = _unpack_words(g_ref[pl.ds(si, n_rows, stride=rpt), :])
        lo.append(l.astype(jnp.bfloat16))
        hi.append(h.astype(jnp.bfloat16))
    return jnp.concatenate(lo + hi, axis=-1)


def _mix_kernel(x_ref, nmw_ref, win_ref, cos_ref, sin_ref, dmat_ref, qdec_ref, kdec_ref,
                rnw_ref, wpool_ref, pscale_ref, wout_ref, nfw_ref,
                x1_ref, h2p_ref,
                state_ref, p1_ref, p2_ref, p4_ref, p8_ref,
                *, tile_decay):
    j = pl.program_id(1)
    n_str = x_ref.shape[0]
    ts = x_ref.shape[2]
    n_heads = RET_HEADS
    rw = rnw_ref.shape[1]
    hd = rw // n_heads
    pgw = wpool_ref.shape[1]
    halo = POOL_HALO
    pool_refs = (p1_ref, p2_ref, p4_ref, p8_ref)

    @pl.when(j == 0)
    def _():
        state_ref[...] = jnp.zeros_like(state_ref)
        for r in pool_refs:
            r[:, 0:halo, :] = jnp.zeros((n_str, halo, r.shape[2]), jnp.float32)

    @pl.when(j > 0)
    def _():
        for r in pool_refs:
            r[:, 0:halo, :] = r[:, ts:ts + halo, :]

    x = jnp.concatenate([x_ref[si, 0] for si in range(n_str)], axis=0)
    h = _rms(x, nmw_ref[...]).astype(jnp.bfloat16)
    z_all = jnp.dot(h, win_ref[...], preferred_element_type=jnp.float32)

    cos = cos_ref[...]
    sin = sin_ref[...]
    k_scale = hd ** -0.5
    tpos = (j * ts + lax.broadcasted_iota(jnp.int32, (ts, pgw), 0) + 1).astype(jnp.float32)

    mix_rows = []
    for si in range(n_str):
        z = z_all[si * ts:(si + 1) * ts]
        parts = []
        for hh in range(n_heads):
            q = z[:, hh * hd:(hh + 1) * hd]
            k = z[:, rw + hh * hd:rw + (hh + 1) * hd]
            v = z[:, 2 * rw + hh * hd:2 * rw + (hh + 1) * hd]
            g = z[:, 3 * rw + hh * hd:3 * rw + (hh + 1) * hd]
            q = q * cos + pltpu.roll(q, hd // 2, 1) * sin
            k = (k * cos + pltpu.roll(k, hd // 2, 1) * sin) * k_scale
            qb = q.astype(jnp.bfloat16)
            kb = k.astype(jnp.bfloat16)
            vb = v.astype(jnp.bfloat16)
            s = lax.dot_general(qb, kb, (((1,), (1,)), ((), ())), preferred_element_type=jnp.float32)
            pmat = (s * dmat_ref[hh]).astype(jnp.bfloat16)
            o = jnp.dot(pmat, vb, preferred_element_type=jnp.float32)
            state = state_ref[si, hh]
            qd = (q * qdec_ref[hh]).astype(jnp.bfloat16)
            o = o + jnp.dot(qd, state.astype(jnp.bfloat16), preferred_element_type=jnp.float32)
            kd = (k * kdec_ref[hh]).astype(jnp.bfloat16)
            u = lax.dot_general(kd, vb, (((0,), (0,)), ((), ())), preferred_element_type=jnp.float32)
            state_ref[si, hh] = state * tile_decay[hh] + u
            o = o * lax.rsqrt(jnp.mean(o * o, axis=-1, keepdims=True) + EPS)
            o = o * rnw_ref[:, hh * hd:(hh + 1) * hd] * (g * (1.0 / (1.0 + jnp.exp(-g))))
            parts.append(o.astype(jnp.bfloat16))

        p = z[:, 4 * rw:]
        p1_ref[si, halo:halo + ts, :] = p
        s2 = p + p1_ref[si, halo - 1:halo - 1 + ts, :]
        p2_ref[si, halo:halo + ts, :] = s2[:, pgw:]
        s4 = s2[:, pgw:] + p2_ref[si, halo - 2:halo - 2 + ts, :]
        p4_ref[si, halo:halo + ts, :] = s4[:, pgw:]
        s8 = s4[:, pgw:] + p4_ref[si, halo - 4:halo - 4 + ts, :]
        p8_ref[si, halo:halo + ts, :] = s8[:, pgw:]
        s16 = s8[:, pgw:] + p8_ref[si, halo - 8:halo - 8 + ts, :]
        sums = (s2[:, :pgw], s4[:, :pgw], s8[:, :pgw], s16)
        for gi, w in enumerate(POOL_WINDOWS):
            cnt = jnp.minimum(tpos, float(w))
            pooled = sums[gi] / cnt - p[:, gi * pgw:(gi + 1) * pgw]
            mixed = jnp.dot(pooled.astype(jnp.bfloat16), wpool_ref[gi], preferred_element_type=jnp.float32)
            mixed = mixed * pscale_ref[:, gi * pgw:(gi + 1) * pgw]
            parts.append(mixed.astype(jnp.bfloat16))
        mix_rows.append(jnp.concatenate(parts, axis=-1))

    mix = jnp.concatenate(mix_rows, axis=0)
    x1 = x + jnp.dot(mix, wout_ref[...], preferred_element_type=jnp.float32)
    h2 = _rms(x1, nfw_ref[...])
    for si in range(n_str):
        x1_ref[si, 0] = x1[si * ts:(si + 1) * ts]
        _pack_rows(h2[si * ts:(si + 1) * ts], h2p_ref.at[si, 0], ts)


def _mix(x, nmw, win_b, cos2, sin2, dmat, qdec, kdec, rnw, wpool_b, pscale, wout_b, nfw, tile_decay):
    bsz, seq, d_model = x.shape
    ts = SEQ_TILE
    n_tok = bsz * seq
    rw = rnw.shape[1]
    hd = rw // RET_HEADS
    pw = pscale.shape[1]
    pgw = pw // len(POOL_WINDOWS)
    nj = seq // ts
    rows_per_tok = d_model // PACK // LANES

    def const(shape):
        return pl.BlockSpec(shape, lambda b, j: (0,) * len(shape))

    n_str = MIX_STREAMS if bsz % MIX_STREAMS == 0 else 1
    per = bsz // n_str
    in_specs = [
        pl.BlockSpec((n_str, 1, ts, d_model), lambda q, j: (0, q, j, 0)),
        const(nmw.shape), const(win_b.shape),
        pl.BlockSpec((ts, hd), lambda q, j: (j, 0)),
        pl.BlockSpec((ts, hd), lambda q, j: (j, 0)),
        const(dmat.shape), const(qdec.shape), const(kdec.shape),
        const(rnw.shape), const(wpool_b.shape), const(pscale.shape), const(wout_b.shape),
        const(nfw.shape),
    ]
    out_shape = [
        jax.ShapeDtypeStruct((n_str, per, seq, d_model), jnp.float32),
        jax.ShapeDtypeStruct((n_str, per, seq * rows_per_tok, LANES), jnp.uint32),
    ]
    out_specs = [
        pl.BlockSpec((n_str, 1, ts, d_model), lambda q, j: (0, q, j, 0)),
        pl.BlockSpec((n_str, 1, ts * rows_per_tok, LANES), lambda q, j: (0, q, j, 0)),
    ]
    scratch = [
        pltpu.VMEM((n_str, RET_HEADS, hd, hd), jnp.float32),
        pltpu.VMEM((n_str, POOL_HALO + ts, pw), jnp.float32),
        pltpu.VMEM((n_str, POOL_HALO + ts, pw - pgw), jnp.float32),
        pltpu.VMEM((n_str, POOL_HALO + ts, pw - 2 * pgw), jnp.float32),
        pltpu.VMEM((n_str, POOL_HALO + ts, pw - 3 * pgw), jnp.float32),
    ]
    x1, h2p = pl.pallas_call(
        functools.partial(_mix_kernel, tile_decay=tile_decay),
        grid=(per, nj),
        in_specs=in_specs,
        out_specs=out_specs,
        out_shape=out_shape,
        scratch_shapes=scratch,
        compiler_params=pltpu.CompilerParams(
            dimension_semantics=("arbitrary", "arbitrary"),
            vmem_limit_bytes=VMEM_LIMIT_BYTES),
        name="mix",
    )(x.reshape(n_str, per, seq, d_model), nmw, win_b, cos2, sin2, dmat, qdec, kdec, rnw, wpool_b,
      pscale, wout_b, nfw)
    return x1.reshape(bsz, seq, d_model), h2p.reshape(n_tok * rows_per_tok, LANES)


def _route_kernel(h2p_ref, wrt_ref, brt_ref, utri_ref,
                  eid_ref, gate_ref, rank_ref, before_ref, cnt_ref, carry_ref):
    i = pl.program_id(0)
    n_exp, d_model = wrt_ref.shape
    rpt = d_model // PACK // LANES
    tr = h2p_ref.shape[0] // rpt
    sub = utri_ref.shape[0]

    @pl.when(i == 0)
    def _():
        carry_ref[...] = jnp.zeros_like(carry_ref)

    h2b = _unpack_rows(h2p_ref, tr, rpt)
    logits = lax.dot_general(wrt_ref[...], h2b, (((1,), (1,)), ((), ())),
                             preferred_element_type=jnp.float32) + brt_ref[...]
    eiota = lax.broadcasted_iota(jnp.int32, (n_exp, tr), 0)
    work = logits
    vals, ids, sels = [], [], []
    for _ in range(TOP_K):
        m = jnp.max(work, axis=0, keepdims=True)
        idx = jnp.min(jnp.where(work == m, eiota, n_exp), axis=0, keepdims=True)
        sel = eiota == idx
        vals.append(m)
        ids.append(idx)
        sels.append(sel)
        work = jnp.where(sel, -jnp.inf, work)
    ex = [jnp.exp(vk - vals[0]) for vk in vals]
    den = ex[0] + ex[1] + ex[2] + ex[3]
    gate_ref[...] = jnp.concatenate([e / den for e in ex], axis=0)
    eid_ref[...] = jnp.concatenate(ids, axis=0)

    member = jnp.zeros((n_exp, tr), jnp.float32)
    for sel in sels:
        member = member + sel.astype(jnp.float32)
    carry = carry_ref[...]
    befores = []
    for c in range(tr // sub):
        before_ref[c] = carry
        mc = member[:, c * sub:(c + 1) * sub]
        bc = jnp.dot(mc.astype(jnp.bfloat16), utri_ref[...], preferred_element_type=jnp.float32)
        befores.append(bc + carry[:, 0:1])
        carry = carry + jnp.sum(mc, axis=1, keepdims=True)
    before = jnp.concatenate(befores, axis=1)
    ranks = [jnp.sum(jnp.where(sel, before, 0.0), axis=0, keepdims=True) for sel in sels]
    rank_ref[...] = jnp.concatenate(ranks, axis=0).astype(jnp.int32)
    carry_ref[...] = carry
    cnt_ref[...] = carry


def _route(h2p, wrt_b, brt, utri):
    n_exp, d_model = wrt_b.shape
    rpt = d_model // PACK // LANES
    n_tok = h2p.shape[0] // rpt
    tr = ROUTE_TILE
    sub = utri.shape[0]

    def const(shape):
        return pl.BlockSpec(shape, lambda i: (0,) * len(shape))

    out_shape = [
        jax.ShapeDtypeStruct((TOP_K, n_tok), jnp.int32),
        jax.ShapeDtypeStruct((TOP_K, n_tok), jnp.float32),
        jax.ShapeDtypeStruct((TOP_K, n_tok), jnp.int32),
        jax.ShapeDtypeStruct((n_tok // sub, n_exp, LANES), jnp.float32),
        jax.ShapeDtypeStruct((n_exp, LANES), jnp.float32),
    ]
    out_specs = [
        pl.BlockSpec((TOP_K, tr), lambda i: (0, i)),
        pl.BlockSpec((TOP_K, tr), lambda i: (0, i)),
        pl.BlockSpec((TOP_K, tr), lambda i: (0, i)),
        pl.BlockSpec((tr // sub, n_exp, LANES), lambda i: (i, 0, 0)),
        pl.BlockSpec((n_exp, LANES), lambda i: (0, 0)),
    ]
    return pl.pallas_call(
        _route_kernel,
        grid=(n_tok // tr,),
        in_specs=[pl.BlockSpec((tr * rpt, LANES), lambda i: (i, 0)),
                  const(wrt_b.shape), const(brt.shape), const(utri.shape)],
        out_specs=out_specs,
        out_shape=out_shape,
        scratch_shapes=[pltpu.VMEM((n_exp, LANES), jnp.float32)],
        compiler_params=pltpu.CompilerParams(
            dimension_semantics=("arbitrary",),
            vmem_limit_bytes=VMEM_LIMIT_BYTES),
        name="route",
    )(h2p, wrt_b, brt, utri)


def _for_set_bits(n, max_bit, fn):
    def piece(bit):
        size = 1 << bit
        offset = (n >> (bit + 1)) << (bit + 1)

        @pl.when((n & size) != 0)
        def _():
            fn(offset, size)

    min_bit = STRIP_ALIGN.bit_length() - 1
    rare_bit = min(max_bit + 1, RARE_STRIP_BIT)

    @pl.when(n >= (1 << rare_bit))
    def _():
        for bit in range(max_bit, rare_bit - 1, -1):
            piece(bit)
    for bit in range(rare_bit - 1, min_bit - 1, -1):
        piece(bit)


def _aligned_len(n):
    return n + (n & (STRIP_ALIGN - 1))


def _for_strips(nbe_ref, lb_ref, hrow_ref, blk, n_exp, max_bit, fn):
    def body(i, carry):
        for u in range(STRIP_UNROLL):
            k = blk * n_exp + i * STRIP_UNROLL + u
            n = _aligned_len(nbe_ref[k])
            lb = lb_ref[k]
            hrow = hrow_ref[k]
            _for_set_bits(n, max_bit, lambda offset, size: fn(lb + offset, hrow + offset, size))
        return carry
    assert n_exp % STRIP_UNROLL == 0
    lax.fori_loop(0, n_exp // STRIP_UNROLL, body, 0)


def _dispatch_kernel(lidx_ref, nbe_ref, lb_ref, hrow_ref, extra_ref, padrow_ref, padlen_ref, nused_ref,
                     h2p_ref, xs_hbm, stage_ref, zero_ref, sem, zsem, *, nblk):
    b = pl.program_id(0)
    rpt = zero_ref.shape[0] // ROW_GRANULE
    tb = h2p_ref.shape[0] // rpt
    n_exp = padrow_ref.shape[0]
    n_tok = lidx_ref.shape[0] // TOP_K
    slot_rows = stage_ref.shape[0] // 2
    max_bit = tb.bit_length() - 1
    slot = b % 2

    def strip_copy(sl, staging_row, sorted_row, size):
        src = pl.multiple_of(sl * slot_rows + staging_row * rpt, SUBLANES)
        dst = pl.multiple_of(sorted_row * rpt, SUBLANES)
        return pltpu.make_async_copy(stage_ref.at[pl.ds(src, size * rpt), :],
                                     xs_hbm.at[pl.ds(dst, size * rpt), :], sem.at[sl])

    def start_strips(blk, sl):
        _for_strips(nbe_ref, lb_ref, hrow_ref, blk, n_exp, max_bit,
                    lambda srow, hrow, size: strip_copy(sl, srow, hrow, size).start())

    def wait_strips(blk, sl):
        def sized_copy(offset, size):
            return pltpu.make_async_copy(stage_ref.at[pl.ds(sl * slot_rows, size * rpt), :],
                                         xs_hbm.at[pl.ds(0, size * rpt), :], sem.at[sl])
        sized_copy(0, tb * TOP_K).wait()
        _for_set_bits(extra_ref[blk], n_exp.bit_length() - 1, lambda o, s: sized_copy(o, s).wait())

    @pl.when(b >= 2)
    def _():
        wait_strips(b - 2, slot)

    def scatter_block(static_slot):
        def scatter_group(c, carry):
            tok0 = c * GATHER_UNROLL
            idx0 = b * tb + tok0
            for u in range(GATHER_UNROLL):
                row = h2p_ref[pl.ds(pl.multiple_of((tok0 + u) * rpt, rpt), rpt), :]
                for k in range(TOP_K):
                    dst = pl.multiple_of(lidx_ref[idx0 + (k * n_tok + u)], rpt)
                    stage_ref[pl.ds(static_slot * slot_rows + dst, rpt), :] = row
            return carry
        lax.fori_loop(0, tb // GATHER_UNROLL, scatter_group, 0)

    for static_slot in range(2):
        pl.when(slot == static_slot)(functools.partial(scatter_block, static_slot))

    def zero_tail(e, carry):
        n = nbe_ref[b * n_exp + e]

        @pl.when((n & (STRIP_ALIGN - 1)) != 0)
        def _():
            dst = pl.multiple_of(slot * slot_rows + (lb_ref[b * n_exp + e] + n) * rpt, rpt)
            stage_ref[pl.ds(dst, rpt), :] = jnp.zeros((rpt, LANES), jnp.uint32)
        return carry
    lax.fori_loop(0, n_exp, zero_tail, 0)

    start_strips(b, slot)

    @pl.when(b == nblk - 1)
    def _():
        if nblk >= 2:
            wait_strips(b - 1, 1 - slot)
        wait_strips(b, slot)
        zero_ref[...] = jnp.zeros_like(zero_ref)
        pad_bit = ROW_GRANULE.bit_length() - 2

        def pad_copy(e, offset, size):
            dst = pl.multiple_of((padrow_ref[e] + offset) * rpt, SUBLANES)
            return pltpu.make_async_copy(zero_ref.at[pl.ds(0, size * rpt), :],
                                         xs_hbm.at[pl.ds(dst, size * rpt), :], zsem)

        def pad_start(e, carry):
            _for_set_bits(padlen_ref[e], pad_bit, lambda o, s: pad_copy(e, o, s).start())
            return carry

        def pad_wait(e, carry):
            _for_set_bits(padlen_ref[e], pad_bit, lambda o, s: pad_copy(e, o, s).wait())
            return carry
        lax.fori_loop(0, n_exp, pad_start, 0)

        def tail_copy(i):
            dst = pl.multiple_of(i * (ROW_GRANULE * rpt), SUBLANES)
            return pltpu.make_async_copy(zero_ref, xs_hbm.at[pl.ds(dst, ROW_GRANULE * rpt), :], zsem)

        def tail_start(i, carry):
            tail_copy(i).start()
            return carry

        def tail_wait(i, carry):
            tail_copy(i).wait()
            return carry
        n_gran = xs_hbm.shape[0] // (ROW_GRANULE * rpt)
        lax.fori_loop(nused_ref[0], n_gran, tail_start, 0)
        lax.fori_loop(0, n_exp, pad_wait, 0)
        lax.fori_loop(nused_ref[0], n_gran, tail_wait, 0)


def _dispatch(lidx, nbe, lb, hrow, extra, padrow, padlen, n_used, h2p, n_rows, d_model):
    tb = TOK_BLOCK
    rpt = d_model // PACK // LANES
    n_tok = h2p.shape[0] // rpt
    n_exp = padrow.shape[0]
    slot_rows = (tb * TOP_K + n_exp * (STRIP_ALIGN - 1)) * rpt
    grid_spec = pltpu.PrefetchScalarGridSpec(
        num_scalar_prefetch=8,
        grid=(n_tok // tb,),
        in_specs=[pl.BlockSpec((tb * rpt, LANES), lambda i, *_: (i, 0))],
        out_specs=pl.BlockSpec(memory_space=pl.ANY),
        scratch_shapes=[pltpu.VMEM((2 * slot_rows, LANES), jnp.uint32),
                        pltpu.VMEM((ROW_GRANULE * rpt, LANES), jnp.uint32),
                        pltpu.SemaphoreType.DMA((2,)),
                        pltpu.SemaphoreType.DMA],
    )
    return pl.pallas_call(
        functools.partial(_dispatch_kernel, nblk=n_tok // tb),
        grid_spec=grid_spec,
        out_shape=jax.ShapeDtypeStruct((n_rows * rpt, LANES), jnp.uint32),
        compiler_params=pltpu.CompilerParams(
            dimension_semantics=("arbitrary",),
            vmem_limit_bytes=VMEM_LIMIT_BYTES),
        name="dispatch",
    )(lidx, nbe, lb, hrow, extra, padrow, padlen, n_used, h2p)


def _experts_kernel(tstart_ref, tcount_ref, wgu_hbm, bgu_ref, wd_hbm, bd_ref, xs_hbm,
                    y_hbm, wgu_f, wd_f, wgu_b, wd_b, xbuf, ybuf, pend_ref, wsem, xsem, ysem, *, n_exp):
    e = pl.program_id(0)
    d_ff = wd_f.shape[0]
    d_model = wd_f.shape[1]
    rpt = d_model // PACK // LANES
    gran_rows = ROW_GRANULE * rpt
    gpt = ROW_TILE // ROW_GRANULE
    n_gran = y_hbm.shape[0] // gran_rows
    g0 = tstart_ref[e]
    ng = tcount_ref[e]
    nt = ng // gpt
    tail = ng - nt * gpt

    def hbm_rows(ref, gran, count):
        return ref.at[pl.ds(pl.multiple_of(gran * gran_rows, SUBLANES), count * gran_rows), :]

    def x_copy(gran, sl, count=gpt):
        return pltpu.make_async_copy(hbm_rows(xs_hbm, gran, count),
                                     xbuf.at[sl, pl.ds(0, count * gran_rows), :], xsem.at[sl])

    def y_copy(gran, sl, count=gpt):
        return pltpu.make_async_copy(ybuf.at[sl, pl.ds(0, count * gran_rows), :],
                                     hbm_rows(y_hbm, gran, count), ysem.at[sl])

    def mlp(sl, tm):
        x = _unpack_rows(xbuf.at[sl], tm, rpt)
        gu = jnp.dot(x, wgu_b[...], preferred_element_type=jnp.float32) + bgu_ref[0]
        x_glu = jnp.minimum(gu[:, :d_ff], SWIGLU_LIMIT)
        x_lin = jnp.clip(gu[:, d_ff:], -SWIGLU_LIMIT, SWIGLU_LIMIT)
        act = x_glu * (1.0 / (1.0 + jnp.exp(-SWIGLU_ALPHA * x_glu))) * (x_lin + 1.0)
        y = jnp.dot(act.astype(jnp.bfloat16), wd_b[...], preferred_element_type=jnp.float32) + bd_ref[0]
        _pack_rows(y, ybuf.at[sl], tm)

    def weight_copies(ex):
        return (pltpu.make_async_copy(wgu_hbm.at[ex], wgu_f, wsem.at[0]),
                pltpu.make_async_copy(wd_hbm.at[ex], wd_f, wsem.at[1]))

    tail_slot = 2
    tail_gran = g0 + nt * gpt

    def wait_pending_tail_y():
        for count in range(1, gpt):
            @pl.when(pend_ref[0] == count)
            def _():
                y_copy(0, tail_slot, count).wait()

    @pl.when(e == 0)
    def _():
        pend_ref[0] = 0

        @pl.when(nt > 0)
        def _():
            x_copy(g0, 0).start()

    for count in range(1, gpt):
        @pl.when(tail == count)
        def _():
            x_copy(tail_gran, tail_slot, count).start()

    @pl.when(e == 0)
    def _():
        for c in weight_copies(0):
            c.start(priority=1)
    for c in weight_copies(e):
        c.wait()
    wgu_b[...] = wgu_f[...].astype(jnp.bfloat16)
    wd_b[...] = wd_f[...].astype(jnp.bfloat16)

    @pl.when(e + 1 < n_exp)
    def _():
        for c in weight_copies(e + 1):
            c.start(priority=1)

    @pl.when(ng > 0)
    def _():
        def tile_step(j, carry):
            sl = j % 2
            gran = g0 + j * gpt

            @pl.when(j + 1 < nt)
            def _():
                x_copy(gran + gpt, 1 - sl).start()
            x_copy(gran, sl).wait()

            @pl.when(j >= 2)
            def _():
                y_copy(gran - 2 * gpt, sl).wait()
            mlp(sl, ROW_TILE)
            y_copy(gran, sl).start()
            return carry
        lax.fori_loop(0, nt, tile_step, 0)

        @pl.when(nt >= 2)
        def _():
            y_copy(g0 + (nt - 2) * gpt, nt % 2).wait()

        @pl.when(nt >= 1)
        def _():
            y_copy(g0 + (nt - 1) * gpt, (nt - 1) % 2).wait()

    @pl.when(e + 1 < n_exp)
    def _():
        nxt = jnp.minimum(e + 1, n_exp - 1)

        @pl.when(tcount_ref[nxt] >= gpt)
        def _():
            x_copy(tstart_ref[nxt], 0).start()

    for count in range(1, gpt):
        @pl.when(tail == count)
        def _():
            x_copy(tail_gran, tail_slot, count).wait()
            wait_pending_tail_y()
            mlp(tail_slot, count * ROW_GRANULE)
            y_copy(tail_gran, tail_slot, count).start()
            pend_ref[0] = count

    @pl.when(e == n_exp - 1)
    def _():
        wait_pending_tail_y()
        ybuf[0, 0:gran_rows, :] = jnp.zeros((gran_rows, LANES), ybuf.dtype)

        def tail_start(i, carry):
            y_copy(i, 0, 1).start()
            return carry

        def tail_wait(i, carry):
            y_copy(i, 0, 1).wait()
            return carry
        lax.fori_loop(g0 + ng, n_gran, tail_start, 0)
        lax.fori_loop(g0 + ng, n_gran, tail_wait, 0)


def _experts(tile_start, tile_count, xs, w_gu, b_gu, w_down, b_down):
    tm = ROW_TILE
    n_exp, d_model, d_gu = w_gu.shape
    d_ff = w_down.shape[1]
    rpt = d_model // PACK // LANES
    grid_spec = pltpu.PrefetchScalarGridSpec(
        num_scalar_prefetch=2,
        grid=(n_exp,),
        in_specs=[
            pl.BlockSpec(memory_space=pl.ANY),
            pl.BlockSpec((1, 1, d_gu), lambda e, *_: (e, 0, 0)),
            pl.BlockSpec(memory_space=pl.ANY),
            pl.BlockSpec((1, 1, d_model), lambda e, *_: (e, 0, 0)),
            pl.BlockSpec(memory_space=pl.ANY),
        ],
        out_specs=pl.BlockSpec(memory_space=pl.ANY),
        scratch_shapes=[pltpu.VMEM((d_model, d_gu), jnp.float32),
                        pltpu.VMEM((d_ff, d_model), jnp.float32),
                        pltpu.VMEM((d_model, d_gu), jnp.bfloat16),
                        pltpu.VMEM((d_ff, d_model), jnp.bfloat16),
                        pltpu.VMEM((3, tm * rpt, LANES), jnp.uint32),
                        pltpu.VMEM((3, tm * rpt, LANES), jnp.uint32),
                        pltpu.SMEM((1,), jnp.int32),
                        pltpu.SemaphoreType.DMA((2,)),
                        pltpu.SemaphoreType.DMA((3,)),
                        pltpu.SemaphoreType.DMA((3,))],
    )
    return pl.pallas_call(
        functools.partial(_experts_kernel, n_exp=n_exp),
        grid_spec=grid_spec,
        out_shape=jax.ShapeDtypeStruct(xs.shape, jnp.uint32),
        compiler_params=pltpu.CompilerParams(
            dimension_semantics=("arbitrary",),
            vmem_limit_bytes=VMEM_LIMIT_BYTES),
        name="experts",
    )(tile_start, tile_count, w_gu, b_gu.reshape(n_exp, 1, d_gu), w_down,
      b_down.reshape(n_exp, 1, d_model), xs)


def _combine_kernel(lidx_ref, gate_ref, nbe_ref, lb_ref, hrow_ref, extra_ref,
                    x1_ref, nw_ref, y_hbm, out_ref, stage_ref, alo_ref, ahi_ref, sem, *, nblk):
    b = pl.program_id(0)
    tb = x1_ref.shape[0]
    rpt = alo_ref.shape[0] // tb
    n_exp = nbe_ref.shape[0] // nblk
    n_tok = lidx_ref.shape[0] // TOP_K
    slot_rows = stage_ref.shape[0] // 2
    max_bit = tb.bit_length() - 1
    slot = b % 2

    def strip_copy(sl, staging_row, sorted_row, size):
        src = pl.multiple_of(sorted_row * rpt, SUBLANES)
        dst = pl.multiple_of(sl * slot_rows + staging_row * rpt, SUBLANES)
        return pltpu.make_async_copy(y_hbm.at[pl.ds(src, size * rpt), :],
                                     stage_ref.at[pl.ds(dst, size * rpt), :], sem.at[sl])

    def start_strips(blk, sl):
        _for_strips(nbe_ref, lb_ref, hrow_ref, blk, n_exp, max_bit,
                    lambda srow, hrow, size: strip_copy(sl, srow, hrow, size).start())

    @pl.when(b == 0)
    def _():
        start_strips(0, 0)

    @pl.when(b + 1 < nblk)
    def _():
        start_strips(b + 1, 1 - slot)

    def sized_copy(offset, size):
        return pltpu.make_async_copy(y_hbm.at[pl.ds(0, size * rpt), :],
                                     stage_ref.at[pl.ds(slot * slot_rows, size * rpt), :], sem.at[slot])
    sized_copy(0, tb * TOP_K).wait()
    _for_set_bits(extra_ref[b], n_exp.bit_length() - 1, lambda o, s: sized_copy(o, s).wait())

    def gather_block(static_slot):
        def gather_group(c, carry):
            tok0 = c * GATHER_UNROLL
            idx0 = b * tb + tok0
            for u in range(GATHER_UNROLL):
                lo = hi = None
                for k in range(TOP_K):
                    src = pl.multiple_of(lidx_ref[idx0 + (k * n_tok + u)], rpt)
                    g = gate_ref[idx0 + (k * n_tok + u)]
                    l, h = _unpack_words(stage_ref[pl.ds(static_slot * slot_rows + src, rpt), :])
                    lo = g * l if lo is None else lo + g * l
                    hi = g * h if hi is None else hi + g * h
                dst = pl.multiple_of((tok0 + u) * rpt, rpt)
                alo_ref[pl.ds(dst, rpt), :] = lo
                ahi_ref[pl.ds(dst, rpt), :] = hi
            return carry
        lax.fori_loop(0, tb // GATHER_UNROLL, gather_group, 0)

    for static_slot in range(2):
        pl.when(slot == static_slot)(functools.partial(gather_block, static_slot))

    parts = [alo_ref[pl.ds(si, tb, stride=rpt), :] for si in range(rpt)]
    parts += [ahi_ref[pl.ds(si, tb, stride=rpt), :] for si in range(rpt)]
    x2 = x1_ref[...] + jnp.concatenate(parts, axis=-1)
    out_ref[...] = _rms(x2, nw_ref[...])


def _combine(lidx, gates, nbe, lb, hrow, extra, x1, norm_w, y_rows):
    tb = TOK_BLOCK
    n_tok, d_model = x1.shape
    rpt = d_model // PACK // LANES
    n_exp = nbe.shape[0] // (n_tok // tb)
    slot_rows = (tb * TOP_K + n_exp * (STRIP_ALIGN - 1)) * rpt
    grid_spec = pltpu.PrefetchScalarGridSpec(
        num_scalar_prefetch=6,
        grid=(n_tok // tb,),
        in_specs=[
            pl.BlockSpec((tb, d_model), lambda i, *_: (i, 0)),
            pl.BlockSpec((1, d_model), lambda i, *_: (0, 0)),
            pl.BlockSpec(memory_space=pl.ANY),
        ],
        out_specs=pl.BlockSpec((tb, d_model), lambda i, *_: (i, 0)),
        scratch_shapes=[pltpu.VMEM((2 * slot_rows, LANES), jnp.uint32),
                        pltpu.VMEM((tb * rpt, LANES), jnp.float32),
                        pltpu.VMEM((tb * rpt, LANES), jnp.float32),
                        pltpu.SemaphoreType.DMA((2,))],
    )
    return pl.pallas_call(
        functools.partial(_combine_kernel, nblk=n_tok // tb),
        grid_spec=grid_spec,
        out_shape=jax.ShapeDtypeStruct((n_tok, d_model), jnp.float32),
        compiler_params=pltpu.CompilerParams(
            dimension_semantics=("arbitrary",),
            vmem_limit_bytes=VMEM_LIMIT_BYTES),
        name="combine",
    )(lidx, gates, nbe, lb, hrow, extra, x1, norm_w, y_rows)


def _retention_tables(seq, hd, ts):
    half = hd // 2
    inv = ROPE_BASE ** (-jnp.arange(half, dtype=jnp.float32) / half)
    ang = jnp.arange(seq, dtype=jnp.float32)[:, None] * inv[None, :]
    cos, sin = jnp.cos(ang), jnp.sin(ang)
    cos2 = jnp.concatenate([cos, cos], axis=-1)
    sin2 = jnp.concatenate([-sin, sin], axis=-1)
    log_g = jnp.log1p(-jnp.exp2(-5.0 - jnp.arange(RET_HEADS, dtype=jnp.float32)))
    pos = jnp.arange(ts, dtype=jnp.float32)
    dist = jnp.abs(pos[:, None] - pos[None, :])
    chunk = jnp.arange(ts, dtype=jnp.int32) // CHUNK
    visible = chunk[None, :] <= chunk[:, None]
    dmat = jnp.where(visible[None], jnp.exp(log_g[:, None, None] * dist[None]), 0.0)
    qdec = jnp.exp(log_g[:, None] * (pos + 1.0)[None, :])
    kdec = jnp.exp(log_g[:, None] * (ts - 1.0 - pos)[None, :])
    qdec = jnp.broadcast_to(qdec[:, :, None], (RET_HEADS, ts, hd))
    kdec = jnp.broadcast_to(kdec[:, :, None], (RET_HEADS, ts, hd))
    tile_decay = tuple(math.exp(math.log1p(-2.0 ** (-5.0 - h)) * ts) for h in range(RET_HEADS))
    return cos2, sin2, dmat, qdec, kdec, tile_decay


def _routing_tables(eid, rank, before, counts, n_tok, n_exp, rows_per_tok):
    tm, tb, ts = ROW_GRANULE, TOK_BLOCK, ROUTE_CHUNK
    cb = before[::tb // ts]
    nbe = jnp.concatenate([cb[1:], counts[None]], axis=0) - cb
    nal = nbe + (nbe & (STRIP_ALIGN - 1))
    region = jnp.sum(nal, axis=0)
    tiles_per = (region + tm - 1) // tm
    tile_end = jnp.cumsum(tiles_per)
    offs = (tile_end - tiles_per) * tm
    n_tiles = (n_tok * TOP_K + (n_tok // tb) * n_exp * (STRIP_ALIGN - 1)) // tm + n_exp
    hrow = offs[None, :] + jnp.cumsum(nal, axis=0) - nal
    lb = jnp.cumsum(nal, axis=1) - nal
    delta = jnp.repeat(lb - cb, tb, axis=0)
    onehot = eid[:, :, None] == jnp.arange(n_exp, dtype=jnp.int32)
    lidx = rank + jnp.sum(jnp.where(onehot, delta[None], 0), axis=-1)
    lidx = lidx * rows_per_tok
    padrow = offs + region
    padlen = tiles_per * tm - region
    as_i32 = lambda a: a.reshape(-1).astype(jnp.int32)
    return (n_tiles, as_i32(tile_end - tiles_per), as_i32(tiles_per), as_i32(tile_end[-1:]),
            as_i32(lidx), as_i32(nbe), as_i32(lb), as_i32(hrow),
            as_i32(jnp.sum(nal, axis=1) - tb * TOP_K), as_i32(padrow), as_i32(padlen))


def _layer(x, norm_mix_w, w_in, ret_norm_w, w_pool, pool_scale, w_out, norm_ffn_w,
           w_router, b_router, w_gu, b_gu, w_down, b_down, final_w):
    bsz, seq, d_model = x.shape
    n_tok = bsz * seq
    n_exp = w_router.shape[1]
    rw = ret_norm_w.shape[0]
    hd = rw // RET_HEADS
    ts = SEQ_TILE
    bf = jnp.bfloat16
    rpt = d_model // PACK // LANES
    assert d_model % (PACK * LANES) == 0 and (STRIP_ALIGN * rpt) % SUBLANES == 0
    assert seq % ts == 0 and ts % CHUNK == 0 and n_tok % TOK_BLOCK == 0
    assert n_tok % ROUTE_TILE == 0 and ROUTE_TILE % ROUTE_CHUNK == 0 and TOK_BLOCK % ROUTE_CHUNK == 0
    assert TOK_BLOCK & (TOK_BLOCK - 1) == 0 and ROW_GRANULE & (ROW_GRANULE - 1) == 0
    assert ROW_TILE % ROW_GRANULE == 0

    cos2, sin2, dmat, qdec, kdec, tile_decay = _retention_tables(seq, hd, ts)
    tri = jnp.arange(ROUTE_CHUNK, dtype=jnp.int32)
    utri = (tri[:, None] < tri[None, :]).astype(bf)

    x1, h2p = _mix(
        x, norm_mix_w[None], w_in.astype(bf), cos2, sin2, dmat, qdec, kdec, ret_norm_w[None],
        w_pool.astype(bf), pool_scale[None], w_out.astype(bf), norm_ffn_w[None], tile_decay)
    eid, gate, rank, before, cnt = _route(h2p, w_router.T.astype(bf), b_router[:, None], utri)

    (n_tiles, tile_start, tile_count, n_used, lidx, nbe, lb, hrow, extra, padrow, padlen) = _routing_tables(
        eid, rank, before[:, :, 0].astype(jnp.int32), cnt[:, 0].astype(jnp.int32), n_tok, n_exp, rpt)

    xs = _dispatch(lidx, nbe, lb, hrow, extra, padrow, padlen, n_used, h2p, n_tiles * ROW_GRANULE, d_model)
    y_rows = _experts(tile_start, tile_count, xs, w_gu, b_gu, w_down, b_down)
    out = _combine(lidx, gate.reshape(-1), nbe, lb, hrow, extra, x1.reshape(n_tok, d_model), final_w[None], y_rows)
    return out.reshape(bsz, seq, d_model)


def kernel(x, norm_mix_w, w_in, ret_norm_w, w_pool, pool_scale, w_out, norm_ffn_w, w_router, b_router,
           w_gu, b_gu, w_down, b_down, norm_final_w):
    depth = norm_mix_w.shape[0]
    for l in range(depth):
        last = l == depth - 1
        assert last, "stacked layers need an un-normalised combine output"
        x = _layer(x, norm_mix_w[l], w_in[l], ret_norm_w[l], w_pool[l], pool_scale[l], w_out[l],
                   norm_ffn_w[l], w_router[l], b_router[l], w_gu[l], b_gu[l], w_down[l], b_down[l],
                   norm_final_w)
    return x
```

```python
import functools
import math

import jax
import jax.numpy as jnp
from jax import lax
from jax.experimental import pallas as pl
from jax.experimental.pallas import tpu as pltpu

CHUNK = 64
RET_HEADS = 4
POOL_WINDOWS = (2, 4, 8, 16)
ROPE_BASE = 10000.0
TOP_K = 4
SWIGLU_LIMIT = 7.0
SWIGLU_ALPHA = 1.702
EPS = 1e-6

LANES = 128
SUBLANES = 8
VMEM_LIMIT_BYTES = 56 * 1024 * 1024

SEQ_TILE = 256
MIX_STREAMS = 4
ROUTE_TILE = 2048
ROUTE_CHUNK = 256
ROW_GRANULE = 128
ROW_TILE = 1024
TOK_BLOCK = 1024
POOL_HALO = 16
PACK = 2
STRIP_ALIGN = 2
GATHER_UNROLL = 16
STRIP_UNROLL = 4
RARE_STRIP_BIT = 8


def _rms(x, w):
    return x * lax.rsqrt(jnp.mean(x * x, axis=-1, keepdims=True) + EPS) * w


def _bf16_bits_hi(x):
    return lax.bitcast_convert_type(x.astype(jnp.bfloat16).astype(jnp.float32), jnp.uint32)


def _pack_rows(x, out_ref, n_rows):
    half = x.shape[1] // PACK
    packed = (_bf16_bits_hi(x[:, :half]) >> 16) | _bf16_bits_hi(x[:, half:])
    rpt = half // LANES
    for si in range(rpt):
        out_ref[pl.ds(si, n_rows, stride=rpt), :] = packed[:, si * LANES:(si + 1) * LANES]


def _unpack_words(u):
    lo = lax.bitcast_convert_type(u << 16, jnp.float32)
    hi = lax.bitcast_convert_type(u & jnp.uint32(0xFFFF0000), jnp.float32)
    return lo, hi


def _unpack_rows(g_ref, n_rows, rpt):
    lo, hi = [], []
    for si in range(rpt):
        l, h = _unpack_words(g_ref[pl.ds(si, n_rows, stride=rpt), :])
        lo.append(l.astype(jnp.bfloat16))
        hi.append(h.astype(jnp.bfloat16))
    return jnp.concatenate(lo + hi, axis=-1)


def _mix_kernel(x_ref, nmw_ref, win_ref, cos_ref, sin_ref, dmat_ref, qdec_ref, kdec_ref,
                rnw_ref, wpool_ref, pscale_ref, wout_ref, nfw_ref,
                x1_ref, h2p_ref,
                state_ref, p1_ref, p2_ref, p4_ref, p8_ref,
                *, tile_decay):
    j = pl.program_id(1)
    n_str = x_ref.shape[0]
    ts = x_ref.shape[2]
    n_heads = RET_HEADS
    rw = rnw_ref.shape[1]
    hd = rw // n_heads
    pgw = wpool_ref.shape[1]
    halo = POOL_HALO
    pool_refs = (p1_ref, p2_ref, p4_ref, p8_ref)

    @pl.when(j == 0)
    def _():
        state_ref[...] = jnp.zeros_like(state_ref)
        for r in pool_refs:
            r[:, 0:halo, :] = jnp.zeros((n_str, halo, r.shape[2]), jnp.float32)

    @pl.when(j > 0)
    def _():
        for r in pool_refs:
            r[:, 0:halo, :] = r[:, ts:ts + halo, :]

    x = jnp.concatenate([x_ref[si, 0] for si in range(n_str)], axis=0)
    h = _rms(x, nmw_ref[...]).astype(jnp.bfloat16)
    z_all = jnp.dot(h, win_ref[...], preferred_element_type=jnp.float32)

    cos = cos_ref[...]
    sin = sin_ref[...]
    k_scale = hd ** -0.5
    tpos = (j * ts + lax.broadcasted_iota(jnp.int32, (ts, pgw), 0) + 1).astype(jnp.float32)

    mix_rows = []
    for si in range(n_str):
        z = z_all[si * ts:(si + 1) * ts]
        parts = []
        for hh in range(n_heads):
            q = z[:, hh * hd:(hh + 1) * hd]
            k = z[:, rw + hh * hd:rw + (hh + 1) * hd]
            v = z[:, 2 * rw + hh * hd:2 * rw + (hh + 1) * hd]
            g = z[:, 3 * rw + hh * hd:3 * rw + (hh + 1) * hd]
            q = q * cos + pltpu.roll(q, hd // 2, 1) * sin
            k = (k * cos + pltpu.roll(k, hd // 2, 1) * sin) * k_scale
            qb = q.astype(jnp.bfloat16)
            kb = k.astype(jnp.bfloat16)
            vb = v.astype(jnp.bfloat16)
            s = lax.dot_general(qb, kb, (((1,), (1,)), ((), ())), preferred_element_type=jnp.float32)
            pmat = (s * dmat_ref[hh]).astype(jnp.bfloat16)
            o = jnp.dot(pmat, vb, preferred_element_type=jnp.float32)
            state = state_ref[si, hh]
            qd = (q * qdec_ref[hh]).astype(jnp.bfloat16)
            o = o + jnp.dot(qd, state.astype(jnp.bfloat16), preferred_element_type=jnp.float32)
            kd = (k * kdec_ref[hh]).astype(jnp.bfloat16)
            u = lax.dot_general(kd, vb, (((0,), (0,)), ((), ())), preferred_element_type=jnp.float32)
            state_ref[si, hh] = state * tile_decay[hh] + u
            o = o * lax.rsqrt(jnp.mean(o * o, axis=-1, keepdims=True) + EPS)
            o = o * rnw_ref[:, hh * hd:(hh + 1) * hd] * (g * (1.0 / (1.0 + jnp.exp(-g))))
            parts.append(o.astype(jnp.bfloat16))

        p = z[:, 4 * rw:]
        p1_ref[si, halo:halo + ts, :] = p
        s2 = p + p1_ref[si, halo - 1:halo - 1 + ts, :]
        p2_ref[si, halo:halo + ts, :] = s2[:, pgw:]
        s4 = s2[:, pgw:] + p2_ref[si, halo - 2:halo - 2 + ts, :]
        p4_ref[si, halo:halo + ts, :] = s4[:, pgw:]
        s8 = s4[:, pgw:] + p4_ref[si, halo - 4:halo - 4 + ts, :]
        p8_ref[si, halo:halo + ts, :] = s8[:, pgw:]
        s16 = s8[:, pgw:] + p8_ref[si, halo - 8:halo - 8 + ts, :]
        sums = (s2[:, :pgw], s4[:, :pgw], s8[:, :pgw], s16)
        for gi, w in enumerate(POOL_WINDOWS):
            cnt = jnp.minimum(tpos, float(w))
            pooled = sums[gi] / cnt - p[:, gi * pgw:(gi + 1) * pgw]
            mixed = jnp.dot(pooled.astype(jnp.bfloat16), wpool_ref[gi], preferred_element_type=jnp.float32)
            mixed = mixed * pscale_ref[:, gi * pgw:(gi + 1) * pgw]
            parts.append(mixed.astype(jnp.bfloat16))
        mix_rows.append(jnp.concatenate(parts, axis=-1))

    mix = jnp.concatenate(mix_rows, axis=0)
    x1 = x + jnp.dot(mix, wout_ref[...], preferred_element_type=jnp.float32)
    h2 = _rms(x1, nfw_ref[...])
    for si in range(n_str):
        x1_ref[si, 0] = x1[si * ts:(si + 1) * ts]
        _pack_rows(h2[si * ts:(si + 1) * ts], h2p_ref.at[si, 0], ts)


def _mix(x, nmw, win_b, cos2, sin2, dmat, qdec, kdec, rnw, wpool_b, pscale, wout_b, nfw, tile_decay):
    bsz, seq, d_model = x.shape
    ts = SEQ_TILE
    n_tok = bsz * seq
    rw = rnw.shape[1]
    hd = rw // RET_HEADS
    pw = pscale.shape[1]
    pgw = pw // len(POOL_WINDOWS)
    nj = seq // ts
    rows_per_tok = d_model // PACK // LANES

    def const(shape):
        return pl.BlockSpec(shape, lambda b, j: (0,) * len(shape))

    n_str = MIX_STREAMS if bsz % MIX_STREAMS == 0 else 1
    per = bsz // n_str
    in_specs = [
        pl.BlockSpec((n_str, 1, ts, d_model), lambda q, j: (0, q, j, 0)),
        const(nmw.shape), const(win_b.shape),
        pl.BlockSpec((ts, hd), lambda q, j: (j, 0)),
        pl.BlockSpec((ts, hd), lambda q, j: (j, 0)),
        const(dmat.shape), const(qdec.shape), const(kdec.shape),
        const(rnw.shape), const(wpool_b.shape), const(pscale.shape), const(wout_b.shape),
        const(nfw.shape),
    ]
    out_shape = [
        jax.ShapeDtypeStruct((n_str, per, seq, d_model), jnp.float32),
        jax.ShapeDtypeStruct((n_str, per, seq * rows_per_tok, LANES), jnp.uint32),
    ]
    out_specs = [
        pl.BlockSpec((n_str, 1, ts, d_model), lambda q, j: (0, q, j, 0)),
        pl.BlockSpec((n_str, 1, ts * rows_per_tok, LANES), lambda q, j: (0, q, j, 0)),
    ]
    scratch = [
        pltpu.VMEM((n_str, RET_HEADS, hd, hd), jnp.float32),
        pltpu.VMEM((n_str, POOL_HALO + ts, pw), jnp.float32),
        pltpu.VMEM((n_str, POOL_HALO + ts, pw - pgw), jnp.float32),
        pltpu.VMEM((n_str, POOL_HALO + ts, pw - 2 * pgw), jnp.float32),
        pltpu.VMEM((n_str, POOL_HALO + ts, pw - 3 * pgw), jnp.float32),
    ]
    x1, h2p = pl.pallas_call(
        functools.partial(_mix_kernel, tile_decay=tile_decay),
        grid=(per, nj),
        in_specs=in_specs,
        out_specs=out_specs,
        out_shape=out_shape,
        scratch_shapes=scratch,
        compiler_params=pltpu.CompilerParams(
            dimension_semantics=("arbitrary", "arbitrary"),
            vmem_limit_bytes=VMEM_LIMIT_BYTES),
        name="mix",
    )(x.reshape(n_str, per, seq, d_model), nmw, win_b, cos2, sin2, dmat, qdec, kdec, rnw, wpool_b,
      pscale, wout_b, nfw)
    return x1.reshape(bsz, seq, d_model), h2p.reshape(n_tok * rows_per_tok, LANES)


def _route_kernel(h2p_ref, wrt_ref, brt_ref, utri_ref,
                  eid_ref, gate_ref, rank_ref, before_ref, cnt_ref, carry_ref):
    i = pl.program_id(0)
    n_exp, d_model = wrt_ref.shape
    rpt = d_model // PACK // LANES
    tr = h2p_ref.shape[0] // rpt
    sub = utri_ref.shape[0]

    @pl.when(i == 0)
    def _():
        carry_ref[...] = jnp.zeros_like(carry_ref)

    h2b = _unpack_rows(h2p_ref, tr, rpt)
    logits = lax.dot_general(wrt_ref[...], h2b, (((1,), (1,)), ((), ())),
                             preferred_element_type=jnp.float32) + brt_ref[...]
    eiota = lax.broadcasted_iota(jnp.int32, (n_exp, tr), 0)
    work = logits
    vals, ids, sels = [], [], []
    for _ in range(TOP_K):
        m = jnp.max(work, axis=0, keepdims=True)
        idx = jnp.min(jnp.where(work == m, eiota, n_exp), axis=0, keepdims=True)
        sel = eiota == idx
        vals.append(m)
        ids.append(idx)
        sels.append(sel)
        work = jnp.where(sel, -jnp.inf, work)
    ex = [jnp.exp(vk - vals[0]) for vk in vals]
    den = ex[0] + ex[1] + ex[2] + ex[3]
    gate_ref[...] = jnp.concatenate([e / den for e in ex], axis=0)
    eid_ref[...] = jnp.concatenate(ids, axis=0)

    member = jnp.zeros((n_exp, tr), jnp.float32)
    for sel in sels:
        member = member + sel.astype(jnp.float32)
    carry = carry_ref[...]
    befores = []
    for c in range(tr // sub):
        before_ref[c] = carry
        mc = member[:, c * sub:(c + 1) * sub]
        bc = jnp.dot(mc.astype(jnp.bfloat16), utri_ref[...], preferred_element_type=jnp.float32)
        befores.append(bc + carry[:, 0:1])
        carry = carry + jnp.sum(mc, axis=1, keepdims=True)
    before = jnp.concatenate(befores, axis=1)
    ranks = [jnp.sum(jnp.where(sel, before, 0.0), axis=0, keepdims=True) for sel in sels]
    rank_ref[...] = jnp.concatenate(ranks, axis=0).astype(jnp.int32)
    carry_ref[...] = carry
    cnt_ref[...] = carry


def _route(h2p, wrt_b, brt, utri):
    n_exp, d_model = wrt_b.shape
    rpt = d_model // PACK // LANES
    n_tok = h2p.shape[0] // rpt
    tr = ROUTE_TILE
    sub = utri.shape[0]

    def const(shape):
        return pl.BlockSpec(shape, lambda i: (0,) * len(shape))

    out_shape = [
        jax.ShapeDtypeStruct((TOP_K, n_tok), jnp.int32),
        jax.ShapeDtypeStruct((TOP_K, n_tok), jnp.float32),
        jax.ShapeDtypeStruct((TOP_K, n_tok), jnp.int32),
        jax.ShapeDtypeStruct((n_tok // sub, n_exp, LANES), jnp.float32),
        jax.ShapeDtypeStruct((n_exp, LANES), jnp.float32),
    ]
    out_specs = [
        pl.BlockSpec((TOP_K, tr), lambda i: (0, i)),
        pl.BlockSpec((TOP_K, tr), lambda i: (0, i)),
        pl.BlockSpec((TOP_K, tr), lambda i: (0, i)),
        pl.BlockSpec((tr // sub, n_exp, LANES), lambda i: (i, 0, 0)),
        pl.BlockSpec((n_exp, LANES), lambda i: (0, 0)),
    ]
    return pl.pallas_call(
        _route_kernel,
        grid=(n_tok // tr,),
        in_specs=[pl.BlockSpec((tr * rpt, LANES), lambda i: (i, 0)),
                  const(wrt_b.shape), const(brt.shape), const(utri.shape)],
        out_specs=out_specs,
        out_shape=out_shape,
        scratch_shapes=[pltpu.VMEM((n_exp, LANES), jnp.float32)],
        compiler_params=pltpu.CompilerParams(
            dimension_semantics=("arbitrary",),
            vmem_limit_bytes=VMEM_LIMIT_BYTES),
        name="route",
    )(h2p, wrt_b, brt, utri)


def _for_set_bits(n, max_bit, fn):
    def piece(bit):
        size = 1 << bit
        offset = (n >> (bit + 1)) << (bit + 1)

        @pl.when((n & size) != 0)
        def _():
            fn(offset, size)

    min_bit = STRIP_ALIGN.bit_length() - 1
    rare_bit = min(max_bit + 1, RARE_STRIP_BIT)

    @pl.when(n >= (1 << rare_bit))
    def _():
        for bit in range(max_bit, rare_bit - 1, -1):
            piece(bit)
    for bit in range(rare_bit - 1, min_bit - 1, -1):
        piece(bit)


def _aligned_len(n):
    return n + (n & (STRIP_ALIGN - 1))


def _for_strips(nbe_ref, lb_ref, hrow_ref, blk, n_exp, max_bit, fn):
    def body(i, carry):
        for u in range(STRIP_UNROLL):
            k = blk * n_exp + i * STRIP_UNROLL + u
            n = _aligned_len(nbe_ref[k])
            lb = lb_ref[k]
            hrow = hrow_ref[k]
            _for_set_bits(n, max_bit, lambda offset, size: fn(lb + offset, hrow + offset, size))
        return carry
    assert n_exp % STRIP_UNROLL == 0
    lax.fori_loop(0, n_exp // STRIP_UNROLL, body, 0)


def _dispatch_kernel(lidx_ref, nbe_ref, lb_ref, hrow_ref, extra_ref, padrow_ref, padlen_ref, nused_ref,
                     h2p_ref, xs_hbm, stage_ref, zero_ref, sem, zsem, *, nblk):
    b = pl.program_id(0)
    rpt = zero_ref.shape[0] // ROW_GRANULE
    tb = h2p_ref.shape[0] // rpt
    n_exp = padrow_ref.shape[0]
    n_tok = lidx_ref.shape[0] // TOP_K
    slot_rows = stage_ref.shape[0] // 2
    max_bit = tb.bit_length() - 1
    slot = b % 2

    def strip_copy(sl, staging_row, sorted_row, size):
        src = pl.multiple_of(sl * slot_rows + staging_row * rpt, SUBLANES)
        dst = pl.multiple_of(sorted_row * rpt, SUBLANES)
        return pltpu.make_async_copy(stage_ref.at[pl.ds(src, size * rpt), :],
                                     xs_hbm.at[pl.ds(dst, size * rpt), :], sem.at[sl])

    def start_strips(blk, sl):
        _for_strips(nbe_ref, lb_ref, hrow_ref, blk, n_exp, max_bit,
                    lambda srow, hrow, size: strip_copy(sl, srow, hrow, size).start())

    def wait_strips(blk, sl):
        def sized_copy(offset, size):
            return pltpu.make_async_copy(stage_ref.at[pl.ds(sl * slot_rows, size * rpt), :],
                                         xs_hbm.at[pl.ds(0, size * rpt), :], sem.at[sl])
        sized_copy(0, tb * TOP_K).wait()
        _for_set_bits(extra_ref[blk], n_exp.bit_length() - 1, lambda o, s: sized_copy(o, s).wait())

    @pl.when(b >= 2)
    def _():
        wait_strips(b - 2, slot)

    def scatter_block(static_slot):
        def scatter_group(c, carry):
            tok0 = c * GATHER_UNROLL
            idx0 = b * tb + tok0
            for u in range(GATHER_UNROLL):
                row = h2p_ref[pl.ds(pl.multiple_of((tok0 + u) * rpt, rpt), rpt), :]
                for k in range(TOP_K):
                    dst = pl.multiple_of(lidx_ref[idx0 + (k * n_tok + u)], rpt)
                    stage_ref[pl.ds(static_slot * slot_rows + dst, rpt), :] = row
            return carry
        lax.fori_loop(0, tb // GATHER_UNROLL, scatter_group, 0)

    for static_slot in range(2):
        pl.when(slot == static_slot)(functools.partial(scatter_block, static_slot))

    def zero_tail(e, carry):
        n = nbe_ref[b * n_exp + e]

        @pl.when((n & (STRIP_ALIGN - 1)) != 0)
        def _():
            dst = pl.multiple_of(slot * slot_rows + (lb_ref[b * n_exp + e] + n) * rpt, rpt)
            stage_ref[pl.ds(dst, rpt), :] = jnp.zeros((rpt, LANES), jnp.uint32)
        return carry
    lax.fori_loop(0, n_exp, zero_tail, 0)

    start_strips(b, slot)

    @pl.when(b == nblk - 1)
    def _():
        if nblk >= 2:
            wait_strips(b - 1, 1 - slot)
        wait_strips(b, slot)
        zero_ref[...] = jnp.zeros_like(zero_ref)
        pad_bit = ROW_GRANULE.bit_length() - 2

        def pad_copy(e, offset, size):
            dst = pl.multiple_of((padrow_ref[e] + offset) * rpt, SUBLANES)
            return pltpu.make_async_copy(zero_ref.at[pl.ds(0, size * rpt), :],
                                         xs_hbm.at[pl.ds(dst, size * rpt), :], zsem)

        def pad_start(e, carry):
            _for_set_bits(padlen_ref[e], pad_bit, lambda o, s: pad_copy(e, o, s).start())
            return carry

        def pad_wait(e, carry):
            _for_set_bits(padlen_ref[e], pad_bit, lambda o, s: pad_copy(e, o, s).wait())
            return carry
        lax.fori_loop(0, n_exp, pad_start, 0)

        def tail_copy(i):
            dst = pl.multiple_of(i * (ROW_GRANULE * rpt), SUBLANES)
            return pltpu.make_async_copy(zero_ref, xs_hbm.at[pl.ds(dst, ROW_GRANULE * rpt), :], zsem)

        def tail_start(i, carry):
            tail_copy(i).start()
            return carry

        def tail_wait(i, carry):
            tail_copy(i).wait()
            return carry
        n_gran = xs_hbm.shape[0] // (ROW_GRANULE * rpt)
        lax.fori_loop(nused_ref[0], n_gran, tail_start, 0)
        lax.fori_loop(0, n_exp, pad_wait, 0)
        lax.fori_loop(nused_ref[0], n_gran, tail_wait, 0)


def _dispatch(lidx, nbe, lb, hrow, extra, padrow, padlen, n_used, h2p, n_rows, d_model):
    tb = TOK_BLOCK
    rpt = d_model // PACK // LANES
    n_tok = h2p.shape[0] // rpt
    n_exp = padrow.shape[0]
    slot_rows = (tb * TOP_K + n_exp * (STRIP_ALIGN - 1)) * rpt
    grid_spec = pltpu.PrefetchScalarGridSpec(
        num_scalar_prefetch=8,
        grid=(n_tok // tb,),
        in_specs=[pl.BlockSpec((tb * rpt, LANES), lambda i, *_: (i, 0))],
        out_specs=pl.BlockSpec(memory_space=pl.ANY),
        scratch_shapes=[pltpu.VMEM((2 * slot_rows, LANES), jnp.uint32),
                        pltpu.VMEM((ROW_GRANULE * rpt, LANES), jnp.uint32),
                        pltpu.SemaphoreType.DMA((2,)),
                        pltpu.SemaphoreType.DMA],
    )
    return pl.pallas_call(
        functools.partial(_dispatch_kernel, nblk=n_tok // tb),
        grid_spec=grid_spec,
        out_shape=jax.ShapeDtypeStruct((n_rows * rpt, LANES), jnp.uint32),
        compiler_params=pltpu.CompilerParams(
            dimension_semantics=("arbitrary",),
            vmem_limit_bytes=VMEM_LIMIT_BYTES),
        name="dispatch",
    )(lidx, nbe, lb, hrow, extra, padrow, padlen, n_used, h2p)


def _experts_kernel(tstart_ref, tcount_ref, wgu_hbm, bgu_ref, wd_hbm, bd_ref, xs_hbm,
                    y_hbm, wgu_f, wd_f, wgu_b, wd_b, xbuf, ybuf, pend_ref, wsem, xsem, ysem, *, n_exp):
    e = pl.program_id(0)
    d_ff = wd_f.shape[0]
    d_model = wd_f.shape[1]
    rpt = d_model // PACK // LANES
    gran_rows = ROW_GRANULE * rpt
    gpt = ROW_TILE // ROW_GRANULE
    n_gran = y_hbm.shape[0] // gran_rows
    g0 = tstart_ref[e]
    ng = tcount_ref[e]
    nt = ng // gpt
    tail = ng - nt * gpt

    def hbm_rows(ref, gran, count):
        return ref.at[pl.ds(pl.multiple_of(gran * gran_rows, SUBLANES), count * gran_rows), :]

    def x_copy(gran, sl, count=gpt):
        return pltpu.make_async_copy(hbm_rows(xs_hbm, gran, count),
                                     xbuf.at[sl, pl.ds(0, count * gran_rows), :], xsem.at[sl])

    def y_copy(gran, sl, count=gpt):
        return pltpu.make_async_copy(ybuf.at[sl, pl.ds(0, count * gran_rows), :],
                                     hbm_rows(y_hbm, gran, count), ysem.at[sl])

    def mlp(sl, tm):
        x = _unpack_rows(xbuf.at[sl], tm, rpt)
        gu = jnp.dot(x, wgu_b[...], preferred_element_type=jnp.float32) + bgu_ref[0]
        x_glu = jnp.minimum(gu[:, :d_ff], SWIGLU_LIMIT)
        x_lin = jnp.clip(gu[:, d_ff:], -SWIGLU_LIMIT, SWIGLU_LIMIT)
        act = x_glu * (1.0 / (1.0 + jnp.exp(-SWIGLU_ALPHA * x_glu))) * (x_lin + 1.0)
        y = jnp.dot(act.astype(jnp.bfloat16), wd_b[...], preferred_element_type=jnp.float32) + bd_ref[0]
        _pack_rows(y, ybuf.at[sl], tm)

    def weight_copies(ex):
        return (pltpu.make_async_copy(wgu_hbm.at[ex], wgu_f, wsem.at[0]),
                pltpu.make_async_copy(wd_hbm.at[ex], wd_f, wsem.at[1]))

    tail_slot = 2
    tail_gran = g0 + nt * gpt

    def wait_pending_tail_y():
        for count in range(1, gpt):
            @pl.when(pend_ref[0] == count)
            def _():
                y_copy(0, tail_slot, count).wait()

    @pl.when(e == 0)
    def _():
        pend_ref[0] = 0

        @pl.when(nt > 0)
        def _():
            x_copy(g0, 0).start()

    for count in range(1, gpt):
        @pl.when(tail == count)
        def _():
            x_copy(tail_gran, tail_slot, count).start()

    @pl.when(e == 0)
    def _():
        for c in weight_copies(0):
            c.start(priority=1)
    for c in weight_copies(e):
        c.wait()
    wgu_b[...] = wgu_f[...].astype(jnp.bfloat16)
    wd_b[...] = wd_f[...].astype(jnp.bfloat16)

    @pl.when(e + 1 < n_exp)
    def _():
        for c in weight_copies(e + 1):
            c.start(priority=1)

    @pl.when(ng > 0)
    def _():
        def tile_step(j, carry):
            sl = j % 2
            gran = g0 + j * gpt

            @pl.when(j + 1 < nt)
            def _():
                x_copy(gran + gpt, 1 - sl).start()
            x_copy(gran, sl).wait()

            @pl.when(j >= 2)
            def _():
                y_copy(gran - 2 * gpt, sl).wait()
            mlp(sl, ROW_TILE)
            y_copy(gran, sl).start()
            return carry
        lax.fori_loop(0, nt, tile_step, 0)

        @pl.when(nt >= 2)
        def _():
            y_copy(g0 + (nt - 2) * gpt, nt % 2).wait()

        @pl.when(nt >= 1)
        def _():
            y_copy(g0 + (nt - 1) * gpt, (nt - 1) % 2).wait()

    @pl.when(e + 1 < n_exp)
    def _():
        nxt = jnp.minimum(e + 1, n_exp - 1)

        @pl.when(tcount_ref[nxt] >= gpt)
        def _():
            x_copy(tstart_ref[nxt], 0).start()

    for count in range(1, gpt):
        @pl.when(tail == count)
        def _():
            x_copy(tail_gran, tail_slot, count).wait()
            wait_pending_tail_y()
            mlp(tail_slot, count * ROW_GRANULE)
            y_copy(tail_gran, tail_slot, count).start()
            pend_ref[0] = count

    @pl.when(e == n_exp - 1)
    def _():
        wait_pending_tail_y()
        ybuf[0, 0:gran_rows, :] = jnp.zeros((gran_rows, LANES), ybuf.dtype)

        def tail_start(i, carry):
            y_copy(i, 0, 1).start()
            return carry

        def tail_wait(i, carry):
            y_copy(i, 0, 1).wait()
            return carry
        lax.fori_loop(g0 + ng, n_gran, tail_start, 0)
        lax.fori_loop(g0 + ng, n_gran, tail_wait, 0)


def _experts(tile_start, tile_count, xs, w_gu, b_gu, w_down, b_down):
    tm = ROW_TILE
    n_exp, d_model, d_gu = w_gu.shape
    d_ff = w_down.shape[1]
    rpt = d_model // PACK // LANES
    grid_spec = pltpu.PrefetchScalarGridSpec(
        num_scalar_prefetch=2,
        grid=(n_exp,),
        in_specs=[
            pl.BlockSpec(memory_space=pl.ANY),
            pl.BlockSpec((1, 1, d_gu), lambda e, *_: (e, 0, 0)),
            pl.BlockSpec(memory_space=pl.ANY),
            pl.BlockSpec((1, 1, d_model), lambda e, *_: (e, 0, 0)),
            pl.BlockSpec(memory_space=pl.ANY),
        ],
        out_specs=pl.BlockSpec(memory_space=pl.ANY),
        scratch_shapes=[pltpu.VMEM((d_model, d_gu), jnp.float32),
                        pltpu.VMEM((d_ff, d_model), jnp.float32),
                        pltpu.VMEM((d_model, d_gu), jnp.bfloat16),
                        pltpu.VMEM((d_ff, d_model), jnp.bfloat16),
                        pltpu.VMEM((3, tm * rpt, LANES), jnp.uint32),
                        pltpu.VMEM((3, tm * rpt, LANES), jnp.uint32),
                        pltpu.SMEM((1,), jnp.int32),
                        pltpu.SemaphoreType.DMA((2,)),
                        pltpu.SemaphoreType.DMA((3,)),
                        pltpu.SemaphoreType.DMA((3,))],
    )
    return pl.pallas_call(
        functools.partial(_experts_kernel, n_exp=n_exp),
        grid_spec=grid_spec,
        out_shape=jax.ShapeDtypeStruct(xs.shape, jnp.uint32),
        compiler_params=pltpu.CompilerParams(
            dimension_semantics=("arbitrary",),
            vmem_limit_bytes=VMEM_LIMIT_BYTES),
        name="experts",
    )(tile_start, tile_count, w_gu, b_gu.reshape(n_exp, 1, d_gu), w_down,
      b_down.reshape(n_exp, 1, d_model), xs)


def _combine_kernel(lidx_ref, gate_ref, nbe_ref, lb_ref, hrow_ref, extra_ref,
                    x1_ref, nw_ref, y_hbm, out_ref, stage_ref, alo_ref, ahi_ref, sem, *, nblk):
    b = pl.program_id(0)
    tb = x1_ref.shape[0]
    rpt = alo_ref.shape[0] // tb
    n_exp = nbe_ref.shape[0] // nblk
    n_tok = lidx_ref.shape[0] // TOP_K
    slot_rows = stage_ref.shape[0] // 2
    max_bit = tb.bit_length() - 1
    slot = b % 2

    def strip_copy(sl, staging_row, sorted_row, size):
        src = pl.multiple_of(sorted_row * rpt, SUBLANES)
        dst = pl.multiple_of(sl * slot_rows + staging_row * rpt, SUBLANES)
        return pltpu.make_async_copy(y_hbm.at[pl.ds(src, size * rpt), :],
                                     stage_ref.at[pl.ds(dst, size * rpt), :], sem.at[sl])

    def start_strips(blk, sl):
        _for_strips(nbe_ref, lb_ref, hrow_ref, blk, n_exp, max_bit,
                    lambda srow, hrow, size: strip_copy(sl, srow, hrow, size).start())

    @pl.when(b == 0)
    def _():
        start_strips(0, 0)

    @pl.when(b + 1 < nblk)
    def _():
        start_strips(b + 1, 1 - slot)

    def sized_copy(offset, size):
        return pltpu.make_async_copy(y_hbm.at[pl.ds(0, size * rpt), :],
                                     stage_ref.at[pl.ds(slot * slot_rows, size * rpt), :], sem.at[slot])
    sized_copy(0, tb * TOP_K).wait()
    _for_set_bits(extra_ref[b], n_exp.bit_length() - 1, lambda o, s: sized_copy(o, s).wait())

    def gather_block(static_slot):
        def gather_group(c, carry):
            tok0 = c * GATHER_UNROLL
            idx0 = b * tb + tok0
            for u in range(GATHER_UNROLL):
                lo = hi = None
                for k in range(TOP_K):
                    src = pl.multiple_of(lidx_ref[idx0 + (k * n_tok + u)], rpt)
                    g = gate_ref[idx0 + (k * n_tok + u)]
                    l, h = _unpack_words(stage_ref[pl.ds(static_slot * slot_rows + src, rpt), :])
                    lo = g * l if lo is None else lo + g * l
                    hi = g * h if hi is None else hi + g * h
                dst = pl.multiple_of((tok0 + u) * rpt, rpt)
                alo_ref[pl.ds(dst, rpt), :] = lo
                ahi_ref[pl.ds(dst, rpt), :] = hi
            return carry
        lax.fori_loop(0, tb // GATHER_UNROLL, gather_group, 0)

    for static_slot in range(2):
        pl.when(slot == static_slot)(functools.partial(gather_block, static_slot))

    parts = [alo_ref[pl.ds(si, tb, stride=rpt), :] for si in range(rpt)]
    parts += [ahi_ref[pl.ds(si, tb, stride=rpt), :] for si in range(rpt)]
    x2 = x1_ref[...] + jnp.concatenate(parts, axis=-1)
    out_ref[...] = _rms(x2, nw_ref[...])


def _combine(lidx, gates, nbe, lb, hrow, extra, x1, norm_w, y_rows):
    tb = TOK_BLOCK
    n_tok, d_model = x1.shape
    rpt = d_model // PACK // LANES
    n_exp = nbe.shape[0] // (n_tok // tb)
    slot_rows = (tb * TOP_K + n_exp * (STRIP_ALIGN - 1)) * rpt
    grid_spec = pltpu.PrefetchScalarGridSpec(
        num_scalar_prefetch=6,
        grid=(n_tok // tb,),
        in_specs=[
            pl.BlockSpec((tb, d_model), lambda i, *_: (i, 0)),
            pl.BlockSpec((1, d_model), lambda i, *_: (0, 0)),
            pl.BlockSpec(memory_space=pl.ANY),
        ],
        out_specs=pl.BlockSpec((tb, d_model), lambda i, *_: (i, 0)),
        scratch_shapes=[pltpu.VMEM((2 * slot_rows, LANES), jnp.uint32),
                        pltpu.VMEM((tb * rpt, LANES), jnp.float32),
                        pltpu.VMEM((tb * rpt, LANES), jnp.float32),
                        pltpu.SemaphoreType.DMA((2,))],
    )
    return pl.pallas_call(
        functools.partial(_combine_kernel, nblk=n_tok // tb),
        grid_spec=grid_spec,
        out_shape=jax.ShapeDtypeStruct((n_tok, d_model), jnp.float32),
        compiler_params=pltpu.CompilerParams(
            dimension_semantics=("arbitrary",),
            vmem_limit_bytes=VMEM_LIMIT_BYTES),
        name="combine",
    )(lidx, gates, nbe, lb, hrow, extra, x1, norm_w, y_rows)


def _retention_tables(seq, hd, ts):
    half = hd // 2
    inv = ROPE_BASE ** (-jnp.arange(half, dtype=jnp.float32) / half)
    ang = jnp.arange(seq, dtype=jnp.float32)[:, None] * inv[None, :]
    cos, sin = jnp.cos(ang), jnp.sin(ang)
    cos2 = jnp.concatenate([cos, cos], axis=-1)
    sin2 = jnp.concatenate([-sin, sin], axis=-1)
    log_g = jnp.log1p(-jnp.exp2(-5.0 - jnp.arange(RET_HEADS, dtype=jnp.float32)))
    pos = jnp.arange(ts, dtype=jnp.float32)
    dist = jnp.abs(pos[:, None] - pos[None, :])
    chunk = jnp.arange(ts, dtype=jnp.int32) // CHUNK
    visible = chunk[None, :] <= chunk[:, None]
    dmat = jnp.where(visible[None], jnp.exp(log_g[:, None, None] * dist[None]), 0.0)
    qdec = jnp.exp(log_g[:, None] * (pos + 1.0)[None, :])
    kdec = jnp.exp(log_g[:, None] * (ts - 1.0 - pos)[None, :])
    qdec = jnp.broadcast_to(qdec[:, :, None], (RET_HEADS, ts, hd))
    kdec = jnp.broadcast_to(kdec[:, :, None], (RET_HEADS, ts, hd))
    tile_decay = tuple(math.exp(math.log1p(-2.0 ** (-5.0 - h)) * ts) for h in range(RET_HEADS))
    return cos2, sin2, dmat, qdec, kdec, tile_decay


def _routing_tables(eid, rank, before, counts, n_tok, n_exp, rows_per_tok):
    tm, tb, ts = ROW_GRANULE, TOK_BLOCK, ROUTE_CHUNK
    cb = before[::tb // ts]
    nbe = jnp.concatenate([cb[1:], counts[None]], axis=0) - cb
    nal = nbe + (nbe & (STRIP_ALIGN - 1))
    region = jnp.sum(nal, axis=0)
    tiles_per = (region + tm - 1) // tm
    tile_end = jnp.cumsum(tiles_per)
    offs = (tile_end - tiles_per) * tm
    n_tiles = (n_tok * TOP_K + (n_tok // tb) * n_exp * (STRIP_ALIGN - 1)) // tm + n_exp
    hrow = offs[None, :] + jnp.cumsum(nal, axis=0) - nal
    lb = jnp.cumsum(nal, axis=1) - nal
    delta = jnp.repeat(lb - cb, tb, axis=0)
    onehot = eid[:, :, None] == jnp.arange(n_exp, dtype=jnp.int32)
    lidx = rank + jnp.sum(jnp.where(onehot, delta[None], 0), axis=-1)
    lidx = lidx * rows_per_tok
    padrow = offs + region
    padlen = tiles_per * tm - region
    as_i32 = lambda a: a.reshape(-1).astype(jnp.int32)
    return (n_tiles, as_i32(tile_end - tiles_per), as_i32(tiles_per), as_i32(tile_end[-1:]),
            as_i32(lidx), as_i32(nbe), as_i32(lb), as_i32(hrow),
            as_i32(jnp.sum(nal, axis=1) - tb * TOP_K), as_i32(padrow), as_i32(padlen))


def _layer(x, norm_mix_w, w_in, ret_norm_w, w_pool, pool_scale, w_out, norm_ffn_w,
           w_router, b_router, w_gu, b_gu, w_down, b_down, final_w):
    bsz, seq, d_model = x.shape
    n_tok = bsz * seq
    n_exp = w_router.shape[1]
    rw = ret_norm_w.shape[0]
    hd = rw // RET_HEADS
    ts = SEQ_TILE
    bf = jnp.bfloat16
    rpt = d_model // PACK // LANES
    assert d_model % (PACK * LANES) == 0 and (STRIP_ALIGN * rpt) % SUBLANES == 0
    assert seq % ts == 0 and ts % CHUNK == 0 and n_tok % TOK_BLOCK == 0
    assert n_tok % ROUTE_TILE == 0 and ROUTE_TILE % ROUTE_CHUNK == 0 and TOK_BLOCK % ROUTE_CHUNK == 0
    assert TOK_BLOCK & (TOK_BLOCK - 1) == 0 and ROW_GRANULE & (ROW_GRANULE - 1) == 0
    assert ROW_TILE % ROW_GRANULE == 0

    cos2, sin2, dmat, qdec, kdec, tile_decay = _retention_tables(seq, hd, ts)
    tri = jnp.arange(ROUTE_CHUNK, dtype=jnp.int32)
    utri = (tri[:, None] < tri[None, :]).astype(bf)

    x1, h2p = _mix(
        x, norm_mix_w[None], w_in.astype(bf), cos2, sin2, dmat, qdec, kdec, ret_norm_w[None],
        w_pool.astype(bf), pool_scale[None], w_out.astype(bf), norm_ffn_w[None], tile_decay)
    eid, gate, rank, before, cnt = _route(h2p, w_router.T.astype(bf), b_router[:, None], utri)

    (n_tiles, tile_start, tile_count, n_used, lidx, nbe, lb, hrow, extra, padrow, padlen) = _routing_tables(
        eid, rank, before[:, :, 0].astype(jnp.int32), cnt[:, 0].astype(jnp.int32), n_tok, n_exp, rpt)

    xs = _dispatch(lidx, nbe, lb, hrow, extra, padrow, padlen, n_used, h2p, n_tiles * ROW_GRANULE, d_model)
    y_rows = _experts(tile_start, tile_count, xs, w_gu, b_gu, w_down, b_down)
    out = _combine(lidx, gate.reshape(-1), nbe, lb, hrow, extra, x1.reshape(n_tok, d_model), final_w[None], y_rows)
    return out.reshape(bsz, seq, d_model)


def kernel(x, norm_mix_w, w_in, ret_norm_w, w_pool, pool_scale, w_out, norm_ffn_w, w_router, b_router,
           w_gu, b_gu, w_down, b_down, norm_final_w):
    depth = norm_mix_w.shape[0]
    for l in range(depth):
        last = l == depth - 1
        assert last, "stacked layers need an un-normalised combine output"
        x = _layer(x, norm_mix_w[l], w_in[l], ret_norm_w[l], w_pool[l], pool_scale[l], w_out[l],
                   norm_ffn_w[l], w_router[l], b_router[l], w_gu[l], b_gu[l], w_down[l], b_down[l],
                   norm_final_w)
    return x
```

```python
import functools
import math

import jax
import jax.numpy as jnp
import numpy as np
from jax import lax
from jax.experimental import pallas as pl
from jax.experimental.pallas import tpu as pltpu

CHUNK = 64
RET_HEADS = 4
POOL_WINDOWS = (2, 4, 8, 16)
ROPE_BASE = 10000.0
TOP_K = 4
SWIGLU_LIMIT = 7.0
SWIGLU_ALPHA = 1.702
EPS = 1e-6

LANES = 128
SUBLANES = 8
VMEM_LIMIT_BYTES = 56 * 1024 * 1024

SEQ_TILE = 256
MIX_STREAMS = 4
ROUTE_TILE = 2048
ROUTE_CHUNK = 256
ROW_GRANULE = 128
ROW_TILE = 512
TOK_BLOCK = 1024
POOL_HALO = 16
PACK = 2
STRIP_ALIGN = 2
GATHER_UNROLL = 16
STRIP_UNROLL = 4
RARE_STRIP_BIT = 8


def _rms(x, w):
    return x * lax.rsqrt(jnp.mean(x * x, axis=-1, keepdims=True) + EPS) * w


def _bf16_bits_hi(x):
    return lax.bitcast_convert_type(x.astype(jnp.bfloat16).astype(jnp.float32), jnp.uint32)


def _pack_rows(x, out_ref, n_rows):
    half = x.shape[1] // PACK
    packed = (_bf16_bits_hi(x[:, :half]) >> 16) | _bf16_bits_hi(x[:, half:])
    rpt = half // LANES
    for si in range(rpt):
        out_ref[pl.ds(si, n_rows, stride=rpt), :] = packed[:, si * LANES:(si + 1) * LANES]


def _unpack_words(u):
    lo = lax.bitcast_convert_type(u << 16, jnp.float32)
    hi = lax.bitcast_convert_type(u & jnp.uint32(0xFFFF0000), jnp.float32)
    return lo, hi


def _unpack_rows(g_ref, n_rows, rpt):
    lo, hi = [], []
    for si in range(rpt):
        l, h = _unpack_words(g_ref[pl.ds(si, n_rows, stride=rpt), :])
        lo.append(l.astype(jnp.bfloat16))
        hi.append(h.astype(jnp.bfloat16))
    return jnp.concatenate(lo + hi, axis=-1)


def _mix_kernel(x_ref, nmw_ref, win_ref, cos_ref, sin_ref, dmat_ref, qdec_ref, kdec_ref,
                rnw_ref, wpool_ref, pscale_ref, wout_ref, nfw_ref,
                x1_ref, h2p_ref,
                state_ref, p1_ref, p2_ref, p4_ref, p8_ref,
                *, tile_decay):
    j = pl.program_id(1)
    n_str = x_ref.shape[0]
    ts = x_ref.shape[2]
    n_heads = RET_HEADS
    rw = rnw_ref.shape[1]
    hd = rw // n_heads
    pgw = wpool_ref.shape[1]
    halo = POOL_HALO
    pool_refs = (p1_ref, p2_ref, p4_ref, p8_ref)

    @pl.when(j == 0)
    def _():
        state_ref[...] = jnp.zeros_like(state_ref)
        for r in pool_refs:
            r[:, 0:halo, :] = jnp.zeros((n_str, halo, r.shape[2]), jnp.float32)

    @pl.when(j > 0)
    def _():
        for r in pool_refs:
            r[:, 0:halo, :] = r[:, ts:ts + halo, :]

    x = jnp.concatenate([x_ref[si, 0] for si in range(n_str)], axis=0)
    h = _rms(x, nmw_ref[...]).astype(jnp.bfloat16)
    z_all = jnp.dot(h, win_ref[...], preferred_element_type=jnp.float32)

    cos = cos_ref[...]
    sin = sin_ref[...]
    k_scale = hd ** -0.5
    tpos = (j * ts + lax.broadcasted_iota(jnp.int32, (ts, pgw), 0) + 1).astype(jnp.float32)

    mix_rows = []
    for si in range(n_str):
        z = z_all[si * ts:(si + 1) * ts]
        parts = []
        for hh in range(n_heads):
            q = z[:, hh * hd:(hh + 1) * hd]
            k = z[:, rw + hh * hd:rw + (hh + 1) * hd]
            v = z[:, 2 * rw + hh * hd:2 * rw + (hh + 1) * hd]
            g = z[:, 3 * rw + hh * hd:3 * rw + (hh + 1) * hd]
            q = q * cos + pltpu.roll(q, hd // 2, 1) * sin
            k = (k * cos + pltpu.roll(k, hd // 2, 1) * sin) * k_scale
            qb = q.astype(jnp.bfloat16)
            kb = k.astype(jnp.bfloat16)
            vb = v.astype(jnp.bfloat16)
            s = lax.dot_general(qb, kb, (((1,), (1,)), ((), ())), preferred_element_type=jnp.float32)
            pmat = (s * dmat_ref[hh]).astype(jnp.bfloat16)
            o = jnp.dot(pmat, vb, preferred_element_type=jnp.float32)
            state = state_ref[si, hh]
            qd = (q * qdec_ref[hh]).astype(jnp.bfloat16)
            o = o + jnp.dot(qd, state.astype(jnp.bfloat16), preferred_element_type=jnp.float32)
            kd = (k * kdec_ref[hh]).astype(jnp.bfloat16)
            u = lax.dot_general(kd, vb, (((0,), (0,)), ((), ())), preferred_element_type=jnp.float32)
            state_ref[si, hh] = state * tile_decay[hh] + u
            o = o * lax.rsqrt(jnp.mean(o * o, axis=-1, keepdims=True) + EPS)
            o = o * rnw_ref[:, hh * hd:(hh + 1) * hd] * (g * (1.0 / (1.0 + jnp.exp(-g))))
            parts.append(o.astype(jnp.bfloat16))

        p = z[:, 4 * rw:]
        p1_ref[si, halo:halo + ts, :] = p
        s2 = p + p1_ref[si, halo - 1:halo - 1 + ts, :]
        p2_ref[si, halo:halo + ts, :] = s2[:, pgw:]
        s4 = s2[:, pgw:] + p2_ref[si, halo - 2:halo - 2 + ts, :]
        p4_ref[si, halo:halo + ts, :] = s4[:, pgw:]
        s8 = s4[:, pgw:] + p4_ref[si, halo - 4:halo - 4 + ts, :]
        p8_ref[si, halo:halo + ts, :] = s8[:, pgw:]
        s16 = s8[:, pgw:] + p8_ref[si, halo - 8:halo - 8 + ts, :]
        sums = (s2[:, :pgw], s4[:, :pgw], s8[:, :pgw], s16)
        for gi, w in enumerate(POOL_WINDOWS):
            cnt = jnp.minimum(tpos, float(w))
            pooled = sums[gi] / cnt - p[:, gi * pgw:(gi + 1) * pgw]
            mixed = jnp.dot(pooled.astype(jnp.bfloat16), wpool_ref[gi], preferred_element_type=jnp.float32)
            mixed = mixed * pscale_ref[:, gi * pgw:(gi + 1) * pgw]
            parts.append(mixed.astype(jnp.bfloat16))
        mix_rows.append(jnp.concatenate(parts, axis=-1))

    mix = jnp.concatenate(mix_rows, axis=0)
    x1 = x + jnp.dot(mix, wout_ref[...], preferred_element_type=jnp.float32)
    h2 = _rms(x1, nfw_ref[...])
    for si in range(n_str):
        x1_ref[si, 0] = x1[si * ts:(si + 1) * ts]
        _pack_rows(h2[si * ts:(si + 1) * ts], h2p_ref.at[si, 0], ts)


def _mix(x, nmw, win_b, cos2, sin2, dmat, qdec, kdec, rnw, wpool_b, pscale, wout_b, nfw, tile_decay):
    bsz, seq, d_model = x.shape
    ts = SEQ_TILE
    n_tok = bsz * seq
    rw = rnw.shape[1]
    hd = rw // RET_HEADS
    pw = pscale.shape[1]
    pgw = pw // len(POOL_WINDOWS)
    nj = seq // ts
    rows_per_tok = d_model // PACK // LANES

    def const(shape):
        return pl.BlockSpec(shape, lambda b, j: (0,) * len(shape))

    n_str = MIX_STREAMS if bsz % MIX_STREAMS == 0 else 1
    per = bsz // n_str
    in_specs = [
        pl.BlockSpec((n_str, 1, ts, d_model), lambda q, j: (0, q, j, 0)),
        const(nmw.shape), const(win_b.shape),
        pl.BlockSpec((ts, hd), lambda q, j: (j, 0)),
        pl.BlockSpec((ts, hd), lambda q, j: (j, 0)),
        const(dmat.shape), const(qdec.shape), const(kdec.shape),
        const(rnw.shape), const(wpool_b.shape), const(pscale.shape), const(wout_b.shape),
        const(nfw.shape),
    ]
    out_shape = [
        jax.ShapeDtypeStruct((n_str, per, seq, d_model), jnp.float32),
        jax.ShapeDtypeStruct((n_str, per, seq * rows_per_tok, LANES), jnp.uint32),
    ]
    out_specs = [
        pl.BlockSpec((n_str, 1, ts, d_model), lambda q, j: (0, q, j, 0)),
        pl.BlockSpec((n_str, 1, ts * rows_per_tok, LANES), lambda q, j: (0, q, j, 0)),
    ]
    scratch = [
        pltpu.VMEM((n_str, RET_HEADS, hd, hd), jnp.float32),
        pltpu.VMEM((n_str, POOL_HALO + ts, pw), jnp.float32),
        pltpu.VMEM((n_str, POOL_HALO + ts, pw - pgw), jnp.float32),
        pltpu.VMEM((n_str, POOL_HALO + ts, pw - 2 * pgw), jnp.float32),
        pltpu.VMEM((n_str, POOL_HALO + ts, pw - 3 * pgw), jnp.float32),
    ]
    x1, h2p = pl.pallas_call(
        functools.partial(_mix_kernel, tile_decay=tile_decay),
        grid=(per, nj),
        in_specs=in_specs,
        out_specs=out_specs,
        out_shape=out_shape,
        scratch_shapes=scratch,
        compiler_params=pltpu.CompilerParams(
            dimension_semantics=("arbitrary", "arbitrary"),
            vmem_limit_bytes=VMEM_LIMIT_BYTES),
        name="mix",
    )(x.reshape(n_str, per, seq, d_model), nmw, win_b, cos2, sin2, dmat, qdec, kdec, rnw, wpool_b,
      pscale, wout_b, nfw)
    return x1.reshape(bsz, seq, d_model), h2p.reshape(n_tok * rows_per_tok, LANES)


def _route_kernel(h2p_ref, wrt_ref, brt_ref, utri_ref,
                  eid_ref, gate_ref, rank_ref, before_ref, cnt_ref, carry_ref):
    i = pl.program_id(0)
    n_exp, d_model = wrt_ref.shape
    rpt = d_model // PACK // LANES
    tr = h2p_ref.shape[0] // rpt
    sub = utri_ref.shape[0]

    @pl.when(i == 0)
    def _():
        carry_ref[...] = jnp.zeros_like(carry_ref)

    h2b = _unpack_rows(h2p_ref, tr, rpt)
    logits = lax.dot_general(wrt_ref[...], h2b, (((1,), (1,)), ((), ())),
                             preferred_element_type=jnp.float32) + brt_ref[...]
    eiota = lax.broadcasted_iota(jnp.int32, (n_exp, tr), 0)
    work = logits
    vals, ids, sels = [], [], []
    for _ in range(TOP_K):
        m = jnp.max(work, axis=0, keepdims=True)
        idx = jnp.min(jnp.where(work == m, eiota, n_exp), axis=0, keepdims=True)
        sel = eiota == idx
        vals.append(m)
        ids.append(idx)
        sels.append(sel)
        work = jnp.where(sel, -jnp.inf, work)
    ex = [jnp.exp(vk - vals[0]) for vk in vals]
    den = ex[0] + ex[1] + ex[2] + ex[3]
    gate_ref[...] = jnp.concatenate([e / den for e in ex], axis=0)
    eid_ref[...] = jnp.concatenate(ids, axis=0)

    member = jnp.zeros((n_exp, tr), jnp.float32)
    for sel in sels:
        member = member + sel.astype(jnp.float32)
    carry = carry_ref[...]
    befores = []
    for c in range(tr // sub):
        before_ref[c] = carry
        mc = member[:, c * sub:(c + 1) * sub]
        bc = jnp.dot(mc.astype(jnp.bfloat16), utri_ref[...], preferred_element_type=jnp.float32)
        befores.append(bc + carry[:, 0:1])
        carry = carry + jnp.sum(mc, axis=1, keepdims=True)
    before = jnp.concatenate(befores, axis=1)
    ranks = [jnp.sum(jnp.where(sel, before, 0.0), axis=0, keepdims=True) for sel in sels]
    rank_ref[...] = jnp.concatenate(ranks, axis=0).astype(jnp.int32)
    carry_ref[...] = carry
    cnt_ref[...] = carry


def _route(h2p, wrt_b, brt, utri):
    n_exp, d_model = wrt_b.shape
    rpt = d_model // PACK // LANES
    n_tok = h2p.shape[0] // rpt
    tr = ROUTE_TILE
    sub = utri.shape[0]

    def const(shape):
        return pl.BlockSpec(shape, lambda i: (0,) * len(shape))

    out_shape = [
        jax.ShapeDtypeStruct((TOP_K, n_tok), jnp.int32),
        jax.ShapeDtypeStruct((TOP_K, n_tok), jnp.float32),
        jax.ShapeDtypeStruct((TOP_K, n_tok), jnp.int32),
        jax.ShapeDtypeStruct((n_tok // sub, n_exp, LANES), jnp.float32),
        jax.ShapeDtypeStruct((n_exp, LANES), jnp.float32),
    ]
    out_specs = [
        pl.BlockSpec((TOP_K, tr), lambda i: (0, i)),
        pl.BlockSpec((TOP_K, tr), lambda i: (0, i)),
        pl.BlockSpec((TOP_K, tr), lambda i: (0, i)),
        pl.BlockSpec((tr // sub, n_exp, LANES), lambda i: (i, 0, 0)),
        pl.BlockSpec((n_exp, LANES), lambda i: (0, 0)),
    ]
    return pl.pallas_call(
        _route_kernel,
        grid=(n_tok // tr,),
        in_specs=[pl.BlockSpec((tr * rpt, LANES), lambda i: (i, 0)),
                  const(wrt_b.shape), const(brt.shape), const(utri.shape)],
        out_specs=out_specs,
        out_shape=out_shape,
        scratch_shapes=[pltpu.VMEM((n_exp, LANES), jnp.float32)],
        compiler_params=pltpu.CompilerParams(
            dimension_semantics=("arbitrary",),
            vmem_limit_bytes=VMEM_LIMIT_BYTES),
        name="route",
    )(h2p, wrt_b, brt, utri)


def _for_set_bits(n, max_bit, fn):
    def piece(bit):
        size = 1 << bit
        offset = (n >> (bit + 1)) << (bit + 1)

        @pl.when((n & size) != 0)
        def _():
            fn(offset, size)

    min_bit = STRIP_ALIGN.bit_length() - 1
    rare_bit = min(max_bit + 1, RARE_STRIP_BIT)

    @pl.when(n >= (1 << rare_bit))
    def _():
        for bit in range(max_bit, rare_bit - 1, -1):
            piece(bit)
    for bit in range(rare_bit - 1, min_bit - 1, -1):
        piece(bit)


def _aligned_len(n):
    return n + (n & (STRIP_ALIGN - 1))


def _for_strips(nbe_ref, lb_ref, hrow_ref, blk, n_exp, max_bit, fn):
    def body(i, carry):
        for u in range(STRIP_UNROLL):
            k = blk * n_exp + i * STRIP_UNROLL + u
            n = _aligned_len(nbe_ref[k])
            lb = lb_ref[k]
            hrow = hrow_ref[k]
            _for_set_bits(n, max_bit, lambda offset, size: fn(lb + offset, hrow + offset, size))
        return carry
    assert n_exp % STRIP_UNROLL == 0
    lax.fori_loop(0, n_exp // STRIP_UNROLL, body, 0)


def _dispatch_kernel(lidx_ref, nbe_ref, lb_ref, hrow_ref, extra_ref, padrow_ref, padlen_ref, nused_ref,
                     h2p_ref, xs_hbm, stage_ref, zero_ref, sem, zsem, *, nblk):
    b = pl.program_id(0)
    rpt = zero_ref.shape[0] // ROW_GRANULE
    tb = h2p_ref.shape[0] // rpt
    n_exp = padrow_ref.shape[0]
    n_tok = lidx_ref.shape[0] // TOP_K
    slot_rows = stage_ref.shape[0] // 2
    max_bit = tb.bit_length() - 1
    slot = b % 2

    def strip_copy(sl, staging_row, sorted_row, size):
        src = pl.multiple_of(sl * slot_rows + staging_row * rpt, SUBLANES)
        dst = pl.multiple_of(sorted_row * rpt, SUBLANES)
        return pltpu.make_async_copy(stage_ref.at[pl.ds(src, size * rpt), :],
                                     xs_hbm.at[pl.ds(dst, size * rpt), :], sem.at[sl])

    def start_strips(blk, sl):
        _for_strips(nbe_ref, lb_ref, hrow_ref, blk, n_exp, max_bit,
                    lambda srow, hrow, size: strip_copy(sl, srow, hrow, size).start())

    def wait_strips(blk, sl):
        def sized_copy(offset, size):
            return pltpu.make_async_copy(stage_ref.at[pl.ds(sl * slot_rows, size * rpt), :],
                                         xs_hbm.at[pl.ds(0, size * rpt), :], sem.at[sl])
        sized_copy(0, tb * TOP_K).wait()
        _for_set_bits(extra_ref[blk], n_exp.bit_length() - 1, lambda o, s: sized_copy(o, s).wait())

    @pl.when(b >= 2)
    def _():
        wait_strips(b - 2, slot)

    def scatter_block(static_slot):
        def scatter_group(c, carry):
            tok0 = c * GATHER_UNROLL
            idx0 = b * tb + tok0
            for u in range(GATHER_UNROLL):
                row = h2p_ref[pl.ds(pl.multiple_of((tok0 + u) * rpt, rpt), rpt), :]
                for k in range(TOP_K):
                    dst = pl.multiple_of(lidx_ref[idx0 + (k * n_tok + u)], rpt)
                    stage_ref[pl.ds(static_slot * slot_rows + dst, rpt), :] = row
            return carry
        lax.fori_loop(0, tb // GATHER_UNROLL, scatter_group, 0)

    for static_slot in range(2):
        pl.when(slot == static_slot)(functools.partial(scatter_block, static_slot))

    def zero_tail(e, carry):
        n = nbe_ref[b * n_exp + e]

        @pl.when((n & (STRIP_ALIGN - 1)) != 0)
        def _():
            dst = pl.multiple_of(slot * slot_rows + (lb_ref[b * n_exp + e] + n) * rpt, rpt)
            stage_ref[pl.ds(dst, rpt), :] = jnp.zeros((rpt, LANES), jnp.uint32)
        return carry
    lax.fori_loop(0, n_exp, zero_tail, 0)

    start_strips(b, slot)

    @pl.when(b == nblk - 1)
    def _():
        if nblk >= 2:
            wait_strips(b - 1, 1 - slot)
        wait_strips(b, slot)
        zero_ref[...] = jnp.zeros_like(zero_ref)
        pad_bit = ROW_GRANULE.bit_length() - 2

        def pad_copy(e, offset, size):
            dst = pl.multiple_of((padrow_ref[e] + offset) * rpt, SUBLANES)
            return pltpu.make_async_copy(zero_ref.at[pl.ds(0, size * rpt), :],
                                         xs_hbm.at[pl.ds(dst, size * rpt), :], zsem)

        def pad_start(e, carry):
            _for_set_bits(padlen_ref[e], pad_bit, lambda o, s: pad_copy(e, o, s).start())
            return carry

        def pad_wait(e, carry):
            _for_set_bits(padlen_ref[e], pad_bit, lambda o, s: pad_copy(e, o, s).wait())
            return carry
        lax.fori_loop(0, n_exp, pad_start, 0)

        def tail_copy(i):
            dst = pl.multiple_of(i * (ROW_GRANULE * rpt), SUBLANES)
            return pltpu.make_async_copy(zero_ref, xs_hbm.at[pl.ds(dst, ROW_GRANULE * rpt), :], zsem)

        def tail_start(i, carry):
            tail_copy(i).start()
            return carry

        def tail_wait(i, carry):
            tail_copy(i).wait()
            return carry
        n_gran = xs_hbm.shape[0] // (ROW_GRANULE * rpt)
        lax.fori_loop(nused_ref[0], n_gran, tail_start, 0)
        lax.fori_loop(0, n_exp, pad_wait, 0)
        lax.fori_loop(nused_ref[0], n_gran, tail_wait, 0)


def _dispatch(lidx, nbe, lb, hrow, extra, padrow, padlen, n_used, h2p, n_rows, d_model):
    tb = TOK_BLOCK
    rpt = d_model // PACK // LANES
    n_tok = h2p.shape[0] // rpt
    n_exp = padrow.shape[0]
    slot_rows = (tb * TOP_K + n_exp * (STRIP_ALIGN - 1)) * rpt
    grid_spec = pltpu.PrefetchScalarGridSpec(
        num_scalar_prefetch=8,
        grid=(n_tok // tb,),
        in_specs=[pl.BlockSpec((tb * rpt, LANES), lambda i, *_: (i, 0))],
        out_specs=pl.BlockSpec(memory_space=pl.ANY),
        scratch_shapes=[pltpu.VMEM((2 * slot_rows, LANES), jnp.uint32),
                        pltpu.VMEM((ROW_GRANULE * rpt, LANES), jnp.uint32),
                        pltpu.SemaphoreType.DMA((2,)),
                        pltpu.SemaphoreType.DMA],
    )
    return pl.pallas_call(
        functools.partial(_dispatch_kernel, nblk=n_tok // tb),
        grid_spec=grid_spec,
        out_shape=jax.ShapeDtypeStruct((n_rows * rpt, LANES), jnp.uint32),
        compiler_params=pltpu.CompilerParams(
            dimension_semantics=("arbitrary",),
            vmem_limit_bytes=VMEM_LIMIT_BYTES),
        name="dispatch",
    )(lidx, nbe, lb, hrow, extra, padrow, padlen, n_used, h2p)


def _experts_kernel(tstart_ref, tcount_ref, wgu_hbm, bgu_ref, wd_hbm, bd_ref, xs_hbm,
                    y_hbm, wgu_f, wd_f, wgu_b, wd_b, xbuf, ybuf, pend_ref, wsem, xsem, ysem,
                    *, n_exp, layer):
    e = pl.program_id(0)
    d_ff = wd_f.shape[0]
    d_model = wd_f.shape[1]
    rpt = d_model // PACK // LANES
    gran_rows = ROW_GRANULE * rpt
    gpt = ROW_TILE // ROW_GRANULE
    n_gran = y_hbm.shape[0] // gran_rows
    g0 = tstart_ref[e]
    ng = tcount_ref[e]
    nt = ng // gpt
    tail = ng - nt * gpt

    def hbm_rows(ref, gran, count):
        return ref.at[pl.ds(pl.multiple_of(gran * gran_rows, SUBLANES), count * gran_rows), :]

    def x_copy(gran, sl, count=gpt):
        return pltpu.make_async_copy(hbm_rows(xs_hbm, gran, count),
                                     xbuf.at[sl, pl.ds(0, count * gran_rows), :], xsem.at[sl])

    def y_copy(gran, sl, count=gpt):
        return pltpu.make_async_copy(ybuf.at[sl, pl.ds(0, count * gran_rows), :],
                                     hbm_rows(y_hbm, gran, count), ysem.at[sl])

    def mlp(sl, tm):
        x = _unpack_rows(xbuf.at[sl], tm, rpt)
        gu = jnp.dot(x, wgu_b[...], preferred_element_type=jnp.float32) + bgu_ref[0]
        x_glu = jnp.minimum(gu[:, :d_ff], SWIGLU_LIMIT)
        x_lin = jnp.clip(gu[:, d_ff:], -SWIGLU_LIMIT, SWIGLU_LIMIT)
        act = x_glu * (1.0 / (1.0 + jnp.exp(-SWIGLU_ALPHA * x_glu))) * (x_lin + 1.0)
        y = jnp.dot(act.astype(jnp.bfloat16), wd_b[...], preferred_element_type=jnp.float32) + bd_ref[0]
        _pack_rows(y, ybuf.at[sl], tm)

    def weight_copies(ex):
        return (pltpu.make_async_copy(wgu_hbm.at[layer, ex], wgu_f, wsem.at[0]),
                pltpu.make_async_copy(wd_hbm.at[layer, ex], wd_f, wsem.at[1]))

    tail_slot = 2
    tail_gran = g0 + nt * gpt

    def wait_pending_tail_y():
        for count in range(1, gpt):
            @pl.when(pend_ref[0] == count)
            def _():
                y_copy(0, tail_slot, count).wait()

    @pl.when(e == 0)
    def _():
        pend_ref[0] = 0

        @pl.when(nt > 0)
        def _():
            x_copy(g0, 0).start()

    for count in range(1, gpt):
        @pl.when(tail == count)
        def _():
            x_copy(tail_gran, tail_slot, count).start()

    @pl.when(e == 0)
    def _():
        for c in weight_copies(0):
            c.start(priority=1)
    for c in weight_copies(e):
        c.wait()
    wgu_b[...] = wgu_f[...].astype(jnp.bfloat16)
    wd_b[...] = wd_f[...].astype(jnp.bfloat16)

    @pl.when(e + 1 < n_exp)
    def _():
        for c in weight_copies(e + 1):
            c.start(priority=1)

    @pl.when(ng > 0)
    def _():
        def tile_step(j, carry):
            sl = j % 2
            gran = g0 + j * gpt

            @pl.when(j + 1 < nt)
            def _():
                x_copy(gran + gpt, 1 - sl).start()
            x_copy(gran, sl).wait()

            @pl.when(j >= 2)
            def _():
                y_copy(gran - 2 * gpt, sl).wait()
            mlp(sl, ROW_TILE)
            y_copy(gran, sl).start()
            return carry
        lax.fori_loop(0, nt, tile_step, 0)

        @pl.when(nt >= 2)
        def _():
            y_copy(g0 + (nt - 2) * gpt, nt % 2).wait()

        @pl.when(nt >= 1)
        def _():
            y_copy(g0 + (nt - 1) * gpt, (nt - 1) % 2).wait()

    @pl.when(e + 1 < n_exp)
    def _():
        nxt = jnp.minimum(e + 1, n_exp - 1)

        @pl.when(tcount_ref[nxt] >= gpt)
        def _():
            x_copy(tstart_ref[nxt], 0).start()

    for count in range(1, gpt):
        @pl.when(tail == count)
        def _():
            x_copy(tail_gran, tail_slot, count).wait()
            wait_pending_tail_y()
            mlp(tail_slot, count * ROW_GRANULE)
            y_copy(tail_gran, tail_slot, count).start()
            pend_ref[0] = count

    @pl.when(e == n_exp - 1)
    def _():
        wait_pending_tail_y()
        ybuf[0, 0:gran_rows, :] = jnp.zeros((gran_rows, LANES), ybuf.dtype)

        def tail_start(i, carry):
            y_copy(i, 0, 1).start()
            return carry

        def tail_wait(i, carry):
            y_copy(i, 0, 1).wait()
            return carry
        lax.fori_loop(g0 + ng, n_gran, tail_start, 0)
        lax.fori_loop(g0 + ng, n_gran, tail_wait, 0)


def _experts(tile_start, tile_count, xs, w_gu, b_gu, w_down, b_down, layer):
    tm = ROW_TILE
    _, n_exp, d_model, d_gu = w_gu.shape
    d_ff = w_down.shape[2]
    rpt = d_model // PACK // LANES
    grid_spec = pltpu.PrefetchScalarGridSpec(
        num_scalar_prefetch=2,
        grid=(n_exp,),
        in_specs=[
            pl.BlockSpec(memory_space=pl.ANY),
            pl.BlockSpec((1, 1, d_gu), lambda e, *_: (e, 0, 0)),
            pl.BlockSpec(memory_space=pl.ANY),
            pl.BlockSpec((1, 1, d_model), lambda e, *_: (e, 0, 0)),
            pl.BlockSpec(memory_space=pl.ANY),
        ],
        out_specs=pl.BlockSpec(memory_space=pl.ANY),
        scratch_shapes=[pltpu.VMEM((d_model, d_gu), jnp.float32),
                        pltpu.VMEM((d_ff, d_model), jnp.float32),
                        pltpu.VMEM((d_model, d_gu), jnp.bfloat16),
                        pltpu.VMEM((d_ff, d_model), jnp.bfloat16),
                        pltpu.VMEM((3, tm * rpt, LANES), jnp.uint32),
                        pltpu.VMEM((3, tm * rpt, LANES), jnp.uint32),
                        pltpu.SMEM((1,), jnp.int32),
                        pltpu.SemaphoreType.DMA((2,)),
                        pltpu.SemaphoreType.DMA((3,)),
                        pltpu.SemaphoreType.DMA((3,))],
    )
    return pl.pallas_call(
        functools.partial(_experts_kernel, n_exp=n_exp, layer=layer),
        grid_spec=grid_spec,
        out_shape=jax.ShapeDtypeStruct(xs.shape, jnp.uint32),
        compiler_params=pltpu.CompilerParams(
            dimension_semantics=("arbitrary",),
            vmem_limit_bytes=VMEM_LIMIT_BYTES),
        name="experts",
    )(tile_start, tile_count, w_gu, b_gu.reshape(n_exp, 1, d_gu), w_down,
      b_down.reshape(n_exp, 1, d_model), xs)


def _combine_kernel(lidx_ref, gate_ref, nbe_ref, lb_ref, hrow_ref, extra_ref,
                    x1_ref, nw_ref, y_hbm, out_ref, stage_ref, alo_ref, ahi_ref, sem, *, nblk):
    b = pl.program_id(0)
    tb = x1_ref.shape[0]
    rpt = alo_ref.shape[0] // tb
    n_exp = nbe_ref.shape[0] // nblk
    n_tok = lidx_ref.shape[0] // TOP_K
    slot_rows = stage_ref.shape[0] // 2
    max_bit = tb.bit_length() - 1
    slot = b % 2

    def strip_copy(sl, staging_row, sorted_row, size):
        src = pl.multiple_of(sorted_row * rpt, SUBLANES)
        dst = pl.multiple_of(sl * slot_rows + staging_row * rpt, SUBLANES)
        return pltpu.make_async_copy(y_hbm.at[pl.ds(src, size * rpt), :],
                                     stage_ref.at[pl.ds(dst, size * rpt), :], sem.at[sl])

    def start_strips(blk, sl):
        _for_strips(nbe_ref, lb_ref, hrow_ref, blk, n_exp, max_bit,
                    lambda srow, hrow, size: strip_copy(sl, srow, hrow, size).start())

    @pl.when(b == 0)
    def _():
        start_strips(0, 0)

    @pl.when(b + 1 < nblk)
    def _():
        start_strips(b + 1, 1 - slot)

    def sized_copy(offset, size):
        return pltpu.make_async_copy(y_hbm.at[pl.ds(0, size * rpt), :],
                                     stage_ref.at[pl.ds(slot * slot_rows, size * rpt), :], sem.at[slot])
    sized_copy(0, tb * TOP_K).wait()
    _for_set_bits(extra_ref[b], n_exp.bit_length() - 1, lambda o, s: sized_copy(o, s).wait())

    def gather_block(static_slot):
        def gather_group(c, carry):
            tok0 = c * GATHER_UNROLL
            idx0 = b * tb + tok0
            for u in range(GATHER_UNROLL):
                lo = hi = None
                for k in range(TOP_K):
                    src = pl.multiple_of(lidx_ref[idx0 + (k * n_tok + u)], rpt)
                    g = gate_ref[idx0 + (k * n_tok + u)]
                    l, h = _unpack_words(stage_ref[pl.ds(static_slot * slot_rows + src, rpt), :])
                    lo = g * l if lo is None else lo + g * l
                    hi = g * h if hi is None else hi + g * h
                dst = pl.multiple_of((tok0 + u) * rpt, rpt)
                alo_ref[pl.ds(dst, rpt), :] = lo
                ahi_ref[pl.ds(dst, rpt), :] = hi
            return carry
        lax.fori_loop(0, tb // GATHER_UNROLL, gather_group, 0)

    for static_slot in range(2):
        pl.when(slot == static_slot)(functools.partial(gather_block, static_slot))

    parts = [alo_ref[pl.ds(si, tb, stride=rpt), :] for si in range(rpt)]
    parts += [ahi_ref[pl.ds(si, tb, stride=rpt), :] for si in range(rpt)]
    x2 = x1_ref[...] + jnp.concatenate(parts, axis=-1)
    out_ref[...] = _rms(x2, nw_ref[...])


def _combine(lidx, gates, nbe, lb, hrow, extra, x1, norm_w, y_rows):
    tb = TOK_BLOCK
    n_tok, d_model = x1.shape
    rpt = d_model // PACK // LANES
    n_exp = nbe.shape[0] // (n_tok // tb)
    slot_rows = (tb * TOP_K + n_exp * (STRIP_ALIGN - 1)) * rpt
    grid_spec = pltpu.PrefetchScalarGridSpec(
        num_scalar_prefetch=6,
        grid=(n_tok // tb,),
        in_specs=[
            pl.BlockSpec((tb, d_model), lambda i, *_: (i, 0)),
            pl.BlockSpec((1, d_model), lambda i, *_: (0, 0)),
            pl.BlockSpec(memory_space=pl.ANY),
        ],
        out_specs=pl.BlockSpec((tb, d_model), lambda i, *_: (i, 0)),
        scratch_shapes=[pltpu.VMEM((2 * slot_rows, LANES), jnp.uint32),
                        pltpu.VMEM((tb * rpt, LANES), jnp.float32),
                        pltpu.VMEM((tb * rpt, LANES), jnp.float32),
                        pltpu.SemaphoreType.DMA((2,))],
    )
    return pl.pallas_call(
        functools.partial(_combine_kernel, nblk=n_tok // tb),
        grid_spec=grid_spec,
        out_shape=jax.ShapeDtypeStruct((n_tok, d_model), jnp.float32),
        compiler_params=pltpu.CompilerParams(
            dimension_semantics=("arbitrary",),
            vmem_limit_bytes=VMEM_LIMIT_BYTES),
        name="combine",
    )(lidx, gates, nbe, lb, hrow, extra, x1, norm_w, y_rows)


def _retention_tables(seq, hd, ts):
    f32 = np.float32
    half = hd // 2
    inv = ROPE_BASE ** (-jnp.arange(half, dtype=jnp.float32) / half)
    ang = jnp.arange(seq, dtype=jnp.float32)[:, None] * inv[None, :]
    cos, sin = jnp.cos(ang), jnp.sin(ang)
    cos2 = jnp.concatenate([cos, cos], axis=-1)
    sin2 = jnp.concatenate([-sin, sin], axis=-1)
    log_g = np.log1p(-np.exp2(f32(-5.0) - np.arange(RET_HEADS, dtype=f32))).astype(f32)
    pos = np.arange(ts, dtype=f32)
    dist = np.abs(pos[:, None] - pos[None, :])
    chunk = np.arange(ts, dtype=np.int32) // CHUNK
    visible = chunk[None, :] <= chunk[:, None]
    dmat = np.where(visible[None], np.exp(log_g[:, None, None] * dist[None]), f32(0.0))
    qdec = np.exp(log_g[:, None] * (pos + f32(1.0))[None, :])
    kdec = np.exp(log_g[:, None] * (f32(ts - 1.0) - pos)[None, :])
    qdec = np.broadcast_to(qdec[:, :, None], (RET_HEADS, ts, hd))
    kdec = np.broadcast_to(kdec[:, :, None], (RET_HEADS, ts, hd))
    tile_decay = tuple(math.exp(math.log1p(-2.0 ** (-5.0 - h)) * ts) for h in range(RET_HEADS))
    dmat, qdec, kdec = (jnp.asarray(np.ascontiguousarray(t, dtype=f32)) for t in (dmat, qdec, kdec))
    return cos2, sin2, dmat, qdec, kdec, tile_decay


def _routing_tables(eid, rank, before, counts, n_tok, n_exp, rows_per_tok):
    tm, tb, ts = ROW_GRANULE, TOK_BLOCK, ROUTE_CHUNK
    cb = before[::tb // ts]
    nbe = jnp.concatenate([cb[1:], counts[None]], axis=0) - cb
    nal = nbe + (nbe & (STRIP_ALIGN - 1))
    region = jnp.sum(nal, axis=0)
    tiles_per = (region + tm - 1) // tm
    tile_end = jnp.cumsum(tiles_per)
    offs = (tile_end - tiles_per) * tm
    n_tiles = (n_tok * TOP_K + (n_tok // tb) * n_exp * (STRIP_ALIGN - 1)) // tm + n_exp
    hrow = offs[None, :] + jnp.cumsum(nal, axis=0) - nal
    lb = jnp.cumsum(nal, axis=1) - nal
    delta = jnp.repeat(lb - cb, tb, axis=0)
    onehot = eid[:, :, None] == jnp.arange(n_exp, dtype=jnp.int32)
    lidx = rank + jnp.sum(jnp.where(onehot, delta[None], 0), axis=-1)
    lidx = lidx * rows_per_tok
    padrow = offs + region
    padlen = tiles_per * tm - region
    as_i32 = lambda a: a.reshape(-1).astype(jnp.int32)
    return (n_tiles, as_i32(tile_end - tiles_per), as_i32(tiles_per), as_i32(tile_end[-1:]),
            as_i32(lidx), as_i32(nbe), as_i32(lb), as_i32(hrow),
            as_i32(jnp.sum(nal, axis=1) - tb * TOP_K), as_i32(padrow), as_i32(padlen))


def _layer(x, norm_mix_w, w_in, ret_norm_w, w_pool, pool_scale, w_out, norm_ffn_w,
           w_router, b_router, w_gu, b_gu, w_down, b_down, final_w, layer):
    bsz, seq, d_model = x.shape
    n_tok = bsz * seq
    n_exp = w_router.shape[1]
    rw = ret_norm_w.shape[0]
    hd = rw // RET_HEADS
    ts = SEQ_TILE
    bf = jnp.bfloat16
    rpt = d_model // PACK // LANES
    assert d_model % (PACK * LANES) == 0 and (STRIP_ALIGN * rpt) % SUBLANES == 0
    assert seq % ts == 0 and ts % CHUNK == 0 and n_tok % TOK_BLOCK == 0
    assert n_tok % ROUTE_TILE == 0 and ROUTE_TILE % ROUTE_CHUNK == 0 and TOK_BLOCK % ROUTE_CHUNK == 0
    assert TOK_BLOCK & (TOK_BLOCK - 1) == 0 and ROW_GRANULE & (ROW_GRANULE - 1) == 0
    assert ROW_TILE % ROW_GRANULE == 0

    cos2, sin2, dmat, qdec, kdec, tile_decay = _retention_tables(seq, hd, ts)
    tri = jnp.arange(ROUTE_CHUNK, dtype=jnp.int32)
    utri = (tri[:, None] < tri[None, :]).astype(bf)

    x1, h2p = _mix(
        x, norm_mix_w[None], w_in.astype(bf), cos2, sin2, dmat, qdec, kdec, ret_norm_w[None],
        w_pool.astype(bf), pool_scale[None], w_out.astype(bf), norm_ffn_w[None], tile_decay)
    eid, gate, rank, before, cnt = _route(h2p, w_router.T.astype(bf), b_router[:, None], utri)

    (n_tiles, tile_start, tile_count, n_used, lidx, nbe, lb, hrow, extra, padrow, padlen) = _routing_tables(
        eid, rank, before[:, :, 0].astype(jnp.int32), cnt[:, 0].astype(jnp.int32), n_tok, n_exp, rpt)

    xs = _dispatch(lidx, nbe, lb, hrow, extra, padrow, padlen, n_used, h2p, n_tiles * ROW_GRANULE, d_model)
    y_rows = _experts(tile_start, tile_count, xs, w_gu, b_gu, w_down, b_down, layer)
    out = _combine(lidx, gate.reshape(-1), nbe, lb, hrow, extra, x1.reshape(n_tok, d_model), final_w[None], y_rows)
    return out.reshape(bsz, seq, d_model)


def kernel(x, norm_mix_w, w_in, ret_norm_w, w_pool, pool_scale, w_out, norm_ffn_w, w_router, b_router,
           w_gu, b_gu, w_down, b_down, norm_final_w):
    depth = norm_mix_w.shape[0]
    for l in range(depth):
        last = l == depth - 1
        assert last, "stacked layers need an un-normalised combine output"
        x = _layer(x, norm_mix_w[l], w_in[l], ret_norm_w[l], w_pool[l], pool_scale[l], w_out[l],
                   norm_ffn_w[l], w_router[l], b_router[l], w_gu, b_gu[l], w_down, b_down[l],
                   norm_final_w, l)
    return x
```

```python
import functools
import math

import jax
import jax.numpy as jnp
from jax import lax
from jax.experimental import pallas as pl
from jax.experimental.pallas import tpu as pltpu

CHUNK = 64
RET_HEADS = 4
POOL_WINDOWS = (2, 4, 8, 16)
ROPE_BASE = 10000.0
TOP_K = 4
SWIGLU_LIMIT = 7.0
SWIGLU_ALPHA = 1.702
EPS = 1e-6

LANES = 128
SUBLANES = 8
VMEM_LIMIT_BYTES = 56 * 1024 * 1024

SEQ_TILE = 256
MIX_STREAMS = 4
ROUTE_TILE = 2048
ROUTE_CHUNK = 256
ROW_GRANULE = 128
ROW_TILE = 512
TOK_BLOCK = 1024
POOL_HALO = 16
PACK = 2
STRIP_ALIGN = 2
GATHER_UNROLL = 16
STRIP_UNROLL = 4
RARE_STRIP_BIT = 8


def _rms(x, w):
    return x * lax.rsqrt(jnp.mean(x * x, axis=-1, keepdims=True) + EPS) * w


def _bf16_bits_hi(x):
    return lax.bitcast_convert_type(x.astype(jnp.bfloat16).astype(jnp.float32), jnp.uint32)


def _pack_rows(x, out_ref, n_rows):
    half = x.shape[1] // PACK
    packed = (_bf16_bits_hi(x[:, :half]) >> 16) | _bf16_bits_hi(x[:, half:])
    rpt = half // LANES
    for si in range(rpt):
        out_ref[pl.ds(si, n_rows, stride=rpt), :] = packed[:, si * LANES:(si + 1) * LANES]


def _unpack_words(u):
    lo = lax.bitcast_convert_type(u << 16, jnp.float32)
    hi = lax.bitcast_convert_type(u & jnp.uint32(0xFFFF0000), jnp.float32)
    return lo, hi


def _unpack_rows(g_ref, n_rows, rpt):
    lo, hi = [], []
    for si in range(rpt):
        l, h = _unpack_words(g_ref[pl.ds(si, n_rows, stride=rpt), :])
        lo.append(l.astype(jnp.bfloat16))
        hi.append(h.astype(jnp.bfloat16))
    return jnp.concatenate(lo + hi, axis=-1)


def _mix_kernel(x_ref, nmw_ref, win_ref, cos_ref, sin_ref, dmat_ref, qdec_ref, kdec_ref,
                rnw_ref, wpool_ref, pscale_ref, wout_ref, nfw_ref,
                x1_ref, h2p_ref,
                state_ref, p1_ref, p2_ref, p4_ref, p8_ref,
                *, tile_decay):
    j = pl.program_id(1)
    n_str = x_ref.shape[0]
    ts = x_ref.shape[2]
    n_heads = RET_HEADS
    rw = rnw_ref.shape[1]
    hd = rw // n_heads
    pgw = wpool_ref.shape[1]
    halo = POOL_HALO
    pool_refs = (p1_ref, p2_ref, p4_ref, p8_ref)

    @pl.when(j == 0)
    def _():
        state_ref[...] = jnp.zeros_like(state_ref)
        for r in pool_refs:
            r[:, 0:halo, :] = jnp.zeros((n_str, halo, r.shape[2]), jnp.float32)

    @pl.when(j > 0)
    def _():
        for r in pool_refs:
            r[:, 0:halo, :] = r[:, ts:ts + halo, :]

    x = jnp.concatenate([x_ref[si, 0] for si in range(n_str)], axis=0)
    h = _rms(x, nmw_ref[...]).astype(jnp.bfloat16)
    z_all = jnp.dot(h, win_ref[...], preferred_element_type=jnp.float32)

    cos = cos_ref[...]
    sin = sin_ref[...]
    k_scale = hd ** -0.5
    tpos = (j * ts + lax.broadcasted_iota(jnp.int32, (ts, pgw), 0) + 1).astype(jnp.float32)

    mix_rows = []
    for si in range(n_str):
        z = z_all[si * ts:(si + 1) * ts]
        parts = []
        for hh in range(n_heads):
            q = z[:, hh * hd:(hh + 1) * hd]
            k = z[:, rw + hh * hd:rw + (hh + 1) * hd]
            v = z[:, 2 * rw + hh * hd:2 * rw + (hh + 1) * hd]
            g = z[:, 3 * rw + hh * hd:3 * rw + (hh + 1) * hd]
            q = q * cos + pltpu.roll(q, hd // 2, 1) * sin
            k = (k * cos + pltpu.roll(k, hd // 2, 1) * sin) * k_scale
            qb = q.astype(jnp.bfloat16)
            kb = k.astype(jnp.bfloat16)
            vb = v.astype(jnp.bfloat16)
            s = lax.dot_general(qb, kb, (((1,), (1,)), ((), ())), preferred_element_type=jnp.float32)
            pmat = (s * dmat_ref[hh]).astype(jnp.bfloat16)
            o = jnp.dot(pmat, vb, preferred_element_type=jnp.float32)
            state = state_ref[si, hh]
            qd = (q * qdec_ref[hh]).astype(jnp.bfloat16)
            o = o + jnp.dot(qd, state.astype(jnp.bfloat16), preferred_element_type=jnp.float32)
            kd = (k * kdec_ref[hh]).astype(jnp.bfloat16)
            u = lax.dot_general(kd, vb, (((0,), (0,)), ((), ())), preferred_element_type=jnp.float32)
            state_ref[si, hh] = state * tile_decay[hh] + u
            o = o * lax.rsqrt(jnp.mean(o * o, axis=-1, keepdims=True) + EPS)
            o = o * rnw_ref[:, hh * hd:(hh + 1) * hd] * (g * (1.0 / (1.0 + jnp.exp(-g))))
            parts.append(o.astype(jnp.bfloat16))

        p = z[:, 4 * rw:]
        p1_ref[si, halo:halo + ts, :] = p
        s2 = p + p1_ref[si, halo - 1:halo - 1 + ts, :]
        p2_ref[si, halo:halo + ts, :] = s2[:, pgw:]
        s4 = s2[:, pgw:] + p2_ref[si, halo - 2:halo - 2 + ts, :]
        p4_ref[si, halo:halo + ts, :] = s4[:, pgw:]
        s8 = s4[:, pgw:] + p4_ref[si, halo - 4:halo - 4 + ts, :]
        p8_ref[si, halo:halo + ts, :] = s8[:, pgw:]
        s16 = s8[:, pgw:] + p8_ref[si, halo - 8:halo - 8 + ts, :]
        sums = (s2[:, :pgw], s4[:, :pgw], s8[:, :pgw], s16)
        for gi, w in enumerate(POOL_WINDOWS):
            cnt = jnp.minimum(tpos, float(w))
            pooled = sums[gi] / cnt - p[:, gi * pgw:(gi + 1) * pgw]
            mixed = jnp.dot(pooled.astype(jnp.bfloat16), wpool_ref[gi], preferred_element_type=jnp.float32)
            mixed = mixed * pscale_ref[:, gi * pgw:(gi + 1) * pgw]
            parts.append(mixed.astype(jnp.bfloat16))
        mix_rows.append(jnp.concatenate(parts, axis=-1))

    mix = jnp.concatenate(mix_rows, axis=0)
    x1 = x + jnp.dot(mix, wout_ref[...], preferred_element_type=jnp.float32)
    h2 = _rms(x1, nfw_ref[...])
    for si in range(n_str):
        x1_ref[si, 0] = x1[si * ts:(si + 1) * ts]
        _pack_rows(h2[si * ts:(si + 1) * ts], h2p_ref.at[si, 0], ts)


def _mix(x, nmw, win_b, cos2, sin2, dmat, qdec, kdec, rnw, wpool_b, pscale, wout_b, nfw, tile_decay):
    bsz, seq, d_model = x.shape
    ts = SEQ_TILE
    n_tok = bsz * seq
    rw = rnw.shape[1]
    hd = rw // RET_HEADS
    pw = pscale.shape[1]
    pgw = pw // len(POOL_WINDOWS)
    nj = seq // ts
    rows_per_tok = d_model // PACK // LANES

    def const(shape):
        return pl.BlockSpec(shape, lambda b, j: (0,) * len(shape))

    n_str = MIX_STREAMS if bsz % MIX_STREAMS == 0 else 1
    per = bsz // n_str
    in_specs = [
        pl.BlockSpec((n_str, 1, ts, d_model), lambda q, j: (0, q, j, 0)),
        const(nmw.shape), const(win_b.shape),
        pl.BlockSpec((ts, hd), lambda q, j: (j, 0)),
        pl.BlockSpec((ts, hd), lambda q, j: (j, 0)),
        const(dmat.shape), const(qdec.shape), const(kdec.shape),
        const(rnw.shape), const(wpool_b.shape), const(pscale.shape), const(wout_b.shape),
        const(nfw.shape),
    ]
    out_shape = [
        jax.ShapeDtypeStruct((n_str, per, seq, d_model), jnp.float32),
        jax.ShapeDtypeStruct((n_str, per, seq * rows_per_tok, LANES), jnp.uint32),
    ]
    out_specs = [
        pl.BlockSpec((n_str, 1, ts, d_model), lambda q, j: (0, q, j, 0)),
        pl.BlockSpec((n_str, 1, ts * rows_per_tok, LANES), lambda q, j: (0, q, j, 0)),
    ]
    scratch = [
        pltpu.VMEM((n_str, RET_HEADS, hd, hd), jnp.float32),
        pltpu.VMEM((n_str, POOL_HALO + ts, pw), jnp.float32),
        pltpu.VMEM((n_str, POOL_HALO + ts, pw - pgw), jnp.float32),
        pltpu.VMEM((n_str, POOL_HALO + ts, pw - 2 * pgw), jnp.float32),
        pltpu.VMEM((n_str, POOL_HALO + ts, pw - 3 * pgw), jnp.float32),
    ]
    x1, h2p = pl.pallas_call(
        functools.partial(_mix_kernel, tile_decay=tile_decay),
        grid=(per, nj),
        in_specs=in_specs,
        out_specs=out_specs,
        out_shape=out_shape,
        scratch_shapes=scratch,
        compiler_params=pltpu.CompilerParams(
            dimension_semantics=("arbitrary", "arbitrary"),
            vmem_limit_bytes=VMEM_LIMIT_BYTES),
        name="mix",
    )(x.reshape(n_str, per, seq, d_model), nmw, win_b, cos2, sin2, dmat, qdec, kdec, rnw, wpool_b,
      pscale, wout_b, nfw)
    return x1.reshape(bsz, seq, d_model), h2p.reshape(n_tok * rows_per_tok, LANES)


def _route_kernel(h2p_ref, wrt_ref, brt_ref, utri_ref, ltri_ref,
                  lidx_ref, gate_ref, nbe_ref, lb_ref, cnt_ref, carry_ref):
    i = pl.program_id(0)
    n_exp, d_model = wrt_ref.shape
    rpt = d_model // PACK // LANES
    tr = h2p_ref.shape[0] // rpt
    sub = utri_ref.shape[0]
    chunks_per_block = TOK_BLOCK // sub

    @pl.when(i == 0)
    def _():
        carry_ref[...] = jnp.zeros_like(carry_ref)

    h2b = _unpack_rows(h2p_ref, tr, rpt)
    logits = lax.dot_general(wrt_ref[...], h2b, (((1,), (1,)), ((), ())),
                             preferred_element_type=jnp.float32) + brt_ref[...]
    eiota = lax.broadcasted_iota(jnp.int32, (n_exp, tr), 0)
    work = logits
    vals, sels = [], []
    for _ in range(TOP_K):
        m = jnp.max(work, axis=0, keepdims=True)
        idx = jnp.min(jnp.where(work == m, eiota, n_exp), axis=0, keepdims=True)
        sel = eiota == idx
        vals.append(m)
        sels.append(sel)
        work = jnp.where(sel, -jnp.inf, work)
    ex = [jnp.exp(vk - vals[0]) for vk in vals]
    den = ex[0] + ex[1] + ex[2] + ex[3]
    gate_ref[...] = jnp.concatenate([e / den for e in ex], axis=0)

    member = jnp.zeros((n_exp, tr), jnp.float32)
    for sel in sels:
        member = member + sel.astype(jnp.float32)
    carry = carry_ref[...]
    places = []
    for blk in range(tr // TOK_BLOCK):
        block_start = carry
        befores = []
        for c in range(blk * chunks_per_block, (blk + 1) * chunks_per_block):
            mc = member[:, c * sub:(c + 1) * sub]
            bc = jnp.dot(mc.astype(jnp.bfloat16), utri_ref[...], preferred_element_type=jnp.float32)
            befores.append(bc + carry[:, 0:1])
            carry = carry + jnp.sum(mc, axis=1, keepdims=True)
        nbe = carry - block_start
        nal = nbe + (nbe - STRIP_ALIGN * jnp.floor(nbe * (1.0 / STRIP_ALIGN)))
        hi = jnp.floor(nal * (1.0 / 32.0))
        lo = nal - 32.0 * hi
        lb = (32.0 * jnp.dot(ltri_ref[...], hi.astype(jnp.bfloat16), preferred_element_type=jnp.float32)
              + jnp.dot(ltri_ref[...], lo.astype(jnp.bfloat16), preferred_element_type=jnp.float32))
        nbe_ref[blk] = nbe
        lb_ref[blk] = lb
        place = jnp.concatenate(befores, axis=1) + (lb[:, 0:1] - block_start[:, 0:1])
        cols = slice(blk * TOK_BLOCK, (blk + 1) * TOK_BLOCK)
        places.append(jnp.concatenate(
            [jnp.sum(jnp.where(sel[:, cols], place, 0.0), axis=0, keepdims=True) for sel in sels], axis=0))
    lidx_ref[...] = (jnp.concatenate(places, axis=1) * float(rpt)).astype(jnp.int32)
    carry_ref[...] = carry
    cnt_ref[...] = carry


def _route(h2p, wrt_b, brt, utri, ltri):
    n_exp, d_model = wrt_b.shape
    rpt = d_model // PACK // LANES
    n_tok = h2p.shape[0] // rpt
    tr = ROUTE_TILE
    bps = tr // TOK_BLOCK

    def const(shape):
        return pl.BlockSpec(shape, lambda i: (0,) * len(shape))

    out_shape = [
        jax.ShapeDtypeStruct((TOP_K, n_tok), jnp.int32),
        jax.ShapeDtypeStruct((TOP_K, n_tok), jnp.float32),
        jax.ShapeDtypeStruct((n_tok // TOK_BLOCK, n_exp, LANES), jnp.float32),
        jax.ShapeDtypeStruct((n_tok // TOK_BLOCK, n_exp, LANES), jnp.float32),
        jax.ShapeDtypeStruct((n_exp, LANES), jnp.float32),
    ]
    out_specs = [
        pl.BlockSpec((TOP_K, tr), lambda i: (0, i)),
        pl.BlockSpec((TOP_K, tr), lambda i: (0, i)),
        pl.BlockSpec((bps, n_exp, LANES), lambda i: (i, 0, 0)),
        pl.BlockSpec((bps, n_exp, LANES), lambda i: (i, 0, 0)),
        pl.BlockSpec((n_exp, LANES), lambda i: (0, 0)),
    ]
    return pl.pallas_call(
        _route_kernel,
        grid=(n_tok // tr,),
        in_specs=[pl.BlockSpec((tr * rpt, LANES), lambda i: (i, 0)),
                  const(wrt_b.shape), const(brt.shape), const(utri.shape), const(ltri.shape)],
        out_specs=out_specs,
        out_shape=out_shape,
        scratch_shapes=[pltpu.VMEM((n_exp, LANES), jnp.float32)],
        compiler_params=pltpu.CompilerParams(
            dimension_semantics=("arbitrary",),
            vmem_limit_bytes=VMEM_LIMIT_BYTES),
        name="route",
    )(h2p, wrt_b, brt, utri, ltri)


def _for_set_bits(n, max_bit, fn):
    def piece(bit):
        size = 1 << bit
        offset = (n >> (bit + 1)) << (bit + 1)

        @pl.when((n & size) != 0)
        def _():
            fn(offset, size)

    min_bit = STRIP_ALIGN.bit_length() - 1
    rare_bit = min(max_bit + 1, RARE_STRIP_BIT)

    @pl.when(n >= (1 << rare_bit))
    def _():
        for bit in range(max_bit, rare_bit - 1, -1):
            piece(bit)
    for bit in range(rare_bit - 1, min_bit - 1, -1):
        piece(bit)


def _aligned_len(n):
    return n + (n & (STRIP_ALIGN - 1))


def _for_strips(nbe_ref, lb_ref, hrow_ref, blk, n_exp, max_bit, fn):
    def body(i, carry):
        for u in range(STRIP_UNROLL):
            k = blk * n_exp + i * STRIP_UNROLL + u
            n = _aligned_len(nbe_ref[k])
            lb = lb_ref[k]
            hrow = hrow_ref[k]
            _for_set_bits(n, max_bit, lambda offset, size: fn(lb + offset, hrow + offset, size))
        return carry
    assert n_exp % STRIP_UNROLL == 0
    lax.fori_loop(0, n_exp // STRIP_UNROLL, body, 0)


def _dispatch_kernel(lidx_ref, nbe_ref, lb_ref, hrow_ref, extra_ref, padrow_ref, padlen_ref, nused_ref,
                     h2p_ref, xs_hbm, stage_ref, zero_ref, sem, zsem, *, nblk):
    b = pl.program_id(0)
    rpt = zero_ref.shape[0] // ROW_GRANULE
    tb = h2p_ref.shape[0] // rpt
    n_exp = padrow_ref.shape[0]
    n_tok = lidx_ref.shape[0] // TOP_K
    slot_rows = stage_ref.shape[0] // 2
    max_bit = tb.bit_length() - 1
    slot = b % 2

    def strip_copy(sl, staging_row, sorted_row, size):
        src = pl.multiple_of(sl * slot_rows + staging_row * rpt, SUBLANES)
        dst = pl.multiple_of(sorted_row * rpt, SUBLANES)
        return pltpu.make_async_copy(stage_ref.at[pl.ds(src, size * rpt), :],
                                     xs_hbm.at[pl.ds(dst, size * rpt), :], sem.at[sl])

    def start_strips(blk, sl):
        _for_strips(nbe_ref, lb_ref, hrow_ref, blk, n_exp, max_bit,
                    lambda srow, hrow, size: strip_copy(sl, srow, hrow, size).start())

    def wait_strips(blk, sl):
        def sized_copy(offset, size):
            return pltpu.make_async_copy(stage_ref.at[pl.ds(sl * slot_rows, size * rpt), :],
                                         xs_hbm.at[pl.ds(0, size * rpt), :], sem.at[sl])
        sized_copy(0, tb * TOP_K).wait()
        _for_set_bits(extra_ref[blk], n_exp.bit_length() - 1, lambda o, s: sized_copy(o, s).wait())

    @pl.when(b >= 2)
    def _():
        wait_strips(b - 2, slot)

    def scatter_block(static_slot):
        def scatter_group(c, carry):
            tok0 = c * GATHER_UNROLL
            idx0 = b * tb + tok0
            for u in range(GATHER_UNROLL):
                row = h2p_ref[pl.ds(pl.multiple_of((tok0 + u) * rpt, rpt), rpt), :]
                for k in range(TOP_K):
                    dst = pl.multiple_of(lidx_ref[idx0 + (k * n_tok + u)], rpt)
                    stage_ref[pl.ds(static_slot * slot_rows + dst, rpt), :] = row
            return carry
        lax.fori_loop(0, tb // GATHER_UNROLL, scatter_group, 0)

    for static_slot in range(2):
        pl.when(slot == static_slot)(functools.partial(scatter_block, static_slot))

    def zero_tail(e, carry):
        n = nbe_ref[b * n_exp + e]

        @pl.when((n & (STRIP_ALIGN - 1)) != 0)
        def _():
            dst = pl.multiple_of(slot * slot_rows + (lb_ref[b * n_exp + e] + n) * rpt, rpt)
            stage_ref[pl.ds(dst, rpt), :] = jnp.zeros((rpt, LANES), jnp.uint32)
        return carry
    lax.fori_loop(0, n_exp, zero_tail, 0)

    start_strips(b, slot)

    @pl.when(b == nblk - 1)
    def _():
        if nblk >= 2:
            wait_strips(b - 1, 1 - slot)
        wait_strips(b, slot)
        zero_ref[...] = jnp.zeros_like(zero_ref)
        pad_bit = ROW_GRANULE.bit_length() - 2

        def pad_copy(e, offset, size):
            dst = pl.multiple_of((padrow_ref[e] + offset) * rpt, SUBLANES)
            return pltpu.make_async_copy(zero_ref.at[pl.ds(0, size * rpt), :],
                                         xs_hbm.at[pl.ds(dst, size * rpt), :], zsem)

        def pad_start(e, carry):
            _for_set_bits(padlen_ref[e], pad_bit, lambda o, s: pad_copy(e, o, s).start())
            return carry

        def pad_wait(e, carry):
            _for_set_bits(padlen_ref[e], pad_bit, lambda o, s: pad_copy(e, o, s).wait())
            return carry
        lax.fori_loop(0, n_exp, pad_start, 0)

        def tail_copy(i):
            dst = pl.multiple_of(i * (ROW_GRANULE * rpt), SUBLANES)
            return pltpu.make_async_copy(zero_ref, xs_hbm.at[pl.ds(dst, ROW_GRANULE * rpt), :], zsem)

        def tail_start(i, carry):
            tail_copy(i).start()
            return carry

        def tail_wait(i, carry):
            tail_copy(i).wait()
            return carry
        n_gran = xs_hbm.shape[0] // (ROW_GRANULE * rpt)
        lax.fori_loop(nused_ref[0], n_gran, tail_start, 0)
        lax.fori_loop(0, n_exp, pad_wait, 0)
        lax.fori_loop(nused_ref[0], n_gran, tail_wait, 0)


def _dispatch(lidx, nbe, lb, hrow, extra, padrow, padlen, n_used, h2p, n_rows, d_model):
    tb = TOK_BLOCK
    rpt = d_model // PACK // LANES
    n_tok = h2p.shape[0] // rpt
    n_exp = padrow.shape[0]
    slot_rows = (tb * TOP_K + n_exp * (STRIP_ALIGN - 1)) * rpt
    grid_spec = pltpu.PrefetchScalarGridSpec(
        num_scalar_prefetch=8,
        grid=(n_tok // tb,),
        in_specs=[pl.BlockSpec((tb * rpt, LANES), lambda i, *_: (i, 0))],
        out_specs=pl.BlockSpec(memory_space=pl.ANY),
        scratch_shapes=[pltpu.VMEM((2 * slot_rows, LANES), jnp.uint32),
                        pltpu.VMEM((ROW_GRANULE * rpt, LANES), jnp.uint32),
                        pltpu.SemaphoreType.DMA((2,)),
                        pltpu.SemaphoreType.DMA],
    )
    return pl.pallas_call(
        functools.partial(_dispatch_kernel, nblk=n_tok // tb),
        grid_spec=grid_spec,
        out_shape=jax.ShapeDtypeStruct((n_rows * rpt, LANES), jnp.uint32),
        compiler_params=pltpu.CompilerParams(
            dimension_semantics=("arbitrary",),
            vmem_limit_bytes=VMEM_LIMIT_BYTES),
        name="dispatch",
    )(lidx, nbe, lb, hrow, extra, padrow, padlen, n_used, h2p)


def _experts_kernel(tstart_ref, tcount_ref, wgu_hbm, bgu_ref, wd_hbm, bd_ref, xs_hbm,
                    y_hbm, wgu_f, wd_f, wgu_b, wd_b, xbuf, ybuf, pend_ref, wsem, xsem, ysem, *, n_exp):
    e = pl.program_id(0)
    d_ff = wd_f.shape[0]
    d_model = wd_f.shape[1]
    rpt = d_model // PACK // LANES
    gran_rows = ROW_GRANULE * rpt
    gpt = ROW_TILE // ROW_GRANULE
    n_gran = y_hbm.shape[0] // gran_rows
    g0 = tstart_ref[e]
    ng = tcount_ref[e]
    nt = ng // gpt
    tail = ng - nt * gpt

    def hbm_rows(ref, gran, count):
        return ref.at[pl.ds(pl.multiple_of(gran * gran_rows, SUBLANES), count * gran_rows), :]

    def x_copy(gran, sl, count=gpt):
        return pltpu.make_async_copy(hbm_rows(xs_hbm, gran, count),
                                     xbuf.at[sl, pl.ds(0, count * gran_rows), :], xsem.at[sl])

    def y_copy(gran, sl, count=gpt):
        return pltpu.make_async_copy(ybuf.at[sl, pl.ds(0, count * gran_rows), :],
                                     hbm_rows(y_hbm, gran, count), ysem.at[sl])

    def mlp(sl, tm):
        x = _unpack_rows(xbuf.at[sl], tm, rpt)
        gu = jnp.dot(x, wgu_b[...], preferred_element_type=jnp.float32) + bgu_ref[0]
        x_glu = jnp.minimum(gu[:, :d_ff], SWIGLU_LIMIT)
        x_lin = jnp.clip(gu[:, d_ff:], -SWIGLU_LIMIT, SWIGLU_LIMIT)
        act = x_glu * (1.0 / (1.0 + jnp.exp(-SWIGLU_ALPHA * x_glu))) * (x_lin + 1.0)
        y = jnp.dot(act.astype(jnp.bfloat16), wd_b[...], preferred_element_type=jnp.float32) + bd_ref[0]
        _pack_rows(y, ybuf.at[sl], tm)

    def weight_copies(ex):
        return (pltpu.make_async_copy(wgu_hbm.at[ex], wgu_f, wsem.at[0]),
                pltpu.make_async_copy(wd_hbm.at[ex], wd_f, wsem.at[1]))

    tail_slot = 2
    tail_gran = g0 + nt * gpt

    def wait_pending_tail_y():
        for count in range(1, gpt):
            @pl.when(pend_ref[0] == count)
            def _():
                y_copy(0, tail_slot, count).wait()

    @pl.when(e == 0)
    def _():
        pend_ref[0] = 0

        @pl.when(nt > 0)
        def _():
            x_copy(g0, 0).start()

    for count in range(1, gpt):
        @pl.when(tail == count)
        def _():
            x_copy(tail_gran, tail_slot, count).start()

    @pl.when(e == 0)
    def _():
        for c in weight_copies(0):
            c.start(priority=1)
    for c in weight_copies(e):
        c.wait()
    wgu_b[...] = wgu_f[...].astype(jnp.bfloat16)
    wd_b[...] = wd_f[...].astype(jnp.bfloat16)

    @pl.when(e + 1 < n_exp)
    def _():
        for c in weight_copies(e + 1):
            c.start(priority=1)

    @pl.when(ng > 0)
    def _():
        def tile_step(j, carry):
            sl = j % 2
            gran = g0 + j * gpt

            @pl.when(j + 1 < nt)
            def _():
                x_copy(gran + gpt, 1 - sl).start()
            x_copy(gran, sl).wait()

            @pl.when(j >= 2)
            def _():
                y_copy(gran - 2 * gpt, sl).wait()
            mlp(sl, ROW_TILE)
            y_copy(gran, sl).start()
            return carry
        lax.fori_loop(0, nt, tile_step, 0)

        @pl.when(nt >= 2)
        def _():
            y_copy(g0 + (nt - 2) * gpt, nt % 2).wait()

        @pl.when(nt >= 1)
        def _():
            y_copy(g0 + (nt - 1) * gpt, (nt - 1) % 2).wait()

    @pl.when(e + 1 < n_exp)
    def _():
        nxt = jnp.minimum(e + 1, n_exp - 1)

        @pl.when(tcount_ref[nxt] >= gpt)
        def _():
            x_copy(tstart_ref[nxt], 0).start()

    for count in range(1, gpt):
        @pl.when(tail == count)
        def _():
            x_copy(tail_gran, tail_slot, count).wait()
            wait_pending_tail_y()
            mlp(tail_slot, count * ROW_GRANULE)
            y_copy(tail_gran, tail_slot, count).start()
            pend_ref[0] = count

    @pl.when(e == n_exp - 1)
    def _():
        wait_pending_tail_y()
        ybuf[0, 0:gran_rows, :] = jnp.zeros((gran_rows, LANES), ybuf.dtype)

        def tail_start(i, carry):
            y_copy(i, 0, 1).start()
            return carry

        def tail_wait(i, carry):
            y_copy(i, 0, 1).wait()
            return carry
        lax.fori_loop(g0 + ng, n_gran, tail_start, 0)
        lax.fori_loop(g0 + ng, n_gran, tail_wait, 0)


def _experts(tile_start, tile_count, xs, w_gu, b_gu, w_down, b_down):
    tm = ROW_TILE
    n_exp, d_model, d_gu = w_gu.shape
    d_ff = w_down.shape[1]
    rpt = d_model // PACK // LANES
    grid_spec = pltpu.PrefetchScalarGridSpec(
        num_scalar_prefetch=2,
        grid=(n_exp,),
        in_specs=[
            pl.BlockSpec(memory_space=pl.ANY),
            pl.BlockSpec((1, 1, d_gu), lambda e, *_: (e, 0, 0)),
            pl.BlockSpec(memory_space=pl.ANY),
            pl.BlockSpec((1, 1, d_model), lambda e, *_: (e, 0, 0)),
            pl.BlockSpec(memory_space=pl.ANY),
        ],
        out_specs=pl.BlockSpec(memory_space=pl.ANY),
        scratch_shapes=[pltpu.VMEM((d_model, d_gu), jnp.float32),
                        pltpu.VMEM((d_ff, d_model), jnp.float32),
                        pltpu.VMEM((d_model, d_gu), jnp.bfloat16),
                        pltpu.VMEM((d_ff, d_model), jnp.bfloat16),
                        pltpu.VMEM((3, tm * rpt, LANES), jnp.uint32),
                        pltpu.VMEM((3, tm * rpt, LANES), jnp.uint32),
                        pltpu.SMEM((1,), jnp.int32),
                        pltpu.SemaphoreType.DMA((2,)),
                        pltpu.SemaphoreType.DMA((3,)),
                        pltpu.SemaphoreType.DMA((3,))],
    )
    return pl.pallas_call(
        functools.partial(_experts_kernel, n_exp=n_exp),
        grid_spec=grid_spec,
        out_shape=jax.ShapeDtypeStruct(xs.shape, jnp.uint32),
        compiler_params=pltpu.CompilerParams(
            dimension_semantics=("arbitrary",),
            vmem_limit_bytes=VMEM_LIMIT_BYTES),
        name="experts",
    )(tile_start, tile_count, w_gu, b_gu.reshape(n_exp, 1, d_gu), w_down,
      b_down.reshape(n_exp, 1, d_model), xs)


def _combine_kernel(lidx_ref, gate_ref, nbe_ref, lb_ref, hrow_ref, extra_ref,
                    x1_ref, nw_ref, y_hbm, out_ref, stage_ref, alo_ref, ahi_ref, sem, *, nblk):
    b = pl.program_id(0)
    tb = x1_ref.shape[0]
    rpt = alo_ref.shape[0] // tb
    n_exp = nbe_ref.shape[0] // nblk
    n_tok = lidx_ref.shape[0] // TOP_K
    slot_rows = stage_ref.shape[0] // 2
    max_bit = tb.bit_length() - 1
    slot = b % 2

    def strip_copy(sl, staging_row, sorted_row, size):
        src = pl.multiple_of(sorted_row * rpt, SUBLANES)
        dst = pl.multiple_of(sl * slot_rows + staging_row * rpt, SUBLANES)
        return pltpu.make_async_copy(y_hbm.at[pl.ds(src, size * rpt), :],
                                     stage_ref.at[pl.ds(dst, size * rpt), :], sem.at[sl])

    def start_strips(blk, sl):
        _for_strips(nbe_ref, lb_ref, hrow_ref, blk, n_exp, max_bit,
                    lambda srow, hrow, size: strip_copy(sl, srow, hrow, size).start())

    @pl.when(b == 0)
    def _():
        start_strips(0, 0)

    @pl.when(b + 1 < nblk)
    def _():
        start_strips(b + 1, 1 - slot)

    def sized_copy(offset, size):
        return pltpu.make_async_copy(y_hbm.at[pl.ds(0, size * rpt), :],
                                     stage_ref.at[pl.ds(slot * slot_rows, size * rpt), :], sem.at[slot])
    sized_copy(0, tb * TOP_K).wait()
    _for_set_bits(extra_ref[b], n_exp.bit_length() - 1, lambda o, s: sized_copy(o, s).wait())

    def gather_block(static_slot):
        def gather_group(c, carry):
            tok0 = c * GATHER_UNROLL
            idx0 = b * tb + tok0
            for u in range(GATHER_UNROLL):
                lo = hi = None
                for k in range(TOP_K):
                    src = pl.multiple_of(lidx_ref[idx0 + (k * n_tok + u)], rpt)
                    g = gate_ref[idx0 + (k * n_tok + u)]
                    l, h = _unpack_words(stage_ref[pl.ds(static_slot * slot_rows + src, rpt), :])
                    lo = g * l if lo is None else lo + g * l
                    hi = g * h if hi is None else hi + g * h
                dst = pl.multiple_of((tok0 + u) * rpt, rpt)
                alo_ref[pl.ds(dst, rpt), :] = lo
                ahi_ref[pl.ds(dst, rpt), :] = hi
            return carry
        lax.fori_loop(0, tb // GATHER_UNROLL, gather_group, 0)

    for static_slot in range(2):
        pl.when(slot == static_slot)(functools.partial(gather_block, static_slot))

    parts = [alo_ref[pl.ds(si, tb, stride=rpt), :] for si in range(rpt)]
    parts += [ahi_ref[pl.ds(si, tb, stride=rpt), :] for si in range(rpt)]
    x2 = x1_ref[...] + jnp.concatenate(parts, axis=-1)
    out_ref[...] = _rms(x2, nw_ref[...])


def _combine(lidx, gates, nbe, lb, hrow, extra, x1, norm_w, y_rows):
    tb = TOK_BLOCK
    n_tok, d_model = x1.shape
    rpt = d_model // PACK // LANES
    n_exp = nbe.shape[0] // (n_tok // tb)
    slot_rows = (tb * TOP_K + n_exp * (STRIP_ALIGN - 1)) * rpt
    grid_spec = pltpu.PrefetchScalarGridSpec(
        num_scalar_prefetch=6,
        grid=(n_tok // tb,),
        in_specs=[
            pl.BlockSpec((tb, d_model), lambda i, *_: (i, 0)),
            pl.BlockSpec((1, d_model), lambda i, *_: (0, 0)),
            pl.BlockSpec(memory_space=pl.ANY),
        ],
        out_specs=pl.BlockSpec((tb, d_model), lambda i, *_: (i, 0)),
        scratch_shapes=[pltpu.VMEM((2 * slot_rows, LANES), jnp.uint32),
                        pltpu.VMEM((tb * rpt, LANES), jnp.float32),
                        pltpu.VMEM((tb * rpt, LANES), jnp.float32),
                        pltpu.SemaphoreType.DMA((2,))],
    )
    return pl.pallas_call(
        functools.partial(_combine_kernel, nblk=n_tok // tb),
        grid_spec=grid_spec,
        out_shape=jax.ShapeDtypeStruct((n_tok, d_model), jnp.float32),
        compiler_params=pltpu.CompilerParams(
            dimension_semantics=("arbitrary",),
            vmem_limit_bytes=VMEM_LIMIT_BYTES),
        name="combine",
    )(lidx, gates, nbe, lb, hrow, extra, x1, norm_w, y_rows)


def _retention_tables(seq, hd, ts):
    half = hd // 2
    inv = ROPE_BASE ** (-jnp.arange(half, dtype=jnp.float32) / half)
    ang = jnp.arange(seq, dtype=jnp.float32)[:, None] * inv[None, :]
    cos, sin = jnp.cos(ang), jnp.sin(ang)
    cos2 = jnp.concatenate([cos, cos], axis=-1)
    sin2 = jnp.concatenate([-sin, sin], axis=-1)
    log_g = jnp.log1p(-jnp.exp2(-5.0 - jnp.arange(RET_HEADS, dtype=jnp.float32)))
    pos = jnp.arange(ts, dtype=jnp.float32)
    dist = jnp.abs(pos[:, None] - pos[None, :])
    chunk = jnp.arange(ts, dtype=jnp.int32) // CHUNK
    visible = chunk[None, :] <= chunk[:, None]
    dmat = jnp.where(visible[None], jnp.exp(log_g[:, None, None] * dist[None]), 0.0)
    qdec = jnp.exp(log_g[:, None] * (pos + 1.0)[None, :])
    kdec = jnp.exp(log_g[:, None] * (ts - 1.0 - pos)[None, :])
    qdec = jnp.broadcast_to(qdec[:, :, None], (RET_HEADS, ts, hd))
    kdec = jnp.broadcast_to(kdec[:, :, None], (RET_HEADS, ts, hd))
    tile_decay = tuple(math.exp(math.log1p(-2.0 ** (-5.0 - h)) * ts) for h in range(RET_HEADS))
    return cos2, sin2, dmat, qdec, kdec, tile_decay


def _routing_tables(nbe, n_tok, n_exp):
    tm, tb = ROW_GRANULE, TOK_BLOCK
    nal = nbe + (nbe & (STRIP_ALIGN - 1))
    region = jnp.sum(nal, axis=0)
    tiles_per = (region + tm - 1) // tm
    tile_end = jnp.cumsum(tiles_per)
    offs = (tile_end - tiles_per) * tm
    n_tiles = (n_tok * TOP_K + (n_tok // tb) * n_exp * (STRIP_ALIGN - 1)) // tm + n_exp
    hrow = offs[None, :] + jnp.cumsum(nal, axis=0) - nal
    padrow = offs + region
    padlen = tiles_per * tm - region
    as_i32 = lambda a: a.reshape(-1).astype(jnp.int32)
    return (n_tiles, as_i32(tile_end - tiles_per), as_i32(tiles_per), as_i32(tile_end[-1:]),
            as_i32(hrow), as_i32(jnp.sum(nal, axis=1) - tb * TOP_K), as_i32(padrow), as_i32(padlen))


def _layer(x, norm_mix_w, w_in, ret_norm_w, w_pool, pool_scale, w_out, norm_ffn_w,
           w_router, b_router, w_gu, b_gu, w_down, b_down, final_w):
    bsz, seq, d_model = x.shape
    n_tok = bsz * seq
    n_exp = w_router.shape[1]
    rw = ret_norm_w.shape[0]
    hd = rw // RET_HEADS
    ts = SEQ_TILE
    bf = jnp.bfloat16
    rpt = d_model // PACK // LANES
    assert d_model % (PACK * LANES) == 0 and (STRIP_ALIGN * rpt) % SUBLANES == 0
    assert seq % ts == 0 and ts % CHUNK == 0 and n_tok % TOK_BLOCK == 0
    assert n_tok % ROUTE_TILE == 0 and ROUTE_TILE % TOK_BLOCK == 0 and TOK_BLOCK % ROUTE_CHUNK == 0
    assert TOK_BLOCK & (TOK_BLOCK - 1) == 0 and ROW_GRANULE & (ROW_GRANULE - 1) == 0
    assert ROW_TILE % ROW_GRANULE == 0

    cos2, sin2, dmat, qdec, kdec, tile_decay = _retention_tables(seq, hd, ts)
    tri = jnp.arange(ROUTE_CHUNK, dtype=jnp.int32)
    utri = (tri[:, None] < tri[None, :]).astype(bf)

    x1, h2p = _mix(
        x, norm_mix_w[None], w_in.astype(bf), cos2, sin2, dmat, qdec, kdec, ret_norm_w[None],
        w_pool.astype(bf), pool_scale[None], w_out.astype(bf), norm_ffn_w[None], tile_decay)
    eri = jnp.arange(n_exp, dtype=jnp.int32)
    ltri = (eri[None, :] < eri[:, None]).astype(bf)
    lidx, gate, nbe, lb, _ = _route(h2p, w_router.T.astype(bf), b_router[:, None], utri, ltri)
    lidx = lidx.reshape(-1)
    nbe = nbe[:, :, 0].astype(jnp.int32)
    lb = lb[:, :, 0].astype(jnp.int32).reshape(-1)

    (n_tiles, tile_start, tile_count, n_used, hrow, extra, padrow, padlen) = _routing_tables(nbe, n_tok, n_exp)
    nbe = nbe.reshape(-1)

    xs = _dispatch(lidx, nbe, lb, hrow, extra, padrow, padlen, n_used, h2p, n_tiles * ROW_GRANULE, d_model)
    y_rows = _experts(tile_start, tile_count, xs, w_gu, b_gu, w_down, b_down)
    out = _combine(lidx, gate.reshape(-1), nbe, lb, hrow, extra, x1.reshape(n_tok, d_model), final_w[None], y_rows)
    return out.reshape(bsz, seq, d_model)


def kernel(x, norm_mix_w, w_in, ret_norm_w, w_pool, pool_scale, w_out, norm_ffn_w, w_router, b_router,
           w_gu, b_gu, w_down, b_down, norm_final_w):
    depth = norm_mix_w.shape[0]
    for l in range(depth):
        last = l == depth - 1
        assert last, "stacked layers need an un-normalised combine output"
        x = _layer(x, norm_mix_w[l], w_in[l], ret_norm_w[l], w_pool[l], pool_scale[l], w_out[l],
                   norm_ffn_w[l], w_router[l], b_router[l], w_gu[l], b_gu[l], w_down[l], b_down[l],
                   norm_final_w)
    return x
```

```python
import functools
import math

import jax
import jax.numpy as jnp
from jax import lax
from jax.experimental import pallas as pl
from jax.experimental.pallas import tpu as pltpu

CHUNK = 64
RET_HEADS = 4
POOL_WINDOWS = (2, 4, 8, 16)
ROPE_BASE = 10000.0
TOP_K = 4
SWIGLU_LIMIT = 7.0
SWIGLU_ALPHA = 1.702
EPS = 1e-6

LANES = 128
SUBLANES = 8
VMEM_LIMIT_BYTES = 56 * 1024 * 1024

SEQ_TILE = 256
MIX_STREAMS = 4
ROUTE_TILE = 2048
ROUTE_CHUNK = 256
ROW_GRANULE = 128
ROW_TILE = 512
TOK_BLOCK = 1024
POOL_HALO = 16
PACK = 2
STRIP_ALIGN = 2
GATHER_UNROLL = 16
STRIP_UNROLL = 4
RARE_STRIP_BIT = 8


def _rms(x, w):
    return x * lax.rsqrt(jnp.mean(x * x, axis=-1, keepdims=True) + EPS) * w


def _bf16_bits_hi(x):
    return lax.bitcast_convert_type(x.astype(jnp.bfloat16).astype(jnp.float32), jnp.uint32)


def _pack_rows(x, out_ref, n_rows):
    half = x.shape[1] // PACK
    packed = (_bf16_bits_hi(x[:, :half]) >> 16) | _bf16_bits_hi(x[:, half:])
    rpt = half // LANES
    for si in range(rpt):
        out_ref[pl.ds(si, n_rows, stride=rpt), :] = packed[:, si * LANES:(si + 1) * LANES]


def _unpack_words(u):
    lo = lax.bitcast_convert_type(u << 16, jnp.float32)
    hi = lax.bitcast_convert_type(u & jnp.uint32(0xFFFF0000), jnp.float32)
    return lo, hi


def _unpack_rows(g_ref, n_rows, rpt):
    lo, hi = [], []
    for si in range(rpt):
        l, h = _unpack_words(g_ref[pl.ds(si, n_rows, stride=rpt), :])
        lo.append(l.astype(jnp.bfloat16))
        hi.append(h.astype(jnp.bfloat16))
    return jnp.concatenate(lo + hi, axis=-1)


def _mix_kernel(x_ref, nmw_ref, win_ref, rope_ref, rope_base_ref, dmat_ref, qdec_ref, kdec_ref,
                rnw_ref, wpool_ref, pscale_ref, wout_ref, nfw_ref,
                x1_ref, h2p_ref,
                state_ref, p1_ref, p2_ref, p4_ref, p8_ref,
                *, tile_decay):
    j = pl.program_id(1)
    n_str = x_ref.shape[0]
    ts = x_ref.shape[2]
    n_heads = RET_HEADS
    rw = rnw_ref.shape[1]
    hd = rw // n_heads
    pgw = wpool_ref.shape[1]
    halo = POOL_HALO
    pool_refs = (p1_ref, p2_ref, p4_ref, p8_ref)

    @pl.when(j == 0)
    def _():
        state_ref[...] = jnp.zeros_like(state_ref)
        for r in pool_refs:
            r[:, 0:halo, :] = jnp.zeros((n_str, halo, r.shape[2]), jnp.float32)

    @pl.when(j > 0)
    def _():
        for r in pool_refs:
            r[:, 0:halo, :] = r[:, ts:ts + halo, :]

    x = jnp.concatenate([x_ref[si, 0] for si in range(n_str)], axis=0)
    h = _rms(x, nmw_ref[...]).astype(jnp.bfloat16)
    z_all = jnp.dot(h, win_ref[...], preferred_element_type=jnp.float32)

    rot = rope_ref[...]
    base = rope_base_ref[0]
    cos = base[0:1] * rot[0] - base[1:2] * rot[1]
    sin_plain = base[1:2] * rot[0] + base[0:1] * rot[1]
    half_sign = jnp.where(lax.broadcasted_iota(jnp.int32, (1, hd), 1) < hd // 2, -1.0, 1.0)
    sin = sin_plain * half_sign
    k_scale = hd ** -0.5
    tpos = (j * ts + lax.broadcasted_iota(jnp.int32, (ts, pgw), 0) + 1).astype(jnp.float32)

    mix_rows = []
    for si in range(n_str):
        z = z_all[si * ts:(si + 1) * ts]
        parts = []
        for hh in range(n_heads):
            q = z[:, hh * hd:(hh + 1) * hd]
            k = z[:, rw + hh * hd:rw + (hh + 1) * hd]
            v = z[:, 2 * rw + hh * hd:2 * rw + (hh + 1) * hd]
            g = z[:, 3 * rw + hh * hd:3 * rw + (hh + 1) * hd]
            q = q * cos + pltpu.roll(q, hd // 2, 1) * sin
            k = (k * cos + pltpu.roll(k, hd // 2, 1) * sin) * k_scale
            qb = q.astype(jnp.bfloat16)
            kb = k.astype(jnp.bfloat16)
            vb = v.astype(jnp.bfloat16)
            s = lax.dot_general(qb, kb, (((1,), (1,)), ((), ())), preferred_element_type=jnp.float32)
            pmat = (s * dmat_ref[hh]).astype(jnp.bfloat16)
            o = jnp.dot(pmat, vb, preferred_element_type=jnp.float32)
            state = state_ref[si, hh]
            qd = (q * qdec_ref[hh]).astype(jnp.bfloat16)
            o = o + jnp.dot(qd, state.astype(jnp.bfloat16), preferred_element_type=jnp.float32)
            kd = (k * kdec_ref[hh]).astype(jnp.bfloat16)
            u = lax.dot_general(kd, vb, (((0,), (0,)), ((), ())), preferred_element_type=jnp.float32)
            state_ref[si, hh] = state * tile_decay[hh] + u
            o = o * lax.rsqrt(jnp.mean(o * o, axis=-1, keepdims=True) + EPS)
            o = o * rnw_ref[:, hh * hd:(hh + 1) * hd] * (g * (1.0 / (1.0 + jnp.exp(-g))))
            parts.append(o.astype(jnp.bfloat16))

        p = z[:, 4 * rw:]
        p1_ref[si, halo:halo + ts, :] = p
        s2 = p + p1_ref[si, halo - 1:halo - 1 + ts, :]
        p2_ref[si, halo:halo + ts, :] = s2[:, pgw:]
        s4 = s2[:, pgw:] + p2_ref[si, halo - 2:halo - 2 + ts, :]
        p4_ref[si, halo:halo + ts, :] = s4[:, pgw:]
        s8 = s4[:, pgw:] + p4_ref[si, halo - 4:halo - 4 + ts, :]
        p8_ref[si, halo:halo + ts, :] = s8[:, pgw:]
        s16 = s8[:, pgw:] + p8_ref[si, halo - 8:halo - 8 + ts, :]
        sums = (s2[:, :pgw], s4[:, :pgw], s8[:, :pgw], s16)
        for gi, w in enumerate(POOL_WINDOWS):
            cnt = jnp.minimum(tpos, float(w))
            pooled = sums[gi] / cnt - p[:, gi * pgw:(gi + 1) * pgw]
            mixed = jnp.dot(pooled.astype(jnp.bfloat16), wpool_ref[gi], preferred_element_type=jnp.float32)
            mixed = mixed * pscale_ref[:, gi * pgw:(gi + 1) * pgw]
            parts.append(mixed.astype(jnp.bfloat16))
        mix_rows.append(jnp.concatenate(parts, axis=-1))

    mix = jnp.concatenate(mix_rows, axis=0)
    x1 = x + jnp.dot(mix, wout_ref[...], preferred_element_type=jnp.float32)
    h2 = _rms(x1, nfw_ref[...])
    for si in range(n_str):
        x1_ref[si, 0] = x1[si * ts:(si + 1) * ts]
        _pack_rows(h2[si * ts:(si + 1) * ts], h2p_ref.at[si, 0], ts)


def _mix(x, nmw, win_b, rope, rope_base, dmat, qdec, kdec, rnw, wpool_b, pscale, wout_b, nfw, tile_decay):
    bsz, seq, d_model = x.shape
    ts = SEQ_TILE
    n_tok = bsz * seq
    rw = rnw.shape[1]
    hd = rw // RET_HEADS
    pw = pscale.shape[1]
    pgw = pw // len(POOL_WINDOWS)
    nj = seq // ts
    rows_per_tok = d_model // PACK // LANES

    def const(shape):
        return pl.BlockSpec(shape, lambda b, j: (0,) * len(shape))

    n_str = MIX_STREAMS if bsz % MIX_STREAMS == 0 else 1
    per = bsz // n_str
    in_specs = [
        pl.BlockSpec((n_str, 1, ts, d_model), lambda q, j: (0, q, j, 0)),
        const(nmw.shape), const(win_b.shape),
        const(rope.shape),
        pl.BlockSpec((1, 2, hd), lambda q, j: (j, 0, 0)),
        const(dmat.shape), const(qdec.shape), const(kdec.shape),
        const(rnw.shape), const(wpool_b.shape), const(pscale.shape), const(wout_b.shape),
        const(nfw.shape),
    ]
    out_shape = [
        jax.ShapeDtypeStruct((n_str, per, seq, d_model), jnp.float32),
        jax.ShapeDtypeStruct((n_str, per, seq * rows_per_tok, LANES), jnp.uint32),
    ]
    out_specs = [
        pl.BlockSpec((n_str, 1, ts, d_model), lambda q, j: (0, q, j, 0)),
        pl.BlockSpec((n_str, 1, ts * rows_per_tok, LANES), lambda q, j: (0, q, j, 0)),
    ]
    scratch = [
        pltpu.VMEM((n_str, RET_HEADS, hd, hd), jnp.float32),
        pltpu.VMEM((n_str, POOL_HALO + ts, pw), jnp.float32),
        pltpu.VMEM((n_str, POOL_HALO + ts, pw - pgw), jnp.float32),
        pltpu.VMEM((n_str, POOL_HALO + ts, pw - 2 * pgw), jnp.float32),
        pltpu.VMEM((n_str, POOL_HALO + ts, pw - 3 * pgw), jnp.float32),
    ]
    x1, h2p = pl.pallas_call(
        functools.partial(_mix_kernel, tile_decay=tile_decay),
        grid=(per, nj),
        in_specs=in_specs,
        out_specs=out_specs,
        out_shape=out_shape,
        scratch_shapes=scratch,
        compiler_params=pltpu.CompilerParams(
            dimension_semantics=("arbitrary", "arbitrary"),
            vmem_limit_bytes=VMEM_LIMIT_BYTES),
        name="mix",
    )(x.reshape(n_str, per, seq, d_model), nmw, win_b, rope, rope_base, dmat, qdec, kdec, rnw, wpool_b,
      pscale, wout_b, nfw)
    return x1.reshape(bsz, seq, d_model), h2p.reshape(n_tok * rows_per_tok, LANES)


def _route_kernel(h2p_ref, wrt_ref, brt_ref, utri_ref, ltri_ref,
                  lidx_ref, gate_ref, nbe_ref, lb_ref, cnt_ref, carry_ref):
    i = pl.program_id(0)
    n_exp, d_model = wrt_ref.shape
    rpt = d_model // PACK // LANES
    tr = h2p_ref.shape[0] // rpt
    sub = utri_ref.shape[0]
    chunks_per_block = TOK_BLOCK // sub

    @pl.when(i == 0)
    def _():
        carry_ref[...] = jnp.zeros_like(carry_ref)

    h2b = _unpack_rows(h2p_ref, tr, rpt)
    logits = lax.dot_general(wrt_ref[...], h2b, (((1,), (1,)), ((), ())),
                             preferred_element_type=jnp.float32) + brt_ref[...]
    eiota = lax.broadcasted_iota(jnp.int32, (n_exp, tr), 0)
    work = logits
    vals, sels = [], []
    for _ in range(TOP_K):
        m = jnp.max(work, axis=0, keepdims=True)
        idx = jnp.min(jnp.where(work == m, eiota, n_exp), axis=0, keepdims=True)
        sel = eiota == idx
        vals.append(m)
        sels.append(sel)
        work = jnp.where(sel, -jnp.inf, work)
    ex = [jnp.exp(vk - vals[0]) for vk in vals]
    den = ex[0] + ex[1] + ex[2] + ex[3]
    gate_ref[...] = jnp.concatenate([e / den for e in ex], axis=0)

    member = jnp.zeros((n_exp, tr), jnp.float32)
    for sel in sels:
        member = member + sel.astype(jnp.float32)
    carry = carry_ref[...]
    places = []
    for blk in range(tr // TOK_BLOCK):
        block_start = carry
        befores = []
        for c in range(blk * chunks_per_block, (blk + 1) * chunks_per_block):
            mc = member[:, c * sub:(c + 1) * sub]
            bc = jnp.dot(mc.astype(jnp.bfloat16), utri_ref[...], preferred_element_type=jnp.float32)
            befores.append(bc + carry[:, 0:1])
            carry = carry + jnp.sum(mc, axis=1, keepdims=True)
        nbe = carry - block_start
        nal = nbe + (nbe - STRIP_ALIGN * jnp.floor(nbe * (1.0 / STRIP_ALIGN)))
        hi = jnp.floor(nal * (1.0 / 32.0))
        lo = nal - 32.0 * hi
        lb = (32.0 * jnp.dot(ltri_ref[...], hi.astype(jnp.bfloat16), preferred_element_type=jnp.float32)
              + jnp.dot(ltri_ref[...], lo.astype(jnp.bfloat16), preferred_element_type=jnp.float32))
        nbe_ref[blk] = nbe
        lb_ref[blk] = lb
        place = jnp.concatenate(befores, axis=1) + (lb[:, 0:1] - block_start[:, 0:1])
        cols = slice(blk * TOK_BLOCK, (blk + 1) * TOK_BLOCK)
        places.append(jnp.concatenate(
            [jnp.sum(jnp.where(sel[:, cols], place, 0.0), axis=0, keepdims=True) for sel in sels], axis=0))
    lidx_ref[...] = (jnp.concatenate(places, axis=1) * float(rpt)).astype(jnp.int32)
    carry_ref[...] = carry
    cnt_ref[...] = carry


def _route(h2p, wrt_b, brt, utri, ltri):
    n_exp, d_model = wrt_b.shape
    rpt = d_model // PACK // LANES
    n_tok = h2p.shape[0] // rpt
    tr = ROUTE_TILE
    bps = tr // TOK_BLOCK

    def const(shape):
        return pl.BlockSpec(shape, lambda i: (0,) * len(shape))

    out_shape = [
        jax.ShapeDtypeStruct((TOP_K, n_tok), jnp.int32),
        jax.ShapeDtypeStruct((TOP_K, n_tok), jnp.float32),
        jax.ShapeDtypeStruct((n_tok // TOK_BLOCK, n_exp, LANES), jnp.float32),
        jax.ShapeDtypeStruct((n_tok // TOK_BLOCK, n_exp, LANES), jnp.float32),
        jax.ShapeDtypeStruct((n_exp, LANES), jnp.float32),
    ]
    out_specs = [
        pl.BlockSpec((TOP_K, tr), lambda i: (0, i)),
        pl.BlockSpec((TOP_K, tr), lambda i: (0, i)),
        pl.BlockSpec((bps, n_exp, LANES), lambda i: (i, 0, 0)),
        pl.BlockSpec((bps, n_exp, LANES), lambda i: (i, 0, 0)),
        pl.BlockSpec((n_exp, LANES), lambda i: (0, 0)),
    ]
    return pl.pallas_call(
        _route_kernel,
        grid=(n_tok // tr,),
        in_specs=[pl.BlockSpec((tr * rpt, LANES), lambda i: (i, 0)),
                  const(wrt_b.shape), const(brt.shape), const(utri.shape), const(ltri.shape)],
        out_specs=out_specs,
        out_shape=out_shape,
        scratch_shapes=[pltpu.VMEM((n_exp, LANES), jnp.float32)],
        compiler_params=pltpu.CompilerParams(
            dimension_semantics=("arbitrary",),
            vmem_limit_bytes=VMEM_LIMIT_BYTES),
        name="route",
    )(h2p, wrt_b, brt, utri, ltri)


def _for_set_bits(n, max_bit, fn):
    def piece(bit):
        size = 1 << bit
        offset = (n >> (bit + 1)) << (bit + 1)

        @pl.when((n & size) != 0)
        def _():
            fn(offset, size)

    min_bit = STRIP_ALIGN.bit_length() - 1
    rare_bit = min(max_bit + 1, RARE_STRIP_BIT)

    @pl.when(n >= (1 << rare_bit))
    def _():
        for bit in range(max_bit, rare_bit - 1, -1):
            piece(bit)
    for bit in range(rare_bit - 1, min_bit - 1, -1):
        piece(bit)


def _aligned_len(n):
    return n + (n & (STRIP_ALIGN - 1))


def _for_strips(nbe_ref, lb_ref, hrow_ref, blk, n_exp, max_bit, fn):
    def body(i, carry):
        for u in range(STRIP_UNROLL):
            k = blk * n_exp + i * STRIP_UNROLL + u
            n = _aligned_len(nbe_ref[k])
            lb = lb_ref[k]
            hrow = hrow_ref[k]
            _for_set_bits(n, max_bit, lambda offset, size: fn(lb + offset, hrow + offset, size))
        return carry
    assert n_exp % STRIP_UNROLL == 0
    lax.fori_loop(0, n_exp // STRIP_UNROLL, body, 0)


def _dispatch_kernel(lidx_ref, nbe_ref, lb_ref, hrow_ref, extra_ref, padrow_ref, padlen_ref, nused_ref,
                     h2p_ref, xs_hbm, stage_ref, zero_ref, sem, zsem, *, nblk):
    b = pl.program_id(0)
    rpt = zero_ref.shape[0] // ROW_GRANULE
    tb = h2p_ref.shape[0] // rpt
    n_exp = padrow_ref.shape[0]
    n_tok = lidx_ref.shape[0] // TOP_K
    slot_rows = stage_ref.shape[0] // 2
    max_bit = tb.bit_length() - 1
    slot = b % 2

    def strip_copy(sl, staging_row, sorted_row, size):
        src = pl.multiple_of(sl * slot_rows + staging_row * rpt, SUBLANES)
        dst = pl.multiple_of(sorted_row * rpt, SUBLANES)
        return pltpu.make_async_copy(stage_ref.at[pl.ds(src, size * rpt), :],
                                     xs_hbm.at[pl.ds(dst, size * rpt), :], sem.at[sl])

    def start_strips(blk, sl):
        _for_strips(nbe_ref, lb_ref, hrow_ref, blk, n_exp, max_bit,
                    lambda srow, hrow, size: strip_copy(sl, srow, hrow, size).start())

    def wait_strips(blk, sl):
        def sized_copy(offset, size):
            return pltpu.make_async_copy(stage_ref.at[pl.ds(sl * slot_rows, size * rpt), :],
                                         xs_hbm.at[pl.ds(0, size * rpt), :], sem.at[sl])
        sized_copy(0, tb * TOP_K).wait()
        _for_set_bits(extra_ref[blk], n_exp.bit_length() - 1, lambda o, s: sized_copy(o, s).wait())

    @pl.when(b >= 2)
    def _():
        wait_strips(b - 2, slot)

    def scatter_block(static_slot):
        def scatter_group(c, carry):
            tok0 = c * GATHER_UNROLL
            idx0 = b * tb + tok0
            for u in range(GATHER_UNROLL):
                row = h2p_ref[pl.ds(pl.multiple_of((tok0 + u) * rpt, rpt), rpt), :]
                for k in range(TOP_K):
                    dst = pl.multiple_of(lidx_ref[idx0 + (k * n_tok + u)], rpt)
                    stage_ref[pl.ds(static_slot * slot_rows + dst, rpt), :] = row
            return carry
        lax.fori_loop(0, tb // GATHER_UNROLL, scatter_group, 0)

    for static_slot in range(2):
        pl.when(slot == static_slot)(functools.partial(scatter_block, static_slot))

    def zero_tail(e, carry):
        n = nbe_ref[b * n_exp + e]

        @pl.when((n & (STRIP_ALIGN - 1)) != 0)
        def _():
            dst = pl.multiple_of(slot * slot_rows + (lb_ref[b * n_exp + e] + n) * rpt, rpt)
            stage_ref[pl.ds(dst, rpt), :] = jnp.zeros((rpt, LANES), jnp.uint32)
        return carry
    lax.fori_loop(0, n_exp, zero_tail, 0)

    start_strips(b, slot)

    @pl.when(b == nblk - 1)
    def _():
        if nblk >= 2:
            wait_strips(b - 1, 1 - slot)
        wait_strips(b, slot)
        zero_ref[...] = jnp.zeros_like(zero_ref)
        pad_bit = ROW_GRANULE.bit_length() - 2

        def pad_copy(e, offset, size):
            dst = pl.multiple_of((padrow_ref[e] + offset) * rpt, SUBLANES)
            return pltpu.make_async_copy(zero_ref.at[pl.ds(0, size * rpt), :],
                                         xs_hbm.at[pl.ds(dst, size * rpt), :], zsem)

        def pad_start(e, carry):
            _for_set_bits(padlen_ref[e], pad_bit, lambda o, s: pad_copy(e, o, s).start())
            return carry

        def pad_wait(e, carry):
            _for_set_bits(padlen_ref[e], pad_bit, lambda o, s: pad_copy(e, o, s).wait())
            return carry
        lax.fori_loop(0, n_exp, pad_start, 0)

        def tail_copy(i):
            dst = pl.multiple_of(i * (ROW_GRANULE * rpt), SUBLANES)
            return pltpu.make_async_copy(zero_ref, xs_hbm.at[pl.ds(dst, ROW_GRANULE * rpt), :], zsem)

        def tail_start(i, carry):
            tail_copy(i).start()
            return carry

        def tail_wait(i, carry):
            tail_copy(i).wait()
            return carry
        n_gran = xs_hbm.shape[0] // (ROW_GRANULE * rpt)
        lax.fori_loop(nused_ref[0], n_gran, tail_start, 0)
        lax.fori_loop(0, n_exp, pad_wait, 0)
        lax.fori_loop(nused_ref[0], n_gran, tail_wait, 0)


def _dispatch(lidx, nbe, lb, hrow, extra, padrow, padlen, n_used, h2p, n_rows, d_model):
    tb = TOK_BLOCK
    rpt = d_model // PACK // LANES
    n_tok = h2p.shape[0] // rpt
    n_exp = padrow.shape[0]
    slot_rows = (tb * TOP_K + n_exp * (STRIP_ALIGN - 1)) * rpt
    grid_spec = pltpu.PrefetchScalarGridSpec(
        num_scalar_prefetch=8,
        grid=(n_tok // tb,),
        in_specs=[pl.BlockSpec((tb * rpt, LANES), lambda i, *_: (i, 0))],
        out_specs=pl.BlockSpec(memory_space=pl.ANY),
        scratch_shapes=[pltpu.VMEM((2 * slot_rows, LANES), jnp.uint32),
                        pltpu.VMEM((ROW_GRANULE * rpt, LANES), jnp.uint32),
                        pltpu.SemaphoreType.DMA((2,)),
                        pltpu.SemaphoreType.DMA],
    )
    return pl.pallas_call(
        functools.partial(_dispatch_kernel, nblk=n_tok // tb),
        grid_spec=grid_spec,
        out_shape=jax.ShapeDtypeStruct((n_rows * rpt, LANES), jnp.uint32),
        compiler_params=pltpu.CompilerParams(
            dimension_semantics=("arbitrary",),
            vmem_limit_bytes=VMEM_LIMIT_BYTES),
        name="dispatch",
    )(lidx, nbe, lb, hrow, extra, padrow, padlen, n_used, h2p)


def _experts_kernel(tstart_ref, tcount_ref, wgu_hbm, bgu_ref, wd_hbm, bd_ref, xs_hbm,
                    y_hbm, wgu_f, wd_f, wgu_b, wd_b, xbuf, ybuf, pend_ref, wsem, xsem, ysem, *, n_exp):
    e = pl.program_id(0)
    d_ff = wd_f.shape[0]
    d_model = wd_f.shape[1]
    rpt = d_model // PACK // LANES
    gran_rows = ROW_GRANULE * rpt
    gpt = ROW_TILE // ROW_GRANULE
    n_gran = y_hbm.shape[0] // gran_rows
    g0 = tstart_ref[e]
    ng = tcount_ref[e]
    nt = ng // gpt
    tail = ng - nt * gpt

    def hbm_rows(ref, gran, count):
        return ref.at[pl.ds(pl.multiple_of(gran * gran_rows, SUBLANES), count * gran_rows), :]

    def x_copy(gran, sl, count=gpt):
        return pltpu.make_async_copy(hbm_rows(xs_hbm, gran, count),
                                     xbuf.at[sl, pl.ds(0, count * gran_rows), :], xsem.at[sl])

    def y_copy(gran, sl, count=gpt):
        return pltpu.make_async_copy(ybuf.at[sl, pl.ds(0, count * gran_rows), :],
                                     hbm_rows(y_hbm, gran, count), ysem.at[sl])

    def mlp(sl, tm):
        x = _unpack_rows(xbuf.at[sl], tm, rpt)
        gu = jnp.dot(x, wgu_b[...], preferred_element_type=jnp.float32) + bgu_ref[0]
        x_glu = jnp.minimum(gu[:, :d_ff], SWIGLU_LIMIT)
        x_lin = jnp.clip(gu[:, d_ff:], -SWIGLU_LIMIT, SWIGLU_LIMIT)
        act = x_glu * (1.0 / (1.0 + jnp.exp(-SWIGLU_ALPHA * x_glu))) * (x_lin + 1.0)
        y = jnp.dot(act.astype(jnp.bfloat16), wd_b[...], preferred_element_type=jnp.float32) + bd_ref[0]
        _pack_rows(y, ybuf.at[sl], tm)

    def weight_copies(ex):
        return (pltpu.make_async_copy(wgu_hbm.at[ex], wgu_f, wsem.at[0]),
                pltpu.make_async_copy(wd_hbm.at[ex], wd_f, wsem.at[1]))

    tail_slot = 2
    tail_gran = g0 + nt * gpt

    def wait_pending_tail_y():
        for count in range(1, gpt):
            @pl.when(pend_ref[0] == count)
            def _():
                y_copy(0, tail_slot, count).wait()

    @pl.when(e == 0)
    def _():
        pend_ref[0] = 0

        @pl.when(nt > 0)
        def _():
            x_copy(g0, 0).start()

    for count in range(1, gpt):
        @pl.when(tail == count)
        def _():
            x_copy(tail_gran, tail_slot, count).start()

    @pl.when(e == 0)
    def _():
        for c in weight_copies(0):
            c.start(priority=1)
    for c in weight_copies(e):
        c.wait()
    wgu_b[...] = wgu_f[...].astype(jnp.bfloat16)
    wd_b[...] = wd_f[...].astype(jnp.bfloat16)

    @pl.when(e + 1 < n_exp)
    def _():
        for c in weight_copies(e + 1):
            c.start(priority=1)

    @pl.when(ng > 0)
    def _():
        def tile_step(j, carry):
            sl = j % 2
            gran = g0 + j * gpt

            @pl.when(j + 1 < nt)
            def _():
                x_copy(gran + gpt, 1 - sl).start()
            x_copy(gran, sl).wait()

            @pl.when(j >= 2)
            def _():
                y_copy(gran - 2 * gpt, sl).wait()
            mlp(sl, ROW_TILE)
            y_copy(gran, sl).start()
            return carry
        lax.fori_loop(0, nt, tile_step, 0)

        @pl.when(nt >= 2)
        def _():
            y_copy(g0 + (nt - 2) * gpt, nt % 2).wait()

        @pl.when(nt >= 1)
        def _():
            y_copy(g0 + (nt - 1) * gpt, (nt - 1) % 2).wait()

    @pl.when(e + 1 < n_exp)
    def _():
        nxt = jnp.minimum(e + 1, n_exp - 1)

        @pl.when(tcount_ref[nxt] >= gpt)
        def _():
            x_copy(tstart_ref[nxt], 0).start()

    for count in range(1, gpt):
        @pl.when(tail == count)
        def _():
            x_copy(tail_gran, tail_slot, count).wait()
            wait_pending_tail_y()
            mlp(tail_slot, count * ROW_GRANULE)
            y_copy(tail_gran, tail_slot, count).start()
            pend_ref[0] = count

    @pl.when(e == n_exp - 1)
    def _():
        wait_pending_tail_y()
        ybuf[0, 0:gran_rows, :] = jnp.zeros((gran_rows, LANES), ybuf.dtype)

        def tail_start(i, carry):
            y_copy(i, 0, 1).start()
            return carry

        def tail_wait(i, carry):
            y_copy(i, 0, 1).wait()
            return carry
        lax.fori_loop(g0 + ng, n_gran, tail_start, 0)
        lax.fori_loop(g0 + ng, n_gran, tail_wait, 0)


def _experts(tile_start, tile_count, xs, w_gu, b_gu, w_down, b_down):
    tm = ROW_TILE
    n_exp, d_model, d_gu = w_gu.shape
    d_ff = w_down.shape[1]
    rpt = d_model // PACK // LANES
    grid_spec = pltpu.PrefetchScalarGridSpec(
        num_scalar_prefetch=2,
        grid=(n_exp,),
        in_specs=[
            pl.BlockSpec(memory_space=pl.ANY),
            pl.BlockSpec((1, 1, d_gu), lambda e, *_: (e, 0, 0)),
            pl.BlockSpec(memory_space=pl.ANY),
            pl.BlockSpec((1, 1, d_model), lambda e, *_: (e, 0, 0)),
            pl.BlockSpec(memory_space=pl.ANY),
        ],
        out_specs=pl.BlockSpec(memory_space=pl.ANY),
        scratch_shapes=[pltpu.VMEM((d_model, d_gu), jnp.float32),
                        pltpu.VMEM((d_ff, d_model), jnp.float32),
                        pltpu.VMEM((d_model, d_gu), jnp.bfloat16),
                        pltpu.VMEM((d_ff, d_model), jnp.bfloat16),
                        pltpu.VMEM((3, tm * rpt, LANES), jnp.uint32),
                        pltpu.VMEM((3, tm * rpt, LANES), jnp.uint32),
                        pltpu.SMEM((1,), jnp.int32),
                        pltpu.SemaphoreType.DMA((2,)),
                        pltpu.SemaphoreType.DMA((3,)),
                        pltpu.SemaphoreType.DMA((3,))],
    )
    return pl.pallas_call(
        functools.partial(_experts_kernel, n_exp=n_exp),
        grid_spec=grid_spec,
        out_shape=jax.ShapeDtypeStruct(xs.shape, jnp.uint32),
        compiler_params=pltpu.CompilerParams(
            dimension_semantics=("arbitrary",),
            vmem_limit_bytes=VMEM_LIMIT_BYTES),
        name="experts",
    )(tile_start, tile_count, w_gu, b_gu.reshape(n_exp, 1, d_gu), w_down,
      b_down.reshape(n_exp, 1, d_model), xs)


def _combine_kernel(lidx_ref, gate_ref, nbe_ref, lb_ref, hrow_ref, extra_ref,
                    x1_ref, nw_ref, y_hbm, out_ref, stage_ref, alo_ref, ahi_ref, sem, *, nblk):
    b = pl.program_id(0)
    tb = x1_ref.shape[0]
    rpt = alo_ref.shape[0] // tb
    n_exp = nbe_ref.shape[0] // nblk
    n_tok = lidx_ref.shape[0] // TOP_K
    slot_rows = stage_ref.shape[0] // 2
    max_bit = tb.bit_length() - 1
    slot = b % 2

    def strip_copy(sl, staging_row, sorted_row, size):
        src = pl.multiple_of(sorted_row * rpt, SUBLANES)
        dst = pl.multiple_of(sl * slot_rows + staging_row * rpt, SUBLANES)
        return pltpu.make_async_copy(y_hbm.at[pl.ds(src, size * rpt), :],
                                     stage_ref.at[pl.ds(dst, size * rpt), :], sem.at[sl])

    def start_strips(blk, sl):
        _for_strips(nbe_ref, lb_ref, hrow_ref, blk, n_exp, max_bit,
                    lambda srow, hrow, size: strip_copy(sl, srow, hrow, size).start())

    @pl.when(b == 0)
    def _():
        start_strips(0, 0)

    @pl.when(b + 1 < nblk)
    def _():
        start_strips(b + 1, 1 - slot)

    def sized_copy(offset, size):
        return pltpu.make_async_copy(y_hbm.at[pl.ds(0, size * rpt), :],
                                     stage_ref.at[pl.ds(slot * slot_rows, size * rpt), :], sem.at[slot])
    sized_copy(0, tb * TOP_K).wait()
    _for_set_bits(extra_ref[b], n_exp.bit_length() - 1, lambda o, s: sized_copy(o, s).wait())

    def gather_block(static_slot):
        def gather_group(c, carry):
            tok0 = c * GATHER_UNROLL
            idx0 = b * tb + tok0
            for u in range(GATHER_UNROLL):
                lo = hi = None
                for k in range(TOP_K):
                    src = pl.multiple_of(lidx_ref[idx0 + (k * n_tok + u)], rpt)
                    g = gate_ref[idx0 + (k * n_tok + u)]
                    l, h = _unpack_words(stage_ref[pl.ds(static_slot * slot_rows + src, rpt), :])
                    lo = g * l if lo is None else lo + g * l
                    hi = g * h if hi is None else hi + g * h
                dst = pl.multiple_of((tok0 + u) * rpt, rpt)
                alo_ref[pl.ds(dst, rpt), :] = lo
                ahi_ref[pl.ds(dst, rpt), :] = hi
            return carry
        lax.fori_loop(0, tb // GATHER_UNROLL, gather_group, 0)

    for static_slot in range(2):
        pl.when(slot == static_slot)(functools.partial(gather_block, static_slot))

    parts = [alo_ref[pl.ds(si, tb, stride=rpt), :] for si in range(rpt)]
    parts += [ahi_ref[pl.ds(si, tb, stride=rpt), :] for si in range(rpt)]
    x2 = x1_ref[...] + jnp.concatenate(parts, axis=-1)
    out_ref[...] = _rms(x2, nw_ref[...])


def _combine(lidx, gates, nbe, lb, hrow, extra, x1, norm_w, y_rows):
    tb = TOK_BLOCK
    n_tok, d_model = x1.shape
    rpt = d_model // PACK // LANES
    n_exp = nbe.shape[0] // (n_tok // tb)
    slot_rows = (tb * TOP_K + n_exp * (STRIP_ALIGN - 1)) * rpt
    grid_spec = pltpu.PrefetchScalarGridSpec(
        num_scalar_prefetch=6,
        grid=(n_tok // tb,),
        in_specs=[
            pl.BlockSpec((tb, d_model), lambda i, *_: (i, 0)),
            pl.BlockSpec((1, d_model), lambda i, *_: (0, 0)),
            pl.BlockSpec(memory_space=pl.ANY),
        ],
        out_specs=pl.BlockSpec((tb, d_model), lambda i, *_: (i, 0)),
        scratch_shapes=[pltpu.VMEM((2 * slot_rows, LANES), jnp.uint32),
                        pltpu.VMEM((tb * rpt, LANES), jnp.float32),
                        pltpu.VMEM((tb * rpt, LANES), jnp.float32),
                        pltpu.SemaphoreType.DMA((2,))],
    )
    return pl.pallas_call(
        functools.partial(_combine_kernel, nblk=n_tok // tb),
        grid_spec=grid_spec,
        out_shape=jax.ShapeDtypeStruct((n_tok, d_model), jnp.float32),
        compiler_params=pltpu.CompilerParams(
            dimension_semantics=("arbitrary",),
            vmem_limit_bytes=VMEM_LIMIT_BYTES),
        name="combine",
    )(lidx, gates, nbe, lb, hrow, extra, x1, norm_w, y_rows)


def _retention_tables(seq, hd, ts):
    half = hd // 2
    inv = ROPE_BASE ** (-jnp.arange(half, dtype=jnp.float32) / half)
    inv2 = jnp.concatenate([inv, inv])

    def cos_sin(positions):
        ang = positions[:, None] * inv2[None, :]
        return jnp.stack([jnp.cos(ang), jnp.sin(ang)], axis=1)
    rope = cos_sin(jnp.arange(ts, dtype=jnp.float32)).transpose(1, 0, 2)
    rope_base = cos_sin(jnp.arange(seq // ts, dtype=jnp.float32) * ts)
    log_g = jnp.log1p(-jnp.exp2(-5.0 - jnp.arange(RET_HEADS, dtype=jnp.float32)))
    pos = jnp.arange(ts, dtype=jnp.float32)
    dist = jnp.abs(pos[:, None] - pos[None, :])
    chunk = jnp.arange(ts, dtype=jnp.int32) // CHUNK
    visible = chunk[None, :] <= chunk[:, None]
    dmat = jnp.where(visible[None], jnp.exp(log_g[:, None, None] * dist[None]), 0.0)
    qdec = jnp.exp(log_g[:, None] * (pos + 1.0)[None, :])
    kdec = jnp.exp(log_g[:, None] * (ts - 1.0 - pos)[None, :])
    qdec = jnp.broadcast_to(qdec[:, :, None], (RET_HEADS, ts, hd))
    kdec = jnp.broadcast_to(kdec[:, :, None], (RET_HEADS, ts, hd))
    tile_decay = tuple(math.exp(math.log1p(-2.0 ** (-5.0 - h)) * ts) for h in range(RET_HEADS))
    return rope, rope_base, dmat, qdec, kdec, tile_decay


def _routing_tables(nbe, n_tok, n_exp):
    tm, tb = ROW_GRANULE, TOK_BLOCK
    nal = nbe + (nbe & (STRIP_ALIGN - 1))
    region = jnp.sum(nal, axis=0)
    tiles_per = (region + tm - 1) // tm
    tile_end = jnp.cumsum(tiles_per)
    offs = (tile_end - tiles_per) * tm
    n_tiles = (n_tok * TOP_K + (n_tok // tb) * n_exp * (STRIP_ALIGN - 1)) // tm + n_exp
    hrow = offs[None, :] + jnp.cumsum(nal, axis=0) - nal
    padrow = offs + region
    padlen = tiles_per * tm - region
    as_i32 = lambda a: a.reshape(-1).astype(jnp.int32)
    return (n_tiles, as_i32(tile_end - tiles_per), as_i32(tiles_per), as_i32(tile_end[-1:]),
            as_i32(hrow), as_i32(jnp.sum(nal, axis=1) - tb * TOP_K), as_i32(padrow), as_i32(padlen))


def _layer(x, norm_mix_w, w_in, ret_norm_w, w_pool, pool_scale, w_out, norm_ffn_w,
           w_router, b_router, w_gu, b_gu, w_down, b_down, final_w):
    bsz, seq, d_model = x.shape
    n_tok = bsz * seq
    n_exp = w_router.shape[1]
    rw = ret_norm_w.shape[0]
    hd = rw // RET_HEADS
    ts = SEQ_TILE
    bf = jnp.bfloat16
    rpt = d_model // PACK // LANES
    assert d_model % (PACK * LANES) == 0 and (STRIP_ALIGN * rpt) % SUBLANES == 0
    assert seq % ts == 0 and ts % CHUNK == 0 and n_tok % TOK_BLOCK == 0
    assert n_tok % ROUTE_TILE == 0 and ROUTE_TILE % TOK_BLOCK == 0 and TOK_BLOCK % ROUTE_CHUNK == 0
    assert TOK_BLOCK & (TOK_BLOCK - 1) == 0 and ROW_GRANULE & (ROW_GRANULE - 1) == 0
    assert ROW_TILE % ROW_GRANULE == 0

    rope, rope_base, dmat, qdec, kdec, tile_decay = _retention_tables(seq, hd, ts)
    tri = jnp.arange(ROUTE_CHUNK, dtype=jnp.int32)
    utri = (tri[:, None] < tri[None, :]).astype(bf)

    x1, h2p = _mix(
        x, norm_mix_w[None], w_in.astype(bf), rope, rope_base, dmat, qdec, kdec, ret_norm_w[None],
        w_pool.astype(bf), pool_scale[None], w_out.astype(bf), norm_ffn_w[None], tile_decay)
    eri = jnp.arange(n_exp, dtype=jnp.int32)
    ltri = (eri[None, :] < eri[:, None]).astype(bf)
    lidx, gate, nbe, lb, _ = _route(h2p, w_router.T.astype(bf), b_router[:, None], utri, ltri)
    lidx = lidx.reshape(-1)
    nbe = nbe[:, :, 0].astype(jnp.int32)
    lb = lb[:, :, 0].astype(jnp.int32).reshape(-1)

    (n_tiles, tile_start, tile_count, n_used, hrow, extra, padrow, padlen) = _routing_tables(nbe, n_tok, n_exp)
    nbe = nbe.reshape(-1)

    xs = _dispatch(lidx, nbe, lb, hrow, extra, padrow, padlen, n_used, h2p, n_tiles * ROW_GRANULE, d_model)
    y_rows = _experts(tile_start, tile_count, xs, w_gu, b_gu, w_down, b_down)
    out = _combine(lidx, gate.reshape(-1), nbe, lb, hrow, extra, x1.reshape(n_tok, d_model), final_w[None], y_rows)
    return out.reshape(bsz, seq, d_model)


def kernel(x, norm_mix_w, w_in, ret_norm_w, w_pool, pool_scale, w_out, norm_ffn_w, w_router, b_router,
           w_gu, b_gu, w_down, b_down, norm_final_w):
    depth = norm_mix_w.shape[0]
    for l in range(depth):
        last = l == depth - 1
        assert last, "stacked layers need an un-normalised combine output"
        x = _layer(x, norm_mix_w[l], w_in[l], ret_norm_w[l], w_pool[l], pool_scale[l], w_out[l],
                   norm_ffn_w[l], w_router[l], b_router[l], w_gu[l], b_gu[l], w_down[l], b_down[l],
                   norm_final_w)
    return x
```

```python
import functools
import math

import jax
import jax.numpy as jnp
from jax import lax
from jax.experimental import pallas as pl
from jax.experimental.pallas import tpu as pltpu

CHUNK = 64
RET_HEADS = 4
POOL_WINDOWS = (2, 4, 8, 16)
ROPE_BASE = 10000.0
TOP_K = 4
SWIGLU_LIMIT = 7.0
SWIGLU_ALPHA = 1.702
EPS = 1e-6

LANES = 128
SUBLANES = 8
VMEM_LIMIT_BYTES = 56 * 1024 * 1024

SEQ_TILE = 256
MIX_STREAMS = 4
ROUTE_TILE = 2048
ROUTE_CHUNK = 256
ROW_GRANULE = 128
ROW_TILE = 512
TOK_BLOCK = 1024
POOL_HALO = 16
PACK = 2
STRIP_ALIGN = 2
GATHER_UNROLL = 32
STRIP_UNROLL = 4
RARE_STRIP_BIT = 8


def _rms(x, w):
    return x * lax.rsqrt(jnp.mean(x * x, axis=-1, keepdims=True) + EPS) * w


def _bf16_bits_hi(x):
    return lax.bitcast_convert_type(x.astype(jnp.bfloat16).astype(jnp.float32), jnp.uint32)


def _pack_rows(x, out_ref, n_rows):
    half = x.shape[1] // PACK
    packed = (_bf16_bits_hi(x[:, :half]) >> 16) | _bf16_bits_hi(x[:, half:])
    rpt = half // LANES
    for si in range(rpt):
        out_ref[pl.ds(si, n_rows, stride=rpt), :] = packed[:, si * LANES:(si + 1) * LANES]


def _unpack_words(u):
    lo = lax.bitcast_convert_type(u << 16, jnp.float32)
    hi = lax.bitcast_convert_type(u & jnp.uint32(0xFFFF0000), jnp.float32)
    return lo, hi


def _unpack_rows(g_ref, n_rows, rpt):
    lo, hi = [], []
    for si in range(rpt):
        l, h = _unpack_words(g_ref[pl.ds(si, n_rows, stride=rpt), :])
        lo.append(l.astype(jnp.bfloat16))
        hi.append(h.astype(jnp.bfloat16))
    return jnp.concatenate(lo + hi, axis=-1)


def _mix_kernel(x_ref, nmw_ref, win_ref, rope_ref, rope_base_ref, dmat_ref, qdec_ref, kdec_ref,
                rnw_ref, wpool_ref, pscale_ref, wout_ref, nfw_ref,
                x1_ref, h2p_ref,
                state_ref, p1_ref, p2_ref, p4_ref, p8_ref,
                *, tile_decay):
    j = pl.program_id(1)
    n_str = x_ref.shape[0]
    ts = x_ref.shape[2]
    n_heads = RET_HEADS
    rw = rnw_ref.shape[1]
    hd = rw // n_heads
    pgw = wpool_ref.shape[1]
    halo = POOL_HALO
    pool_refs = (p1_ref, p2_ref, p4_ref, p8_ref)

    @pl.when(j == 0)
    def _():
        state_ref[...] = jnp.zeros_like(state_ref)
        for r in pool_refs:
            r[:, 0:halo, :] = jnp.zeros((n_str, halo, r.shape[2]), jnp.float32)

    @pl.when(j > 0)
    def _():
        for r in pool_refs:
            r[:, 0:halo, :] = r[:, ts:ts + halo, :]

    x = jnp.concatenate([x_ref[si, 0] for si in range(n_str)], axis=0)
    h = _rms(x, nmw_ref[...]).astype(jnp.bfloat16)
    z_all = jnp.dot(h, win_ref[...], preferred_element_type=jnp.float32)

    rot = rope_ref[...]
    base = rope_base_ref[0]
    cos = base[0:1] * rot[0] - base[1:2] * rot[1]
    sin_plain = base[1:2] * rot[0] + base[0:1] * rot[1]
    half_sign = jnp.where(lax.broadcasted_iota(jnp.int32, (1, hd), 1) < hd // 2, -1.0, 1.0)
    sin = sin_plain * half_sign
    k_scale = hd ** -0.5
    tpos = (j * ts + lax.broadcasted_iota(jnp.int32, (ts, pgw), 0) + 1).astype(jnp.float32)

    mix_rows = []
    for si in range(n_str):
        z = z_all[si * ts:(si + 1) * ts]
        parts = []
        for hh in range(n_heads):
            q = z[:, hh * hd:(hh + 1) * hd]
            k = z[:, rw + hh * hd:rw + (hh + 1) * hd]
            v = z[:, 2 * rw + hh * hd:2 * rw + (hh + 1) * hd]
            g = z[:, 3 * rw + hh * hd:3 * rw + (hh + 1) * hd]
            q = q * cos + pltpu.roll(q, hd // 2, 1) * sin
            k = (k * cos + pltpu.roll(k, hd // 2, 1) * sin) * k_scale
            qb = q.astype(jnp.bfloat16)
            kb = k.astype(jnp.bfloat16)
            vb = v.astype(jnp.bfloat16)
            s = lax.dot_general(qb, kb, (((1,), (1,)), ((), ())), preferred_element_type=jnp.float32)
            pmat = (s * dmat_ref[hh]).astype(jnp.bfloat16)
            o = jnp.dot(pmat, vb, preferred_element_type=jnp.float32)
            state = state_ref[si, hh]
            qd = (q * qdec_ref[hh]).astype(jnp.bfloat16)
            o = o + jnp.dot(qd, state.astype(jnp.bfloat16), preferred_element_type=jnp.float32)
            kd = (k * kdec_ref[hh]).astype(jnp.bfloat16)
            u = lax.dot_general(kd, vb, (((0,), (0,)), ((), ())), preferred_element_type=jnp.float32)
            state_ref[si, hh] = state * tile_decay[hh] + u
            o = o * lax.rsqrt(jnp.mean(o * o, axis=-1, keepdims=True) + EPS)
            o = o * rnw_ref[:, hh * hd:(hh + 1) * hd] * (g * (1.0 / (1.0 + jnp.exp(-g))))
            parts.append(o.astype(jnp.bfloat16))

        p = z[:, 4 * rw:]
        p1_ref[si, halo:halo + ts, :] = p
        s2 = p + p1_ref[si, halo - 1:halo - 1 + ts, :]
        p2_ref[si, halo:halo + ts, :] = s2[:, pgw:]
        s4 = s2[:, pgw:] + p2_ref[si, halo - 2:halo - 2 + ts, :]
        p4_ref[si, halo:halo + ts, :] = s4[:, pgw:]
        s8 = s4[:, pgw:] + p4_ref[si, halo - 4:halo - 4 + ts, :]
        p8_ref[si, halo:halo + ts, :] = s8[:, pgw:]
        s16 = s8[:, pgw:] + p8_ref[si, halo - 8:halo - 8 + ts, :]
        sums = (s2[:, :pgw], s4[:, :pgw], s8[:, :pgw], s16)
        for gi, w in enumerate(POOL_WINDOWS):
            cnt = jnp.minimum(tpos, float(w))
            pooled = sums[gi] / cnt - p[:, gi * pgw:(gi + 1) * pgw]
            mixed = jnp.dot(pooled.astype(jnp.bfloat16), wpool_ref[gi], preferred_element_type=jnp.float32)
            mixed = mixed * pscale_ref[:, gi * pgw:(gi + 1) * pgw]
            parts.append(mixed.astype(jnp.bfloat16))
        mix_rows.append(jnp.concatenate(parts, axis=-1))

    mix = jnp.concatenate(mix_rows, axis=0)
    x1 = x + jnp.dot(mix, wout_ref[...], preferred_element_type=jnp.float32)
    h2 = _rms(x1, nfw_ref[...])
    for si in range(n_str):
        x1_ref[si, 0] = x1[si * ts:(si + 1) * ts]
        _pack_rows(h2[si * ts:(si + 1) * ts], h2p_ref.at[si, 0], ts)


def _mix(x, nmw, win_b, rope, rope_base, dmat, qdec, kdec, rnw, wpool_b, pscale, wout_b, nfw, tile_decay):
    bsz, seq, d_model = x.shape
    ts = SEQ_TILE
    n_tok = bsz * seq
    rw = rnw.shape[1]
    hd = rw // RET_HEADS
    pw = pscale.shape[1]
    pgw = pw // len(POOL_WINDOWS)
    nj = seq // ts
    rows_per_tok = d_model // PACK // LANES

    def const(shape):
        return pl.BlockSpec(shape, lambda b, j: (0,) * len(shape))

    n_str = MIX_STREAMS if bsz % MIX_STREAMS == 0 else 1
    per = bsz // n_str
    in_specs = [
        pl.BlockSpec((n_str, 1, ts, d_model), lambda q, j: (0, q, j, 0)),
        const(nmw.shape), const(win_b.shape),
        const(rope.shape),
        pl.BlockSpec((1, 2, hd), lambda q, j: (j, 0, 0)),
        const(dmat.shape), const(qdec.shape), const(kdec.shape),
        const(rnw.shape), const(wpool_b.shape), const(pscale.shape), const(wout_b.shape),
        const(nfw.shape),
    ]
    out_shape = [
        jax.ShapeDtypeStruct((n_str, per, seq, d_model), jnp.float32),
        jax.ShapeDtypeStruct((n_str, per, seq * rows_per_tok, LANES), jnp.uint32),
    ]
    out_specs = [
        pl.BlockSpec((n_str, 1, ts, d_model), lambda q, j: (0, q, j, 0)),
        pl.BlockSpec((n_str, 1, ts * rows_per_tok, LANES), lambda q, j: (0, q, j, 0)),
    ]
    scratch = [
        pltpu.VMEM((n_str, RET_HEADS, hd, hd), jnp.float32),
        pltpu.VMEM((n_str, POOL_HALO + ts, pw), jnp.float32),
        pltpu.VMEM((n_str, POOL_HALO + ts, pw - pgw), jnp.float32),
        pltpu.VMEM((n_str, POOL_HALO + ts, pw - 2 * pgw), jnp.float32),
        pltpu.VMEM((n_str, POOL_HALO + ts, pw - 3 * pgw), jnp.float32),
    ]
    x1, h2p = pl.pallas_call(
        functools.partial(_mix_kernel, tile_decay=tile_decay),
        grid=(per, nj),
        in_specs=in_specs,
        out_specs=out_specs,
        out_shape=out_shape,
        scratch_shapes=scratch,
        compiler_params=pltpu.CompilerParams(
            dimension_semantics=("arbitrary", "arbitrary"),
            vmem_limit_bytes=VMEM_LIMIT_BYTES),
        name="mix",
    )(x.reshape(n_str, per, seq, d_model), nmw, win_b, rope, rope_base, dmat, qdec, kdec, rnw, wpool_b,
      pscale, wout_b, nfw)
    return x1.reshape(bsz, seq, d_model), h2p.reshape(n_tok * rows_per_tok, LANES)


def _route_kernel(h2p_ref, wrt_ref, brt_ref, utri_ref, ltri_ref,
                  lidx_ref, gate_ref, nbe_ref, lb_ref, cnt_ref, carry_ref):
    i = pl.program_id(0)
    n_exp, d_model = wrt_ref.shape
    rpt = d_model // PACK // LANES
    tr = h2p_ref.shape[0] // rpt
    sub = utri_ref.shape[0]
    chunks_per_block = TOK_BLOCK // sub

    @pl.when(i == 0)
    def _():
        carry_ref[...] = jnp.zeros_like(carry_ref)

    h2b = _unpack_rows(h2p_ref, tr, rpt)
    logits = lax.dot_general(wrt_ref[...], h2b, (((1,), (1,)), ((), ())),
                             preferred_element_type=jnp.float32) + brt_ref[...]
    eiota = lax.broadcasted_iota(jnp.int32, (n_exp, tr), 0)
    work = logits
    vals, sels = [], []
    for _ in range(TOP_K):
        m = jnp.max(work, axis=0, keepdims=True)
        idx = jnp.min(jnp.where(work == m, eiota, n_exp), axis=0, keepdims=True)
        sel = eiota == idx
        vals.append(m)
        sels.append(sel)
        work = jnp.where(sel, -jnp.inf, work)
    ex = [jnp.exp(vk - vals[0]) for vk in vals]
    den = ex[0] + ex[1] + ex[2] + ex[3]
    gate_ref[...] = jnp.concatenate([e / den for e in ex], axis=0)

    member = jnp.zeros((n_exp, tr), jnp.float32)
    for sel in sels:
        member = member + sel.astype(jnp.float32)
    carry = carry_ref[...]
    places = []
    for blk in range(tr // TOK_BLOCK):
        block_start = carry
        befores = []
        for c in range(blk * chunks_per_block, (blk + 1) * chunks_per_block):
            mc = member[:, c * sub:(c + 1) * sub]
            bc = jnp.dot(mc.astype(jnp.bfloat16), utri_ref[...], preferred_element_type=jnp.float32)
            befores.append(bc + carry[:, 0:1])
            carry = carry + jnp.sum(mc, axis=1, keepdims=True)
        nbe = carry - block_start
        nal = nbe + (nbe - STRIP_ALIGN * jnp.floor(nbe * (1.0 / STRIP_ALIGN)))
        hi = jnp.floor(nal * (1.0 / 32.0))
        lo = nal - 32.0 * hi
        lb = (32.0 * jnp.dot(ltri_ref[...], hi.astype(jnp.bfloat16), preferred_element_type=jnp.float32)
              + jnp.dot(ltri_ref[...], lo.astype(jnp.bfloat16), preferred_element_type=jnp.float32))
        nbe_ref[blk] = nbe
        lb_ref[blk] = lb
        place = jnp.concatenate(befores, axis=1) + (lb[:, 0:1] - block_start[:, 0:1])
        cols = slice(blk * TOK_BLOCK, (blk + 1) * TOK_BLOCK)
        places.append(jnp.concatenate(
            [jnp.sum(jnp.where(sel[:, cols], place, 0.0), axis=0, keepdims=True) for sel in sels], axis=0))
    lidx_ref[...] = (jnp.concatenate(places, axis=1) * float(rpt)).astype(jnp.int32)
    carry_ref[...] = carry
    cnt_ref[...] = carry


def _route(h2p, wrt_b, brt, utri, ltri):
    n_exp, d_model = wrt_b.shape
    rpt = d_model // PACK // LANES
    n_tok = h2p.shape[0] // rpt
    tr = ROUTE_TILE
    bps = tr // TOK_BLOCK

    def const(shape):
        return pl.BlockSpec(shape, lambda i: (0,) * len(shape))

    out_shape = [
        jax.ShapeDtypeStruct((TOP_K, n_tok), jnp.int32),
        jax.ShapeDtypeStruct((TOP_K, n_tok), jnp.float32),
        jax.ShapeDtypeStruct((n_tok // TOK_BLOCK, n_exp, LANES), jnp.float32),
        jax.ShapeDtypeStruct((n_tok // TOK_BLOCK, n_exp, LANES), jnp.float32),
        jax.ShapeDtypeStruct((n_exp, LANES), jnp.float32),
    ]
    out_specs = [
        pl.BlockSpec((TOP_K, tr), lambda i: (0, i)),
        pl.BlockSpec((TOP_K, tr), lambda i: (0, i)),
        pl.BlockSpec((bps, n_exp, LANES), lambda i: (i, 0, 0)),
        pl.BlockSpec((bps, n_exp, LANES), lambda i: (i, 0, 0)),
        pl.BlockSpec((n_exp, LANES), lambda i: (0, 0)),
    ]
    return pl.pallas_call(
        _route_kernel,
        grid=(n_tok // tr,),
        in_specs=[pl.BlockSpec((tr * rpt, LANES), lambda i: (i, 0)),
                  const(wrt_b.shape), const(brt.shape), const(utri.shape), const(ltri.shape)],
        out_specs=out_specs,
        out_shape=out_shape,
        scratch_shapes=[pltpu.VMEM((n_exp, LANES), jnp.float32)],
        compiler_params=pltpu.CompilerParams(
            dimension_semantics=("arbitrary",),
            vmem_limit_bytes=VMEM_LIMIT_BYTES),
        name="route",
    )(h2p, wrt_b, brt, utri, ltri)


def _for_set_bits(n, max_bit, fn):
    def piece(bit):
        size = 1 << bit
        offset = (n >> (bit + 1)) << (bit + 1)

        @pl.when((n & size) != 0)
        def _():
            fn(offset, size)

    min_bit = STRIP_ALIGN.bit_length() - 1
    rare_bit = min(max_bit + 1, RARE_STRIP_BIT)

    @pl.when(n >= (1 << rare_bit))
    def _():
        for bit in range(max_bit, rare_bit - 1, -1):
            piece(bit)
    for bit in range(rare_bit - 1, min_bit - 1, -1):
        piece(bit)


def _aligned_len(n):
    return n + (n & (STRIP_ALIGN - 1))


def _for_strips(nbe_ref, lb_ref, hrow_ref, blk, n_exp, max_bit, fn):
    def body(i, carry):
        for u in range(STRIP_UNROLL):
            k = blk * n_exp + i * STRIP_UNROLL + u
            n = _aligned_len(nbe_ref[k])
            lb = lb_ref[k]
            hrow = hrow_ref[k]
            _for_set_bits(n, max_bit, lambda offset, size: fn(lb + offset, hrow + offset, size))
        return carry
    assert n_exp % STRIP_UNROLL == 0
    lax.fori_loop(0, n_exp // STRIP_UNROLL, body, 0)


def _dispatch_kernel(lidx_ref, nbe_ref, lb_ref, hrow_ref, extra_ref, padrow_ref, padlen_ref, nused_ref,
                     h2p_ref, xs_hbm, stage_ref, zero_ref, sem, zsem, *, nblk):
    b = pl.program_id(0)
    rpt = zero_ref.shape[0] // ROW_GRANULE
    tb = h2p_ref.shape[0] // rpt
    n_exp = padrow_ref.shape[0]
    n_tok = lidx_ref.shape[0] // TOP_K
    slot_rows = stage_ref.shape[0] // 2
    max_bit = tb.bit_length() - 1
    slot = b % 2

    def strip_copy(sl, staging_row, sorted_row, size):
        src = pl.multiple_of(sl * slot_rows + staging_row * rpt, SUBLANES)
        dst = pl.multiple_of(sorted_row * rpt, SUBLANES)
        return pltpu.make_async_copy(stage_ref.at[pl.ds(src, size * rpt), :],
                                     xs_hbm.at[pl.ds(dst, size * rpt), :], sem.at[sl])

    def start_strips(blk, sl):
        _for_strips(nbe_ref, lb_ref, hrow_ref, blk, n_exp, max_bit,
                    lambda srow, hrow, size: strip_copy(sl, srow, hrow, size).start())

    def wait_strips(blk, sl):
        def sized_copy(offset, size):
            return pltpu.make_async_copy(stage_ref.at[pl.ds(sl * slot_rows, size * rpt), :],
                                         xs_hbm.at[pl.ds(0, size * rpt), :], sem.at[sl])
        sized_copy(0, tb * TOP_K).wait()
        _for_set_bits(extra_ref[blk], n_exp.bit_length() - 1, lambda o, s: sized_copy(o, s).wait())

    @pl.when(b >= 2)
    def _():
        wait_strips(b - 2, slot)

    def scatter_block(static_slot):
        def scatter_group(c, carry):
            tok0 = c * GATHER_UNROLL
            idx0 = b * tb + tok0
            for u in range(GATHER_UNROLL):
                row = h2p_ref[pl.ds(pl.multiple_of((tok0 + u) * rpt, rpt), rpt), :]
                for k in range(TOP_K):
                    dst = pl.multiple_of(lidx_ref[idx0 + (k * n_tok + u)], rpt)
                    stage_ref[pl.ds(static_slot * slot_rows + dst, rpt), :] = row
            return carry
        lax.fori_loop(0, tb // GATHER_UNROLL, scatter_group, 0)

    for static_slot in range(2):
        pl.when(slot == static_slot)(functools.partial(scatter_block, static_slot))

    def zero_tail(e, carry):
        n = nbe_ref[b * n_exp + e]

        @pl.when((n & (STRIP_ALIGN - 1)) != 0)
        def _():
            dst = pl.multiple_of(slot * slot_rows + (lb_ref[b * n_exp + e] + n) * rpt, rpt)
            stage_ref[pl.ds(dst, rpt), :] = jnp.zeros((rpt, LANES), jnp.uint32)
        return carry
    lax.fori_loop(0, n_exp, zero_tail, 0)

    start_strips(b, slot)

    @pl.when(b == nblk - 1)
    def _():
        if nblk >= 2:
            wait_strips(b - 1, 1 - slot)
        wait_strips(b, slot)
        zero_ref[...] = jnp.zeros_like(zero_ref)
        pad_bit = ROW_GRANULE.bit_length() - 2

        def pad_copy(e, offset, size):
            dst = pl.multiple_of((padrow_ref[e] + offset) * rpt, SUBLANES)
            return pltpu.make_async_copy(zero_ref.at[pl.ds(0, size * rpt), :],
                                         xs_hbm.at[pl.ds(dst, size * rpt), :], zsem)

        def pad_start(e, carry):
            _for_set_bits(padlen_ref[e], pad_bit, lambda o, s: pad_copy(e, o, s).start())
            return carry

        def pad_wait(e, carry):
            _for_set_bits(padlen_ref[e], pad_bit, lambda o, s: pad_copy(e, o, s).wait())
            return carry
        lax.fori_loop(0, n_exp, pad_start, 0)

        def tail_copy(i):
            dst = pl.multiple_of(i * (ROW_GRANULE * rpt), SUBLANES)
            return pltpu.make_async_copy(zero_ref, xs_hbm.at[pl.ds(dst, ROW_GRANULE * rpt), :], zsem)

        def tail_start(i, carry):
            tail_copy(i).start()
            return carry

        def tail_wait(i, carry):
            tail_copy(i).wait()
            return carry
        n_gran = xs_hbm.shape[0] // (ROW_GRANULE * rpt)
        lax.fori_loop(nused_ref[0], n_gran, tail_start, 0)
        lax.fori_loop(0, n_exp, pad_wait, 0)
        lax.fori_loop(nused_ref[0], n_gran, tail_wait, 0)


def _dispatch(lidx, nbe, lb, hrow, extra, padrow, padlen, n_used, h2p, n_rows, d_model):
    tb = TOK_BLOCK
    rpt = d_model // PACK // LANES
    n_tok = h2p.shape[0] // rpt
    n_exp = padrow.shape[0]
    slot_rows = (tb * TOP_K + n_exp * (STRIP_ALIGN - 1)) * rpt
    grid_spec = pltpu.PrefetchScalarGridSpec(
        num_scalar_prefetch=8,
        grid=(n_tok // tb,),
        in_specs=[pl.BlockSpec((tb * rpt, LANES), lambda i, *_: (i, 0))],
        out_specs=pl.BlockSpec(memory_space=pl.ANY),
        scratch_shapes=[pltpu.VMEM((2 * slot_rows, LANES), jnp.uint32),
                        pltpu.VMEM((ROW_GRANULE * rpt, LANES), jnp.uint32),
                        pltpu.SemaphoreType.DMA((2,)),
                        pltpu.SemaphoreType.DMA],
    )
    return pl.pallas_call(
        functools.partial(_dispatch_kernel, nblk=n_tok // tb),
        grid_spec=grid_spec,
        out_shape=jax.ShapeDtypeStruct((n_rows * rpt, LANES), jnp.uint32),
        compiler_params=pltpu.CompilerParams(
            dimension_semantics=("arbitrary",),
            vmem_limit_bytes=VMEM_LIMIT_BYTES),
        name="dispatch",
    )(lidx, nbe, lb, hrow, extra, padrow, padlen, n_used, h2p)


def _experts_kernel(tstart_ref, tcount_ref, wgu_hbm, bgu_ref, wd_hbm, bd_ref, xs_hbm,
                    y_hbm, wgu_f, wd_f, wgu_b, wd_b, xbuf, ybuf, pend_ref, wsem, xsem, ysem, *, n_exp):
    e = pl.program_id(0)
    d_ff = wd_f.shape[0]
    d_model = wd_f.shape[1]
    rpt = d_model // PACK // LANES
    gran_rows = ROW_GRANULE * rpt
    gpt = ROW_TILE // ROW_GRANULE
    n_gran = y_hbm.shape[0] // gran_rows
    g0 = tstart_ref[e]
    ng = tcount_ref[e]
    nt = ng // gpt
    tail = ng - nt * gpt

    def hbm_rows(ref, gran, count):
        return ref.at[pl.ds(pl.multiple_of(gran * gran_rows, SUBLANES), count * gran_rows), :]

    def x_copy(gran, sl, count=gpt):
        return pltpu.make_async_copy(hbm_rows(xs_hbm, gran, count),
                                     xbuf.at[sl, pl.ds(0, count * gran_rows), :], xsem.at[sl])

    def y_copy(gran, sl, count=gpt):
        return pltpu.make_async_copy(ybuf.at[sl, pl.ds(0, count * gran_rows), :],
                                     hbm_rows(y_hbm, gran, count), ysem.at[sl])

    def mlp(sl, tm):
        x = _unpack_rows(xbuf.at[sl], tm, rpt)
        gu = jnp.dot(x, wgu_b[...], preferred_element_type=jnp.float32) + bgu_ref[0]
        x_glu = jnp.minimum(gu[:, :d_ff], SWIGLU_LIMIT)
        x_lin = jnp.clip(gu[:, d_ff:], -SWIGLU_LIMIT, SWIGLU_LIMIT)
        act = x_glu * (1.0 / (1.0 + jnp.exp(-SWIGLU_ALPHA * x_glu))) * (x_lin + 1.0)
        y = jnp.dot(act.astype(jnp.bfloat16), wd_b[...], preferred_element_type=jnp.float32) + bd_ref[0]
        _pack_rows(y, ybuf.at[sl], tm)

    def weight_copies(ex):
        return (pltpu.make_async_copy(wgu_hbm.at[ex], wgu_f, wsem.at[0]),
                pltpu.make_async_copy(wd_hbm.at[ex], wd_f, wsem.at[1]))

    tail_slot = 2
    tail_gran = g0 + nt * gpt

    def wait_pending_tail_y():
        for count in range(1, gpt):
            @pl.when(pend_ref[0] == count)
            def _():
                y_copy(0, tail_slot, count).wait()

    @pl.when(e == 0)
    def _():
        pend_ref[0] = 0

        @pl.when(nt > 0)
        def _():
            x_copy(g0, 0).start()

    for count in range(1, gpt):
        @pl.when(tail == count)
        def _():
            x_copy(tail_gran, tail_slot, count).start()

    @pl.when(e == 0)
    def _():
        for c in weight_copies(0):
            c.start(priority=1)
    for c in weight_copies(e):
        c.wait()
    wgu_b[...] = wgu_f[...].astype(jnp.bfloat16)
    wd_b[...] = wd_f[...].astype(jnp.bfloat16)

    @pl.when(e + 1 < n_exp)
    def _():
        for c in weight_copies(e + 1):
            c.start(priority=1)

    @pl.when(ng > 0)
    def _():
        def tile_step(j, carry):
            sl = j % 2
            gran = g0 + j * gpt

            @pl.when(j + 1 < nt)
            def _():
                x_copy(gran + gpt, 1 - sl).start()
            x_copy(gran, sl).wait()

            @pl.when(j >= 2)
            def _():
                y_copy(gran - 2 * gpt, sl).wait()
            mlp(sl, ROW_TILE)
            y_copy(gran, sl).start()
            return carry
        lax.fori_loop(0, nt, tile_step, 0)

        @pl.when(nt >= 2)
        def _():
            y_copy(g0 + (nt - 2) * gpt, nt % 2).wait()

        @pl.when(nt >= 1)
        def _():
            y_copy(g0 + (nt - 1) * gpt, (nt - 1) % 2).wait()

    @pl.when(e + 1 < n_exp)
    def _():
        nxt = jnp.minimum(e + 1, n_exp - 1)

        @pl.when(tcount_ref[nxt] >= gpt)
        def _():
            x_copy(tstart_ref[nxt], 0).start()

    for count in range(1, gpt):
        @pl.when(tail == count)
        def _():
            x_copy(tail_gran, tail_slot, count).wait()
            wait_pending_tail_y()
            mlp(tail_slot, count * ROW_GRANULE)
            y_copy(tail_gran, tail_slot, count).start()
            pend_ref[0] = count

    @pl.when(e == n_exp - 1)
    def _():
        wait_pending_tail_y()
        ybuf[0, 0:gran_rows, :] = jnp.zeros((gran_rows, LANES), ybuf.dtype)

        def tail_start(i, carry):
            y_copy(i, 0, 1).start()
            return carry

        def tail_wait(i, carry):
            y_copy(i, 0, 1).wait()
            return carry
        lax.fori_loop(g0 + ng, n_gran, tail_start, 0)
        lax.fori_loop(g0 + ng, n_gran, tail_wait, 0)


def _experts(tile_start, tile_count, xs, w_gu, b_gu, w_down, b_down):
    tm = ROW_TILE
    n_exp, d_model, d_gu = w_gu.shape
    d_ff = w_down.shape[1]
    rpt = d_model // PACK // LANES
    grid_spec = pltpu.PrefetchScalarGridSpec(
        num_scalar_prefetch=2,
        grid=(n_exp,),
        in_specs=[
            pl.BlockSpec(memory_space=pl.ANY),
            pl.BlockSpec((1, 1, d_gu), lambda e, *_: (e, 0, 0)),
            pl.BlockSpec(memory_space=pl.ANY),
            pl.BlockSpec((1, 1, d_model), lambda e, *_: (e, 0, 0)),
            pl.BlockSpec(memory_space=pl.ANY),
        ],
        out_specs=pl.BlockSpec(memory_space=pl.ANY),
        scratch_shapes=[pltpu.VMEM((d_model, d_gu), jnp.float32),
                        pltpu.VMEM((d_ff, d_model), jnp.float32),
                        pltpu.VMEM((d_model, d_gu), jnp.bfloat16),
                        pltpu.VMEM((d_ff, d_model), jnp.bfloat16),
                        pltpu.VMEM((3, tm * rpt, LANES), jnp.uint32),
                        pltpu.VMEM((3, tm * rpt, LANES), jnp.uint32),
                        pltpu.SMEM((1,), jnp.int32),
                        pltpu.SemaphoreType.DMA((2,)),
                        pltpu.SemaphoreType.DMA((3,)),
                        pltpu.SemaphoreType.DMA((3,))],
    )
    return pl.pallas_call(
        functools.partial(_experts_kernel, n_exp=n_exp),
        grid_spec=grid_spec,
        out_shape=jax.ShapeDtypeStruct(xs.shape, jnp.uint32),
        compiler_params=pltpu.CompilerParams(
            dimension_semantics=("arbitrary",),
            vmem_limit_bytes=VMEM_LIMIT_BYTES),
        name="experts",
    )(tile_start, tile_count, w_gu, b_gu.reshape(n_exp, 1, d_gu), w_down,
      b_down.reshape(n_exp, 1, d_model), xs)


def _combine_kernel(lidx_ref, gate_ref, nbe_ref, lb_ref, hrow_ref, extra_ref,
                    x1_ref, nw_ref, y_hbm, out_ref, stage_ref, alo_ref, ahi_ref, sem, *, nblk):
    b = pl.program_id(0)
    tb = x1_ref.shape[0]
    rpt = alo_ref.shape[0] // tb
    n_exp = nbe_ref.shape[0] // nblk
    n_tok = lidx_ref.shape[0] // TOP_K
    slot_rows = stage_ref.shape[0] // 2
    max_bit = tb.bit_length() - 1
    slot = b % 2

    def strip_copy(sl, staging_row, sorted_row, size):
        src = pl.multiple_of(sorted_row * rpt, SUBLANES)
        dst = pl.multiple_of(sl * slot_rows + staging_row * rpt, SUBLANES)
        return pltpu.make_async_copy(y_hbm.at[pl.ds(src, size * rpt), :],
                                     stage_ref.at[pl.ds(dst, size * rpt), :], sem.at[sl])

    def start_strips(blk, sl):
        _for_strips(nbe_ref, lb_ref, hrow_ref, blk, n_exp, max_bit,
                    lambda srow, hrow, size: strip_copy(sl, srow, hrow, size).start())

    @pl.when(b == 0)
    def _():
        start_strips(0, 0)

    @pl.when(b + 1 < nblk)
    def _():
        start_strips(b + 1, 1 - slot)

    def sized_copy(offset, size):
        return pltpu.make_async_copy(y_hbm.at[pl.ds(0, size * rpt), :],
                                     stage_ref.at[pl.ds(slot * slot_rows, size * rpt), :], sem.at[slot])
    sized_copy(0, tb * TOP_K).wait()
    _for_set_bits(extra_ref[b], n_exp.bit_length() - 1, lambda o, s: sized_copy(o, s).wait())

    def gather_block(static_slot):
        def gather_group(c, carry):
            tok0 = c * GATHER_UNROLL
            idx0 = b * tb + tok0
            for u in range(GATHER_UNROLL):
                lo = hi = None
                for k in range(TOP_K):
                    src = pl.multiple_of(lidx_ref[idx0 + (k * n_tok + u)], rpt)
                    g = gate_ref[idx0 + (k * n_tok + u)]
                    l, h = _unpack_words(stage_ref[pl.ds(static_slot * slot_rows + src, rpt), :])
                    lo = g * l if lo is None else lo + g * l
                    hi = g * h if hi is None else hi + g * h
                dst = pl.multiple_of((tok0 + u) * rpt, rpt)
                alo_ref[pl.ds(dst, rpt), :] = lo
                ahi_ref[pl.ds(dst, rpt), :] = hi
            return carry
        lax.fori_loop(0, tb // GATHER_UNROLL, gather_group, 0)

    for static_slot in range(2):
        pl.when(slot == static_slot)(functools.partial(gather_block, static_slot))

    parts = [alo_ref[pl.ds(si, tb, stride=rpt), :] for si in range(rpt)]
    parts += [ahi_ref[pl.ds(si, tb, stride=rpt), :] for si in range(rpt)]
    x2 = x1_ref[...] + jnp.concatenate(parts, axis=-1)
    out_ref[...] = _rms(x2, nw_ref[...])


def _combine(lidx, gates, nbe, lb, hrow, extra, x1, norm_w, y_rows):
    tb = TOK_BLOCK
    n_tok, d_model = x1.shape
    rpt = d_model // PACK // LANES
    n_exp = nbe.shape[0] // (n_tok // tb)
    slot_rows = (tb * TOP_K + n_exp * (STRIP_ALIGN - 1)) * rpt
    grid_spec = pltpu.PrefetchScalarGridSpec(
        num_scalar_prefetch=6,
        grid=(n_tok // tb,),
        in_specs=[
            pl.BlockSpec((tb, d_model), lambda i, *_: (i, 0)),
            pl.BlockSpec((1, d_model), lambda i, *_: (0, 0)),
            pl.BlockSpec(memory_space=pl.ANY),
        ],
        out_specs=pl.BlockSpec((tb, d_model), lambda i, *_: (i, 0)),
        scratch_shapes=[pltpu.VMEM((2 * slot_rows, LANES), jnp.uint32),
                        pltpu.VMEM((tb * rpt, LANES), jnp.float32),
                        pltpu.VMEM((tb * rpt, LANES), jnp.float32),
                        pltpu.SemaphoreType.DMA((2,))],
    )
    return pl.pallas_call(
        functools.partial(_combine_kernel, nblk=n_tok // tb),
        grid_spec=grid_spec,
        out_shape=jax.ShapeDtypeStruct((n_tok, d_model), jnp.float32),
        compiler_params=pltpu.CompilerParams(
            dimension_semantics=("arbitrary",),
            vmem_limit_bytes=VMEM_LIMIT_BYTES),
        name="combine",
    )(lidx, gates, nbe, lb, hrow, extra, x1, norm_w, y_rows)


def _retention_tables(seq, hd, ts):
    half = hd // 2
    inv = ROPE_BASE ** (-jnp.arange(half, dtype=jnp.float32) / half)
    inv2 = jnp.concatenate([inv, inv])

    def cos_sin(positions):
        ang = positions[:, None] * inv2[None, :]
        return jnp.stack([jnp.cos(ang), jnp.sin(ang)], axis=1)
    rope = cos_sin(jnp.arange(ts, dtype=jnp.float32)).transpose(1, 0, 2)
    rope_base = cos_sin(jnp.arange(seq // ts, dtype=jnp.float32) * ts)
    log_g = jnp.log1p(-jnp.exp2(-5.0 - jnp.arange(RET_HEADS, dtype=jnp.float32)))
    pos = jnp.arange(ts, dtype=jnp.float32)
    dist = jnp.abs(pos[:, None] - pos[None, :])
    chunk = jnp.arange(ts, dtype=jnp.int32) // CHUNK
    visible = chunk[None, :] <= chunk[:, None]
    dmat = jnp.where(visible[None], jnp.exp(log_g[:, None, None] * dist[None]), 0.0)
    qdec = jnp.exp(log_g[:, None] * (pos + 1.0)[None, :])
    kdec = jnp.exp(log_g[:, None] * (ts - 1.0 - pos)[None, :])
    qdec = jnp.broadcast_to(qdec[:, :, None], (RET_HEADS, ts, hd))
    kdec = jnp.broadcast_to(kdec[:, :, None], (RET_HEADS, ts, hd))
    tile_decay = tuple(math.exp(math.log1p(-2.0 ** (-5.0 - h)) * ts) for h in range(RET_HEADS))
    return rope, rope_base, dmat, qdec, kdec, tile_decay


def _routing_tables(nbe, n_tok, n_exp):
    tm, tb = ROW_GRANULE, TOK_BLOCK
    nal = nbe + (nbe & (STRIP_ALIGN - 1))
    region = jnp.sum(nal, axis=0)
    tiles_per = (region + tm - 1) // tm
    tile_end = jnp.cumsum(tiles_per)
    offs = (tile_end - tiles_per) * tm
    n_tiles = (n_tok * TOP_K + (n_tok // tb) * n_exp * (STRIP_ALIGN - 1)) // tm + n_exp
    hrow = offs[None, :] + jnp.cumsum(nal, axis=0) - nal
    padrow = offs + region
    padlen = tiles_per * tm - region
    as_i32 = lambda a: a.reshape(-1).astype(jnp.int32)
    return (n_tiles, as_i32(tile_end - tiles_per), as_i32(tiles_per), as_i32(tile_end[-1:]),
            as_i32(hrow), as_i32(jnp.sum(nal, axis=1) - tb * TOP_K), as_i32(padrow), as_i32(padlen))


def _layer(x, norm_mix_w, w_in, ret_norm_w, w_pool, pool_scale, w_out, norm_ffn_w,
           w_router, b_router, w_gu, b_gu, w_down, b_down, final_w):
    bsz, seq, d_model = x.shape
    n_tok = bsz * seq
    n_exp = w_router.shape[1]
    rw = ret_norm_w.shape[0]
    hd = rw // RET_HEADS
    ts = SEQ_TILE
    bf = jnp.bfloat16
    rpt = d_model // PACK // LANES
    assert d_model % (PACK * LANES) == 0 and (STRIP_ALIGN * rpt) % SUBLANES == 0
    assert seq % ts == 0 and ts % CHUNK == 0 and n_tok % TOK_BLOCK == 0
    assert n_tok % ROUTE_TILE == 0 and ROUTE_TILE % TOK_BLOCK == 0 and TOK_BLOCK % ROUTE_CHUNK == 0
    assert TOK_BLOCK & (TOK_BLOCK - 1) == 0 and ROW_GRANULE & (ROW_GRANULE - 1) == 0
    assert ROW_TILE % ROW_GRANULE == 0

    rope, rope_base, dmat, qdec, kdec, tile_decay = _retention_tables(seq, hd, ts)
    tri = jnp.arange(ROUTE_CHUNK, dtype=jnp.int32)
    utri = (tri[:, None] < tri[None, :]).astype(bf)

    x1, h2p = _mix(
        x, norm_mix_w[None], w_in.astype(bf), rope, rope_base, dmat, qdec, kdec, ret_norm_w[None],
        w_pool.astype(bf), pool_scale[None], w_out.astype(bf), norm_ffn_w[None], tile_decay)
    eri = jnp.arange(n_exp, dtype=jnp.int32)
    ltri = (eri[None, :] < eri[:, None]).astype(bf)
    lidx, gate, nbe, lb, _ = _route(h2p, w_router.T.astype(bf), b_router[:, None], utri, ltri)
    lidx = lidx.reshape(-1)
    nbe = nbe[:, :, 0].astype(jnp.int32)
    lb = lb[:, :, 0].astype(jnp.int32).reshape(-1)

    (n_tiles, tile_start, tile_count, n_used, hrow, extra, padrow, padlen) = _routing_tables(nbe, n_tok, n_exp)
    nbe = nbe.reshape(-1)

    xs = _dispatch(lidx, nbe, lb, hrow, extra, padrow, padlen, n_used, h2p, n_tiles * ROW_GRANULE, d_model)
    y_rows = _experts(tile_start, tile_count, xs, w_gu, b_gu, w_down, b_down)
    out = _combine(lidx, gate.reshape(-1), nbe, lb, hrow, extra, x1.reshape(n_tok, d_model), final_w[None], y_rows)
    return out.reshape(bsz, seq, d_model)


def kernel(x, norm_mix_w, w_in, ret_norm_w, w_pool, pool_scale, w_out, norm_ffn_w, w_router, b_router,
           w_gu, b_gu, w_down, b_down, norm_final_w):
    depth = norm_mix_w.shape[0]
    for l in range(depth):
        last = l == depth - 1
        assert last, "stacked layers need an un-normalised combine output"
        x = _layer(x, norm_mix_w[l], w_in[l], ret_norm_w[l], w_pool[l], pool_scale[l], w_out[l],
                   norm_ffn_w[l], w_router[l], b_router[l], w_gu[l], b_gu[l], w_down[l], b_down[l],
                   norm_final_w)
    return x
```

```python
import functools
import math

import jax
import jax.numpy as jnp
from jax import lax
from jax.experimental import pallas as pl
from jax.experimental.pallas import tpu as pltpu

CHUNK = 64
RET_HEADS = 4
POOL_WINDOWS = (2, 4, 8, 16)
ROPE_BASE = 10000.0
TOP_K = 4
SWIGLU_LIMIT = 7.0
SWIGLU_ALPHA = 1.702
EPS = 1e-6

LANES = 128
SUBLANES = 8
VMEM_LIMIT_BYTES = 56 * 1024 * 1024

SEQ_TILE = 256
MIX_STREAMS = 4
ROUTE_TILE = 2048
ROUTE_CHUNK = 256
ROW_GRANULE = 128
ROW_TILE = 512
TOK_BLOCK = 1024
POOL_HALO = 16
PACK = 2
STRIP_ALIGN = 2
GATHER_UNROLL = 32
STRIP_UNROLL = 4
RARE_STRIP_BIT = 8


def _rms(x, w):
    return x * lax.rsqrt(jnp.mean(x * x, axis=-1, keepdims=True) + EPS) * w


def _bf16_bits_hi(x):
    return lax.bitcast_convert_type(x.astype(jnp.bfloat16).astype(jnp.float32), jnp.uint32)


def _pack_rows(x, out_ref, n_rows):
    half = x.shape[1] // PACK
    packed = (_bf16_bits_hi(x[:, :half]) >> 16) | _bf16_bits_hi(x[:, half:])
    rpt = half // LANES
    for si in range(rpt):
        out_ref[pl.ds(si, n_rows, stride=rpt), :] = packed[:, si * LANES:(si + 1) * LANES]


def _unpack_words(u):
    lo = lax.bitcast_convert_type(u << 16, jnp.float32)
    hi = lax.bitcast_convert_type(u & jnp.uint32(0xFFFF0000), jnp.float32)
    return lo, hi


def _unpack_rows(g_ref, n_rows, rpt):
    lo, hi = [], []
    for si in range(rpt):
        l, h = _unpack_words(g_ref[pl.ds(si, n_rows, stride=rpt), :])
        lo.append(l.astype(jnp.bfloat16))
        hi.append(h.astype(jnp.bfloat16))
    return jnp.concatenate(lo + hi, axis=-1)


def _mix_kernel(x_ref, nmw_ref, win_ref, rope_ref, rope_base_ref, dmat_ref, qdec_ref, kdec_ref,
                rnw_ref, wpool_ref, pscale_ref, wout_ref, nfw_ref,
                x1_ref, h2p_ref,
                state_ref, p1_ref, p2_ref, p4_ref, p8_ref,
                *, tile_decay):
    j = pl.program_id(1)
    n_str = x_ref.shape[0]
    ts = x_ref.shape[2]
    n_heads = RET_HEADS
    rw = rnw_ref.shape[1]
    hd = rw // n_heads
    pgw = wpool_ref.shape[1]
    halo = POOL_HALO
    pool_refs = (p1_ref, p2_ref, p4_ref, p8_ref)

    @pl.when(j == 0)
    def _():
        state_ref[...] = jnp.zeros_like(state_ref)
        for r in pool_refs:
            r[:, 0:halo, :] = jnp.zeros((n_str, halo, r.shape[2]), jnp.float32)

    @pl.when(j > 0)
    def _():
        for r in pool_refs:
            r[:, 0:halo, :] = r[:, ts:ts + halo, :]

    x = jnp.concatenate([x_ref[si, 0] for si in range(n_str)], axis=0)
    h = _rms(x, nmw_ref[...]).astype(jnp.bfloat16)
    z_all = jnp.dot(h, win_ref[...], preferred_element_type=jnp.float32)

    rot = rope_ref[...]
    base = rope_base_ref[0]
    cos = base[0:1] * rot[0] - base[1:2] * rot[1]
    sin_plain = base[1:2] * rot[0] + base[0:1] * rot[1]
    half_sign = jnp.where(lax.broadcasted_iota(jnp.int32, (1, hd), 1) < hd // 2, -1.0, 1.0)
    sin = sin_plain * half_sign
    k_scale = hd ** -0.5
    tpos = (j * ts + lax.broadcasted_iota(jnp.int32, (ts, pgw), 0) + 1).astype(jnp.float32)

    mix_rows = []
    for si in range(n_str):
        z = z_all[si * ts:(si + 1) * ts]
        parts = []
        for hh in range(n_heads):
            q = z[:, hh * hd:(hh + 1) * hd]
            k = z[:, rw + hh * hd:rw + (hh + 1) * hd]
            v = z[:, 2 * rw + hh * hd:2 * rw + (hh + 1) * hd]
            g = z[:, 3 * rw + hh * hd:3 * rw + (hh + 1) * hd]
            q = q * cos + pltpu.roll(q, hd // 2, 1) * sin
            k = (k * cos + pltpu.roll(k, hd // 2, 1) * sin) * k_scale
            qb = q.astype(jnp.bfloat16)
            kb = k.astype(jnp.bfloat16)
            vb = v.astype(jnp.bfloat16)
            s = lax.dot_general(qb, kb, (((1,), (1,)), ((), ())), preferred_element_type=jnp.float32)
            pmat = (s * dmat_ref[hh]).astype(jnp.bfloat16)
            o = jnp.dot(pmat, vb, preferred_element_type=jnp.float32)
            state = state_ref[si, hh]
            qd = (q * qdec_ref[hh]).astype(jnp.bfloat16)
            o = o + jnp.dot(qd, state.astype(jnp.bfloat16), preferred_element_type=jnp.float32)
            kd = (k * kdec_ref[hh]).astype(jnp.bfloat16)
            u = lax.dot_general(kd, vb, (((0,), (0,)), ((), ())), preferred_element_type=jnp.float32)
            state_ref[si, hh] = state * tile_decay[hh] + u
            o = o * lax.rsqrt(jnp.mean(o * o, axis=-1, keepdims=True) + EPS)
            o = o * rnw_ref[:, hh * hd:(hh + 1) * hd] * (g * (1.0 / (1.0 + jnp.exp(-g))))
            parts.append(o.astype(jnp.bfloat16))

        p = z[:, 4 * rw:]
        p1_ref[si, halo:halo + ts, :] = p
        s2 = p + p1_ref[si, halo - 1:halo - 1 + ts, :]
        p2_ref[si, halo:halo + ts, :] = s2[:, pgw:]
        s4 = s2[:, pgw:] + p2_ref[si, halo - 2:halo - 2 + ts, :]
        p4_ref[si, halo:halo + ts, :] = s4[:, pgw:]
        s8 = s4[:, pgw:] + p4_ref[si, halo - 4:halo - 4 + ts, :]
        p8_ref[si, halo:halo + ts, :] = s8[:, pgw:]
        s16 = s8[:, pgw:] + p8_ref[si, halo - 8:halo - 8 + ts, :]
        sums = (s2[:, :pgw], s4[:, :pgw], s8[:, :pgw], s16)
        for gi, w in enumerate(POOL_WINDOWS):
            cnt = jnp.minimum(tpos, float(w))
            pooled = sums[gi] / cnt - p[:, gi * pgw:(gi + 1) * pgw]
            mixed = jnp.dot(pooled.astype(jnp.bfloat16), wpool_ref[gi], preferred_element_type=jnp.float32)
            mixed = mixed * pscale_ref[:, gi * pgw:(gi + 1) * pgw]
            parts.append(mixed.astype(jnp.bfloat16))
        mix_rows.append(jnp.concatenate(parts, axis=-1))

    mix = jnp.concatenate(mix_rows, axis=0)
    x1 = x + jnp.dot(mix, wout_ref[...], preferred_element_type=jnp.float32)
    h2 = _rms(x1, nfw_ref[...])
    for si in range(n_str):
        x1_ref[si, 0] = x1[si * ts:(si + 1) * ts]
        _pack_rows(h2[si * ts:(si + 1) * ts], h2p_ref.at[si, 0], ts)


def _mix(x, nmw, win_b, rope, rope_base, dmat, qdec, kdec, rnw, wpool_b, pscale, wout_b, nfw, tile_decay):
    bsz, seq, d_model = x.shape
    ts = SEQ_TILE
    n_tok = bsz * seq
    rw = rnw.shape[1]
    hd = rw // RET_HEADS
    pw = pscale.shape[1]
    pgw = pw // len(POOL_WINDOWS)
    nj = seq // ts
    rows_per_tok = d_model // PACK // LANES

    def const(shape):
        return pl.BlockSpec(shape, lambda b, j: (0,) * len(shape))

    n_str = MIX_STREAMS if bsz % MIX_STREAMS == 0 else 1
    per = bsz // n_str
    in_specs = [
        pl.BlockSpec((n_str, 1, ts, d_model), lambda q, j: (0, q, j, 0)),
        const(nmw.shape), const(win_b.shape),
        const(rope.shape),
        pl.BlockSpec((1, 2, hd), lambda q, j: (j, 0, 0)),
        const(dmat.shape), const(qdec.shape), const(kdec.shape),
        const(rnw.shape), const(wpool_b.shape), const(pscale.shape), const(wout_b.shape),
        const(nfw.shape),
    ]
    out_shape = [
        jax.ShapeDtypeStruct((n_str, per, seq, d_model), jnp.float32),
        jax.ShapeDtypeStruct((n_str, per, seq * rows_per_tok, LANES), jnp.uint32),
    ]
    out_specs = [
        pl.BlockSpec((n_str, 1, ts, d_model), lambda q, j: (0, q, j, 0)),
        pl.BlockSpec((n_str, 1, ts * rows_per_tok, LANES), lambda q, j: (0, q, j, 0)),
    ]
    scratch = [
        pltpu.VMEM((n_str, RET_HEADS, hd, hd), jnp.float32),
        pltpu.VMEM((n_str, POOL_HALO + ts, pw), jnp.float32),
        pltpu.VMEM((n_str, POOL_HALO + ts, pw - pgw), jnp.float32),
        pltpu.VMEM((n_str, POOL_HALO + ts, pw - 2 * pgw), jnp.float32),
        pltpu.VMEM((n_str, POOL_HALO + ts, pw - 3 * pgw), jnp.float32),
    ]
    x1, h2p = pl.pallas_call(
        functools.partial(_mix_kernel, tile_decay=tile_decay),
        grid=(per, nj),
        in_specs=in_specs,
        out_specs=out_specs,
        out_shape=out_shape,
        scratch_shapes=scratch,
        compiler_params=pltpu.CompilerParams(
            dimension_semantics=("arbitrary", "arbitrary"),
            vmem_limit_bytes=VMEM_LIMIT_BYTES),
        name="mix",
    )(x.reshape(n_str, per, seq, d_model), nmw, win_b, rope, rope_base, dmat, qdec, kdec, rnw, wpool_b,
      pscale, wout_b, nfw)
    return x1.reshape(bsz, seq, d_model), h2p.reshape(n_tok * rows_per_tok, LANES)


def _route_kernel(h2p_ref, wrt_ref, brt_ref, utri_ref, ltri_ref,
                  lidx_ref, gate_ref, nbe_ref, lb_ref, cnt_ref, carry_ref):
    i = pl.program_id(0)
    n_exp, d_model = wrt_ref.shape
    rpt = d_model // PACK // LANES
    tr = h2p_ref.shape[0] // rpt
    sub = utri_ref.shape[0]
    chunks_per_block = TOK_BLOCK // sub

    @pl.when(i == 0)
    def _():
        carry_ref[...] = jnp.zeros_like(carry_ref)

    h2b = _unpack_rows(h2p_ref, tr, rpt)
    logits = lax.dot_general(wrt_ref[...], h2b, (((1,), (1,)), ((), ())),
                             preferred_element_type=jnp.float32) + brt_ref[...]
    eiota = lax.broadcasted_iota(jnp.int32, (n_exp, tr), 0)
    work = logits
    vals, sels = [], []
    for _ in range(TOP_K):
        m = jnp.max(work, axis=0, keepdims=True)
        idx = jnp.min(jnp.where(work == m, eiota, n_exp), axis=0, keepdims=True)
        sel = eiota == idx
        vals.append(m)
        sels.append(sel)
        work = jnp.where(sel, -jnp.inf, work)
    ex = [jnp.exp(vk - vals[0]) for vk in vals]
    den = ex[0] + ex[1] + ex[2] + ex[3]
    gate_ref[...] = jnp.concatenate([e / den for e in ex], axis=0)

    member = jnp.zeros((n_exp, tr), jnp.float32)
    for sel in sels:
        member = member + sel.astype(jnp.float32)
    carry = carry_ref[...]
    places = []
    for blk in range(tr // TOK_BLOCK):
        block_start = carry
        befores = []
        for c in range(blk * chunks_per_block, (blk + 1) * chunks_per_block):
            mc = member[:, c * sub:(c + 1) * sub]
            bc = jnp.dot(mc.astype(jnp.bfloat16), utri_ref[...], preferred_element_type=jnp.float32)
            befores.append(bc + carry[:, 0:1])
            carry = carry + jnp.sum(mc, axis=1, keepdims=True)
        nbe = carry - block_start
        nal = nbe + (nbe - STRIP_ALIGN * jnp.floor(nbe * (1.0 / STRIP_ALIGN)))
        hi = jnp.floor(nal * (1.0 / 32.0))
        lo = nal - 32.0 * hi
        lb = (32.0 * jnp.dot(ltri_ref[...], hi.astype(jnp.bfloat16), preferred_element_type=jnp.float32)
              + jnp.dot(ltri_ref[...], lo.astype(jnp.bfloat16), preferred_element_type=jnp.float32))
        nbe_ref[blk] = nbe
        lb_ref[blk] = lb
        place = jnp.concatenate(befores, axis=1) + (lb[:, 0:1] - block_start[:, 0:1])
        cols = slice(blk * TOK_BLOCK, (blk + 1) * TOK_BLOCK)
        places.append(jnp.concatenate(
            [jnp.sum(jnp.where(sel[:, cols], place, 0.0), axis=0, keepdims=True) for sel in sels], axis=0))
    lidx_ref[...] = (jnp.concatenate(places, axis=1) * float(rpt)).astype(jnp.int32)
    carry_ref[...] = carry
    cnt_ref[...] = carry


def _route(h2p, wrt_b, brt, utri, ltri):
    n_exp, d_model = wrt_b.shape
    rpt = d_model // PACK // LANES
    n_tok = h2p.shape[0] // rpt
    tr = ROUTE_TILE
    bps = tr // TOK_BLOCK

    def const(shape):
        return pl.BlockSpec(shape, lambda i: (0,) * len(shape))

    out_shape = [
        jax.ShapeDtypeStruct((TOP_K, n_tok), jnp.int32),
        jax.ShapeDtypeStruct((TOP_K, n_tok), jnp.float32),
        jax.ShapeDtypeStruct((n_tok // TOK_BLOCK, n_exp, LANES), jnp.float32),
        jax.ShapeDtypeStruct((n_tok // TOK_BLOCK, n_exp, LANES), jnp.float32),
        jax.ShapeDtypeStruct((n_exp, LANES), jnp.float32),
    ]
    out_specs = [
        pl.BlockSpec((TOP_K, tr), lambda i: (0, i)),
        pl.BlockSpec((TOP_K, tr), lambda i: (0, i)),
        pl.BlockSpec((bps, n_exp, LANES), lambda i: (i, 0, 0)),
        pl.BlockSpec((bps, n_exp, LANES), lambda i: (i, 0, 0)),
        pl.BlockSpec((n_exp, LANES), lambda i: (0, 0)),
    ]
    return pl.pallas_call(
        _route_kernel,
        grid=(n_tok // tr,),
        in_specs=[pl.BlockSpec((tr * rpt, LANES), lambda i: (i, 0)),
                  const(wrt_b.shape), const(brt.shape), const(utri.shape), const(ltri.shape)],
        out_specs=out_specs,
        out_shape=out_shape,
        scratch_shapes=[pltpu.VMEM((n_exp, LANES), jnp.float32)],
        compiler_params=pltpu.CompilerParams(
            dimension_semantics=("arbitrary",),
            vmem_limit_bytes=VMEM_LIMIT_BYTES),
        name="route",
    )(h2p, wrt_b, brt, utri, ltri)


def _for_set_bits(n, max_bit, fn):
    def piece(bit):
        size = 1 << bit
        offset = (n >> (bit + 1)) << (bit + 1)

        @pl.when((n & size) != 0)
        def _():
            fn(offset, size)

    min_bit = STRIP_ALIGN.bit_length() - 1
    rare_bit = min(max_bit + 1, RARE_STRIP_BIT)

    @pl.when(n >= (1 << rare_bit))
    def _():
        for bit in range(max_bit, rare_bit - 1, -1):
            piece(bit)
    for bit in range(rare_bit - 1, min_bit - 1, -1):
        piece(bit)


def _aligned_len(n):
    return n + (n & (STRIP_ALIGN - 1))


def _for_strips(nbe_ref, lb_ref, hrow_ref, blk, n_exp, max_bit, fn):
    def body(i, carry):
        for u in range(STRIP_UNROLL):
            k = blk * n_exp + i * STRIP_UNROLL + u
            n = _aligned_len(nbe_ref[k])
            lb = lb_ref[k]
            hrow = hrow_ref[k]
            _for_set_bits(n, max_bit, lambda offset, size: fn(lb + offset, hrow + offset, size))
        return carry
    assert n_exp % STRIP_UNROLL == 0
    lax.fori_loop(0, n_exp // STRIP_UNROLL, body, 0)


def _dispatch_kernel(lidx_ref, nbe_ref, lb_ref, hrow_ref, extra_ref, padrow_ref, padlen_ref, nused_ref,
                     h2p_ref, xs_hbm, stage_ref, zero_ref, sem, zsem, *, nblk):
    b = pl.program_id(0)
    rpt = zero_ref.shape[0] // ROW_GRANULE
    tb = h2p_ref.shape[0] // rpt
    n_exp = padrow_ref.shape[0]
    n_tok = lidx_ref.shape[0] // TOP_K
    slot_rows = stage_ref.shape[0] // 2
    max_bit = tb.bit_length() - 1
    slot = b % 2

    def strip_copy(sl, staging_row, sorted_row, size):
        src = pl.multiple_of(sl * slot_rows + staging_row * rpt, SUBLANES)
        dst = pl.multiple_of(sorted_row * rpt, SUBLANES)
        return pltpu.make_async_copy(stage_ref.at[pl.ds(src, size * rpt), :],
                                     xs_hbm.at[pl.ds(dst, size * rpt), :], sem.at[sl])

    def start_strips(blk, sl):
        _for_strips(nbe_ref, lb_ref, hrow_ref, blk, n_exp, max_bit,
                    lambda srow, hrow, size: strip_copy(sl, srow, hrow, size).start())

    def wait_strips(blk, sl):
        def sized_copy(offset, size):
            return pltpu.make_async_copy(stage_ref.at[pl.ds(sl * slot_rows, size * rpt), :],
                                         xs_hbm.at[pl.ds(0, size * rpt), :], sem.at[sl])
        sized_copy(0, tb * TOP_K).wait()
        _for_set_bits(extra_ref[blk], n_exp.bit_length() - 1, lambda o, s: sized_copy(o, s).wait())

    @pl.when(b >= 2)
    def _():
        wait_strips(b - 2, slot)

    def zero_after(i, carry):
        for u in range(STRIP_UNROLL):
            k = b * n_exp + i * STRIP_UNROLL + u
            dst = pl.multiple_of(slot * slot_rows + (lb_ref[k] + nbe_ref[k]) * rpt, rpt)
            stage_ref[pl.ds(dst, rpt), :] = jnp.zeros((rpt, LANES), jnp.uint32)
        return carry
    lax.fori_loop(0, n_exp // STRIP_UNROLL, zero_after, 0)

    def scatter_block(static_slot):
        def scatter_group(c, carry):
            tok0 = c * GATHER_UNROLL
            idx0 = b * tb + tok0
            for u in range(GATHER_UNROLL):
                row = h2p_ref[pl.ds(pl.multiple_of((tok0 + u) * rpt, rpt), rpt), :]
                for k in range(TOP_K):
                    dst = pl.multiple_of(lidx_ref[idx0 + (k * n_tok + u)], rpt)
                    stage_ref[pl.ds(static_slot * slot_rows + dst, rpt), :] = row
            return carry
        lax.fori_loop(0, tb // GATHER_UNROLL, scatter_group, 0)

    for static_slot in range(2):
        pl.when(slot == static_slot)(functools.partial(scatter_block, static_slot))

    start_strips(b, slot)

    @pl.when(b == nblk - 1)
    def _():
        if nblk >= 2:
            wait_strips(b - 1, 1 - slot)
        wait_strips(b, slot)
        zero_ref[...] = jnp.zeros_like(zero_ref)
        pad_bit = ROW_GRANULE.bit_length() - 2

        def pad_copy(e, offset, size):
            dst = pl.multiple_of((padrow_ref[e] + offset) * rpt, SUBLANES)
            return pltpu.make_async_copy(zero_ref.at[pl.ds(0, size * rpt), :],
                                         xs_hbm.at[pl.ds(dst, size * rpt), :], zsem)

        def pad_start(e, carry):
            _for_set_bits(padlen_ref[e], pad_bit, lambda o, s: pad_copy(e, o, s).start())
            return carry

        def pad_wait(e, carry):
            _for_set_bits(padlen_ref[e], pad_bit, lambda o, s: pad_copy(e, o, s).wait())
            return carry
        lax.fori_loop(0, n_exp, pad_start, 0)

        def tail_copy(i):
            dst = pl.multiple_of(i * (ROW_GRANULE * rpt), SUBLANES)
            return pltpu.make_async_copy(zero_ref, xs_hbm.at[pl.ds(dst, ROW_GRANULE * rpt), :], zsem)

        def tail_start(i, carry):
            tail_copy(i).start()
            return carry

        def tail_wait(i, carry):
            tail_copy(i).wait()
            return carry
        n_gran = xs_hbm.shape[0] // (ROW_GRANULE * rpt)
        lax.fori_loop(nused_ref[0], n_gran, tail_start, 0)
        lax.fori_loop(0, n_exp, pad_wait, 0)
        lax.fori_loop(nused_ref[0], n_gran, tail_wait, 0)


def _dispatch(lidx, nbe, lb, hrow, extra, padrow, padlen, n_used, h2p, n_rows, d_model):
    tb = TOK_BLOCK
    rpt = d_model // PACK // LANES
    n_tok = h2p.shape[0] // rpt
    n_exp = padrow.shape[0]
    slot_rows = (tb * TOP_K + n_exp * (STRIP_ALIGN - 1)) * rpt
    grid_spec = pltpu.PrefetchScalarGridSpec(
        num_scalar_prefetch=8,
        grid=(n_tok // tb,),
        in_specs=[pl.BlockSpec((tb * rpt, LANES), lambda i, *_: (i, 0))],
        out_specs=pl.BlockSpec(memory_space=pl.ANY),
        scratch_shapes=[pltpu.VMEM((2 * slot_rows, LANES), jnp.uint32),
                        pltpu.VMEM((ROW_GRANULE * rpt, LANES), jnp.uint32),
                        pltpu.SemaphoreType.DMA((2,)),
                        pltpu.SemaphoreType.DMA],
    )
    return pl.pallas_call(
        functools.partial(_dispatch_kernel, nblk=n_tok // tb),
        grid_spec=grid_spec,
        out_shape=jax.ShapeDtypeStruct((n_rows * rpt, LANES), jnp.uint32),
        compiler_params=pltpu.CompilerParams(
            dimension_semantics=("arbitrary",),
            vmem_limit_bytes=VMEM_LIMIT_BYTES),
        name="dispatch",
    )(lidx, nbe, lb, hrow, extra, padrow, padlen, n_used, h2p)


def _experts_kernel(tstart_ref, tcount_ref, wgu_hbm, bgu_ref, wd_hbm, bd_ref, xs_hbm,
                    y_hbm, wgu_f, wd_f, wgu_b, wd_b, xbuf, ybuf, pend_ref, wsem, xsem, ysem, *, n_exp):
    e = pl.program_id(0)
    d_ff = wd_f.shape[0]
    d_model = wd_f.shape[1]
    rpt = d_model // PACK // LANES
    gran_rows = ROW_GRANULE * rpt
    gpt = ROW_TILE // ROW_GRANULE
    n_gran = y_hbm.shape[0] // gran_rows
    g0 = tstart_ref[e]
    ng = tcount_ref[e]
    nt = ng // gpt
    tail = ng - nt * gpt

    def hbm_rows(ref, gran, count):
        return ref.at[pl.ds(pl.multiple_of(gran * gran_rows, SUBLANES), count * gran_rows), :]

    def x_copy(gran, sl, count=gpt):
        return pltpu.make_async_copy(hbm_rows(xs_hbm, gran, count),
                                     xbuf.at[sl, pl.ds(0, count * gran_rows), :], xsem.at[sl])

    def y_copy(gran, sl, count=gpt):
        return pltpu.make_async_copy(ybuf.at[sl, pl.ds(0, count * gran_rows), :],
                                     hbm_rows(y_hbm, gran, count), ysem.at[sl])

    def mlp(sl, tm):
        x = _unpack_rows(xbuf.at[sl], tm, rpt)
        gu = jnp.dot(x, wgu_b[...], preferred_element_type=jnp.float32) + bgu_ref[0]
        x_glu = jnp.minimum(gu[:, :d_ff], SWIGLU_LIMIT)
        x_lin = jnp.clip(gu[:, d_ff:], -SWIGLU_LIMIT, SWIGLU_LIMIT)
        act = x_glu * (1.0 / (1.0 + jnp.exp(-SWIGLU_ALPHA * x_glu))) * (x_lin + 1.0)
        y = jnp.dot(act.astype(jnp.bfloat16), wd_b[...], preferred_element_type=jnp.float32) + bd_ref[0]
        _pack_rows(y, ybuf.at[sl], tm)

    def weight_copies(ex):
        return (pltpu.make_async_copy(wgu_hbm.at[ex], wgu_f, wsem.at[0]),
                pltpu.make_async_copy(wd_hbm.at[ex], wd_f, wsem.at[1]))

    tail_slot = 2
    tail_gran = g0 + nt * gpt

    def wait_pending_tail_y():
        for count in range(1, gpt):
            @pl.when(pend_ref[0] == count)
            def _():
                y_copy(0, tail_slot, count).wait()

    @pl.when(e == 0)
    def _():
        pend_ref[0] = 0

        @pl.when(nt > 0)
        def _():
            x_copy(g0, 0).start()

    for count in range(1, gpt):
        @pl.when(tail == count)
        def _():
            x_copy(tail_gran, tail_slot, count).start()

    @pl.when(e == 0)
    def _():
        for c in weight_copies(0):
            c.start(priority=1)
    for c in weight_copies(e):
        c.wait()
    wgu_b[...] = wgu_f[...].astype(jnp.bfloat16)
    wd_b[...] = wd_f[...].astype(jnp.bfloat16)

    @pl.when(e + 1 < n_exp)
    def _():
        for c in weight_copies(e + 1):
            c.start(priority=1)

    @pl.when(ng > 0)
    def _():
        def tile_step(j, carry):
            sl = j % 2
            gran = g0 + j * gpt

            @pl.when(j + 1 < nt)
            def _():
                x_copy(gran + gpt, 1 - sl).start()
            x_copy(gran, sl).wait()

            @pl.when(j >= 2)
            def _():
                y_copy(gran - 2 * gpt, sl).wait()
            mlp(sl, ROW_TILE)
            y_copy(gran, sl).start()
            return carry
        lax.fori_loop(0, nt, tile_step, 0)

        @pl.when(nt >= 2)
        def _():
            y_copy(g0 + (nt - 2) * gpt, nt % 2).wait()

        @pl.when(nt >= 1)
        def _():
            y_copy(g0 + (nt - 1) * gpt, (nt - 1) % 2).wait()

    @pl.when(e + 1 < n_exp)
    def _():
        nxt = jnp.minimum(e + 1, n_exp - 1)

        @pl.when(tcount_ref[nxt] >= gpt)
        def _():
            x_copy(tstart_ref[nxt], 0).start()

    for count in range(1, gpt):
        @pl.when(tail == count)
        def _():
            x_copy(tail_gran, tail_slot, count).wait()
            wait_pending_tail_y()
            mlp(tail_slot, count * ROW_GRANULE)
            y_copy(tail_gran, tail_slot, count).start()
            pend_ref[0] = count

    @pl.when(e == n_exp - 1)
    def _():
        wait_pending_tail_y()
        ybuf[0, 0:gran_rows, :] = jnp.zeros((gran_rows, LANES), ybuf.dtype)

        def tail_start(i, carry):
            y_copy(i, 0, 1).start()
            return carry

        def tail_wait(i, carry):
            y_copy(i, 0, 1).wait()
            return carry
        lax.fori_loop(g0 + ng, n_gran, tail_start, 0)
        lax.fori_loop(g0 + ng, n_gran, tail_wait, 0)


def _experts(tile_start, tile_count, xs, w_gu, b_gu, w_down, b_down):
    tm = ROW_TILE
    n_exp, d_model, d_gu = w_gu.shape
    d_ff = w_down.shape[1]
    rpt = d_model // PACK // LANES
    grid_spec = pltpu.PrefetchScalarGridSpec(
        num_scalar_prefetch=2,
        grid=(n_exp,),
        in_specs=[
            pl.BlockSpec(memory_space=pl.ANY),
            pl.BlockSpec((1, 1, d_gu), lambda e, *_: (e, 0, 0)),
            pl.BlockSpec(memory_space=pl.ANY),
            pl.BlockSpec((1, 1, d_model), lambda e, *_: (e, 0, 0)),
            pl.BlockSpec(memory_space=pl.ANY),
        ],
        out_specs=pl.BlockSpec(memory_space=pl.ANY),
        scratch_shapes=[pltpu.VMEM((d_model, d_gu), jnp.float32),
                        pltpu.VMEM((d_ff, d_model), jnp.float32),
                        pltpu.VMEM((d_model, d_gu), jnp.bfloat16),
                        pltpu.VMEM((d_ff, d_model), jnp.bfloat16),
                        pltpu.VMEM((3, tm * rpt, LANES), jnp.uint32),
                        pltpu.VMEM((3, tm * rpt, LANES), jnp.uint32),
                        pltpu.SMEM((1,), jnp.int32),
                        pltpu.SemaphoreType.DMA((2,)),
                        pltpu.SemaphoreType.DMA((3,)),
                        pltpu.SemaphoreType.DMA((3,))],
    )
    return pl.pallas_call(
        functools.partial(_experts_kernel, n_exp=n_exp),
        grid_spec=grid_spec,
        out_shape=jax.ShapeDtypeStruct(xs.shape, jnp.uint32),
        compiler_params=pltpu.CompilerParams(
            dimension_semantics=("arbitrary",),
            vmem_limit_bytes=VMEM_LIMIT_BYTES),
        name="experts",
    )(tile_start, tile_count, w_gu, b_gu.reshape(n_exp, 1, d_gu), w_down,
      b_down.reshape(n_exp, 1, d_model), xs)


def _combine_kernel(lidx_ref, gate_ref, nbe_ref, lb_ref, hrow_ref, extra_ref,
                    x1_ref, nw_ref, y_hbm, out_ref, stage_ref, alo_ref, ahi_ref, sem, *, nblk):
    b = pl.program_id(0)
    tb = x1_ref.shape[0]
    rpt = alo_ref.shape[0] // tb
    n_exp = nbe_ref.shape[0] // nblk
    n_tok = lidx_ref.shape[0] // TOP_K
    slot_rows = stage_ref.shape[0] // 2
    max_bit = tb.bit_length() - 1
    slot = b % 2

    def strip_copy(sl, staging_row, sorted_row, size):
        src = pl.multiple_of(sorted_row * rpt, SUBLANES)
        dst = pl.multiple_of(sl * slot_rows + staging_row * rpt, SUBLANES)
        return pltpu.make_async_copy(y_hbm.at[pl.ds(src, size * rpt), :],
                                     stage_ref.at[pl.ds(dst, size * rpt), :], sem.at[sl])

    def start_strips(blk, sl):
        _for_strips(nbe_ref, lb_ref, hrow_ref, blk, n_exp, max_bit,
                    lambda srow, hrow, size: strip_copy(sl, srow, hrow, size).start())

    @pl.when(b == 0)
    def _():
        start_strips(0, 0)

    @pl.when(b + 1 < nblk)
    def _():
        start_strips(b + 1, 1 - slot)

    def sized_copy(offset, size):
        return pltpu.make_async_copy(y_hbm.at[pl.ds(0, size * rpt), :],
                                     stage_ref.at[pl.ds(slot * slot_rows, size * rpt), :], sem.at[slot])
    sized_copy(0, tb * TOP_K).wait()
    _for_set_bits(extra_ref[b], n_exp.bit_length() - 1, lambda o, s: sized_copy(o, s).wait())

    def gather_block(static_slot):
        def gather_group(c, carry):
            tok0 = c * GATHER_UNROLL
            idx0 = b * tb + tok0
            for u in range(GATHER_UNROLL):
                lo = hi = None
                for k in range(TOP_K):
                    src = pl.multiple_of(lidx_ref[idx0 + (k * n_tok + u)], rpt)
                    g = gate_ref[idx0 + (k * n_tok + u)]
                    l, h = _unpack_words(stage_ref[pl.ds(static_slot * slot_rows + src, rpt), :])
                    lo = g * l if lo is None else lo + g * l
                    hi = g * h if hi is None else hi + g * h
                dst = pl.multiple_of((tok0 + u) * rpt, rpt)
                alo_ref[pl.ds(dst, rpt), :] = lo
                ahi_ref[pl.ds(dst, rpt), :] = hi
            return carry
        lax.fori_loop(0, tb // GATHER_UNROLL, gather_group, 0)

    for static_slot in range(2):
        pl.when(slot == static_slot)(functools.partial(gather_block, static_slot))

    parts = [alo_ref[pl.ds(si, tb, stride=rpt), :] for si in range(rpt)]
    parts += [ahi_ref[pl.ds(si, tb, stride=rpt), :] for si in range(rpt)]
    x2 = x1_ref[...] + jnp.concatenate(parts, axis=-1)
    out_ref[...] = _rms(x2, nw_ref[...])


def _combine(lidx, gates, nbe, lb, hrow, extra, x1, norm_w, y_rows):
    tb = TOK_BLOCK
    n_tok, d_model = x1.shape
    rpt = d_model // PACK // LANES
    n_exp = nbe.shape[0] // (n_tok // tb)
    slot_rows = (tb * TOP_K + n_exp * (STRIP_ALIGN - 1)) * rpt
    grid_spec = pltpu.PrefetchScalarGridSpec(
        num_scalar_prefetch=6,
        grid=(n_tok // tb,),
        in_specs=[
            pl.BlockSpec((tb, d_model), lambda i, *_: (i, 0)),
            pl.BlockSpec((1, d_model), lambda i, *_: (0, 0)),
            pl.BlockSpec(memory_space=pl.ANY),
        ],
        out_specs=pl.BlockSpec((tb, d_model), lambda i, *_: (i, 0)),
        scratch_shapes=[pltpu.VMEM((2 * slot_rows, LANES), jnp.uint32),
                        pltpu.VMEM((tb * rpt, LANES), jnp.float32),
                        pltpu.VMEM((tb * rpt, LANES), jnp.float32),
                        pltpu.SemaphoreType.DMA((2,))],
    )
    return pl.pallas_call(
        functools.partial(_combine_kernel, nblk=n_tok // tb),
        grid_spec=grid_spec,
        out_shape=jax.ShapeDtypeStruct((n_tok, d_model), jnp.float32),
        compiler_params=pltpu.CompilerParams(
            dimension_semantics=("arbitrary",),
            vmem_limit_bytes=VMEM_LIMIT_BYTES),
        name="combine",
    )(lidx, gates, nbe, lb, hrow, extra, x1, norm_w, y_rows)


def _retention_tables(seq, hd, ts):
    half = hd // 2
    inv = ROPE_BASE ** (-jnp.arange(half, dtype=jnp.float32) / half)
    inv2 = jnp.concatenate([inv, inv])

    def cos_sin(positions):
        ang = positions[:, None] * inv2[None, :]
        return jnp.stack([jnp.cos(ang), jnp.sin(ang)], axis=1)
    rope = cos_sin(jnp.arange(ts, dtype=jnp.float32)).transpose(1, 0, 2)
    rope_base = cos_sin(jnp.arange(seq // ts, dtype=jnp.float32) * ts)
    log_g = jnp.log1p(-jnp.exp2(-5.0 - jnp.arange(RET_HEADS, dtype=jnp.float32)))
    pos = jnp.arange(ts, dtype=jnp.float32)
    dist = jnp.abs(pos[:, None] - pos[None, :])
    chunk = jnp.arange(ts, dtype=jnp.int32) // CHUNK
    visible = chunk[None, :] <= chunk[:, None]
    dmat = jnp.where(visible[None], jnp.exp(log_g[:, None, None] * dist[None]), 0.0)
    qdec = jnp.exp(log_g[:, None] * (pos + 1.0)[None, :])
    kdec = jnp.exp(log_g[:, None] * (ts - 1.0 - pos)[None, :])
    qdec = jnp.broadcast_to(qdec[:, :, None], (RET_HEADS, ts, hd))
    kdec = jnp.broadcast_to(kdec[:, :, None], (RET_HEADS, ts, hd))
    tile_decay = tuple(math.exp(math.log1p(-2.0 ** (-5.0 - h)) * ts) for h in range(RET_HEADS))
    return rope, rope_base, dmat, qdec, kdec, tile_decay


def _routing_tables(nbe, n_tok, n_exp):
    tm, tb = ROW_GRANULE, TOK_BLOCK
    nal = nbe + (nbe & (STRIP_ALIGN - 1))
    region = jnp.sum(nal, axis=0)
    tiles_per = (region + tm - 1) // tm
    tile_end = jnp.cumsum(tiles_per)
    offs = (tile_end - tiles_per) * tm
    n_tiles = (n_tok * TOP_K + (n_tok // tb) * n_exp * (STRIP_ALIGN - 1)) // tm + n_exp
    hrow = offs[None, :] + jnp.cumsum(nal, axis=0) - nal
    padrow = offs + region
    padlen = tiles_per * tm - region
    as_i32 = lambda a: a.reshape(-1).astype(jnp.int32)
    return (n_tiles, as_i32(tile_end - tiles_per), as_i32(tiles_per), as_i32(tile_end[-1:]),
            as_i32(hrow), as_i32(jnp.sum(nal, axis=1) - tb * TOP_K), as_i32(padrow), as_i32(padlen))


def _layer(x, norm_mix_w, w_in, ret_norm_w, w_pool, pool_scale, w_out, norm_ffn_w,
           w_router, b_router, w_gu, b_gu, w_down, b_down, final_w):
    bsz, seq, d_model = x.shape
    n_tok = bsz * seq
    n_exp = w_router.shape[1]
    rw = ret_norm_w.shape[0]
    hd = rw // RET_HEADS
    ts = SEQ_TILE
    bf = jnp.bfloat16
    rpt = d_model // PACK // LANES
    assert d_model % (PACK * LANES) == 0 and (STRIP_ALIGN * rpt) % SUBLANES == 0
    assert seq % ts == 0 and ts % CHUNK == 0 and n_tok % TOK_BLOCK == 0
    assert n_tok % ROUTE_TILE == 0 and ROUTE_TILE % TOK_BLOCK == 0 and TOK_BLOCK % ROUTE_CHUNK == 0
    assert TOK_BLOCK & (TOK_BLOCK - 1) == 0 and ROW_GRANULE & (ROW_GRANULE - 1) == 0
    assert ROW_TILE % ROW_GRANULE == 0

    rope, rope_base, dmat, qdec, kdec, tile_decay = _retention_tables(seq, hd, ts)
    tri = jnp.arange(ROUTE_CHUNK, dtype=jnp.int32)
    utri = (tri[:, None] < tri[None, :]).astype(bf)

    x1, h2p = _mix(
        x, norm_mix_w[None], w_in.astype(bf), rope, rope_base, dmat, qdec, kdec, ret_norm_w[None],
        w_pool.astype(bf), pool_scale[None], w_out.astype(bf), norm_ffn_w[None], tile_decay)
    eri = jnp.arange(n_exp, dtype=jnp.int32)
    ltri = (eri[None, :] < eri[:, None]).astype(bf)
    lidx, gate, nbe, lb, _ = _route(h2p, w_router.T.astype(bf), b_router[:, None], utri, ltri)
    lidx = lidx.reshape(-1)
    nbe = nbe[:, :, 0].astype(jnp.int32)
    lb = lb[:, :, 0].astype(jnp.int32).reshape(-1)

    (n_tiles, tile_start, tile_count, n_used, hrow, extra, padrow, padlen) = _routing_tables(nbe, n_tok, n_exp)
    nbe = nbe.reshape(-1)

    xs = _dispatch(lidx, nbe, lb, hrow, extra, padrow, padlen, n_used, h2p, n_tiles * ROW_GRANULE, d_model)
    y_rows = _experts(tile_start, tile_count, xs, w_gu, b_gu, w_down, b_down)
    out = _combine(lidx, gate.reshape(-1), nbe, lb, hrow, extra, x1.reshape(n_tok, d_model), final_w[None], y_rows)
    return out.reshape(bsz, seq, d_model)


def kernel(x, norm_mix_w, w_in, ret_norm_w, w_pool, pool_scale, w_out, norm_ffn_w, w_router, b_router,
           w_gu, b_gu, w_down, b_down, norm_final_w):
    depth = norm_mix_w.shape[0]
    for l in range(depth):
        last = l == depth - 1
        assert last, "stacked layers need an un-normalised combine output"
        x = _layer(x, norm_mix_w[l], w_in[l], ret_norm_w[l], w_pool[l], pool_scale[l], w_out[l],
                   norm_ffn_w[l], w_router[l], b_router[l], w_gu[l], b_gu[l], w_down[l], b_down[l],
                   norm_final_w)
    return x
```

```python
import functools
import math

import jax
import jax.numpy as jnp
from jax import lax
from jax.experimental import pallas as pl
from jax.experimental.pallas import tpu as pltpu

CHUNK = 64
RET_HEADS = 4
POOL_WINDOWS = (2, 4, 8, 16)
ROPE_BASE = 10000.0
TOP_K = 4
SWIGLU_LIMIT = 7.0
SWIGLU_ALPHA = 1.702
EPS = 1e-6

LANES = 128
SUBLANES = 8
VMEM_LIMIT_BYTES = 56 * 1024 * 1024

SEQ_TILE = 256
MIX_STREAMS = 4
ROUTE_TILE = 2048
ROUTE_CHUNK = 256
ROW_GRANULE = 128
ROW_TILE = 512
FF_CHUNKS = 2
TOK_BLOCK = 1024
POOL_HALO = 16
PACK = 2
STRIP_ALIGN = 2
GATHER_UNROLL = 32
STRIP_UNROLL = 4
RARE_STRIP_BIT = 8


def _rms(x, w):
    return x * lax.rsqrt(jnp.mean(x * x, axis=-1, keepdims=True) + EPS) * w


def _bf16_bits_hi(x):
    return lax.bitcast_convert_type(x.astype(jnp.bfloat16).astype(jnp.float32), jnp.uint32)


def _pack_rows(x, out_ref, n_rows):
    half = x.shape[1] // PACK
    packed = (_bf16_bits_hi(x[:, :half]) >> 16) | _bf16_bits_hi(x[:, half:])
    rpt = half // LANES
    for si in range(rpt):
        out_ref[pl.ds(si, n_rows, stride=rpt), :] = packed[:, si * LANES:(si + 1) * LANES]


def _unpack_words(u):
    lo = lax.bitcast_convert_type(u << 16, jnp.float32)
    hi = lax.bitcast_convert_type(u & jnp.uint32(0xFFFF0000), jnp.float32)
    return lo, hi


def _unpack_rows(g_ref, n_rows, rpt):
    lo, hi = [], []
    for si in range(rpt):
        l, h = _unpack_words(g_ref[pl.ds(si, n_rows, stride=rpt), :])
        lo.append(l.astype(jnp.bfloat16))
        hi.append(h.astype(jnp.bfloat16))
    return jnp.concatenate(lo + hi, axis=-1)


def _mix_kernel(x_ref, nmw_ref, win_ref, rope_ref, rope_base_ref, dmat_ref, qdec_ref, kdec_ref,
                rnw_ref, wpool_ref, pscale_ref, wout_ref, nfw_ref,
                x1_ref, h2p_ref,
                state_ref, p1_ref, p2_ref, p4_ref, p8_ref,
                *, tile_decay):
    j = pl.program_id(1)
    n_str = x_ref.shape[0]
    ts = x_ref.shape[2]
    n_heads = RET_HEADS
    rw = rnw_ref.shape[1]
    hd = rw // n_heads
    pgw = wpool_ref.shape[1]
    halo = POOL_HALO
    pool_refs = (p1_ref, p2_ref, p4_ref, p8_ref)

    @pl.when(j == 0)
    def _():
        state_ref[...] = jnp.zeros_like(state_ref)
        for r in pool_refs:
            r[:, 0:halo, :] = jnp.zeros((n_str, halo, r.shape[2]), jnp.float32)

    @pl.when(j > 0)
    def _():
        for r in pool_refs:
            r[:, 0:halo, :] = r[:, ts:ts + halo, :]

    x = jnp.concatenate([x_ref[si, 0] for si in range(n_str)], axis=0)
    h = _rms(x, nmw_ref[...]).astype(jnp.bfloat16)
    z_all = jnp.dot(h, win_ref[...], preferred_element_type=jnp.float32)

    rot = rope_ref[...]
    base = rope_base_ref[0]
    cos = base[0:1] * rot[0] - base[1:2] * rot[1]
    sin_plain = base[1:2] * rot[0] + base[0:1] * rot[1]
    half_sign = jnp.where(lax.broadcasted_iota(jnp.int32, (1, hd), 1) < hd // 2, -1.0, 1.0)
    sin = sin_plain * half_sign
    k_scale = hd ** -0.5
    tpos = (j * ts + lax.broadcasted_iota(jnp.int32, (ts, pgw), 0) + 1).astype(jnp.float32)

    mix_rows = []
    for si in range(n_str):
        z = z_all[si * ts:(si + 1) * ts]
        parts = []
        for hh in range(n_heads):
            q = z[:, hh * hd:(hh + 1) * hd]
            k = z[:, rw + hh * hd:rw + (hh + 1) * hd]
            v = z[:, 2 * rw + hh * hd:2 * rw + (hh + 1) * hd]
            g = z[:, 3 * rw + hh * hd:3 * rw + (hh + 1) * hd]
            q = q * cos + pltpu.roll(q, hd // 2, 1) * sin
            k = (k * cos + pltpu.roll(k, hd // 2, 1) * sin) * k_scale
            qb = q.astype(jnp.bfloat16)
            kb = k.astype(jnp.bfloat16)
            vb = v.astype(jnp.bfloat16)
            s = lax.dot_general(qb, kb, (((1,), (1,)), ((), ())), preferred_element_type=jnp.float32)
            pmat = (s * dmat_ref[hh]).astype(jnp.bfloat16)
            o = jnp.dot(pmat, vb, preferred_element_type=jnp.float32)
            state = state_ref[si, hh]
            qd = (q * qdec_ref[hh]).astype(jnp.bfloat16)
            o = o + jnp.dot(qd, state.astype(jnp.bfloat16), preferred_element_type=jnp.float32)
            kd = (k * kdec_ref[hh]).astype(jnp.bfloat16)
            u = lax.dot_general(kd, vb, (((0,), (0,)), ((), ())), preferred_element_type=jnp.float32)
            state_ref[si, hh] = state * tile_decay[hh] + u
            o = o * lax.rsqrt(jnp.mean(o * o, axis=-1, keepdims=True) + EPS)
            o = o * rnw_ref[:, hh * hd:(hh + 1) * hd] * (g * (1.0 / (1.0 + jnp.exp(-g))))
            parts.append(o.astype(jnp.bfloat16))

        p = z[:, 4 * rw:]
        p1_ref[si, halo:halo + ts, :] = p
        s2 = p + p1_ref[si, halo - 1:halo - 1 + ts, :]
        p2_ref[si, halo:halo + ts, :] = s2[:, pgw:]
        s4 = s2[:, pgw:] + p2_ref[si, halo - 2:halo - 2 + ts, :]
        p4_ref[si, halo:halo + ts, :] = s4[:, pgw:]
        s8 = s4[:, pgw:] + p4_ref[si, halo - 4:halo - 4 + ts, :]
        p8_ref[si, halo:halo + ts, :] = s8[:, pgw:]
        s16 = s8[:, pgw:] + p8_ref[si, halo - 8:halo - 8 + ts, :]
        sums = (s2[:, :pgw], s4[:, :pgw], s8[:, :pgw], s16)
        for gi, w in enumerate(POOL_WINDOWS):
            cnt = jnp.minimum(tpos, float(w))
            pooled = sums[gi] / cnt - p[:, gi * pgw:(gi + 1) * pgw]
            mixed = jnp.dot(pooled.astype(jnp.bfloat16), wpool_ref[gi], preferred_element_type=jnp.float32)
            mixed = mixed * pscale_ref[:, gi * pgw:(gi + 1) * pgw]
            parts.append(mixed.astype(jnp.bfloat16))
        mix_rows.append(jnp.concatenate(parts, axis=-1))

    mix = jnp.concatenate(mix_rows, axis=0)
    x1 = x + jnp.dot(mix, wout_ref[...], preferred_element_type=jnp.float32)
    h2 = _rms(x1, nfw_ref[...])
    for si in range(n_str):
        x1_ref[si, 0] = x1[si * ts:(si + 1) * ts]
        _pack_rows(h2[si * ts:(si + 1) * ts], h2p_ref.at[si, 0], ts)


def _mix(x, nmw, win_b, rope, rope_base, dmat, qdec, kdec, rnw, wpool_b, pscale, wout_b, nfw, tile_decay):
    bsz, seq, d_model = x.shape
    ts = SEQ_TILE
    n_tok = bsz * seq
    rw = rnw.shape[1]
    hd = rw // RET_HEADS
    pw = pscale.shape[1]
    pgw = pw // len(POOL_WINDOWS)
    nj = seq // ts
    rows_per_tok = d_model // PACK // LANES

    def const(shape):
        return pl.BlockSpec(shape, lambda b, j: (0,) * len(shape))

    n_str = MIX_STREAMS if bsz % MIX_STREAMS == 0 else 1
    per = bsz // n_str
    in_specs = [
        pl.BlockSpec((n_str, 1, ts, d_model), lambda q, j: (0, q, j, 0)),
        const(nmw.shape), const(win_b.shape),
        const(rope.shape),
        pl.BlockSpec((1, 2, hd), lambda q, j: (j, 0, 0)),
        const(dmat.shape), const(qdec.shape), const(kdec.shape),
        const(rnw.shape), const(wpool_b.shape), const(pscale.shape), const(wout_b.shape),
        const(nfw.shape),
    ]
    out_shape = [
        jax.ShapeDtypeStruct((n_str, per, seq, d_model), jnp.float32),
        jax.ShapeDtypeStruct((n_str, per, seq * rows_per_tok, LANES), jnp.uint32),
    ]
    out_specs = [
        pl.BlockSpec((n_str, 1, ts, d_model), lambda q, j: (0, q, j, 0)),
        pl.BlockSpec((n_str, 1, ts * rows_per_tok, LANES), lambda q, j: (0, q, j, 0)),
    ]
    scratch = [
        pltpu.VMEM((n_str, RET_HEADS, hd, hd), jnp.float32),
        pltpu.VMEM((n_str, POOL_HALO + ts, pw), jnp.float32),
        pltpu.VMEM((n_str, POOL_HALO + ts, pw - pgw), jnp.float32),
        pltpu.VMEM((n_str, POOL_HALO + ts, pw - 2 * pgw), jnp.float32),
        pltpu.VMEM((n_str, POOL_HALO + ts, pw - 3 * pgw), jnp.float32),
    ]
    x1, h2p = pl.pallas_call(
        functools.partial(_mix_kernel, tile_decay=tile_decay),
        grid=(per, nj),
        in_specs=in_specs,
        out_specs=out_specs,
        out_shape=out_shape,
        scratch_shapes=scratch,
        compiler_params=pltpu.CompilerParams(
            dimension_semantics=("arbitrary", "arbitrary"),
            vmem_limit_bytes=VMEM_LIMIT_BYTES),
        name="mix",
    )(x.reshape(n_str, per, seq, d_model), nmw, win_b, rope, rope_base, dmat, qdec, kdec, rnw, wpool_b,
      pscale, wout_b, nfw)
    return x1.reshape(bsz, seq, d_model), h2p.reshape(n_tok * rows_per_tok, LANES)


def _route_kernel(h2p_ref, wrt_ref, brt_ref, utri_ref, ltri_ref,
                  lidx_ref, gate_ref, nbe_ref, lb_ref, cnt_ref, carry_ref):
    i = pl.program_id(0)
    n_exp, d_model = wrt_ref.shape
    rpt = d_model // PACK // LANES
    tr = h2p_ref.shape[0] // rpt
    sub = utri_ref.shape[0]
    chunks_per_block = TOK_BLOCK // sub

    @pl.when(i == 0)
    def _():
        carry_ref[...] = jnp.zeros_like(carry_ref)

    h2b = _unpack_rows(h2p_ref, tr, rpt)
    logits = lax.dot_general(wrt_ref[...], h2b, (((1,), (1,)), ((), ())),
                             preferred_element_type=jnp.float32) + brt_ref[...]
    eiota = lax.broadcasted_iota(jnp.int32, (n_exp, tr), 0)
    work = logits
    vals, sels = [], []
    for _ in range(TOP_K):
        m = jnp.max(work, axis=0, keepdims=True)
        idx = jnp.min(jnp.where(work == m, eiota, n_exp), axis=0, keepdims=True)
        sel = eiota == idx
        vals.append(m)
        sels.append(sel)
        work = jnp.where(sel, -jnp.inf, work)
    ex = [jnp.exp(vk - vals[0]) for vk in vals]
    den = ex[0] + ex[1] + ex[2] + ex[3]
    gate_ref[...] = jnp.concatenate([e / den for e in ex], axis=0)

    member = jnp.zeros((n_exp, tr), jnp.float32)
    for sel in sels:
        member = member + sel.astype(jnp.float32)
    carry = carry_ref[...]
    places = []
    for blk in range(tr // TOK_BLOCK):
        block_start = carry
        befores = []
        for c in range(blk * chunks_per_block, (blk + 1) * chunks_per_block):
            mc = member[:, c * sub:(c + 1) * sub]
            bc = jnp.dot(mc.astype(jnp.bfloat16), utri_ref[...], preferred_element_type=jnp.float32)
            befores.append(bc + carry[:, 0:1])
            carry = carry + jnp.sum(mc, axis=1, keepdims=True)
        nbe = carry - block_start
        nal = nbe + (nbe - STRIP_ALIGN * jnp.floor(nbe * (1.0 / STRIP_ALIGN)))
        hi = jnp.floor(nal * (1.0 / 32.0))
        lo = nal - 32.0 * hi
        lb = (32.0 * jnp.dot(ltri_ref[...], hi.astype(jnp.bfloat16), preferred_element_type=jnp.float32)
              + jnp.dot(ltri_ref[...], lo.astype(jnp.bfloat16), preferred_element_type=jnp.float32))
        nbe_ref[blk] = nbe
        lb_ref[blk] = lb
        place = jnp.concatenate(befores, axis=1) + (lb[:, 0:1] - block_start[:, 0:1])
        cols = slice(blk * TOK_BLOCK, (blk + 1) * TOK_BLOCK)
        places.append(jnp.concatenate(
            [jnp.sum(jnp.where(sel[:, cols], place, 0.0), axis=0, keepdims=True) for sel in sels], axis=0))
    lidx_ref[...] = (jnp.concatenate(places, axis=1) * float(rpt)).astype(jnp.int32)
    carry_ref[...] = carry
    cnt_ref[...] = carry


def _route(h2p, wrt_b, brt, utri, ltri):
    n_exp, d_model = wrt_b.shape
    rpt = d_model // PACK // LANES
    n_tok = h2p.shape[0] // rpt
    tr = ROUTE_TILE
    bps = tr // TOK_BLOCK

    def const(shape):
        return pl.BlockSpec(shape, lambda i: (0,) * len(shape))

    out_shape = [
        jax.ShapeDtypeStruct((TOP_K, n_tok), jnp.int32),
        jax.ShapeDtypeStruct((TOP_K, n_tok), jnp.float32),
        jax.ShapeDtypeStruct((n_tok // TOK_BLOCK, n_exp, LANES), jnp.float32),
        jax.ShapeDtypeStruct((n_tok // TOK_BLOCK, n_exp, LANES), jnp.float32),
        jax.ShapeDtypeStruct((n_exp, LANES), jnp.float32),
    ]
    out_specs = [
        pl.BlockSpec((TOP_K, tr), lambda i: (0, i)),
        pl.BlockSpec((TOP_K, tr), lambda i: (0, i)),
        pl.BlockSpec((bps, n_exp, LANES), lambda i: (i, 0, 0)),
        pl.BlockSpec((bps, n_exp, LANES), lambda i: (i, 0, 0)),
        pl.BlockSpec((n_exp, LANES), lambda i: (0, 0)),
    ]
    return pl.pallas_call(
        _route_kernel,
        grid=(n_tok // tr,),
        in_specs=[pl.BlockSpec((tr * rpt, LANES), lambda i: (i, 0)),
                  const(wrt_b.shape), const(brt.shape), const(utri.shape), const(ltri.shape)],
        out_specs=out_specs,
        out_shape=out_shape,
        scratch_shapes=[pltpu.VMEM((n_exp, LANES), jnp.float32)],
        compiler_params=pltpu.CompilerParams(
            dimension_semantics=("arbitrary",),
            vmem_limit_bytes=VMEM_LIMIT_BYTES),
        name="route",
    )(h2p, wrt_b, brt, utri, ltri)


def _for_set_bits(n, max_bit, fn):
    def piece(bit):
        size = 1 << bit
        offset = (n >> (bit + 1)) << (bit + 1)

        @pl.when((n & size) != 0)
        def _():
            fn(offset, size)

    min_bit = STRIP_ALIGN.bit_length() - 1
    rare_bit = min(max_bit + 1, RARE_STRIP_BIT)

    @pl.when(n >= (1 << rare_bit))
    def _():
        for bit in range(max_bit, rare_bit - 1, -1):
            piece(bit)
    for bit in range(rare_bit - 1, min_bit - 1, -1):
        piece(bit)


def _aligned_len(n):
    return n + (n & (STRIP_ALIGN - 1))


def _for_strips(nbe_ref, lb_ref, hrow_ref, blk, n_exp, max_bit, fn):
    def body(i, carry):
        for u in range(STRIP_UNROLL):
            k = blk * n_exp + i * STRIP_UNROLL + u
            n = _aligned_len(nbe_ref[k])
            lb = lb_ref[k]
            hrow = hrow_ref[k]
            _for_set_bits(n, max_bit, lambda offset, size: fn(lb + offset, hrow + offset, size))
        return carry
    assert n_exp % STRIP_UNROLL == 0
    lax.fori_loop(0, n_exp // STRIP_UNROLL, body, 0)


def _dispatch_kernel(lidx_ref, nbe_ref, lb_ref, hrow_ref, extra_ref, padrow_ref, padlen_ref, nused_ref,
                     h2p_ref, xs_hbm, stage_ref, zero_ref, sem, zsem, *, nblk):
    b = pl.program_id(0)
    rpt = zero_ref.shape[0] // ROW_GRANULE
    tb = h2p_ref.shape[0] // rpt
    n_exp = padrow_ref.shape[0]
    n_tok = lidx_ref.shape[0] // TOP_K
    slot_rows = stage_ref.shape[0] // 2
    max_bit = tb.bit_length() - 1
    slot = b % 2

    def strip_copy(sl, staging_row, sorted_row, size):
        src = pl.multiple_of(sl * slot_rows + staging_row * rpt, SUBLANES)
        dst = pl.multiple_of(sorted_row * rpt, SUBLANES)
        return pltpu.make_async_copy(stage_ref.at[pl.ds(src, size * rpt), :],
                                     xs_hbm.at[pl.ds(dst, size * rpt), :], sem.at[sl])

    def start_strips(blk, sl):
        _for_strips(nbe_ref, lb_ref, hrow_ref, blk, n_exp, max_bit,
                    lambda srow, hrow, size: strip_copy(sl, srow, hrow, size).start())

    def wait_strips(blk, sl):
        def sized_copy(offset, size):
            return pltpu.make_async_copy(stage_ref.at[pl.ds(sl * slot_rows, size * rpt), :],
                                         xs_hbm.at[pl.ds(0, size * rpt), :], sem.at[sl])
        sized_copy(0, tb * TOP_K).wait()
        _for_set_bits(extra_ref[blk], n_exp.bit_length() - 1, lambda o, s: sized_copy(o, s).wait())

    @pl.when(b >= 2)
    def _():
        wait_strips(b - 2, slot)

    def zero_after(i, carry):
        for u in range(STRIP_UNROLL):
            k = b * n_exp + i * STRIP_UNROLL + u
            dst = pl.multiple_of(slot * slot_rows + (lb_ref[k] + nbe_ref[k]) * rpt, rpt)
            stage_ref[pl.ds(dst, rpt), :] = jnp.zeros((rpt, LANES), jnp.uint32)
        return carry
    lax.fori_loop(0, n_exp // STRIP_UNROLL, zero_after, 0)

    def scatter_block(static_slot):
        def scatter_group(c, carry):
            tok0 = c * GATHER_UNROLL
            idx0 = b * tb + tok0
            for u in range(GATHER_UNROLL):
                row = h2p_ref[pl.ds(pl.multiple_of((tok0 + u) * rpt, rpt), rpt), :]
                for k in range(TOP_K):
                    dst = pl.multiple_of(lidx_ref[idx0 + (k * n_tok + u)], rpt)
                    stage_ref[pl.ds(static_slot * slot_rows + dst, rpt), :] = row
            return carry
        lax.fori_loop(0, tb // GATHER_UNROLL, scatter_group, 0)

    for static_slot in range(2):
        pl.when(slot == static_slot)(functools.partial(scatter_block, static_slot))

    start_strips(b, slot)

    @pl.when(b == nblk - 1)
    def _():
        if nblk >= 2:
            wait_strips(b - 1, 1 - slot)
        wait_strips(b, slot)
        zero_ref[...] = jnp.zeros_like(zero_ref)
        pad_bit = ROW_GRANULE.bit_length() - 2

        def pad_copy(e, offset, size):
            dst = pl.multiple_of((padrow_ref[e] + offset) * rpt, SUBLANES)
            return pltpu.make_async_copy(zero_ref.at[pl.ds(0, size * rpt), :],
                                         xs_hbm.at[pl.ds(dst, size * rpt), :], zsem)

        def pad_start(e, carry):
            _for_set_bits(padlen_ref[e], pad_bit, lambda o, s: pad_copy(e, o, s).start())
            return carry

        def pad_wait(e, carry):
            _for_set_bits(padlen_ref[e], pad_bit, lambda o, s: pad_copy(e, o, s).wait())
            return carry
        lax.fori_loop(0, n_exp, pad_start, 0)

        def tail_copy(i):
            dst = pl.multiple_of(i * (ROW_GRANULE * rpt), SUBLANES)
            return pltpu.make_async_copy(zero_ref, xs_hbm.at[pl.ds(dst, ROW_GRANULE * rpt), :], zsem)

        def tail_start(i, carry):
            tail_copy(i).start()
            return carry

        def tail_wait(i, carry):
            tail_copy(i).wait()
            return carry
        n_gran = xs_hbm.shape[0] // (ROW_GRANULE * rpt)
        lax.fori_loop(nused_ref[0], n_gran, tail_start, 0)
        lax.fori_loop(0, n_exp, pad_wait, 0)
        lax.fori_loop(nused_ref[0], n_gran, tail_wait, 0)


def _dispatch(lidx, nbe, lb, hrow, extra, padrow, padlen, n_used, h2p, n_rows, d_model):
    tb = TOK_BLOCK
    rpt = d_model // PACK // LANES
    n_tok = h2p.shape[0] // rpt
    n_exp = padrow.shape[0]
    slot_rows = (tb * TOP_K + n_exp * (STRIP_ALIGN - 1)) * rpt
    grid_spec = pltpu.PrefetchScalarGridSpec(
        num_scalar_prefetch=8,
        grid=(n_tok // tb,),
        in_specs=[pl.BlockSpec((tb * rpt, LANES), lambda i, *_: (i, 0))],
        out_specs=pl.BlockSpec(memory_space=pl.ANY),
        scratch_shapes=[pltpu.VMEM((2 * slot_rows, LANES), jnp.uint32),
                        pltpu.VMEM((ROW_GRANULE * rpt, LANES), jnp.uint32),
                        pltpu.SemaphoreType.DMA((2,)),
                        pltpu.SemaphoreType.DMA],
    )
    return pl.pallas_call(
        functools.partial(_dispatch_kernel, nblk=n_tok // tb),
        grid_spec=grid_spec,
        out_shape=jax.ShapeDtypeStruct((n_rows * rpt, LANES), jnp.uint32),
        compiler_params=pltpu.CompilerParams(
            dimension_semantics=("arbitrary",),
            vmem_limit_bytes=VMEM_LIMIT_BYTES),
        name="dispatch",
    )(lidx, nbe, lb, hrow, extra, padrow, padlen, n_used, h2p)


def _experts_kernel(tstart_ref, tcount_ref, wgu_hbm, bgu_ref, wd_hbm, bd_ref, xs_hbm,
                    y_hbm, wgu_f, wd_f, wgu_b, wd_b, xbuf, ybuf, pend_ref, wsem, xsem, ysem, *, n_exp):
    e = pl.program_id(0)
    d_ff = wd_f.shape[0]
    d_model = wd_f.shape[1]
    rpt = d_model // PACK // LANES
    gran_rows = ROW_GRANULE * rpt
    gpt = ROW_TILE // ROW_GRANULE
    n_gran = y_hbm.shape[0] // gran_rows
    g0 = tstart_ref[e]
    ng = tcount_ref[e]
    nt = ng // gpt
    tail = ng - nt * gpt

    def hbm_rows(ref, gran, count):
        return ref.at[pl.ds(pl.multiple_of(gran * gran_rows, SUBLANES), count * gran_rows), :]

    def x_copy(gran, sl, count=gpt):
        return pltpu.make_async_copy(hbm_rows(xs_hbm, gran, count),
                                     xbuf.at[sl, pl.ds(0, count * gran_rows), :], xsem.at[sl])

    def y_copy(gran, sl, count=gpt):
        return pltpu.make_async_copy(ybuf.at[sl, pl.ds(0, count * gran_rows), :],
                                     hbm_rows(y_hbm, gran, count), ysem.at[sl])

    def mlp(sl, tm):
        x = _unpack_rows(xbuf.at[sl], tm, rpt)
        fc = d_ff // FF_CHUNKS
        y = None
        for c in range(FF_CHUNKS):
            glu = jnp.dot(x, wgu_b[:, c * fc:(c + 1) * fc], preferred_element_type=jnp.float32)
            lin = jnp.dot(x, wgu_b[:, d_ff + c * fc:d_ff + (c + 1) * fc], preferred_element_type=jnp.float32)
            x_glu = jnp.minimum(glu + bgu_ref[0, :, c * fc:(c + 1) * fc], SWIGLU_LIMIT)
            x_lin = jnp.clip(lin + bgu_ref[0, :, d_ff + c * fc:d_ff + (c + 1) * fc], -SWIGLU_LIMIT, SWIGLU_LIMIT)
            act = x_glu * (1.0 / (1.0 + jnp.exp(-SWIGLU_ALPHA * x_glu))) * (x_lin + 1.0)
            part = jnp.dot(act.astype(jnp.bfloat16), wd_b[c * fc:(c + 1) * fc, :],
                           preferred_element_type=jnp.float32)
            y = part if y is None else y + part
        _pack_rows(y + bd_ref[0], ybuf.at[sl], tm)

    def weight_copies(ex):
        return (pltpu.make_async_copy(wgu_hbm.at[ex], wgu_f, wsem.at[0]),
                pltpu.make_async_copy(wd_hbm.at[ex], wd_f, wsem.at[1]))

    tail_slot = 2
    tail_gran = g0 + nt * gpt

    def wait_pending_tail_y():
        for count in range(1, gpt):
            @pl.when(pend_ref[0] == count)
            def _():
                y_copy(0, tail_slot, count).wait()

    @pl.when(e == 0)
    def _():
        pend_ref[0] = 0

        @pl.when(nt > 0)
        def _():
            x_copy(g0, 0).start()

    for count in range(1, gpt):
        @pl.when(tail == count)
        def _():
            x_copy(tail_gran, tail_slot, count).start()

    @pl.when(e == 0)
    def _():
        for c in weight_copies(0):
            c.start(priority=1)
    for c in weight_copies(e):
        c.wait()
    wgu_b[...] = wgu_f[...].astype(jnp.bfloat16)
    wd_b[...] = wd_f[...].astype(jnp.bfloat16)

    @pl.when(e + 1 < n_exp)
    def _():
        for c in weight_copies(e + 1):
            c.start(priority=1)

    @pl.when(ng > 0)
    def _():
        def tile_step(j, carry):
            sl = j % 2
            gran = g0 + j * gpt

            @pl.when(j + 1 < nt)
            def _():
                x_copy(gran + gpt, 1 - sl).start()
            x_copy(gran, sl).wait()

            @pl.when(j >= 2)
            def _():
                y_copy(gran - 2 * gpt, sl).wait()
            mlp(sl, ROW_TILE)
            y_copy(gran, sl).start()
            return carry
        lax.fori_loop(0, nt, tile_step, 0)

        @pl.when(nt >= 2)
        def _():
            y_copy(g0 + (nt - 2) * gpt, nt % 2).wait()

        @pl.when(nt >= 1)
        def _():
            y_copy(g0 + (nt - 1) * gpt, (nt - 1) % 2).wait()

    @pl.when(e + 1 < n_exp)
    def _():
        nxt = jnp.minimum(e + 1, n_exp - 1)

        @pl.when(tcount_ref[nxt] >= gpt)
        def _():
            x_copy(tstart_ref[nxt], 0).start()

    for count in range(1, gpt):
        @pl.when(tail == count)
        def _():
            x_copy(tail_gran, tail_slot, count).wait()
            wait_pending_tail_y()
            mlp(tail_slot, count * ROW_GRANULE)
            y_copy(tail_gran, tail_slot, count).start()
            pend_ref[0] = count

    @pl.when(e == n_exp - 1)
    def _():
        wait_pending_tail_y()
        ybuf[0, 0:gran_rows, :] = jnp.zeros((gran_rows, LANES), ybuf.dtype)

        def tail_start(i, carry):
            y_copy(i, 0, 1).start()
            return carry

        def tail_wait(i, carry):
            y_copy(i, 0, 1).wait()
            return carry
        lax.fori_loop(g0 + ng, n_gran, tail_start, 0)
        lax.fori_loop(g0 + ng, n_gran, tail_wait, 0)


def _experts(tile_start, tile_count, xs, w_gu, b_gu, w_down, b_down):
    tm = ROW_TILE
    n_exp, d_model, d_gu = w_gu.shape
    d_ff = w_down.shape[1]
    rpt = d_model // PACK // LANES
    grid_spec = pltpu.PrefetchScalarGridSpec(
        num_scalar_prefetch=2,
        grid=(n_exp,),
        in_specs=[
            pl.BlockSpec(memory_space=pl.ANY),
            pl.BlockSpec((1, 1, d_gu), lambda e, *_: (e, 0, 0)),
            pl.BlockSpec(memory_space=pl.ANY),
            pl.BlockSpec((1, 1, d_model), lambda e, *_: (e, 0, 0)),
            pl.BlockSpec(memory_space=pl.ANY),
        ],
        out_specs=pl.BlockSpec(memory_space=pl.ANY),
        scratch_shapes=[pltpu.VMEM((d_model, d_gu), jnp.float32),
                        pltpu.VMEM((d_ff, d_model), jnp.float32),
                        pltpu.VMEM((d_model, d_gu), jnp.bfloat16),
                        pltpu.VMEM((d_ff, d_model), jnp.bfloat16),
                        pltpu.VMEM((3, tm * rpt, LANES), jnp.uint32),
                        pltpu.VMEM((3, tm * rpt, LANES), jnp.uint32),
                        pltpu.SMEM((1,), jnp.int32),
                        pltpu.SemaphoreType.DMA((2,)),
                        pltpu.SemaphoreType.DMA((3,)),
                        pltpu.SemaphoreType.DMA((3,))],
    )
    return pl.pallas_call(
        functools.partial(_experts_kernel, n_exp=n_exp),
        grid_spec=grid_spec,
        out_shape=jax.ShapeDtypeStruct(xs.shape, jnp.uint32),
        compiler_params=pltpu.CompilerParams(
            dimension_semantics=("arbitrary",),
            vmem_limit_bytes=VMEM_LIMIT_BYTES),
        name="experts",
    )(tile_start, tile_count, w_gu, b_gu.reshape(n_exp, 1, d_gu), w_down,
      b_down.reshape(n_exp, 1, d_model), xs)


def _combine_kernel(lidx_ref, gate_ref, nbe_ref, lb_ref, hrow_ref, extra_ref,
                    x1_ref, nw_ref, y_hbm, out_ref, stage_ref, alo_ref, ahi_ref, sem, *, nblk):
    b = pl.program_id(0)
    tb = x1_ref.shape[0]
    rpt = alo_ref.shape[0] // tb
    n_exp = nbe_ref.shape[0] // nblk
    n_tok = lidx_ref.shape[0] // TOP_K
    slot_rows = stage_ref.shape[0] // 2
    max_bit = tb.bit_length() - 1
    slot = b % 2

    def strip_copy(sl, staging_row, sorted_row, size):
        src = pl.multiple_of(sorted_row * rpt, SUBLANES)
        dst = pl.multiple_of(sl * slot_rows + staging_row * rpt, SUBLANES)
        return pltpu.make_async_copy(y_hbm.at[pl.ds(src, size * rpt), :],
                                     stage_ref.at[pl.ds(dst, size * rpt), :], sem.at[sl])

    def start_strips(blk, sl):
        _for_strips(nbe_ref, lb_ref, hrow_ref, blk, n_exp, max_bit,
                    lambda srow, hrow, size: strip_copy(sl, srow, hrow, size).start())

    @pl.when(b == 0)
    def _():
        start_strips(0, 0)

    @pl.when(b + 1 < nblk)
    def _():
        start_strips(b + 1, 1 - slot)

    def sized_copy(offset, size):
        return pltpu.make_async_copy(y_hbm.at[pl.ds(0, size * rpt), :],
                                     stage_ref.at[pl.ds(slot * slot_rows, size * rpt), :], sem.at[slot])
    sized_copy(0, tb * TOP_K).wait()
    _for_set_bits(extra_ref[b], n_exp.bit_length() - 1, lambda o, s: sized_copy(o, s).wait())

    def gather_block(static_slot):
        def gather_group(c, carry):
            tok0 = c * GATHER_UNROLL
            idx0 = b * tb + tok0
            for u in range(GATHER_UNROLL):
                lo = hi = None
                for k in range(TOP_K):
                    src = pl.multiple_of(lidx_ref[idx0 + (k * n_tok + u)], rpt)
                    g = gate_ref[idx0 + (k * n_tok + u)]
                    l, h = _unpack_words(stage_ref[pl.ds(static_slot * slot_rows + src, rpt), :])
                    lo = g * l if lo is None else lo + g * l
                    hi = g * h if hi is None else hi + g * h
                dst = pl.multiple_of((tok0 + u) * rpt, rpt)
                alo_ref[pl.ds(dst, rpt), :] = lo
                ahi_ref[pl.ds(dst, rpt), :] = hi
            return carry
        lax.fori_loop(0, tb // GATHER_UNROLL, gather_group, 0)

    for static_slot in range(2):
        pl.when(slot == static_slot)(functools.partial(gather_block, static_slot))

    parts = [alo_ref[pl.ds(si, tb, stride=rpt), :] for si in range(rpt)]
    parts += [ahi_ref[pl.ds(si, tb, stride=rpt), :] for si in range(rpt)]
    x2 = x1_ref[...] + jnp.concatenate(parts, axis=-1)
    out_ref[...] = _rms(x2, nw_ref[...])


def _combine(lidx, gates, nbe, lb, hrow, extra, x1, norm_w, y_rows):
    tb = TOK_BLOCK
    n_tok, d_model = x1.shape
    rpt = d_model // PACK // LANES
    n_exp = nbe.shape[0] // (n_tok // tb)
    slot_rows = (tb * TOP_K + n_exp * (STRIP_ALIGN - 1)) * rpt
    grid_spec = pltpu.PrefetchScalarGridSpec(
        num_scalar_prefetch=6,
        grid=(n_tok // tb,),
        in_specs=[
            pl.BlockSpec((tb, d_model), lambda i, *_: (i, 0)),
            pl.BlockSpec((1, d_model), lambda i, *_: (0, 0)),
            pl.BlockSpec(memory_space=pl.ANY),
        ],
        out_specs=pl.BlockSpec((tb, d_model), lambda i, *_: (i, 0)),
        scratch_shapes=[pltpu.VMEM((2 * slot_rows, LANES), jnp.uint32),
                        pltpu.VMEM((tb * rpt, LANES), jnp.float32),
                        pltpu.VMEM((tb * rpt, LANES), jnp.float32),
                        pltpu.SemaphoreType.DMA((2,))],
    )
    return pl.pallas_call(
        functools.partial(_combine_kernel, nblk=n_tok // tb),
        grid_spec=grid_spec,
        out_shape=jax.ShapeDtypeStruct((n_tok, d_model), jnp.float32),
        compiler_params=pltpu.CompilerParams(
            dimension_semantics=("arbitrary",),
            vmem_limit_bytes=VMEM_LIMIT_BYTES),
        name="combine",
    )(lidx, gates, nbe, lb, hrow, extra, x1, norm_w, y_rows)


def _retention_tables(seq, hd, ts):
    half = hd // 2
    inv = ROPE_BASE ** (-jnp.arange(half, dtype=jnp.float32) / half)
    inv2 = jnp.concatenate([inv, inv])

    def cos_sin(positions):
        ang = positions[:, None] * inv2[None, :]
        return jnp.stack([jnp.cos(ang), jnp.sin(ang)], axis=1)
    rope = cos_sin(jnp.arange(ts, dtype=jnp.float32)).transpose(1, 0, 2)
    rope_base = cos_sin(jnp.arange(seq // ts, dtype=jnp.float32) * ts)
    log_g = jnp.log1p(-jnp.exp2(-5.0 - jnp.arange(RET_HEADS, dtype=jnp.float32)))
    pos = jnp.arange(ts, dtype=jnp.float32)
    dist = jnp.abs(pos[:, None] - pos[None, :])
    chunk = jnp.arange(ts, dtype=jnp.int32) // CHUNK
    visible = chunk[None, :] <= chunk[:, None]
    dmat = jnp.where(visible[None], jnp.exp(log_g[:, None, None] * dist[None]), 0.0)
    qdec = jnp.exp(log_g[:, None] * (pos + 1.0)[None, :])
    kdec = jnp.exp(log_g[:, None] * (ts - 1.0 - pos)[None, :])
    qdec = jnp.broadcast_to(qdec[:, :, None], (RET_HEADS, ts, hd))
    kdec = jnp.broadcast_to(kdec[:, :, None], (RET_HEADS, ts, hd))
    tile_decay = tuple(math.exp(math.log1p(-2.0 ** (-5.0 - h)) * ts) for h in range(RET_HEADS))
    return rope, rope_base, dmat, qdec, kdec, tile_decay


def _routing_tables(nbe, n_tok, n_exp):
    tm, tb = ROW_GRANULE, TOK_BLOCK
    nal = nbe + (nbe & (STRIP_ALIGN - 1))
    region = jnp.sum(nal, axis=0)
    tiles_per = (region + tm - 1) // tm
    tile_end = jnp.cumsum(tiles_per)
    offs = (tile_end - tiles_per) * tm
    n_tiles = (n_tok * TOP_K + (n_tok // tb) * n_exp * (STRIP_ALIGN - 1)) // tm + n_exp
    hrow = offs[None, :] + jnp.cumsum(nal, axis=0) - nal
    padrow = offs + region
    padlen = tiles_per * tm - region
    as_i32 = lambda a: a.reshape(-1).astype(jnp.int32)
    return (n_tiles, as_i32(tile_end - tiles_per), as_i32(tiles_per), as_i32(tile_end[-1:]),
            as_i32(hrow), as_i32(jnp.sum(nal, axis=1) - tb * TOP_K), as_i32(padrow), as_i32(padlen))


def _layer(x, norm_mix_w, w_in, ret_norm_w, w_pool, pool_scale, w_out, norm_ffn_w,
           w_router, b_router, w_gu, b_gu, w_down, b_down, final_w):
    bsz, seq, d_model = x.shape
    n_tok = bsz * seq
    n_exp = w_router.shape[1]
    rw = ret_norm_w.shape[0]
    hd = rw // RET_HEADS
    ts = SEQ_TILE
    bf = jnp.bfloat16
    rpt = d_model // PACK // LANES
    assert d_model % (PACK * LANES) == 0 and (STRIP_ALIGN * rpt) % SUBLANES == 0
    assert seq % ts == 0 and ts % CHUNK == 0 and n_tok % TOK_BLOCK == 0
    assert n_tok % ROUTE_TILE == 0 and ROUTE_TILE % TOK_BLOCK == 0 and TOK_BLOCK % ROUTE_CHUNK == 0
    assert TOK_BLOCK & (TOK_BLOCK - 1) == 0 and ROW_GRANULE & (ROW_GRANULE - 1) == 0
    assert ROW_TILE % ROW_GRANULE == 0

    rope, rope_base, dmat, qdec, kdec, tile_decay = _retention_tables(seq, hd, ts)
    tri = jnp.arange(ROUTE_CHUNK, dtype=jnp.int32)
    utri = (tri[:, None] < tri[None, :]).astype(bf)

    x1, h2p = _mix(
        x, norm_mix_w[None], w_in.astype(bf), rope, rope_base, dmat, qdec, kdec, ret_norm_w[None],
        w_pool.astype(bf), pool_scale[None], w_out.astype(bf), norm_ffn_w[None], tile_decay)
    eri = jnp.arange(n_exp, dtype=jnp.int32)
    ltri = (eri[None, :] < eri[:, None]).astype(bf)
    lidx, gate, nbe, lb, _ = _route(h2p, w_router.T.astype(bf), b_router[:, None], utri, ltri)
    lidx = lidx.reshape(-1)
    nbe = nbe[:, :, 0].astype(jnp.int32)
    lb = lb[:, :, 0].astype(jnp.int32).reshape(-1)

    (n_tiles, tile_start, tile_count, n_used, hrow, extra, padrow, padlen) = _routing_tables(nbe, n_tok, n_exp)
    nbe = nbe.reshape(-1)

    xs = _dispatch(lidx, nbe, lb, hrow, extra, padrow, padlen, n_used, h2p, n_tiles * ROW_GRANULE, d_model)
    y_rows = _experts(tile_start, tile_count, xs, w_gu, b_gu, w_down, b_down)
    out = _combine(lidx, gate.reshape(-1), nbe, lb, hrow, extra, x1.reshape(n_tok, d_model), final_w[None], y_rows)
    return out.reshape(bsz, seq, d_model)


def kernel(x, norm_mix_w, w_in, ret_norm_w, w_pool, pool_scale, w_out, norm_ffn_w, w_router, b_router,
           w_gu, b_gu, w_down, b_down, norm_final_w):
    depth = norm_mix_w.shape[0]
    for l in range(depth):
        last = l == depth - 1
        assert last, "stacked layers need an un-normalised combine output"
        x = _layer(x, norm_mix_w[l], w_in[l], ret_norm_w[l], w_pool[l], pool_scale[l], w_out[l],
                   norm_ffn_w[l], w_router[l], b_router[l], w_gu[l], b_gu[l], w_down[l], b_down[l],
                   norm_final_w)
    return x
```

```python
import functools
import math

import jax
import jax.numpy as jnp
from jax import lax
from jax.experimental import pallas as pl
from jax.experimental.pallas import tpu as pltpu

CHUNK = 64
RET_HEADS = 4
POOL_WINDOWS = (2, 4, 8, 16)
ROPE_BASE = 10000.0
TOP_K = 4
SWIGLU_LIMIT = 7.0
SWIGLU_ALPHA = 1.702
EPS = 1e-6

LANES = 128
SUBLANES = 8
VMEM_LIMIT_BYTES = 56 * 1024 * 1024

SEQ_TILE = 256
MIX_STREAMS = 4
ROUTE_TILE = 2048
ROUTE_CHUNK = 256
ROW_GRANULE = 128
ROW_TILE = 512
FF_CHUNKS = 2
TOK_BLOCK = 1024
POOL_HALO = 16
PACK = 2
STRIP_ALIGN = 2
GATHER_UNROLL = 32
STRIP_UNROLL = 4
RARE_STRIP_BIT = 8


def _rms(x, w):
    return x * lax.rsqrt(jnp.mean(x * x, axis=-1, keepdims=True) + EPS) * w


def _bf16_bits_hi(x):
    return lax.bitcast_convert_type(x.astype(jnp.bfloat16).astype(jnp.float32), jnp.uint32)


def _pack_rows(x, out_ref, n_rows):
    half = x.shape[1] // PACK
    packed = (_bf16_bits_hi(x[:, :half]) >> 16) | _bf16_bits_hi(x[:, half:])
    rpt = half // LANES
    for si in range(rpt):
        out_ref[pl.ds(si, n_rows, stride=rpt), :] = packed[:, si * LANES:(si + 1) * LANES]


def _unpack_words(u):
    lo = lax.bitcast_convert_type(u << 16, jnp.float32)
    hi = lax.bitcast_convert_type(u & jnp.uint32(0xFFFF0000), jnp.float32)
    return lo, hi


def _unpack_rows(g_ref, n_rows, rpt):
    lo, hi = [], []
    for si in range(rpt):
        l, h = _unpack_words(g_ref[pl.ds(si, n_rows, stride=rpt), :])
        lo.append(l.astype(jnp.bfloat16))
        hi.append(h.astype(jnp.bfloat16))
    return jnp.concatenate(lo + hi, axis=-1)


def _mix_kernel(x_ref, nmw_ref, win_ref, rope_ref, rope_base_ref, dmat_ref, qdec_ref, kdec_ref,
                rnw_ref, wpool_ref, pscale_ref, wout_ref, nfw_ref,
                x1_ref, h2p_ref,
                state_ref, p1_ref, p2_ref, p4_ref, p8_ref,
                *, tile_decay):
    j = pl.program_id(1)
    n_str = x_ref.shape[0]
    ts = x_ref.shape[2]
    n_heads = RET_HEADS
    rw = rnw_ref.shape[1]
    hd = rw // n_heads
    pgw = wpool_ref.shape[1]
    halo = POOL_HALO
    pool_refs = (p1_ref, p2_ref, p4_ref, p8_ref)

    @pl.when(j == 0)
    def _():
        state_ref[...] = jnp.zeros_like(state_ref)
        for r in pool_refs:
            r[:, 0:halo, :] = jnp.zeros((n_str, halo, r.shape[2]), jnp.float32)

    @pl.when(j > 0)
    def _():
        for r in pool_refs:
            r[:, 0:halo, :] = r[:, ts:ts + halo, :]

    x = jnp.concatenate([x_ref[si, 0] for si in range(n_str)], axis=0)
    h = _rms(x, nmw_ref[...]).astype(jnp.bfloat16)
    n_grp = n_str
    grp = n_str * ts // n_grp
    z_all = jnp.concatenate(
        [jnp.dot(h[gi * grp:(gi + 1) * grp], win_ref[...], preferred_element_type=jnp.float32)
         for gi in range(n_grp)], axis=0)

    rot = rope_ref[...]
    base = rope_base_ref[0]
    cos = base[0:1] * rot[0] - base[1:2] * rot[1]
    sin_plain = base[1:2] * rot[0] + base[0:1] * rot[1]
    half_sign = jnp.where(lax.broadcasted_iota(jnp.int32, (1, hd), 1) < hd // 2, -1.0, 1.0)
    sin = sin_plain * half_sign
    k_scale = hd ** -0.5
    tpos = (j * ts + lax.broadcasted_iota(jnp.int32, (ts, pgw), 0) + 1).astype(jnp.float32)

    mix_rows = []
    for si in range(n_str):
        z = z_all[si * ts:(si + 1) * ts]
        parts = []
        for hh in range(n_heads):
            q = z[:, hh * hd:(hh + 1) * hd]
            k = z[:, rw + hh * hd:rw + (hh + 1) * hd]
            v = z[:, 2 * rw + hh * hd:2 * rw + (hh + 1) * hd]
            g = z[:, 3 * rw + hh * hd:3 * rw + (hh + 1) * hd]
            q = q * cos + pltpu.roll(q, hd // 2, 1) * sin
            k = (k * cos + pltpu.roll(k, hd // 2, 1) * sin) * k_scale
            qb = q.astype(jnp.bfloat16)
            kb = k.astype(jnp.bfloat16)
            vb = v.astype(jnp.bfloat16)
            s = lax.dot_general(qb, kb, (((1,), (1,)), ((), ())), preferred_element_type=jnp.float32)
            pmat = (s * dmat_ref[hh]).astype(jnp.bfloat16)
            o = jnp.dot(pmat, vb, preferred_element_type=jnp.float32)
            state = state_ref[si, hh]
            qd = (q * qdec_ref[hh]).astype(jnp.bfloat16)
            o = o + jnp.dot(qd, state.astype(jnp.bfloat16), preferred_element_type=jnp.float32)
            kd = (k * kdec_ref[hh]).astype(jnp.bfloat16)
            u = lax.dot_general(kd, vb, (((0,), (0,)), ((), ())), preferred_element_type=jnp.float32)
            state_ref[si, hh] = state * tile_decay[hh] + u
            o = o * lax.rsqrt(jnp.mean(o * o, axis=-1, keepdims=True) + EPS)
            o = o * rnw_ref[:, hh * hd:(hh + 1) * hd] * (g * (1.0 / (1.0 + jnp.exp(-g))))
            parts.append(o.astype(jnp.bfloat16))

        p = z[:, 4 * rw:]
        p1_ref[si, halo:halo + ts, :] = p
        s2 = p + p1_ref[si, halo - 1:halo - 1 + ts, :]
        p2_ref[si, halo:halo + ts, :] = s2[:, pgw:]
        s4 = s2[:, pgw:] + p2_ref[si, halo - 2:halo - 2 + ts, :]
        p4_ref[si, halo:halo + ts, :] = s4[:, pgw:]
        s8 = s4[:, pgw:] + p4_ref[si, halo - 4:halo - 4 + ts, :]
        p8_ref[si, halo:halo + ts, :] = s8[:, pgw:]
        s16 = s8[:, pgw:] + p8_ref[si, halo - 8:halo - 8 + ts, :]
        sums = (s2[:, :pgw], s4[:, :pgw], s8[:, :pgw], s16)
        for gi, w in enumerate(POOL_WINDOWS):
            cnt = jnp.minimum(tpos, float(w))
            pooled = sums[gi] / cnt - p[:, gi * pgw:(gi + 1) * pgw]
            mixed = jnp.dot(pooled.astype(jnp.bfloat16), wpool_ref[gi], preferred_element_type=jnp.float32)
            mixed = mixed * pscale_ref[:, gi * pgw:(gi + 1) * pgw]
            parts.append(mixed.astype(jnp.bfloat16))
        mix_rows.append(jnp.concatenate(parts, axis=-1))

    mix = jnp.concatenate(mix_rows, axis=0)
    x1 = x + jnp.dot(mix, wout_ref[...], preferred_element_type=jnp.float32)
    h2 = _rms(x1, nfw_ref[...])
    for si in range(n_str):
        x1_ref[si, 0] = x1[si * ts:(si + 1) * ts]
        _pack_rows(h2[si * ts:(si + 1) * ts], h2p_ref.at[si, 0], ts)


def _mix(x, nmw, win_b, rope, rope_base, dmat, qdec, kdec, rnw, wpool_b, pscale, wout_b, nfw, tile_decay):
    bsz, seq, d_model = x.shape
    ts = SEQ_TILE
    n_tok = bsz * seq
    rw = rnw.shape[1]
    hd = rw // RET_HEADS
    pw = pscale.shape[1]
    pgw = pw // len(POOL_WINDOWS)
    nj = seq // ts
    rows_per_tok = d_model // PACK // LANES

    def const(shape):
        return pl.BlockSpec(shape, lambda b, j: (0,) * len(shape))

    n_str = MIX_STREAMS if bsz % MIX_STREAMS == 0 else 1
    per = bsz // n_str
    in_specs = [
        pl.BlockSpec((n_str, 1, ts, d_model), lambda q, j: (0, q, j, 0)),
        const(nmw.shape), const(win_b.shape),
        const(rope.shape),
        pl.BlockSpec((1, 2, hd), lambda q, j: (j, 0, 0)),
        const(dmat.shape), const(qdec.shape), const(kdec.shape),
        const(rnw.shape), const(wpool_b.shape), const(pscale.shape), const(wout_b.shape),
        const(nfw.shape),
    ]
    out_shape = [
        jax.ShapeDtypeStruct((n_str, per, seq, d_model), jnp.float32),
        jax.ShapeDtypeStruct((n_str, per, seq * rows_per_tok, LANES), jnp.uint32),
    ]
    out_specs = [
        pl.BlockSpec((n_str, 1, ts, d_model), lambda q, j: (0, q, j, 0)),
        pl.BlockSpec((n_str, 1, ts * rows_per_tok, LANES), lambda q, j: (0, q, j, 0)),
    ]
    scratch = [
        pltpu.VMEM((n_str, RET_HEADS, hd, hd), jnp.float32),
        pltpu.VMEM((n_str, POOL_HALO + ts, pw), jnp.float32),
        pltpu.VMEM((n_str, POOL_HALO + ts, pw - pgw), jnp.float32),
        pltpu.VMEM((n_str, POOL_HALO + ts, pw - 2 * pgw), jnp.float32),
        pltpu.VMEM((n_str, POOL_HALO + ts, pw - 3 * pgw), jnp.float32),
    ]
    x1, h2p = pl.pallas_call(
        functools.partial(_mix_kernel, tile_decay=tile_decay),
        grid=(per, nj),
        in_specs=in_specs,
        out_specs=out_specs,
        out_shape=out_shape,
        scratch_shapes=scratch,
        compiler_params=pltpu.CompilerParams(
            dimension_semantics=("arbitrary", "arbitrary"),
            vmem_limit_bytes=VMEM_LIMIT_BYTES),
        name="mix",
    )(x.reshape(n_str, per, seq, d_model), nmw, win_b, rope, rope_base, dmat, qdec, kdec, rnw, wpool_b,
      pscale, wout_b, nfw)
    return x1.reshape(bsz, seq, d_model), h2p.reshape(n_tok * rows_per_tok, LANES)


def _route_kernel(h2p_ref, wrt_ref, brt_ref, utri_ref, ltri_ref,
                  lidx_ref, gate_ref, nbe_ref, lb_ref, cnt_ref, carry_ref):
    i = pl.program_id(0)
    n_exp, d_model = wrt_ref.shape
    rpt = d_model // PACK // LANES
    tr = h2p_ref.shape[0] // rpt
    sub = utri_ref.shape[0]
    chunks_per_block = TOK_BLOCK // sub

    @pl.when(i == 0)
    def _():
        carry_ref[...] = jnp.zeros_like(carry_ref)

    h2b = _unpack_rows(h2p_ref, tr, rpt)
    logits = lax.dot_general(wrt_ref[...], h2b, (((1,), (1,)), ((), ())),
                             preferred_element_type=jnp.float32) + brt_ref[...]
    eiota = lax.broadcasted_iota(jnp.int32, (n_exp, tr), 0)
    work = logits
    vals, sels = [], []
    for _ in range(TOP_K):
        m = jnp.max(work, axis=0, keepdims=True)
        idx = jnp.min(jnp.where(work == m, eiota, n_exp), axis=0, keepdims=True)
        sel = eiota == idx
        vals.append(m)
        sels.append(sel)
        work = jnp.where(sel, -jnp.inf, work)
    ex = [jnp.exp(vk - vals[0]) for vk in vals]
    den = ex[0] + ex[1] + ex[2] + ex[3]
    gate_ref[...] = jnp.concatenate([e / den for e in ex], axis=0)

    member = jnp.zeros((n_exp, tr), jnp.float32)
    for sel in sels:
        member = member + sel.astype(jnp.float32)
    carry = carry_ref[...]
    places = []
    for blk in range(tr // TOK_BLOCK):
        block_start = carry
        befores = []
        for c in range(blk * chunks_per_block, (blk + 1) * chunks_per_block):
            mc = member[:, c * sub:(c + 1) * sub]
            bc = jnp.dot(mc.astype(jnp.bfloat16), utri_ref[...], preferred_element_type=jnp.float32)
            befores.append(bc + carry[:, 0:1])
            carry = carry + jnp.sum(mc, axis=1, keepdims=True)
        nbe = carry - block_start
        nal = nbe + (nbe - STRIP_ALIGN * jnp.floor(nbe * (1.0 / STRIP_ALIGN)))
        hi = jnp.floor(nal * (1.0 / 32.0))
        lo = nal - 32.0 * hi
        lb = (32.0 * jnp.dot(ltri_ref[...], hi.astype(jnp.bfloat16), preferred_element_type=jnp.float32)
              + jnp.dot(ltri_ref[...], lo.astype(jnp.bfloat16), preferred_element_type=jnp.float32))
        nbe_ref[blk] = nbe
        lb_ref[blk] = lb
        place = jnp.concatenate(befores, axis=1) + (lb[:, 0:1] - block_start[:, 0:1])
        cols = slice(blk * TOK_BLOCK, (blk + 1) * TOK_BLOCK)
        places.append(jnp.concatenate(
            [jnp.sum(jnp.where(sel[:, cols], place, 0.0), axis=0, keepdims=True) for sel in sels], axis=0))
    lidx_ref[...] = (jnp.concatenate(places, axis=1) * float(rpt)).astype(jnp.int32)
    carry_ref[...] = carry
    cnt_ref[...] = carry


def _route(h2p, wrt_b, brt, utri, ltri):
    n_exp, d_model = wrt_b.shape
    rpt = d_model // PACK // LANES
    n_tok = h2p.shape[0] // rpt
    tr = ROUTE_TILE
    bps = tr // TOK_BLOCK

    def const(shape):
        return pl.BlockSpec(shape, lambda i: (0,) * len(shape))

    out_shape = [
        jax.ShapeDtypeStruct((TOP_K, n_tok), jnp.int32),
        jax.ShapeDtypeStruct((TOP_K, n_tok), jnp.float32),
        jax.ShapeDtypeStruct((n_tok // TOK_BLOCK, n_exp, LANES), jnp.float32),
        jax.ShapeDtypeStruct((n_tok // TOK_BLOCK, n_exp, LANES), jnp.float32),
        jax.ShapeDtypeStruct((n_exp, LANES), jnp.float32),
    ]
    out_specs = [
        pl.BlockSpec((TOP_K, tr), lambda i: (0, i)),
        pl.BlockSpec((TOP_K, tr), lambda i: (0, i)),
        pl.BlockSpec((bps, n_exp, LANES), lambda i: (i, 0, 0)),
        pl.BlockSpec((bps, n_exp, LANES), lambda i: (i, 0, 0)),
        pl.BlockSpec((n_exp, LANES), lambda i: (0, 0)),
    ]
    return pl.pallas_call(
        _route_kernel,
        grid=(n_tok // tr,),
        in_specs=[pl.BlockSpec((tr * rpt, LANES), lambda i: (i, 0)),
                  const(wrt_b.shape), const(brt.shape), const(utri.shape), const(ltri.shape)],
        out_specs=out_specs,
        out_shape=out_shape,
        scratch_shapes=[pltpu.VMEM((n_exp, LANES), jnp.float32)],
        compiler_params=pltpu.CompilerParams(
            dimension_semantics=("arbitrary",),
            vmem_limit_bytes=VMEM_LIMIT_BYTES),
        name="route",
    )(h2p, wrt_b, brt, utri, ltri)


def _for_set_bits(n, max_bit, fn):
    def piece(bit):
        size = 1 << bit
        offset = (n >> (bit + 1)) << (bit + 1)

        @pl.when((n & size) != 0)
        def _():
            fn(offset, size)

    min_bit = STRIP_ALIGN.bit_length() - 1
    rare_bit = min(max_bit + 1, RARE_STRIP_BIT)

    @pl.when(n >= (1 << rare_bit))
    def _():
        for bit in range(max_bit, rare_bit - 1, -1):
            piece(bit)
    for bit in range(rare_bit - 1, min_bit - 1, -1):
        piece(bit)


def _aligned_len(n):
    return n + (n & (STRIP_ALIGN - 1))


def _for_strips(nbe_ref, lb_ref, hrow_ref, blk, n_exp, max_bit, fn):
    def body(i, carry):
        for u in range(STRIP_UNROLL):
            k = blk * n_exp + i * STRIP_UNROLL + u
            n = _aligned_len(nbe_ref[k])
            lb = lb_ref[k]
            hrow = hrow_ref[k]
            _for_set_bits(n, max_bit, lambda offset, size: fn(lb + offset, hrow + offset, size))
        return carry
    assert n_exp % STRIP_UNROLL == 0
    lax.fori_loop(0, n_exp // STRIP_UNROLL, body, 0)


def _dispatch_kernel(lidx_ref, nbe_ref, lb_ref, hrow_ref, extra_ref, padrow_ref, padlen_ref, nused_ref,
                     h2p_ref, xs_hbm, stage_ref, zero_ref, sem, zsem, *, nblk):
    b = pl.program_id(0)
    rpt = zero_ref.shape[0] // ROW_GRANULE
    tb = h2p_ref.shape[0] // rpt
    n_exp = padrow_ref.shape[0]
    n_tok = lidx_ref.shape[0] // TOP_K
    slot_rows = stage_ref.shape[0] // 2
    max_bit = tb.bit_length() - 1
    slot = b % 2

    def strip_copy(sl, staging_row, sorted_row, size):
        src = pl.multiple_of(sl * slot_rows + staging_row * rpt, SUBLANES)
        dst = pl.multiple_of(sorted_row * rpt, SUBLANES)
        return pltpu.make_async_copy(stage_ref.at[pl.ds(src, size * rpt), :],
                                     xs_hbm.at[pl.ds(dst, size * rpt), :], sem.at[sl])

    def start_strips(blk, sl):
        _for_strips(nbe_ref, lb_ref, hrow_ref, blk, n_exp, max_bit,
                    lambda srow, hrow, size: strip_copy(sl, srow, hrow, size).start())

    def wait_strips(blk, sl):
        def sized_copy(offset, size):
            return pltpu.make_async_copy(stage_ref.at[pl.ds(sl * slot_rows, size * rpt), :],
                                         xs_hbm.at[pl.ds(0, size * rpt), :], sem.at[sl])
        sized_copy(0, tb * TOP_K).wait()
        _for_set_bits(extra_ref[blk], n_exp.bit_length() - 1, lambda o, s: sized_copy(o, s).wait())

    @pl.when(b >= 2)
    def _():
        wait_strips(b - 2, slot)

    def zero_after(i, carry):
        for u in range(STRIP_UNROLL):
            k = b * n_exp + i * STRIP_UNROLL + u
            dst = pl.multiple_of(slot * slot_rows + (lb_ref[k] + nbe_ref[k]) * rpt, rpt)
            stage_ref[pl.ds(dst, rpt), :] = jnp.zeros((rpt, LANES), jnp.uint32)
        return carry
    lax.fori_loop(0, n_exp // STRIP_UNROLL, zero_after, 0)

    def scatter_block(static_slot):
        def scatter_group(c, carry):
            tok0 = c * GATHER_UNROLL
            idx0 = b * tb + tok0
            for u in range(GATHER_UNROLL):
                row = h2p_ref[pl.ds(pl.multiple_of((tok0 + u) * rpt, rpt), rpt), :]
                for k in range(TOP_K):
                    dst = pl.multiple_of(lidx_ref[idx0 + (k * n_tok + u)], rpt)
                    stage_ref[pl.ds(static_slot * slot_rows + dst, rpt), :] = row
            return carry
        lax.fori_loop(0, tb // GATHER_UNROLL, scatter_group, 0)

    for static_slot in range(2):
        pl.when(slot == static_slot)(functools.partial(scatter_block, static_slot))

    start_strips(b, slot)

    @pl.when(b == nblk - 1)
    def _():
        if nblk >= 2:
            wait_strips(b - 1, 1 - slot)
        wait_strips(b, slot)
        zero_ref[...] = jnp.zeros_like(zero_ref)
        pad_bit = ROW_GRANULE.bit_length() - 2

        def pad_copy(e, offset, size):
            dst = pl.multiple_of((padrow_ref[e] + offset) * rpt, SUBLANES)
            return pltpu.make_async_copy(zero_ref.at[pl.ds(0, size * rpt), :],
                                         xs_hbm.at[pl.ds(dst, size * rpt), :], zsem)

        def pad_start(e, carry):
            _for_set_bits(padlen_ref[e], pad_bit, lambda o, s: pad_copy(e, o, s).start())
            return carry

        def pad_wait(e, carry):
            _for_set_bits(padlen_ref[e], pad_bit, lambda o, s: pad_copy(e, o, s).wait())
            return carry
        lax.fori_loop(0, n_exp, pad_start, 0)

        def tail_copy(i):
            dst = pl.multiple_of(i * (ROW_GRANULE * rpt), SUBLANES)
            return pltpu.make_async_copy(zero_ref, xs_hbm.at[pl.ds(dst, ROW_GRANULE * rpt), :], zsem)

        def tail_start(i, carry):
            tail_copy(i).start()
            return carry

        def tail_wait(i, carry):
            tail_copy(i).wait()
            return carry
        n_gran = xs_hbm.shape[0] // (ROW_GRANULE * rpt)
        lax.fori_loop(nused_ref[0], n_gran, tail_start, 0)
        lax.fori_loop(0, n_exp, pad_wait, 0)
        lax.fori_loop(nused_ref[0], n_gran, tail_wait, 0)


def _dispatch(lidx, nbe, lb, hrow, extra, padrow, padlen, n_used, h2p, n_rows, d_model):
    tb = TOK_BLOCK
    rpt = d_model // PACK // LANES
    n_tok = h2p.shape[0] // rpt
    n_exp = padrow.shape[0]
    slot_rows = (tb * TOP_K + n_exp * (STRIP_ALIGN - 1)) * rpt
    grid_spec = pltpu.PrefetchScalarGridSpec(
        num_scalar_prefetch=8,
        grid=(n_tok // tb,),
        in_specs=[pl.BlockSpec((tb * rpt, LANES), lambda i, *_: (i, 0))],
        out_specs=pl.BlockSpec(memory_space=pl.ANY),
        scratch_shapes=[pltpu.VMEM((2 * slot_rows, LANES), jnp.uint32),
                        pltpu.VMEM((ROW_GRANULE * rpt, LANES), jnp.uint32),
                        pltpu.SemaphoreType.DMA((2,)),
                        pltpu.SemaphoreType.DMA],
    )
    return pl.pallas_call(
        functools.partial(_dispatch_kernel, nblk=n_tok // tb),
        grid_spec=grid_spec,
        out_shape=jax.ShapeDtypeStruct((n_rows * rpt, LANES), jnp.uint32),
        compiler_params=pltpu.CompilerParams(
            dimension_semantics=("arbitrary",),
            vmem_limit_bytes=VMEM_LIMIT_BYTES),
        name="dispatch",
    )(lidx, nbe, lb, hrow, extra, padrow, padlen, n_used, h2p)


def _experts_kernel(tstart_ref, tcount_ref, wgu_hbm, bgu_ref, wd_hbm, bd_ref, xs_hbm,
                    y_hbm, wgu_f, wd_f, wgu_b, wd_b, xbuf, ybuf, pend_ref, wsem, xsem, ysem, *, n_exp):
    e = pl.program_id(0)
    d_ff = wd_f.shape[0]
    d_model = wd_f.shape[1]
    rpt = d_model // PACK // LANES
    gran_rows = ROW_GRANULE * rpt
    gpt = ROW_TILE // ROW_GRANULE
    n_gran = y_hbm.shape[0] // gran_rows
    g0 = tstart_ref[e]
    ng = tcount_ref[e]
    nt = ng // gpt
    tail = ng - nt * gpt

    def hbm_rows(ref, gran, count):
        return ref.at[pl.ds(pl.multiple_of(gran * gran_rows, SUBLANES), count * gran_rows), :]

    def x_copy(gran, sl, count=gpt):
        return pltpu.make_async_copy(hbm_rows(xs_hbm, gran, count),
                                     xbuf.at[sl, pl.ds(0, count * gran_rows), :], xsem.at[sl])

    def y_copy(gran, sl, count=gpt):
        return pltpu.make_async_copy(ybuf.at[sl, pl.ds(0, count * gran_rows), :],
                                     hbm_rows(y_hbm, gran, count), ysem.at[sl])

    def mlp(sl, tm):
        x = _unpack_rows(xbuf.at[sl], tm, rpt)
        fc = d_ff // FF_CHUNKS
        y = None
        for c in range(FF_CHUNKS):
            glu = jnp.dot(x, wgu_b[:, c * fc:(c + 1) * fc], preferred_element_type=jnp.float32)
            lin = jnp.dot(x, wgu_b[:, d_ff + c * fc:d_ff + (c + 1) * fc], preferred_element_type=jnp.float32)
            x_glu = jnp.minimum(glu + bgu_ref[0, :, c * fc:(c + 1) * fc], SWIGLU_LIMIT)
            x_lin = jnp.clip(lin + bgu_ref[0, :, d_ff + c * fc:d_ff + (c + 1) * fc], -SWIGLU_LIMIT, SWIGLU_LIMIT)
            act = x_glu * (1.0 / (1.0 + jnp.exp(-SWIGLU_ALPHA * x_glu))) * (x_lin + 1.0)
            part = jnp.dot(act.astype(jnp.bfloat16), wd_b[c * fc:(c + 1) * fc, :],
                           preferred_element_type=jnp.float32)
            y = part if y is None else y + part
        _pack_rows(y + bd_ref[0], ybuf.at[sl], tm)

    def weight_copies(ex):
        return (pltpu.make_async_copy(wgu_hbm.at[ex], wgu_f, wsem.at[0]),
                pltpu.make_async_copy(wd_hbm.at[ex], wd_f, wsem.at[1]))

    tail_slot = 2
    tail_gran = g0 + nt * gpt

    def wait_pending_tail_y():
        for count in range(1, gpt):
            @pl.when(pend_ref[0] == count)
            def _():
                y_copy(0, tail_slot, count).wait()

    @pl.when(e == 0)
    def _():
        pend_ref[0] = 0

        @pl.when(nt > 0)
        def _():
            x_copy(g0, 0).start()

    for count in range(1, gpt):
        @pl.when(tail == count)
        def _():
            x_copy(tail_gran, tail_slot, count).start()

    @pl.when(e == 0)
    def _():
        for c in weight_copies(0):
            c.start(priority=1)
    for c in weight_copies(e):
        c.wait()
    wgu_b[...] = wgu_f[...].astype(jnp.bfloat16)
    wd_b[...] = wd_f[...].astype(jnp.bfloat16)

    @pl.when(e + 1 < n_exp)
    def _():
        for c in weight_copies(e + 1):
            c.start(priority=1)

    @pl.when(ng > 0)
    def _():
        def tile_step(j, carry):
            sl = j % 2
            gran = g0 + j * gpt

            @pl.when(j + 1 < nt)
            def _():
                x_copy(gran + gpt, 1 - sl).start()
            x_copy(gran, sl).wait()

            @pl.when(j >= 2)
            def _():
                y_copy(gran - 2 * gpt, sl).wait()
            mlp(sl, ROW_TILE)
            y_copy(gran, sl).start()
            return carry
        lax.fori_loop(0, nt, tile_step, 0)

        @pl.when(nt >= 2)
        def _():
            y_copy(g0 + (nt - 2) * gpt, nt % 2).wait()

        @pl.when(nt >= 1)
        def _():
            y_copy(g0 + (nt - 1) * gpt, (nt - 1) % 2).wait()

    @pl.when(e + 1 < n_exp)
    def _():
        nxt = jnp.minimum(e + 1, n_exp - 1)

        @pl.when(tcount_ref[nxt] >= gpt)
        def _():
            x_copy(tstart_ref[nxt], 0).start()

    for count in range(1, gpt):
        @pl.when(tail == count)
        def _():
            x_copy(tail_gran, tail_slot, count).wait()
            wait_pending_tail_y()
            mlp(tail_slot, count * ROW_GRANULE)
            y_copy(tail_gran, tail_slot, count).start()
            pend_ref[0] = count

    @pl.when(e == n_exp - 1)
    def _():
        wait_pending_tail_y()
        ybuf[0, 0:gran_rows, :] = jnp.zeros((gran_rows, LANES), ybuf.dtype)

        def tail_start(i, carry):
            y_copy(i, 0, 1).start()
            return carry

        def tail_wait(i, carry):
            y_copy(i, 0, 1).wait()
            return carry
        lax.fori_loop(g0 + ng, n_gran, tail_start, 0)
        lax.fori_loop(g0 + ng, n_gran, tail_wait, 0)


def _experts(tile_start, tile_count, xs, w_gu, b_gu, w_down, b_down):
    tm = ROW_TILE
    n_exp, d_model, d_gu = w_gu.shape
    d_ff = w_down.shape[1]
    rpt = d_model // PACK // LANES
    grid_spec = pltpu.PrefetchScalarGridSpec(
        num_scalar_prefetch=2,
        grid=(n_exp,),
        in_specs=[
            pl.BlockSpec(memory_space=pl.ANY),
            pl.BlockSpec((1, 1, d_gu), lambda e, *_: (e, 0, 0)),
            pl.BlockSpec(memory_space=pl.ANY),
            pl.BlockSpec((1, 1, d_model), lambda e, *_: (e, 0, 0)),
            pl.BlockSpec(memory_space=pl.ANY),
        ],
        out_specs=pl.BlockSpec(memory_space=pl.ANY),
        scratch_shapes=[pltpu.VMEM((d_model, d_gu), jnp.float32),
                        pltpu.VMEM((d_ff, d_model), jnp.float32),
                        pltpu.VMEM((d_model, d_gu), jnp.bfloat16),
                        pltpu.VMEM((d_ff, d_model), jnp.bfloat16),
                        pltpu.VMEM((3, tm * rpt, LANES), jnp.uint32),
                        pltpu.VMEM((3, tm * rpt, LANES), jnp.uint32),
                        pltpu.SMEM((1,), jnp.int32),
                        pltpu.SemaphoreType.DMA((2,)),
                        pltpu.SemaphoreType.DMA((3,)),
                        pltpu.SemaphoreType.DMA((3,))],
    )
    return pl.pallas_call(
        functools.partial(_experts_kernel, n_exp=n_exp),
        grid_spec=grid_spec,
        out_shape=jax.ShapeDtypeStruct(xs.shape, jnp.uint32),
        compiler_params=pltpu.CompilerParams(
            dimension_semantics=("arbitrary",),
            vmem_limit_bytes=VMEM_LIMIT_BYTES),
        name="experts",
    )(tile_start, tile_count, w_gu, b_gu.reshape(n_exp, 1, d_gu), w_down,
      b_down.reshape(n_exp, 1, d_model), xs)


def _combine_kernel(lidx_ref, gate_ref, nbe_ref, lb_ref, hrow_ref, extra_ref,
                    x1_ref, nw_ref, y_hbm, out_ref, stage_ref, alo_ref, ahi_ref, sem, *, nblk):
    b = pl.program_id(0)
    tb = x1_ref.shape[0]
    rpt = alo_ref.shape[0] // tb
    n_exp = nbe_ref.shape[0] // nblk
    n_tok = lidx_ref.shape[0] // TOP_K
    slot_rows = stage_ref.shape[0] // 2
    max_bit = tb.bit_length() - 1
    slot = b % 2

    def strip_copy(sl, staging_row, sorted_row, size):
        src = pl.multiple_of(sorted_row * rpt, SUBLANES)
        dst = pl.multiple_of(sl * slot_rows + staging_row * rpt, SUBLANES)
        return pltpu.make_async_copy(y_hbm.at[pl.ds(src, size * rpt), :],
                                     stage_ref.at[pl.ds(dst, size * rpt), :], sem.at[sl])

    def start_strips(blk, sl):
        _for_strips(nbe_ref, lb_ref, hrow_ref, blk, n_exp, max_bit,
                    lambda srow, hrow, size: strip_copy(sl, srow, hrow, size).start())

    @pl.when(b == 0)
    def _():
        start_strips(0, 0)

    @pl.when(b + 1 < nblk)
    def _():
        start_strips(b + 1, 1 - slot)

    def sized_copy(offset, size):
        return pltpu.make_async_copy(y_hbm.at[pl.ds(0, size * rpt), :],
                                     stage_ref.at[pl.ds(slot * slot_rows, size * rpt), :], sem.at[slot])
    sized_copy(0, tb * TOP_K).wait()
    _for_set_bits(extra_ref[b], n_exp.bit_length() - 1, lambda o, s: sized_copy(o, s).wait())

    def gather_block(static_slot):
        def gather_group(c, carry):
            tok0 = c * GATHER_UNROLL
            idx0 = b * tb + tok0
            for u in range(GATHER_UNROLL):
                lo = hi = None
                for k in range(TOP_K):
                    src = pl.multiple_of(lidx_ref[idx0 + (k * n_tok + u)], rpt)
                    g = gate_ref[idx0 + (k * n_tok + u)]
                    l, h = _unpack_words(stage_ref[pl.ds(static_slot * slot_rows + src, rpt), :])
                    lo = g * l if lo is None else lo + g * l
                    hi = g * h if hi is None else hi + g * h
                dst = pl.multiple_of((tok0 + u) * rpt, rpt)
                alo_ref[pl.ds(dst, rpt), :] = lo
                ahi_ref[pl.ds(dst, rpt), :] = hi
            return carry
        lax.fori_loop(0, tb // GATHER_UNROLL, gather_group, 0)

    for static_slot in range(2):
        pl.when(slot == static_slot)(functools.partial(gather_block, static_slot))

    parts = [alo_ref[pl.ds(si, tb, stride=rpt), :] for si in range(rpt)]
    parts += [ahi_ref[pl.ds(si, tb, stride=rpt), :] for si in range(rpt)]
    x2 = x1_ref[...] + jnp.concatenate(parts, axis=-1)
    out_ref[...] = _rms(x2, nw_ref[...])


def _combine(lidx, gates, nbe, lb, hrow, extra, x1, norm_w, y_rows):
    tb = TOK_BLOCK
    n_tok, d_model = x1.shape
    rpt = d_model // PACK // LANES
    n_exp = nbe.shape[0] // (n_tok // tb)
    slot_rows = (tb * TOP_K + n_exp * (STRIP_ALIGN - 1)) * rpt
    grid_spec = pltpu.PrefetchScalarGridSpec(
        num_scalar_prefetch=6,
        grid=(n_tok // tb,),
        in_specs=[
            pl.BlockSpec((tb, d_model), lambda i, *_: (i, 0)),
            pl.BlockSpec((1, d_model), lambda i, *_: (0, 0)),
            pl.BlockSpec(memory_space=pl.ANY),
        ],
        out_specs=pl.BlockSpec((tb, d_model), lambda i, *_: (i, 0)),
        scratch_shapes=[pltpu.VMEM((2 * slot_rows, LANES), jnp.uint32),
                        pltpu.VMEM((tb * rpt, LANES), jnp.float32),
                        pltpu.VMEM((tb * rpt, LANES), jnp.float32),
                        pltpu.SemaphoreType.DMA((2,))],
    )
    return pl.pallas_call(
        functools.partial(_combine_kernel, nblk=n_tok // tb),
        grid_spec=grid_spec,
        out_shape=jax.ShapeDtypeStruct((n_tok, d_model), jnp.float32),
        compiler_params=pltpu.CompilerParams(
            dimension_semantics=("arbitrary",),
            vmem_limit_bytes=VMEM_LIMIT_BYTES),
        name="combine",
    )(lidx, gates, nbe, lb, hrow, extra, x1, norm_w, y_rows)


def _retention_tables(seq, hd, ts):
    half = hd // 2
    inv = ROPE_BASE ** (-jnp.arange(half, dtype=jnp.float32) / half)
    inv2 = jnp.concatenate([inv, inv])

    def cos_sin(positions):
        ang = positions[:, None] * inv2[None, :]
        return jnp.stack([jnp.cos(ang), jnp.sin(ang)], axis=1)
    rope = cos_sin(jnp.arange(ts, dtype=jnp.float32)).transpose(1, 0, 2)
    rope_base = cos_sin(jnp.arange(seq // ts, dtype=jnp.float32) * ts)
    log_g = jnp.log1p(-jnp.exp2(-5.0 - jnp.arange(RET_HEADS, dtype=jnp.float32)))
    pos = jnp.arange(ts, dtype=jnp.float32)
    dist = jnp.abs(pos[:, None] - pos[None, :])
    chunk = jnp.arange(ts, dtype=jnp.int32) // CHUNK
    visible = chunk[None, :] <= chunk[:, None]
    dmat = jnp.where(visible[None], jnp.exp(log_g[:, None, None] * dist[None]), 0.0)
    qdec = jnp.exp(log_g[:, None] * (pos + 1.0)[None, :])
    kdec = jnp.exp(log_g[:, None] * (ts - 1.0 - pos)[None, :])
    qdec = jnp.broadcast_to(qdec[:, :, None], (RET_HEADS, ts, hd))
    kdec = jnp.broadcast_to(kdec[:, :, None], (RET_HEADS, ts, hd))
    tile_decay = tuple(math.exp(math.log1p(-2.0 ** (-5.0 - h)) * ts) for h in range(RET_HEADS))
    return rope, rope_base, dmat, qdec, kdec, tile_decay


def _routing_tables(nbe, n_tok, n_exp):
    tm, tb = ROW_GRANULE, TOK_BLOCK
    nal = nbe + (nbe & (STRIP_ALIGN - 1))
    region = jnp.sum(nal, axis=0)
    tiles_per = (region + tm - 1) // tm
    tile_end = jnp.cumsum(tiles_per)
    offs = (tile_end - tiles_per) * tm
    n_tiles = (n_tok * TOP_K + (n_tok // tb) * n_exp * (STRIP_ALIGN - 1)) // tm + n_exp
    hrow = offs[None, :] + jnp.cumsum(nal, axis=0) - nal
    padrow = offs + region
    padlen = tiles_per * tm - region
    as_i32 = lambda a: a.reshape(-1).astype(jnp.int32)
    return (n_tiles, as_i32(tile_end - tiles_per), as_i32(tiles_per), as_i32(tile_end[-1:]),
            as_i32(hrow), as_i32(jnp.sum(nal, axis=1) - tb * TOP_K), as_i32(padrow), as_i32(padlen))


def _layer(x, norm_mix_w, w_in, ret_norm_w, w_pool, pool_scale, w_out, norm_ffn_w,
           w_router, b_router, w_gu, b_gu, w_down, b_down, final_w):
    bsz, seq, d_model = x.shape
    n_tok = bsz * seq
    n_exp = w_router.shape[1]
    rw = ret_norm_w.shape[0]
    hd = rw // RET_HEADS
    ts = SEQ_TILE
    bf = jnp.bfloat16
    rpt = d_model // PACK // LANES
    assert d_model % (PACK * LANES) == 0 and (STRIP_ALIGN * rpt) % SUBLANES == 0
    assert seq % ts == 0 and ts % CHUNK == 0 and n_tok % TOK_BLOCK == 0
    assert n_tok % ROUTE_TILE == 0 and ROUTE_TILE % TOK_BLOCK == 0 and TOK_BLOCK % ROUTE_CHUNK == 0
    assert TOK_BLOCK & (TOK_BLOCK - 1) == 0 and ROW_GRANULE & (ROW_GRANULE - 1) == 0
    assert ROW_TILE % ROW_GRANULE == 0

    rope, rope_base, dmat, qdec, kdec, tile_decay = _retention_tables(seq, hd, ts)
    tri = jnp.arange(ROUTE_CHUNK, dtype=jnp.int32)
    utri = (tri[:, None] < tri[None, :]).astype(bf)

    x1, h2p = _mix(
        x, norm_mix_w[None], w_in.astype(bf), rope, rope_base, dmat, qdec, kdec, ret_norm_w[None],
        w_pool.astype(bf), pool_scale[None], w_out.astype(bf), norm_ffn_w[None], tile_decay)
    eri = jnp.arange(n_exp, dtype=jnp.int32)
    ltri = (eri[None, :] < eri[:, None]).astype(bf)
    lidx, gate, nbe, lb, _ = _route(h2p, w_router.T.astype(bf), b_router[:, None], utri, ltri)
    lidx = lidx.reshape(-1)
    nbe = nbe[:, :, 0].astype(jnp.int32)
    lb = lb[:, :, 0].astype(jnp.int32).reshape(-1)

    (n_tiles, tile_start, tile_count, n_used, hrow, extra, padrow, padlen) = _routing_tables(nbe, n_tok, n_exp)
    nbe = nbe.reshape(-1)

    xs = _dispatch(lidx, nbe, lb, hrow, extra, padrow, padlen, n_used, h2p, n_tiles * ROW_GRANULE, d_model)
    y_rows = _experts(tile_start, tile_count, xs, w_gu, b_gu, w_down, b_down)
    out = _combine(lidx, gate.reshape(-1), nbe, lb, hrow, extra, x1.reshape(n_tok, d_model), final_w[None], y_rows)
    return out.reshape(bsz, seq, d_model)


def kernel(x, norm_mix_w, w_in, ret_norm_w, w_pool, pool_scale, w_out, norm_ffn_w, w_router, b_router,
           w_gu, b_gu, w_down, b_down, norm_final_w):
    depth = norm_mix_w.shape[0]
    for l in range(depth):
        last = l == depth - 1
        assert last, "stacked layers need an un-normalised combine output"
        x = _layer(x, norm_mix_w[l], w_in[l], ret_norm_w[l], w_pool[l], pool_scale[l], w_out[l],
                   norm_ffn_w[l], w_router[l], b_router[l], w_gu[l], b_gu[l], w_down[l], b_down[l],
                   norm_final_w)
    return x
```
